```python
import math
import jax, jax.numpy as jnp
from jax import lax
import numpy as np

D_MODEL = 1024
BATCH = 2
SEQ = 8192
DEPTH = 2

HEAD_DIM = 64
MIX_WIDTH = D_MODEL
ATTN_HEADS = 6
ATTN_KV_HEADS = 2
ATTN_GROUP = ATTN_HEADS // ATTN_KV_HEADS
ATTN_WIDTH = ATTN_HEADS * HEAD_DIM
ATTN_KV_WIDTH = ATTN_KV_HEADS * HEAD_DIM
WINDOW = 128
BLOCK_Q = 128
N_BUCKETS = 32
MAX_EXACT = N_BUCKETS // 2
MAX_DISTANCE = WINDOW
RWKV_HEADS = 6
RWKV_HEAD_DIM = 64
RWKV_WIDTH = RWKV_HEADS * RWKV_HEAD_DIM
D_DECAY_LORA = 32
D_AAA_LORA = 32
D_MV_LORA = 16
D_GATE_LORA = 64
RWKV_SHIFT_COLS = 3 * RWKV_WIDTH + D_DECAY_LORA + D_AAA_LORA + D_GATE_LORA
MEM_TOKENS = 256
MEM_HEADS = 4
MEM_WIDTH = MIX_WIDTH - ATTN_WIDTH - RWKV_WIDTH
IN_BASE = ATTN_WIDTH + 2 * ATTN_KV_WIDTH + RWKV_SHIFT_COLS + MEM_WIDTH
D_FF = 2816
CONV_WIDTH = 3
EPS = 1e-6
GN_EPS = 64e-5
L2_EPS = 1e-12

kernel_name = "hybrid_swa_rwkv7_memory_convffn"


def split_cols(t, sizes):
    return jnp.split(t, [int(c) for c in np.cumsum(sizes)[:-1]], axis=-1)


def rms_norm(x, g):
    xf = x.astype(jnp.float32)
    y = xf * lax.rsqrt(jnp.mean(xf * xf, axis=-1, keepdims=True) + EPS)
    return (y * g.astype(jnp.float32)).astype(x.dtype)


def token_shift(p, mu):
    prev = jnp.pad(p, ((0, 0), (1, 0), (0, 0)))[:, :-1]
    return p + mu * (prev - p)


def t5_causal_bucket(dist):
    is_small = dist < MAX_EXACT
    d = jnp.maximum(dist, 1).astype(jnp.float32)
    large = MAX_EXACT + (jnp.log(d / MAX_EXACT) / math.log(MAX_DISTANCE / MAX_EXACT)
                         * (N_BUCKETS - MAX_EXACT)).astype(jnp.int32)
    large = jnp.minimum(large, N_BUCKETS - 1)
    return jnp.where(is_small, dist, large)


def sliding_window_attention(q, k, v, sinks, rel_bias):
    b, s = q.shape[0], q.shape[1]
    nb = s // BLOCK_Q
    qb = q.reshape(b, nb, BLOCK_Q, ATTN_KV_HEADS, ATTN_GROUP, HEAD_DIM)

    def band(t):
        tb = t.reshape(b, nb, BLOCK_Q, ATTN_KV_HEADS, HEAD_DIM)
        prev = jnp.concatenate([jnp.zeros_like(tb[:, :1]), tb[:, :-1]], axis=1)
        return jnp.concatenate([prev, tb], axis=2)

    kw, vw = band(k), band(v)
    logits = jnp.einsum('bnqhgd,bnkhd->bnhgqk', qb, kw,
                        preferred_element_type=jnp.float32) * (HEAD_DIM ** -0.5)
    qi = jnp.arange(BLOCK_Q)[:, None]
    kj = jnp.arange(2 * BLOCK_Q)[None, :]
    dist = qi + BLOCK_Q - kj
    in_band = (dist >= 0) & (dist < WINDOW)
    key_pos = jnp.arange(nb)[:, None, None] * BLOCK_Q - BLOCK_Q + kj[None]
    valid = in_band[None] & (key_pos >= 0)
    bucket = t5_causal_bucket(jnp.maximum(dist, 0))
    bias = rel_bias.astype(jnp.float32)[bucket]
    bias = bias.transpose(2, 0, 1).reshape(ATTN_KV_HEADS, ATTN_GROUP, BLOCK_Q, 2 * BLOCK_Q)
    logits = jnp.where(valid[None, :, None, None], logits + bias, -jnp.inf)
    sink = sinks.astype(jnp.float32).reshape(ATTN_KV_HEADS, ATTN_GROUP)[None, None, :, :, None, None]
    m = jnp.maximum(jnp.max(logits, axis=-1, keepdims=True), sink)
    e = jnp.exp(logits - m)
    p = e / (jnp.sum(e, axis=-1, keepdims=True) + jnp.exp(sink - m))
    out = jnp.einsum('bnhgqk,bnkhd->bnqhgd', p.astype(v.dtype), vw)
    return out.reshape(b, s, ATTN_WIDTH)


def rwkv7_scan(r, w, k, v, a, bvec):
    bsz = r.shape[0]
    decay = jnp.exp(-jnp.exp(w))

    def step(state, inp):
        r_t, d_t, k_t, v_t, a_t, b_t = inp
        sa = jnp.einsum('bhvk,bhk->bhv', state, a_t)
        state = (state * d_t[:, :, None, :] + sa[..., None] * b_t[:, :, None, :]
                 + v_t[..., None] * k_t[:, :, None, :])
        return state, jnp.einsum('bhvk,bhk->bhv', state, r_t)

    xs = tuple(jnp.moveaxis(t, 1, 0) for t in (r, decay, k, v, a, bvec))
    s0 = jnp.zeros((bsz, RWKV_HEADS, RWKV_HEAD_DIM, RWKV_HEAD_DIM), jnp.float32)
    _, y = lax.scan(step, s0, xs)
    return jnp.moveaxis(y, 0, 1)


def rwkv7_mixer(pb, value_residual, w0, w2, a0, a2, g2, k_k, k_a, r_k, ln_w, ln_b):
    out_dtype = pb.dtype
    f32 = lambda t: t.astype(jnp.float32)
    r, k, v, wd, ad, gd = split_cols(f32(pb), [RWKV_WIDTH, RWKV_WIDTH, RWKV_WIDTH,
                                               D_DECAY_LORA, D_AAA_LORA, D_GATE_LORA])
    w = -jax.nn.softplus(-(f32(w0) + jnp.tanh(wd) @ f32(w2))) - 0.5
    a = jax.nn.sigmoid(f32(a0) + ad @ f32(a2))
    g = jax.nn.sigmoid(gd) @ f32(g2)
    if value_residual is not None:
        v_first, v_down, v0, v2 = value_residual
        v = v + (f32(v_first) - v) * jax.nn.sigmoid(f32(v0) + f32(v_down) @ f32(v2))
    v_out = v
    b, s = pb.shape[0], pb.shape[1]
    heads = lambda t: t.reshape(b, s, RWKV_HEADS, RWKV_HEAD_DIM)
    kk = heads(k * f32(k_k))
    kk = kk / jnp.maximum(jnp.sqrt(jnp.sum(kk * kk, axis=-1, keepdims=True)), L2_EPS)
    k = k * (1.0 + (a - 1.0) * f32(k_a))
    rh, kh, vh, ah = heads(r), heads(k), heads(v), heads(a)
    y = rwkv7_scan(rh, heads(w), kh, vh, -kk, kk * ah)
    mu = jnp.mean(y, axis=-1, keepdims=True)
    var = jnp.mean(jnp.square(y - mu), axis=-1, keepdims=True)
    y = ((y - mu) * lax.rsqrt(var + GN_EPS)).reshape(b, s, RWKV_WIDTH) * f32(ln_w) + f32(ln_b)
    bonus = jnp.sum(rh * kh * f32(r_k), axis=-1, keepdims=True) * vh
    y = (y + bonus.reshape(b, s, RWKV_WIDTH)) * g
    return y.astype(out_dtype), v_out


def memory_attention(q, mk, mv):
    logits = jnp.einsum('bshd,bmhd->bhsm', q, mk,
                        preferred_element_type=jnp.float32) * (HEAD_DIM ** -0.5)
    p = jax.nn.softmax(logits, axis=-1)
    out = jnp.einsum('bhsm,bmhd->bshd', p.astype(mv.dtype), mv)
    return out.reshape(q.shape[0], q.shape[1], MEM_WIDTH)


def conv_ffn(h, w_up, conv_w, conv_b, w_down):
    u = h @ w_up
    s = u.shape[1]
    up_pad = jnp.pad(u, ((0, 0), (CONV_WIDTH - 1, 0), (0, 0)))
    uc = conv_b + sum(conv_w[i] * up_pad[:, i:i + s] for i in range(CONV_WIDTH))
    gate, val = jnp.split(uc, 2, axis=-1)
    return (jax.nn.silu(gate) * val) @ w_down


def setup_inputs(seed: int = 0) -> dict:
    key = jax.random.key(seed)
    ks = iter(jax.random.split(key, 40))
    n = lambda shape: jax.random.normal(next(ks), shape, jnp.float32)
    L, Lv = DEPTH, DEPTH - 1
    C = RWKV_WIDTH
    return {
        "x": n((BATCH, SEQ, D_MODEL)),
        "mem": n((BATCH, MEM_TOKENS, D_MODEL)),
        "rel_bias": 0.5 * n((N_BUCKETS, ATTN_HEADS)),
        "mix_norm_g": 1.0 + 0.1 * n((L, D_MODEL)),
        "w_in": n((L, D_MODEL, IN_BASE)) * D_MODEL ** -0.5,
        "w_in_vres": n((Lv, D_MODEL, D_MV_LORA)) * D_MODEL ** -0.5,
        "attn_q_norm": 1.0 + 0.1 * n((L, HEAD_DIM)),
        "attn_k_norm": 1.0 + 0.1 * n((L, HEAD_DIM)),
        "attn_sinks": n((L, ATTN_HEADS)),
        "rwkv_mu": jax.random.uniform(next(ks), (L, RWKV_SHIFT_COLS), jnp.float32),
        "rwkv_mu_vres": jax.random.uniform(next(ks), (Lv, D_MV_LORA), jnp.float32),
        "rwkv_w0": -2.0 + 0.5 * n((L, C)),
        "rwkv_w2": 0.5 * n((L, D_DECAY_LORA, C)) * D_DECAY_LORA ** -0.5,
        "rwkv_a0": 0.5 * n((L, C)),
        "rwkv_a2": 0.5 * n((L, D_AAA_LORA, C)) * D_AAA_LORA ** -0.5,
        "rwkv_v0": 0.5 * n((Lv, C)),
        "rwkv_v2": 0.5 * n((Lv, D_MV_LORA, C)) * D_MV_LORA ** -0.5,
        "rwkv_g2": n((L, D_GATE_LORA, C)) * D_GATE_LORA ** -0.5,
        "rwkv_k_k": 0.85 + 0.1 * n((L, C)),
        "rwkv_k_a": 1.0 + 0.1 * n((L, C)),
        "rwkv_r_k": 0.1 * n((L, RWKV_HEADS, RWKV_HEAD_DIM)),
        "rwkv_ln_w": 1.0 + 0.1 * n((L, C)),
        "rwkv_ln_b": 0.02 * n((L, C)),
        "mem_norm_g": 1.0 + 0.1 * n((L, D_MODEL)),
        "w_mem_kv": n((L, D_MODEL, 2 * MEM_WIDTH)) * D_MODEL ** -0.5,
        "mem_q_norm": 1.0 + 0.1 * n((L, HEAD_DIM)),
        "mem_k_norm": 1.0 + 0.1 * n((L, HEAD_DIM)),
        "w_out": n((L, MIX_WIDTH, D_MODEL)) * MIX_WIDTH ** -0.5,
        "ffn_norm_g": 1.0 + 0.1 * n((L, D_MODEL)),
        "w_up": n((L, D_MODEL, 2 * D_FF)) * D_MODEL ** -0.5,
        "conv_w": n((L, CONV_WIDTH, 2 * D_FF)) * CONV_WIDTH ** -0.5,
        "conv_b": 0.02 * n((L, 2 * D_FF)),
        "w_down": n((L, D_FF, D_MODEL)) * D_FF ** -0.5,
    }


def reference(x, mem, rel_bias, mix_norm_g, w_in, w_in_vres, attn_q_norm, attn_k_norm, attn_sinks,
              rwkv_mu, rwkv_mu_vres, rwkv_w0, rwkv_w2, rwkv_a0, rwkv_a2, rwkv_v0, rwkv_v2, rwkv_g2,
              rwkv_k_k, rwkv_k_a, rwkv_r_k, rwkv_ln_w, rwkv_ln_b, mem_norm_g, w_mem_kv,
              mem_q_norm, mem_k_norm, w_out, ffn_norm_g, w_up, conv_w, conv_b, w_down):
    b, s = x.shape[0], x.shape[1]
    m_tok = mem.shape[1]
    v_first = None
    for l in range(DEPTH):
        h = rms_norm(x, mix_norm_g[l])
        w_l = w_in[l] if l == 0 else jnp.concatenate([w_in[l], w_in_vres[l - 1]], axis=1)
        proj = h @ w_l
        sizes = [ATTN_WIDTH, ATTN_KV_WIDTH, ATTN_KV_WIDTH, RWKV_SHIFT_COLS, MEM_WIDTH]
        if l > 0:
            sizes = sizes + [D_MV_LORA]
        parts = split_cols(proj, sizes)
        qa, ka, va, pb, qm = parts[:5]

        qa = rms_norm(qa.reshape(b, s, ATTN_HEADS, HEAD_DIM), attn_q_norm[l])
        ka = rms_norm(ka.reshape(b, s, ATTN_KV_HEADS, HEAD_DIM), attn_k_norm[l])
        va = va.reshape(b, s, ATTN_KV_HEADS, HEAD_DIM)
        out_a = sliding_window_attention(qa, ka, va, attn_sinks[l], rel_bias)

        pb = token_shift(pb, rwkv_mu[l])
        if l == 0:
            vres = None
        else:
            v_down = token_shift(parts[5], rwkv_mu_vres[l - 1])
            vres = (v_first, v_down, rwkv_v0[l - 1], rwkv_v2[l - 1])
        out_b, v_l = rwkv7_mixer(pb, vres, rwkv_w0[l], rwkv_w2[l], rwkv_a0[l], rwkv_a2[l],
                                 rwkv_g2[l], rwkv_k_k[l], rwkv_k_a[l], rwkv_r_k[l],
                                 rwkv_ln_w[l], rwkv_ln_b[l])
        if l == 0:
            v_first = v_l

        mkv = rms_norm(mem, mem_norm_g[l]) @ w_mem_kv[l]
        mk, mv = jnp.split(mkv, 2, axis=-1)
        mk = rms_norm(mk.reshape(b, m_tok, MEM_HEADS, HEAD_DIM), mem_k_norm[l])
        mv = mv.reshape(b, m_tok, MEM_HEADS, HEAD_DIM)
        qm = rms_norm(qm.reshape(b, s, MEM_HEADS, HEAD_DIM), mem_q_norm[l])
        out_m = memory_attention(qm, mk, mv)

        x = x + jnp.concatenate([out_a, out_b, out_m], axis=-1) @ w_out[l]

        x = x + conv_ffn(rms_norm(x, ffn_norm_g[l]), w_up[l], conv_w[l], conv_b[l], w_down[l])
    return x
```

```python
import functools
import math

import jax
import jax.numpy as jnp
import numpy as np
from jax import lax
from jax.experimental import pallas as pl
from jax.experimental.pallas import tpu as pltpu

F32 = jnp.float32
BF16 = jnp.bfloat16

HEAD_DIM = 64
SCAN_CHUNK = 64
VMEM_LIMIT_BYTES = 56 * 1024 * 1024


def _dot(a, b):
    return jnp.dot(a, b, preferred_element_type=F32)


def _dot_nt(a, b):
    return lax.dot_general(a, b, (((1,), (1,)), ((), ())), preferred_element_type=F32)


def _dot_tn(a, b):
    return lax.dot_general(a, b, (((0,), (0,)), ((), ())), preferred_element_type=F32)


def _bf(x):
    return x.astype(BF16)


def _scan_kernel(r_ref, lw_ref, k_ref, v_ref, a_ref, b_ref, y_ref, h_ref, *, nb, nh):
    C, N = SCAN_CHUNK, HEAD_DIM

    @pl.when(pl.program_id(0) == 0)
    def _():
        h_ref[...] = jnp.zeros_like(h_ref)

    row = lax.broadcasted_iota(jnp.int32, (C, C), 0)
    col = lax.broadcasted_iota(jnp.int32, (C, C), 1)
    incl = row >= col
    strict = row > col
    tri = incl.astype(BF16)
    eye = (row == col).astype(F32)

    pre = []
    for b in range(nb):
        lw = lw_ref[b]
        l1 = _bf(lw)
        e1 = lw - l1.astype(F32)
        l2 = _bf(e1)
        l3 = _bf(e1 - l2.astype(F32))
        cum = _dot(tri, l1) + _dot(tri, l2) + _dot(tri, l3)
        cum_last = cum[C - 1:C, :]
        g_inv = jnp.exp(-cum)
        g_out = jnp.exp(cum_last - cum)
        kk = k_ref[b]
        bb = b_ref[b]
        pre.append(dict(
            a_t=a_ref[b] * jnp.exp(cum - lw), r_t=r_ref[b] * jnp.exp(cum), b_t=bb * g_inv, k_t=kk * g_inv,
            b_h=bb * g_out, k_h=kk * g_out, g_c=jnp.broadcast_to(jnp.exp(cum_last), cum.shape), v=v_ref[b]))

    heads = [(b, h) for b in range(nb) for h in range(nh)]
    hs = lambda name, b, h: pre[b][name][:, h * N:(h + 1) * N]
    a_b = [_bf(hs("a_t", b, h)) for b, h in heads]
    r_b = [_bf(hs("r_t", b, h)) for b, h in heads]
    v_b = [_bf(hs("v", b, h)) for b, h in heads]
    ar = [jnp.concatenate([x, y], axis=0) for x, y in zip(a_b, r_b)]
    pb = [_dot_nt(ar[i], _bf(hs("b_t", b, h))) for i, (b, h) in enumerate(heads)]
    pk = [_dot_nt(ar[i], _bf(hs("k_t", b, h))) for i, (b, h) in enumerate(heads)]
    l_ab = [jnp.where(strict, p[:C], 0.0) for p in pb]
    m_rb = [_bf(jnp.where(incl, p[C:], 0.0)) for p in pb]
    l_ak = [_bf(jnp.where(strict, p[:C], 0.0)) for p in pk]
    m_rk = [_bf(jnp.where(incl, p[C:], 0.0)) for p in pk]
    t_inv = [eye + l for l in l_ab]
    pw = l_ab
    for _ in range(int(math.log2(C)) - 1):
        pw = [_dot(_bf(p), _bf(p)) for p in pw]
        t_inv = [t + _dot(_bf(t), _bf(p)) for t, p in zip(t_inv, pw)]
    t_b = [_bf(t) for t in t_inv]
    w1 = [_bf(_dot(t, x)) for t, x in zip(t_b, a_b)]
    lv = [_bf(_dot(l, x)) for l, x in zip(l_ak, v_b)]
    w2 = [_dot(t, x) for t, x in zip(t_b, lv)]
    y_v = [_dot(m, x) for m, x in zip(m_rk, v_b)]
    kv = [_dot_tn(_bf(hs("k_h", b, h)), v_b[i]) for i, (b, h) in enumerate(heads)]
    g_col = [hs("g_c", b, h).T for b, h in heads]
    h0 = [h_ref[i] for i in range(len(heads))]
    h0_b = [_bf(x) for x in h0]
    u = [_dot(w, x) + z for w, x, z in zip(w1, h0_b, w2)]
    u_b = [_bf(x) for x in u]
    y = [_dot(r_b[i], h0_b[i]) + _dot(m_rb[i], u_b[i]) + y_v[i] for i in range(len(heads))]
    for i, (b, h) in enumerate(heads):
        h_ref[i] = g_col[i] * h0[i] + _dot_tn(_bf(hs("b_h", b, h)), u_b[i]) + kv[i]
        y_ref[b, :, h * N:(h + 1) * N] = y[i]


def _rwkv_scan(r, lw, k, v, a, b, *, nb):
    T, W = r.shape
    S = T // nb
    nh = W // HEAD_DIM
    C = SCAN_CHUNK
    assert S % C == 0
    spec = pl.BlockSpec((nb, C, W), lambda c: (0, c, 0))
    args = [t.reshape(nb, S, W) for t in (r, lw, k, v, a, b)]
    y = pl.pallas_call(
        functools.partial(_scan_kernel, nb=nb, nh=nh),
        grid=(S // C,),
        in_specs=[spec] * 6,
        out_specs=spec,
        out_shape=jax.ShapeDtypeStruct((nb, S, W), F32),
        scratch_shapes=[pltpu.VMEM((nb * nh, HEAD_DIM, HEAD_DIM), F32)],
        compiler_params=pltpu.CompilerParams(dimension_semantics=("arbitrary",), vmem_limit_bytes=VMEM_LIMIT_BYTES),
        name="rwkv_scan",
    )(*args)
    return y.reshape(T, W)


D_MODEL = 1024
ATTN_HEADS = 6
ATTN_KV_HEADS = 2
ATTN_GROUP = ATTN_HEADS // ATTN_KV_HEADS
ATTN_W = ATTN_HEADS * HEAD_DIM
KV_W = ATTN_KV_HEADS * HEAD_DIM
BLOCK_Q = 128
WINDOW = 128
N_BUCKETS = 32
MAX_EXACT = N_BUCKETS // 2
RWKV_W = 6 * HEAD_DIM
LORA_W = 128
SHIFT_W = 3 * RWKV_W + LORA_W
MEM_HEADS = 4
MEM_W = MEM_HEADS * HEAD_DIM
IN_BASE = ATTN_W + 2 * KV_W + SHIFT_W + MEM_W
PB_OFF = ATTN_W + 2 * KV_W
QM_OFF = PB_OFF + SHIFT_W
D_FF = 2816
EPS = 1e-6
GN_EPS = 64e-5
L2_EPS = 1e-12
HALO = 16

PROJ_TM = 512
MIX_TQ = 512
FFN_TM = 512
FFN_FC = 256


def _block_diag_ones(width):
    idx = np.arange(width) // HEAD_DIM
    return jnp.asarray((idx[:, None] == idx[None, :]).astype(np.float32), dtype=BF16)


def _head_sum(t, bd):
    return _dot(_bf(t), bd)


def _head_rms(t, bd, gain):
    ms = _head_sum(t * t, bd) * (1.0 / HEAD_DIM)
    return t * lax.rsqrt(ms + EPS) * gain


def _rms_rows(x, g):
    ms = jnp.mean(x * x, axis=-1, keepdims=True)
    return x * lax.rsqrt(ms + EPS) * g


def _proj_kernel(*refs, tm, tiles_per_batch, has_vres):
    it = iter(refs)
    x_ref, g_ref, w_ref = next(it), next(it), next(it)
    wv_ref = next(it) if has_vres else None
    qg_ref, kg_ref, mg_ref, mu_ref = next(it), next(it), next(it), next(it)
    w0_ref, a0_ref, kk_ref, ka_ref = next(it), next(it), next(it), next(it)
    w2_ref, a2_ref, g2_ref = next(it), next(it), next(it)
    bd384_ref, bd128_ref, bd256_ref = next(it), next(it), next(it)
    if has_vres:
        vfirst_ref, v0_ref, v2_ref = next(it), next(it), next(it)
    qa_ref, ka_out_ref, va_ref, qm_ref = next(it), next(it), next(it), next(it)
    r_ref, lw_ref, k_ref, v_ref, a_ref, b_ref, gate_ref = (next(it) for _ in range(7))
    pbs_ref = next(it)

    x = x_ref[...]
    hn = _bf(_rms_rows(x, g_ref[...]))
    proj = _dot(hn, w_ref[...])
    bd384 = bd384_ref[...]
    qa_ref[...] = _bf(_head_rms(proj[:, :ATTN_W], bd384, qg_ref[...]))
    ka_out_ref[...] = _bf(_head_rms(proj[:, ATTN_W:ATTN_W + KV_W], bd128_ref[...], kg_ref[...]))
    va_ref[...] = _bf(proj[:, ATTN_W + KV_W:PB_OFF])
    qm_ref[...] = _bf(_head_rms(proj[:, QM_OFF:QM_OFF + MEM_W], bd256_ref[...], mg_ref[...]))

    @pl.when(pl.program_id(0) % tiles_per_batch == 0)
    def _():
        pbs_ref[0:8, :] = jnp.zeros((8, pbs_ref.shape[1]), F32)

    pbs_ref[8:tm + 8, 0:SHIFT_W] = proj[:, PB_OFF:PB_OFF + SHIFT_W]
    if has_vres:
        pbs_ref[8:tm + 8, SHIFT_W:SHIFT_W + LORA_W] = _dot(hn, wv_ref[...])
    cur = pbs_ref[8:tm + 8, :]
    prev = pbs_ref[7:tm + 7, :]
    sh = cur + mu_ref[...] * (prev - cur)
    pbs_ref[0:8, :] = pbs_ref[tm:tm + 8, :]

    r = sh[:, 0:RWKV_W]
    k = sh[:, RWKV_W:2 * RWKV_W]
    v = sh[:, 2 * RWKV_W:3 * RWKV_W]
    z = sh[:, 3 * RWKV_W:SHIFT_W]
    t = w0_ref[...] + _dot(_bf(jnp.tanh(z)), w2_ref[...])
    lw_ref[...] = -math.exp(-0.5) * jax.nn.sigmoid(t)
    a = jax.nn.sigmoid(a0_ref[...] + _dot(_bf(z), a2_ref[...]))
    gate_ref[...] = _dot(_bf(jax.nn.sigmoid(z)), g2_ref[...])
    if has_vres:
        vd = sh[:, SHIFT_W:SHIFT_W + LORA_W]
        v = v + (vfirst_ref[...] - v) * jax.nn.sigmoid(v0_ref[...] + _dot(_bf(vd), v2_ref[...]))
    kk = k * kk_ref[...]
    kk = kk / jnp.maximum(jnp.sqrt(_head_sum(kk * kk, bd384)), L2_EPS)
    r_ref[...] = r
    k_ref[...] = k * (1.0 + (a - 1.0) * ka_ref[...])
    v_ref[...] = v
    a_ref[...] = -kk
    b_ref[...] = kk * a


def _pad_rows(w, rows, at):
    out = jnp.zeros((rows, w.shape[1]), w.dtype)
    return lax.dynamic_update_slice(out, w, (at, 0))


def _proj_call(x, p, l, v_first, *, nb):
    T, D = x.shape
    tm = PROJ_TM
    assert T % tm == 0 and (T // nb) % tm == 0
    has_vres = l > 0
    row = lambda a: a.reshape(1, -1).astype(F32)
    tile6 = lambda a: jnp.tile(a, ATTN_HEADS)
    scale = HEAD_DIM ** -0.5
    mu = p["rwkv_mu"][l]
    if has_vres:
        mu = jnp.concatenate([mu, p["rwkv_mu_vres"][l - 1], jnp.zeros((LORA_W - 16,), F32)])
    ins = [x, row(p["mix_norm_g"][l]), _bf(p["w_in"][l])]
    if has_vres:
        ins.append(_bf(jnp.pad(p["w_in_vres"][l - 1], ((0, 0), (0, LORA_W - 16)))))
    ins += [row(tile6(p["attn_q_norm"][l]) * scale), row(jnp.tile(p["attn_k_norm"][l], ATTN_KV_HEADS)),
            row(jnp.tile(p["mem_q_norm"][l], MEM_HEADS) * scale), row(mu),
            row(p["rwkv_w0"][l]), row(p["rwkv_a0"][l]), row(p["rwkv_k_k"][l]), row(p["rwkv_k_a"][l]),
            _bf(_pad_rows(p["rwkv_w2"][l], LORA_W, 0)), _bf(_pad_rows(p["rwkv_a2"][l], LORA_W, 32)),
            _bf(_pad_rows(p["rwkv_g2"][l], LORA_W, 64)),
            _block_diag_ones(ATTN_W), _block_diag_ones(KV_W), _block_diag_ones(MEM_W)]
    if has_vres:
        ins += [v_first, row(p["rwkv_v0"][l - 1]), _bf(_pad_rows(p["rwkv_v2"][l - 1], LORA_W, 0))]

    def spec(a):
        if a.shape[0] == T:
            return pl.BlockSpec((tm, a.shape[1]), lambda i: (i, 0))
        return pl.BlockSpec(a.shape, lambda i: (0, 0))

    out_shapes = [jax.ShapeDtypeStruct((T, ATTN_W), BF16), jax.ShapeDtypeStruct((T, KV_W), BF16),
                  jax.ShapeDtypeStruct((T, KV_W), BF16), jax.ShapeDtypeStruct((T, MEM_W), BF16)]
    out_shapes += [jax.ShapeDtypeStruct((T, RWKV_W), F32)] * 7
    sw = SHIFT_W + (LORA_W if has_vres else 0)
    return pl.pallas_call(
        functools.partial(_proj_kernel, tm=tm, tiles_per_batch=(T // nb) // tm, has_vres=has_vres),
        grid=(T // tm,),
        in_specs=[spec(a) for a in ins],
        out_specs=[pl.BlockSpec((tm, s.shape[1]), lambda i: (i, 0)) for s in out_shapes],
        out_shape=out_shapes,
        scratch_shapes=[pltpu.VMEM((tm + 8, sw), F32)],
        compiler_params=pltpu.CompilerParams(dimension_semantics=("arbitrary",), vmem_limit_bytes=VMEM_LIMIT_BYTES),
        name=f"proj_l{l}",
    )(*ins)


def _bucket_table():
    qi = np.arange(BLOCK_Q)[:, None]
    kj = np.arange(2 * BLOCK_Q)[None, :]
    dist = qi + BLOCK_Q - kj
    in_band = (dist >= 0) & (dist < WINDOW)
    d = np.maximum(dist, 1).astype(np.float32)
    large = MAX_EXACT + (np.log(d / np.float32(MAX_EXACT)) / np.float32(math.log(WINDOW / MAX_EXACT))
                         * np.float32(N_BUCKETS - MAX_EXACT)).astype(np.int32)
    large = np.minimum(large, N_BUCKETS - 1)
    bucket = np.where(dist < MAX_EXACT, np.maximum(dist, 0), large)
    return np.where(in_band, bucket, -1).astype(np.int32)


def _bias_kernel(rb_ref, bucket_ref, out_ref):
    bucket = bucket_ref[...]
    for h in range(ATTN_HEADS):
        acc = jnp.full(bucket.shape, -jnp.inf, F32)
        for j in range(N_BUCKETS):
            acc = jnp.where(bucket == j, rb_ref[j, h], acc)
        out_ref[h] = acc


def _bias_call(rel_bias):
    tab = pl.pallas_call(
        _bias_kernel,
        in_specs=[pl.BlockSpec(memory_space=pltpu.SMEM), pl.BlockSpec(memory_space=pltpu.VMEM)],
        out_specs=pl.BlockSpec(memory_space=pltpu.VMEM),
        out_shape=jax.ShapeDtypeStruct((ATTN_HEADS, BLOCK_Q, 2 * BLOCK_Q), F32),
        name="rel_bias_table",
    )(rel_bias.astype(F32), jnp.asarray(_bucket_table()))
    return tab.reshape(ATTN_KV_HEADS, ATTN_GROUP * BLOCK_Q, 2 * BLOCK_Q)


def _memkv_kernel(mem_ref, g_ref, w_ref, kg_ref, bd_ref, mk_ref, mv_ref):
    hn = _bf(_rms_rows(mem_ref[...], g_ref[...]))
    kv = _dot(hn, w_ref[...])
    mk_ref[...] = _bf(_head_rms(kv[:, :MEM_W], bd_ref[...], kg_ref[...]))
    mv_ref[...] = _bf(kv[:, MEM_W:])


def _memkv_call(mem2d, p, l):
    rows = mem2d.shape[0]
    vm = pl.BlockSpec(memory_space=pltpu.VMEM)
    return pl.pallas_call(
        _memkv_kernel,
        in_specs=[vm] * 5,
        out_specs=[vm, vm],
        out_shape=[jax.ShapeDtypeStruct((rows, MEM_W), BF16)] * 2,
        compiler_params=pltpu.CompilerParams(vmem_limit_bytes=VMEM_LIMIT_BYTES),
        name=f"mem_kv_l{l}",
    )(mem2d, p["mem_norm_g"][l].reshape(1, -1), _bf(p["w_mem_kv"][l]),
      jnp.tile(p["mem_k_norm"][l], MEM_HEADS).reshape(1, -1), _block_diag_ones(MEM_W))


def _mix_kernel(x_ref, qa_ref, kc_ref, kp_ref, vc_ref, vp_ref, bias_ref, sink_ref, qm_ref, mk_ref, mv_ref,
                y_ref, r_ref, k_ref, v_ref, gate_ref, lnw_ref, lnb_ref, rk_ref, bd_ref, wout_ref, out_ref,
                *, tq, tiles_per_batch):
    N, BQ = HEAD_DIM, BLOCK_Q
    seq_start = pl.program_id(0) % tiles_per_batch == 0
    nqb = tq // BQ
    qa = qa_ref[...]
    kall = jnp.concatenate([kp_ref[...], kc_ref[...]], axis=0)
    vall = jnp.concatenate([vp_ref[...], vc_ref[...]], axis=0)
    before_seq = lax.broadcasted_iota(jnp.int32, (ATTN_GROUP * BQ, 2 * BQ), 1) < BQ

    head_rows = [[None] * nqb for _ in range(ATTN_HEADS)]
    for j in range(nqb):
        kw = kall[j * BQ:(j + 2) * BQ]
        vw = vall[j * BQ:(j + 2) * BQ]
        for g in range(ATTN_KV_HEADS):
            qg = jnp.concatenate(
                [qa[j * BQ:(j + 1) * BQ, (ATTN_GROUP * g + i) * N:(ATTN_GROUP * g + i + 1) * N] for i in range(ATTN_GROUP)],
                axis=0)
            logits = _dot_nt(qg, kw[:, g * N:(g + 1) * N]) + bias_ref[g]
            if j == 0:
                logits = jnp.where(jnp.logical_and(seq_start, before_seq), -jnp.inf, logits)
            sink = sink_ref[g]
            m = jnp.maximum(jnp.max(logits, axis=-1, keepdims=True), sink)
            e = jnp.exp(logits - m)
            denom = jnp.sum(e, axis=-1, keepdims=True) + jnp.exp(sink - m)
            o = _dot(_bf(e / denom), vw[:, g * N:(g + 1) * N])
            for i in range(ATTN_GROUP):
                head_rows[ATTN_GROUP * g + i][j] = o[i * BQ:(i + 1) * BQ]
    out_a = jnp.concatenate([jnp.concatenate(rows, axis=0) for rows in head_rows], axis=-1)

    qm = qm_ref[...]
    mk = mk_ref[0]
    mv = mv_ref[0]
    mem_heads = []
    for h in range(MEM_HEADS):
        hs = slice(h * N, (h + 1) * N)
        logits = _dot_nt(qm[:, hs], mk[:, hs])
        e = jnp.exp(logits - jnp.max(logits, axis=-1, keepdims=True))
        pr = e / jnp.sum(e, axis=-1, keepdims=True)
        mem_heads.append(_dot(_bf(pr), mv[:, hs]))
    out_m = jnp.concatenate(mem_heads, axis=-1)

    bd = bd_ref[...]
    y = y_ref[...]
    d = y - _head_sum(y, bd) * (1.0 / N)
    var = _head_sum(d * d, bd) * (1.0 / N)
    yn = d * lax.rsqrt(var + GN_EPS) * lnw_ref[...] + lnb_ref[...]
    bonus = _head_sum(r_ref[...] * k_ref[...] * rk_ref[...], bd) * v_ref[...]
    out_b = (yn + bonus) * gate_ref[...]

    mixed = _bf(jnp.concatenate([out_a, out_b, out_m], axis=-1))
    out_ref[...] = x_ref[...] + _dot(mixed, wout_ref[...])


def _mix_call(x, qa, ka, va, qm, mk, mv, y, r, k, v, gate, bias, p, l, *, nb):
    T, D = x.shape
    tq = MIX_TQ
    S = T // nb
    assert S % tq == 0 and tq % BLOCK_Q == 0
    tpb = S // tq
    qpb = tq // BLOCK_Q
    mem_tokens = mk.shape[0] // nb
    row = lambda a: a.reshape(1, -1).astype(F32)
    sink = jnp.repeat(p["attn_sinks"][l].astype(F32), BLOCK_Q).reshape(ATTN_KV_HEADS, ATTN_GROUP * BLOCK_Q, 1)
    tile = lambda w: pl.BlockSpec((tq, w), lambda i: (i, 0))
    prev = pl.BlockSpec((BLOCK_Q, KV_W), lambda i: (jnp.maximum(i * qpb - 1, 0), 0))
    full = lambda a: pl.BlockSpec(a.shape, lambda i: (0,) * a.ndim)
    memspec = pl.BlockSpec((1, mem_tokens, MEM_W), lambda i: (i // tpb, 0, 0))
    lnw, lnb, rk = row(p["rwkv_ln_w"][l]), row(p["rwkv_ln_b"][l]), row(p["rwkv_r_k"][l])
    bd = _block_diag_ones(RWKV_W)
    wout = _bf(p["w_out"][l])
    mk3 = mk.reshape(nb, mem_tokens, MEM_W)
    mv3 = mv.reshape(nb, mem_tokens, MEM_W)
    return pl.pallas_call(
        functools.partial(_mix_kernel, tq=tq, tiles_per_batch=tpb),
        grid=(T // tq,),
        in_specs=[tile(D), tile(ATTN_W), tile(KV_W), prev, tile(KV_W), prev, full(bias), full(sink), tile(MEM_W),
                  memspec, memspec] + [tile(RWKV_W)] * 5 + [full(lnw), full(lnb), full(rk), full(bd), full(wout)],
        out_specs=tile(D),
        out_shape=jax.ShapeDtypeStruct((T, D), F32),
        compiler_params=pltpu.CompilerParams(dimension_semantics=("arbitrary",), vmem_limit_bytes=VMEM_LIMIT_BYTES),
        name=f"mix_l{l}",
    )(x, qa, ka, ka, va, va, bias, sink, qm, mk3, mv3, y, r, k, v, gate, lnw, lnb, rk, bd, wout)


def _ffn_kernel(x_ref, xh_ref, g_ref, wup_ref, cw_ref, cb_ref, wdn_ref, out_ref, ug_ref, uv_ref,
                *, tm, tiles_per_batch):
    seq_start = pl.program_id(0) % tiles_per_batch == 0
    g = g_ref[...]
    x = x_ref[...]
    halo = jnp.where(seq_start, 0.0, _rms_rows(xh_ref[...], g))
    h_ext = _bf(jnp.concatenate([halo, _rms_rows(x, g)], axis=0))

    def conv(buf_ref, cols):
        w = cw_ref[:, cols]
        return (cb_ref[:, cols] + w[0:1] * buf_ref[HALO - 2:HALO - 2 + tm] + w[1:2] * buf_ref[HALO - 1:HALO - 1 + tm]
                + w[2:3] * buf_ref[HALO:HALO + tm])

    acc = None
    for c in range(D_FF // FFN_FC):
        gs = slice(c * FFN_FC, (c + 1) * FFN_FC)
        vs = slice(D_FF + c * FFN_FC, D_FF + (c + 1) * FFN_FC)
        ug_ref[...] = _dot(h_ext, wup_ref[:, gs])
        uv_ref[...] = _dot(h_ext, wup_ref[:, vs])
        gt = conv(ug_ref, gs)
        act = _bf(gt * jax.nn.sigmoid(gt) * conv(uv_ref, vs))
        part = _dot(act, wdn_ref[gs, :])
        acc = part if acc is None else acc + part
    out_ref[...] = x + acc


def _ffn_call(x, p, l, *, nb):
    T, D = x.shape
    tm = FFN_TM
    S = T // nb
    assert S % tm == 0 and tm % HALO == 0 and D_FF % FFN_FC == 0
    tpb = S // tm
    g = p["ffn_norm_g"][l].reshape(1, -1)
    wup, wdn = _bf(p["w_up"][l]), _bf(p["w_down"][l])
    cw, cb = p["conv_w"][l], p["conv_b"][l].reshape(1, -1)
    full = lambda a: pl.BlockSpec(a.shape, lambda i: (0,) * a.ndim)
    return pl.pallas_call(
        functools.partial(_ffn_kernel, tm=tm, tiles_per_batch=tpb),
        grid=(T // tm,),
        in_specs=[pl.BlockSpec((tm, D), lambda i: (i, 0)),
                  pl.BlockSpec((HALO, D), lambda i: (jnp.maximum(i * (tm // HALO) - 1, 0), 0)),
                  full(g), full(wup), full(cw), full(cb), full(wdn)],
        out_specs=pl.BlockSpec((tm, D), lambda i: (i, 0)),
        out_shape=jax.ShapeDtypeStruct((T, D), F32),
        scratch_shapes=[pltpu.VMEM((HALO + tm, FFN_FC), F32)] * 2,
        compiler_params=pltpu.CompilerParams(dimension_semantics=("arbitrary",), vmem_limit_bytes=VMEM_LIMIT_BYTES),
        name=f"ffn_l{l}",
    )(x, x, g, wup, cw, cb, wdn)


_PARAM_NAMES = (
    "rel_bias", "mix_norm_g", "w_in", "w_in_vres", "attn_q_norm", "attn_k_norm", "attn_sinks", "rwkv_mu",
    "rwkv_mu_vres", "rwkv_w0", "rwkv_w2", "rwkv_a0", "rwkv_a2", "rwkv_v0", "rwkv_v2", "rwkv_g2", "rwkv_k_k",
    "rwkv_k_a", "rwkv_r_k", "rwkv_ln_w", "rwkv_ln_b", "mem_norm_g", "w_mem_kv", "mem_q_norm", "mem_k_norm",
    "w_out", "ffn_norm_g", "w_up", "conv_w", "conv_b", "w_down")


def kernel(x, mem, rel_bias, mix_norm_g, w_in, w_in_vres, attn_q_norm, attn_k_norm, attn_sinks, rwkv_mu,
           rwkv_mu_vres, rwkv_w0, rwkv_w2, rwkv_a0, rwkv_a2, rwkv_v0, rwkv_v2, rwkv_g2, rwkv_k_k, rwkv_k_a,
           rwkv_r_k, rwkv_ln_w, rwkv_ln_b, mem_norm_g, w_mem_kv, mem_q_norm, mem_k_norm, w_out, ffn_norm_g,
           w_up, conv_w, conv_b, w_down):
    p = dict(zip(_PARAM_NAMES, (
        rel_bias, mix_norm_g, w_in, w_in_vres, attn_q_norm, attn_k_norm, attn_sinks, rwkv_mu, rwkv_mu_vres,
        rwkv_w0, rwkv_w2, rwkv_a0, rwkv_a2, rwkv_v0, rwkv_v2, rwkv_g2, rwkv_k_k, rwkv_k_a, rwkv_r_k, rwkv_ln_w,
        rwkv_ln_b, mem_norm_g, w_mem_kv, mem_q_norm, mem_k_norm, w_out, ffn_norm_g, w_up, conv_w, conv_b, w_down)))
    nb, S, D = x.shape
    xt = x.reshape(nb * S, D)
    mem2d = mem.reshape(nb * mem.shape[1], D)
    bias = _bias_call(rel_bias)
    v_first = None
    for l in range(w_in.shape[0]):
        qa, ka, va, qm, r, lw, k, v, a, b, gate = _proj_call(xt, p, l, v_first, nb=nb)
        if l == 0:
            v_first = v
        y = _rwkv_scan(r, lw, k, v, a, b, nb=nb)
        mk, mv = _memkv_call(mem2d, p, l)
        xt = _mix_call(xt, qa, ka, va, qm, mk, mv, y, r, k, v, gate, bias, p, l, nb=nb)
        xt = _ffn_call(xt, p, l, nb=nb)
    return xt.reshape(nb, S, D)
```

```python
import functools
import math

import jax
import jax.numpy as jnp
import numpy as np
from jax import lax
from jax.experimental import pallas as pl
from jax.experimental.pallas import tpu as pltpu

F32 = jnp.float32
BF16 = jnp.bfloat16

HEAD_DIM = 64
SCAN_CHUNK = 64
VMEM_LIMIT_BYTES = 56 * 1024 * 1024


def _dot(a, b):
    return jnp.dot(a, b, preferred_element_type=F32)


def _dot_nt(a, b):
    return lax.dot_general(a, b, (((1,), (1,)), ((), ())), preferred_element_type=F32)


def _dot_tn(a, b):
    return lax.dot_general(a, b, (((0,), (0,)), ((), ())), preferred_element_type=F32)


def _bf(x):
    return x.astype(BF16)


def _scan_kernel(r_ref, lw_ref, k_ref, v_ref, a_ref, b_ref, y_ref, h_ref, *, nb, nh):
    C, N = SCAN_CHUNK, HEAD_DIM

    @pl.when(pl.program_id(0) == 0)
    def _():
        h_ref[...] = jnp.zeros_like(h_ref)

    row = lax.broadcasted_iota(jnp.int32, (C, C), 0)
    col = lax.broadcasted_iota(jnp.int32, (C, C), 1)
    incl = row >= col
    strict = row > col
    tri = incl.astype(BF16)
    eye = (row == col).astype(F32)

    pre = []
    for b in range(nb):
        lw = lw_ref[b]
        l1 = _bf(lw)
        e1 = lw - l1.astype(F32)
        l2 = _bf(e1)
        l3 = _bf(e1 - l2.astype(F32))
        cum = _dot(tri, l1) + _dot(tri, l2) + _dot(tri, l3)
        cum_last = cum[C - 1:C, :]
        g_inv = jnp.exp(-cum)
        g_out = jnp.exp(cum_last - cum)
        kk = k_ref[b]
        bb = b_ref[b]
        pre.append(dict(
            a_t=a_ref[b] * jnp.exp(cum - lw), r_t=r_ref[b] * jnp.exp(cum), b_t=bb * g_inv, k_t=kk * g_inv,
            b_h=bb * g_out, k_h=kk * g_out, g_c=jnp.broadcast_to(jnp.exp(cum_last), cum.shape), v=v_ref[b]))

    heads = [(b, h) for b in range(nb) for h in range(nh)]
    hs = lambda name, b, h: pre[b][name][:, h * N:(h + 1) * N]
    a_b = [_bf(hs("a_t", b, h)) for b, h in heads]
    r_b = [_bf(hs("r_t", b, h)) for b, h in heads]
    v_b = [_bf(hs("v", b, h)) for b, h in heads]
    ar = [jnp.concatenate([x, y], axis=0) for x, y in zip(a_b, r_b)]
    pb = [_dot_nt(ar[i], _bf(hs("b_t", b, h))) for i, (b, h) in enumerate(heads)]
    pk = [_dot_nt(ar[i], _bf(hs("k_t", b, h))) for i, (b, h) in enumerate(heads)]
    l_ab = [jnp.where(strict, p[:C], 0.0) for p in pb]
    m_rb = [_bf(jnp.where(incl, p[C:], 0.0)) for p in pb]
    l_ak = [_bf(jnp.where(strict, p[:C], 0.0)) for p in pk]
    m_rk = [_bf(jnp.where(incl, p[C:], 0.0)) for p in pk]
    t_inv = [eye + l for l in l_ab]
    pw = l_ab
    for _ in range(int(math.log2(C)) - 1):
        pw = [_dot(_bf(p), _bf(p)) for p in pw]
        t_inv = [t + _dot(_bf(t), _bf(p)) for t, p in zip(t_inv, pw)]
    t_b = [_bf(t) for t in t_inv]
    w1 = [_bf(_dot(t, x)) for t, x in zip(t_b, a_b)]
    lv = [_bf(_dot(l, x)) for l, x in zip(l_ak, v_b)]
    w2 = [_dot(t, x) for t, x in zip(t_b, lv)]
    y_v = [_dot(m, x) for m, x in zip(m_rk, v_b)]
    kv = [_dot_tn(_bf(hs("k_h", b, h)), v_b[i]) for i, (b, h) in enumerate(heads)]
    g_col = [hs("g_c", b, h).T for b, h in heads]
    h0 = [h_ref[i] for i in range(len(heads))]
    h0_b = [_bf(x) for x in h0]
    u = [_dot(w, x) + z for w, x, z in zip(w1, h0_b, w2)]
    u_b = [_bf(x) for x in u]
    y = [_dot(r_b[i], h0_b[i]) + _dot(m_rb[i], u_b[i]) + y_v[i] for i in range(len(heads))]
    for i, (b, h) in enumerate(heads):
        h_ref[i] = g_col[i] * h0[i] + _dot_tn(_bf(hs("b_h", b, h)), u_b[i]) + kv[i]
        y_ref[b, :, h * N:(h + 1) * N] = y[i]


def _rwkv_scan(r, lw, k, v, a, b, *, nb):
    T, W = r.shape
    S = T // nb
    nh = W // HEAD_DIM
    C = SCAN_CHUNK
    assert S % C == 0
    spec = pl.BlockSpec((nb, C, W), lambda c: (0, c, 0))
    args = [t.reshape(nb, S, W) for t in (r, lw, k, v, a, b)]
    y = pl.pallas_call(
        functools.partial(_scan_kernel, nb=nb, nh=nh),
        grid=(S // C,),
        in_specs=[spec] * 6,
        out_specs=spec,
        out_shape=jax.ShapeDtypeStruct((nb, S, W), F32),
        scratch_shapes=[pltpu.VMEM((nb * nh, HEAD_DIM, HEAD_DIM), F32)],
        compiler_params=pltpu.CompilerParams(dimension_semantics=("arbitrary",), vmem_limit_bytes=VMEM_LIMIT_BYTES),
        name="rwkv_scan",
    )(*args)
    return y.reshape(T, W)


D_MODEL = 1024
ATTN_HEADS = 6
ATTN_KV_HEADS = 2
ATTN_GROUP = ATTN_HEADS // ATTN_KV_HEADS
ATTN_W = ATTN_HEADS * HEAD_DIM
KV_W = ATTN_KV_HEADS * HEAD_DIM
BLOCK_Q = 128
WINDOW = 128
N_BUCKETS = 32
MAX_EXACT = N_BUCKETS // 2
RWKV_W = 6 * HEAD_DIM
LORA_W = 128
SHIFT_W = 3 * RWKV_W + LORA_W
MEM_HEADS = 4
MEM_W = MEM_HEADS * HEAD_DIM
IN_BASE = ATTN_W + 2 * KV_W + SHIFT_W + MEM_W
PB_OFF = ATTN_W + 2 * KV_W
QM_OFF = PB_OFF + SHIFT_W
D_FF = 2816
EPS = 1e-6
GN_EPS = 64e-5
L2_EPS = 1e-12
HALO = 16

PROJ_TM = 512
MIX_TQ = 512
FFN_TM = 512
FFN_FC = 256


def _block_diag_ones(width):
    idx = np.arange(width) // HEAD_DIM
    return jnp.asarray((idx[:, None] == idx[None, :]).astype(np.float32), dtype=BF16)


def _head_sum(t, bd):
    return _dot(_bf(t), bd)


def _head_rms(t, bd, gain):
    ms = _head_sum(t * t, bd) * (1.0 / HEAD_DIM)
    return t * lax.rsqrt(ms + EPS) * gain


def _rms_rows(x, g):
    ms = jnp.mean(x * x, axis=-1, keepdims=True)
    return x * lax.rsqrt(ms + EPS) * g


def _proj_kernel(*refs, tm, tiles_per_batch, has_vres):
    it = iter(refs)
    x_ref, g_ref, w_ref = next(it), next(it), next(it)
    wv_ref = next(it) if has_vres else None
    qg_ref, kg_ref, mg_ref, mu_ref = next(it), next(it), next(it), next(it)
    w0_ref, a0_ref, kk_ref, ka_ref = next(it), next(it), next(it), next(it)
    w2_ref, a2_ref, g2_ref = next(it), next(it), next(it)
    bd384_ref, bd128_ref, bd256_ref = next(it), next(it), next(it)
    if has_vres:
        vfirst_ref, v0_ref, v2_ref = next(it), next(it), next(it)
    qa_ref, ka_out_ref, va_ref, qm_ref = next(it), next(it), next(it), next(it)
    r_ref, lw_ref, k_ref, v_ref, a_ref, b_ref, gate_ref = (next(it) for _ in range(7))
    pbs_ref = next(it)

    x = x_ref[...]
    hn = _bf(_rms_rows(x, g_ref[...]))
    proj = _dot(hn, w_ref[...])
    bd384 = bd384_ref[...]
    qa_ref[...] = _bf(_head_rms(proj[:, :ATTN_W], bd384, qg_ref[...]))
    ka_out_ref[...] = _bf(_head_rms(proj[:, ATTN_W:ATTN_W + KV_W], bd128_ref[...], kg_ref[...]))
    va_ref[...] = _bf(proj[:, ATTN_W + KV_W:PB_OFF])
    qm_ref[...] = _bf(_head_rms(proj[:, QM_OFF:QM_OFF + MEM_W], bd256_ref[...], mg_ref[...]))

    @pl.when(pl.program_id(0) % tiles_per_batch == 0)
    def _():
        pbs_ref[0:8, :] = jnp.zeros((8, pbs_ref.shape[1]), F32)

    pbs_ref[8:tm + 8, 0:SHIFT_W] = proj[:, PB_OFF:PB_OFF + SHIFT_W]
    if has_vres:
        pbs_ref[8:tm + 8, SHIFT_W:SHIFT_W + LORA_W] = _dot(hn, wv_ref[...])
    cur = pbs_ref[8:tm + 8, :]
    prev = pbs_ref[7:tm + 7, :]
    sh = cur + mu_ref[...] * (prev - cur)
    pbs_ref[0:8, :] = pbs_ref[tm:tm + 8, :]

    r = sh[:, 0:RWKV_W]
    k = sh[:, RWKV_W:2 * RWKV_W]
    v = sh[:, 2 * RWKV_W:3 * RWKV_W]
    z = sh[:, 3 * RWKV_W:SHIFT_W]
    t = w0_ref[...] + _dot(_bf(jnp.tanh(z)), w2_ref[...])
    lw_ref[...] = -math.exp(-0.5) * jax.nn.sigmoid(t)
    a = jax.nn.sigmoid(a0_ref[...] + _dot(_bf(z), a2_ref[...]))
    gate_ref[...] = _dot(_bf(jax.nn.sigmoid(z)), g2_ref[...])
    if has_vres:
        vd = sh[:, SHIFT_W:SHIFT_W + LORA_W]
        v = v + (vfirst_ref[...] - v) * jax.nn.sigmoid(v0_ref[...] + _dot(_bf(vd), v2_ref[...]))
    kk = k * kk_ref[...]
    kk = kk / jnp.maximum(jnp.sqrt(_head_sum(kk * kk, bd384)), L2_EPS)
    r_ref[...] = r
    k_ref[...] = k * (1.0 + (a - 1.0) * ka_ref[...])
    v_ref[...] = v
    a_ref[...] = -kk
    b_ref[...] = kk * a


def _pad_rows(w, rows, at):
    out = jnp.zeros((rows, w.shape[1]), w.dtype)
    return lax.dynamic_update_slice(out, w, (at, 0))


def _proj_call(x, p, l, v_first, *, nb):
    T, D = x.shape
    tm = PROJ_TM
    assert T % tm == 0 and (T // nb) % tm == 0
    has_vres = l > 0
    row = lambda a: a.reshape(1, -1).astype(F32)
    tile6 = lambda a: jnp.tile(a, ATTN_HEADS)
    scale = HEAD_DIM ** -0.5
    mu = p["rwkv_mu"][l]
    if has_vres:
        mu = jnp.concatenate([mu, p["rwkv_mu_vres"][l - 1], jnp.zeros((LORA_W - 16,), F32)])
    ins = [x, row(p["mix_norm_g"][l]), _bf(p["w_in"][l])]
    if has_vres:
        ins.append(_bf(jnp.pad(p["w_in_vres"][l - 1], ((0, 0), (0, LORA_W - 16)))))
    ins += [row(tile6(p["attn_q_norm"][l]) * scale), row(jnp.tile(p["attn_k_norm"][l], ATTN_KV_HEADS)),
            row(jnp.tile(p["mem_q_norm"][l], MEM_HEADS) * scale), row(mu),
            row(p["rwkv_w0"][l]), row(p["rwkv_a0"][l]), row(p["rwkv_k_k"][l]), row(p["rwkv_k_a"][l]),
            _bf(_pad_rows(p["rwkv_w2"][l], LORA_W, 0)), _bf(_pad_rows(p["rwkv_a2"][l], LORA_W, 32)),
            _bf(_pad_rows(p["rwkv_g2"][l], LORA_W, 64)),
            _block_diag_ones(ATTN_W), _block_diag_ones(KV_W), _block_diag_ones(MEM_W)]
    if has_vres:
        ins += [v_first, row(p["rwkv_v0"][l - 1]), _bf(_pad_rows(p["rwkv_v2"][l - 1], LORA_W, 0))]

    def spec(a):
        if a.shape[0] == T:
            return pl.BlockSpec((tm, a.shape[1]), lambda i: (i, 0))
        return pl.BlockSpec(a.shape, lambda i: (0, 0))

    out_shapes = [jax.ShapeDtypeStruct((T, ATTN_W), BF16), jax.ShapeDtypeStruct((T, KV_W), BF16),
                  jax.ShapeDtypeStruct((T, KV_W), BF16), jax.ShapeDtypeStruct((T, MEM_W), BF16)]
    out_shapes += [jax.ShapeDtypeStruct((T, RWKV_W), F32)] * 7
    sw = SHIFT_W + (LORA_W if has_vres else 0)
    return pl.pallas_call(
        functools.partial(_proj_kernel, tm=tm, tiles_per_batch=(T // nb) // tm, has_vres=has_vres),
        grid=(T // tm,),
        in_specs=[spec(a) for a in ins],
        out_specs=[pl.BlockSpec((tm, s.shape[1]), lambda i: (i, 0)) for s in out_shapes],
        out_shape=out_shapes,
        scratch_shapes=[pltpu.VMEM((tm + 8, sw), F32)],
        compiler_params=pltpu.CompilerParams(dimension_semantics=("arbitrary",), vmem_limit_bytes=VMEM_LIMIT_BYTES),
        name=f"proj_l{l}",
    )(*ins)


def _bucket_table():
    qi = np.arange(BLOCK_Q)[:, None]
    kj = np.arange(2 * BLOCK_Q)[None, :]
    dist = qi + BLOCK_Q - kj
    in_band = (dist >= 0) & (dist < WINDOW)
    d = np.maximum(dist, 1).astype(np.float32)
    large = MAX_EXACT + (np.log(d / np.float32(MAX_EXACT)) / np.float32(math.log(WINDOW / MAX_EXACT))
                         * np.float32(N_BUCKETS - MAX_EXACT)).astype(np.int32)
    large = np.minimum(large, N_BUCKETS - 1)
    bucket = np.where(dist < MAX_EXACT, np.maximum(dist, 0), large)
    return np.where(in_band, bucket, -1).astype(np.int32)


def _bias_kernel(rb_ref, bucket_ref, out_ref):
    bucket = bucket_ref[...]
    for h in range(ATTN_HEADS):
        acc = jnp.full(bucket.shape, -jnp.inf, F32)
        for j in range(N_BUCKETS):
            acc = jnp.where(bucket == j, rb_ref[j, h], acc)
        out_ref[h] = acc


def _bias_call(rel_bias):
    tab = pl.pallas_call(
        _bias_kernel,
        in_specs=[pl.BlockSpec(memory_space=pltpu.SMEM), pl.BlockSpec(memory_space=pltpu.VMEM)],
        out_specs=pl.BlockSpec(memory_space=pltpu.VMEM),
        out_shape=jax.ShapeDtypeStruct((ATTN_HEADS, BLOCK_Q, 2 * BLOCK_Q), F32),
        name="rel_bias_table",
    )(rel_bias.astype(F32), jnp.asarray(_bucket_table()))
    return tab.reshape(ATTN_KV_HEADS, ATTN_GROUP * BLOCK_Q, 2 * BLOCK_Q)


def _memkv_kernel(mem_ref, g_ref, w_ref, kg_ref, bd_ref, mk_ref, mv_ref):
    hn = _bf(_rms_rows(mem_ref[...], g_ref[...]))
    kv = _dot(hn, w_ref[...])
    mk_ref[...] = _bf(_head_rms(kv[:, :MEM_W], bd_ref[...], kg_ref[...]))
    mv_ref[...] = _bf(kv[:, MEM_W:])


def _memkv_call(mem2d, p, l):
    rows = mem2d.shape[0]
    vm = pl.BlockSpec(memory_space=pltpu.VMEM)
    return pl.pallas_call(
        _memkv_kernel,
        in_specs=[vm] * 5,
        out_specs=[vm, vm],
        out_shape=[jax.ShapeDtypeStruct((rows, MEM_W), BF16)] * 2,
        compiler_params=pltpu.CompilerParams(vmem_limit_bytes=VMEM_LIMIT_BYTES),
        name=f"mem_kv_l{l}",
    )(mem2d, p["mem_norm_g"][l].reshape(1, -1), _bf(p["w_mem_kv"][l]),
      jnp.tile(p["mem_k_norm"][l], MEM_HEADS).reshape(1, -1), _block_diag_ones(MEM_W))


def _mix_kernel(x_ref, qa_ref, kc_ref, kp_ref, vc_ref, vp_ref, bias_ref, sink_ref, qm_ref, mk_ref, mv_ref,
                y_ref, r_ref, k_ref, v_ref, gate_ref, lnw_ref, lnb_ref, rk_ref, bd_ref, wout_ref, out_ref,
                *, tq, tiles_per_batch):
    N, BQ = HEAD_DIM, BLOCK_Q
    seq_start = pl.program_id(0) % tiles_per_batch == 0
    nqb = tq // BQ
    qa = qa_ref[...]
    kall = jnp.concatenate([kp_ref[...], kc_ref[...]], axis=0)
    vall = jnp.concatenate([vp_ref[...], vc_ref[...]], axis=0)
    before_seq = lax.broadcasted_iota(jnp.int32, (ATTN_GROUP * BQ, 2 * BQ), 1) < BQ

    qm = qm_ref[...]
    mk = mk_ref[0]
    mv = mv_ref[0]

    swa = [(j, g) for j in range(nqb) for g in range(ATTN_KV_HEADS)]
    logits, values, sinks = [], [], []
    for j, g in swa:
        qg = jnp.concatenate(
            [qa[j * BQ:(j + 1) * BQ, (ATTN_GROUP * g + i) * N:(ATTN_GROUP * g + i + 1) * N] for i in range(ATTN_GROUP)],
            axis=0)
        lg = _dot_nt(qg, kall[j * BQ:(j + 2) * BQ, g * N:(g + 1) * N]) + bias_ref[g]
        if j == 0:
            lg = jnp.where(jnp.logical_and(seq_start, before_seq), -jnp.inf, lg)
        logits.append(lg)
        values.append(vall[j * BQ:(j + 2) * BQ, g * N:(g + 1) * N])
        sinks.append(sink_ref[g])
    for h in range(MEM_HEADS):
        hs = slice(h * N, (h + 1) * N)
        logits.append(_dot_nt(qm[:, hs], mk[:, hs]))
        values.append(mv[:, hs])
        sinks.append(None)
    row_max = [jnp.max(lg, axis=-1, keepdims=True) for lg in logits]
    m = [rm if s is None else jnp.maximum(rm, s) for rm, s in zip(row_max, sinks)]
    e = [jnp.exp(lg - mm) for lg, mm in zip(logits, m)]
    denom = [jnp.sum(ee, axis=-1, keepdims=True) for ee in e]
    denom = [d if s is None else d + jnp.exp(s - mm) for d, s, mm in zip(denom, sinks, m)]
    outs = [_dot(_bf(ee), vv) / d for ee, vv, d in zip(e, values, denom)]

    head_rows = [[None] * nqb for _ in range(ATTN_HEADS)]
    for (j, g), o in zip(swa, outs):
        for i in range(ATTN_GROUP):
            head_rows[ATTN_GROUP * g + i][j] = o[i * BQ:(i + 1) * BQ]
    out_a = jnp.concatenate([jnp.concatenate(rows, axis=0) for rows in head_rows], axis=-1)
    out_m = jnp.concatenate(outs[len(swa):], axis=-1)

    bd = bd_ref[...]
    y = y_ref[...]
    d = y - _head_sum(y, bd) * (1.0 / N)
    var = _head_sum(d * d, bd) * (1.0 / N)
    yn = d * lax.rsqrt(var + GN_EPS) * lnw_ref[...] + lnb_ref[...]
    bonus = _head_sum(r_ref[...] * k_ref[...] * rk_ref[...], bd) * v_ref[...]
    out_b = (yn + bonus) * gate_ref[...]

    mixed = _bf(jnp.concatenate([out_a, out_b, out_m], axis=-1))
    out_ref[...] = x_ref[...] + _dot(mixed, wout_ref[...])


def _mix_call(x, qa, ka, va, qm, mk, mv, y, r, k, v, gate, bias, p, l, *, nb):
    T, D = x.shape
    tq = MIX_TQ
    S = T // nb
    assert S % tq == 0 and tq % BLOCK_Q == 0
    tpb = S // tq
    qpb = tq // BLOCK_Q
    mem_tokens = mk.shape[0] // nb
    row = lambda a: a.reshape(1, -1).astype(F32)
    sink = jnp.repeat(p["attn_sinks"][l].astype(F32), BLOCK_Q).reshape(ATTN_KV_HEADS, ATTN_GROUP * BLOCK_Q, 1)
    tile = lambda w: pl.BlockSpec((tq, w), lambda i: (i, 0))
    prev = pl.BlockSpec((BLOCK_Q, KV_W), lambda i: (jnp.maximum(i * qpb - 1, 0), 0))
    full = lambda a: pl.BlockSpec(a.shape, lambda i: (0,) * a.ndim)
    memspec = pl.BlockSpec((1, mem_tokens, MEM_W), lambda i: (i // tpb, 0, 0))
    lnw, lnb, rk = row(p["rwkv_ln_w"][l]), row(p["rwkv_ln_b"][l]), row(p["rwkv_r_k"][l])
    bd = _block_diag_ones(RWKV_W)
    wout = _bf(p["w_out"][l])
    mk3 = mk.reshape(nb, mem_tokens, MEM_W)
    mv3 = mv.reshape(nb, mem_tokens, MEM_W)
    return pl.pallas_call(
        functools.partial(_mix_kernel, tq=tq, tiles_per_batch=tpb),
        grid=(T // tq,),
        in_specs=[tile(D), tile(ATTN_W), tile(KV_W), prev, tile(KV_W), prev, full(bias), full(sink), tile(MEM_W),
                  memspec, memspec] + [tile(RWKV_W)] * 5 + [full(lnw), full(lnb), full(rk), full(bd), full(wout)],
        out_specs=tile(D),
        out_shape=jax.ShapeDtypeStruct((T, D), F32),
        compiler_params=pltpu.CompilerParams(dimension_semantics=("arbitrary",), vmem_limit_bytes=VMEM_LIMIT_BYTES),
        name=f"mix_l{l}",
    )(x, qa, ka, ka, va, va, bias, sink, qm, mk3, mv3, y, r, k, v, gate, lnw, lnb, rk, bd, wout)


def _ffn_kernel(x_ref, xh_ref, g_ref, wup_ref, cw_ref, cb_ref, wdn_ref, out_ref, ug_ref, uv_ref,
                *, tm, tiles_per_batch):
    seq_start = pl.program_id(0) % tiles_per_batch == 0
    g = g_ref[...]
    x = x_ref[...]
    halo = jnp.where(seq_start, 0.0, _rms_rows(xh_ref[...], g))
    h_ext = _bf(jnp.concatenate([halo, _rms_rows(x, g)], axis=0))

    n_chunks = D_FF // FFN_FC
    gate_cols = lambda c: slice(c * FFN_FC, (c + 1) * FFN_FC)
    val_cols = lambda c: slice(D_FF + c * FFN_FC, D_FF + (c + 1) * FFN_FC)

    def up(c):
        ug_ref[c % 2] = _dot(h_ext, wup_ref[:, gate_cols(c)])
        uv_ref[c % 2] = _dot(h_ext, wup_ref[:, val_cols(c)])

    def conv(buf_ref, c, cols):
        w = cw_ref[:, cols]
        s = c % 2
        return (cb_ref[:, cols] + w[0:1] * buf_ref[s, HALO - 2:HALO - 2 + tm]
                + w[1:2] * buf_ref[s, HALO - 1:HALO - 1 + tm] + w[2:3] * buf_ref[s, HALO:HALO + tm])

    up(0)
    acc = None
    for c in range(n_chunks):
        if c + 1 < n_chunks:
            up(c + 1)
        gt = conv(ug_ref, c, gate_cols(c))
        act = _bf(gt * jax.nn.sigmoid(gt) * conv(uv_ref, c, val_cols(c)))
        part = _dot(act, wdn_ref[gate_cols(c), :])
        acc = part if acc is None else acc + part
    out_ref[...] = x + acc


def _ffn_call(x, p, l, *, nb):
    T, D = x.shape
    tm = FFN_TM
    S = T // nb
    assert S % tm == 0 and tm % HALO == 0 and D_FF % FFN_FC == 0
    tpb = S // tm
    g = p["ffn_norm_g"][l].reshape(1, -1)
    wup, wdn = _bf(p["w_up"][l]), _bf(p["w_down"][l])
    cw, cb = p["conv_w"][l], p["conv_b"][l].reshape(1, -1)
    full = lambda a: pl.BlockSpec(a.shape, lambda i: (0,) * a.ndim)
    return pl.pallas_call(
        functools.partial(_ffn_kernel, tm=tm, tiles_per_batch=tpb),
        grid=(T // tm,),
        in_specs=[pl.BlockSpec((tm, D), lambda i: (i, 0)),
                  pl.BlockSpec((HALO, D), lambda i: (jnp.maximum(i * (tm // HALO) - 1, 0), 0)),
                  full(g), full(wup), full(cw), full(cb), full(wdn)],
        out_specs=pl.BlockSpec((tm, D), lambda i: (i, 0)),
        out_shape=jax.ShapeDtypeStruct((T, D), F32),
        scratch_shapes=[pltpu.VMEM((2, HALO + tm, FFN_FC), F32)] * 2,
        compiler_params=pltpu.CompilerParams(dimension_semantics=("arbitrary",), vmem_limit_bytes=VMEM_LIMIT_BYTES),
        name=f"ffn_l{l}",
    )(x, x, g, wup, cw, cb, wdn)


_PARAM_NAMES = (
    "rel_bias", "mix_norm_g", "w_in", "w_in_vres", "attn_q_norm", "attn_k_norm", "attn_sinks", "rwkv_mu",
    "rwkv_mu_vres", "rwkv_w0", "rwkv_w2", "rwkv_a0", "rwkv_a2", "rwkv_v0", "rwkv_v2", "rwkv_g2", "rwkv_k_k",
    "rwkv_k_a", "rwkv_r_k", "rwkv_ln_w", "rwkv_ln_b", "mem_norm_g", "w_mem_kv", "mem_q_norm", "mem_k_norm",
    "w_out", "ffn_norm_g", "w_up", "conv_w", "conv_b", "w_down")


def kernel(x, mem, rel_bias, mix_norm_g, w_in, w_in_vres, attn_q_norm, attn_k_norm, attn_sinks, rwkv_mu,
           rwkv_mu_vres, rwkv_w0, rwkv_w2, rwkv_a0, rwkv_a2, rwkv_v0, rwkv_v2, rwkv_g2, rwkv_k_k, rwkv_k_a,
           rwkv_r_k, rwkv_ln_w, rwkv_ln_b, mem_norm_g, w_mem_kv, mem_q_norm, mem_k_norm, w_out, ffn_norm_g,
           w_up, conv_w, conv_b, w_down):
    p = dict(zip(_PARAM_NAMES, (
        rel_bias, mix_norm_g, w_in, w_in_vres, attn_q_norm, attn_k_norm, attn_sinks, rwkv_mu, rwkv_mu_vres,
        rwkv_w0, rwkv_w2, rwkv_a0, rwkv_a2, rwkv_v0, rwkv_v2, rwkv_g2, rwkv_k_k, rwkv_k_a, rwkv_r_k, rwkv_ln_w,
        rwkv_ln_b, mem_norm_g, w_mem_kv, mem_q_norm, mem_k_norm, w_out, ffn_norm_g, w_up, conv_w, conv_b, w_down)))
    nb, S, D = x.shape
    xt = x.reshape(nb * S, D)
    mem2d = mem.reshape(nb * mem.shape[1], D)
    bias = _bias_call(rel_bias)
    v_first = None
    for l in range(w_in.shape[0]):
        qa, ka, va, qm, r, lw, k, v, a, b, gate = _proj_call(xt, p, l, v_first, nb=nb)
        if l == 0:
            v_first = v
        y = _rwkv_scan(r, lw, k, v, a, b, nb=nb)
        mk, mv = _memkv_call(mem2d, p, l)
        xt = _mix_call(xt, qa, ka, va, qm, mk, mv, y, r, k, v, gate, bias, p, l, nb=nb)
        xt = _ffn_call(xt, p, l, nb=nb)
    return xt.reshape(nb, S, D)
```

```python
import functools
import math

import jax
import jax.numpy as jnp
import numpy as np
from jax import lax
from jax.experimental import pallas as pl
from jax.experimental.pallas import tpu as pltpu

F32 = jnp.float32
BF16 = jnp.bfloat16

HEAD_DIM = 64
SCAN_CHUNK = 64
VMEM_LIMIT_BYTES = 56 * 1024 * 1024


def _dot(a, b):
    return jnp.dot(a, b, preferred_element_type=F32)


def _dot_nt(a, b):
    return lax.dot_general(a, b, (((1,), (1,)), ((), ())), preferred_element_type=F32)


def _dot_tn(a, b):
    return lax.dot_general(a, b, (((0,), (0,)), ((), ())), preferred_element_type=F32)


def _bf(x):
    return x.astype(BF16)


def _scan_kernel(r_ref, lw_ref, k_ref, v_ref, a_ref, b_ref, y_ref, h_ref, *, nb, nh):
    C, N = SCAN_CHUNK, HEAD_DIM

    @pl.when(pl.program_id(0) == 0)
    def _():
        h_ref[...] = jnp.zeros_like(h_ref)

    row = lax.broadcasted_iota(jnp.int32, (C, C), 0)
    col = lax.broadcasted_iota(jnp.int32, (C, C), 1)
    incl = row >= col
    strict = row > col
    tri = incl.astype(BF16)
    eye = (row == col).astype(F32)

    pre = []
    for b in range(nb):
        lw = lw_ref[b]
        l1 = _bf(lw)
        e1 = lw - l1.astype(F32)
        l2 = _bf(e1)
        l3 = _bf(e1 - l2.astype(F32))
        cum = _dot(tri, l1) + _dot(tri, l2) + _dot(tri, l3)
        cum_last = cum[C - 1:C, :]
        g_inv = jnp.exp(-cum)
        g_out = jnp.exp(cum_last - cum)
        kk = k_ref[b]
        bb = b_ref[b]
        pre.append(dict(
            a_t=a_ref[b] * jnp.exp(cum - lw), r_t=r_ref[b] * jnp.exp(cum), b_t=bb * g_inv, k_t=kk * g_inv,
            b_h=bb * g_out, k_h=kk * g_out, g_c=jnp.broadcast_to(jnp.exp(cum_last), cum.shape), v=v_ref[b]))

    heads = [(b, h) for b in range(nb) for h in range(nh)]
    hs = lambda name, b, h: pre[b][name][:, h * N:(h + 1) * N]
    a_b = [_bf(hs("a_t", b, h)) for b, h in heads]
    r_b = [_bf(hs("r_t", b, h)) for b, h in heads]
    v_b = [_bf(hs("v", b, h)) for b, h in heads]
    ar = [jnp.concatenate([x, y], axis=0) for x, y in zip(a_b, r_b)]
    pb = [_dot_nt(ar[i], _bf(hs("b_t", b, h))) for i, (b, h) in enumerate(heads)]
    pk = [_dot_nt(ar[i], _bf(hs("k_t", b, h))) for i, (b, h) in enumerate(heads)]
    l_ab = [jnp.where(strict, p[:C], 0.0) for p in pb]
    m_rb = [_bf(jnp.where(incl, p[C:], 0.0)) for p in pb]
    l_ak = [_bf(jnp.where(strict, p[:C], 0.0)) for p in pk]
    m_rk = [_bf(jnp.where(incl, p[C:], 0.0)) for p in pk]
    t_inv = [eye + l for l in l_ab]
    pw = l_ab
    for _ in range(int(math.log2(C)) - 1):
        pw = [_dot(_bf(p), _bf(p)) for p in pw]
        t_inv = [t + _dot(_bf(t), _bf(p)) for t, p in zip(t_inv, pw)]
    t_b = [_bf(t) for t in t_inv]
    w1 = [_bf(_dot(t, x)) for t, x in zip(t_b, a_b)]
    lv = [_bf(_dot(l, x)) for l, x in zip(l_ak, v_b)]
    w2 = [_dot(t, x) for t, x in zip(t_b, lv)]
    y_v = [_dot(m, x) for m, x in zip(m_rk, v_b)]
    kv = [_dot_tn(_bf(hs("k_h", b, h)), v_b[i]) for i, (b, h) in enumerate(heads)]
    g_col = [hs("g_c", b, h).T for b, h in heads]
    h0 = [h_ref[i] for i in range(len(heads))]
    h0_b = [_bf(x) for x in h0]
    u = [_dot(w, x) + z for w, x, z in zip(w1, h0_b, w2)]
    u_b = [_bf(x) for x in u]
    y = [_dot(r_b[i], h0_b[i]) + _dot(m_rb[i], u_b[i]) + y_v[i] for i in range(len(heads))]
    for i, (b, h) in enumerate(heads):
        h_ref[i] = g_col[i] * h0[i] + _dot_tn(_bf(hs("b_h", b, h)), u_b[i]) + kv[i]
        y_ref[b, :, h * N:(h + 1) * N] = y[i]


def _rwkv_scan(r, lw, k, v, a, b, *, nb):
    T, W = r.shape
    S = T // nb
    nh = W // HEAD_DIM
    C = SCAN_CHUNK
    assert S % C == 0
    spec = pl.BlockSpec((nb, C, W), lambda c: (0, c, 0))
    args = [t.reshape(nb, S, W) for t in (r, lw, k, v, a, b)]
    y = pl.pallas_call(
        functools.partial(_scan_kernel, nb=nb, nh=nh),
        grid=(S // C,),
        in_specs=[spec] * 6,
        out_specs=spec,
        out_shape=jax.ShapeDtypeStruct((nb, S, W), F32),
        scratch_shapes=[pltpu.VMEM((nb * nh, HEAD_DIM, HEAD_DIM), F32)],
        compiler_params=pltpu.CompilerParams(dimension_semantics=("arbitrary",), vmem_limit_bytes=VMEM_LIMIT_BYTES),
        name="rwkv_scan",
    )(*args)
    return y.reshape(T, W)


D_MODEL = 1024
ATTN_HEADS = 6
ATTN_KV_HEADS = 2
ATTN_GROUP = ATTN_HEADS // ATTN_KV_HEADS
ATTN_W = ATTN_HEADS * HEAD_DIM
KV_W = ATTN_KV_HEADS * HEAD_DIM
BLOCK_Q = 128
WINDOW = 128
N_BUCKETS = 32
MAX_EXACT = N_BUCKETS // 2
RWKV_W = 6 * HEAD_DIM
LORA_W = 128
SHIFT_W = 3 * RWKV_W + LORA_W
MEM_HEADS = 4
MEM_W = MEM_HEADS * HEAD_DIM
IN_BASE = ATTN_W + 2 * KV_W + SHIFT_W + MEM_W
PB_OFF = ATTN_W + 2 * KV_W
QM_OFF = PB_OFF + SHIFT_W
D_FF = 2816
EPS = 1e-6
GN_EPS = 64e-5
L2_EPS = 1e-12
HALO = 16

PROJ_TM = 512
MIX_TQ = 512
FFN_TM = 512
FFN_FC = 256


def _block_diag_ones(width):
    idx = np.arange(width) // HEAD_DIM
    return jnp.asarray((idx[:, None] == idx[None, :]).astype(np.float32), dtype=BF16)


def _head_sum(t, bd):
    return _dot(_bf(t), bd)


def _head_rms(t, bd, gain):
    ms = _head_sum(t * t, bd) * (1.0 / HEAD_DIM)
    return t * lax.rsqrt(ms + EPS) * gain


def _rms_rows(x, g):
    ms = jnp.mean(x * x, axis=-1, keepdims=True)
    return x * lax.rsqrt(ms + EPS) * g


def _proj_kernel(*refs, tm, tiles_per_batch, has_vres):
    it = iter(refs)
    x_ref, g_ref, w_ref = next(it), next(it), next(it)
    wv_ref = next(it) if has_vres else None
    qg_ref, kg_ref, mg_ref, mu_ref = next(it), next(it), next(it), next(it)
    w0_ref, a0_ref, kk_ref, ka_ref = next(it), next(it), next(it), next(it)
    w2_ref, a2_ref, g2_ref = next(it), next(it), next(it)
    bd384_ref, bd128_ref, bd256_ref = next(it), next(it), next(it)
    if has_vres:
        vfirst_ref, v0_ref, v2_ref = next(it), next(it), next(it)
    qa_ref, ka_out_ref, va_ref, qm_ref = next(it), next(it), next(it), next(it)
    r_ref, lw_ref, k_ref, v_ref, a_ref, b_ref, gate_ref = (next(it) for _ in range(7))
    pbs_ref = next(it)

    x = x_ref[...]
    hn = _bf(_rms_rows(x, g_ref[...]))
    proj = _dot(hn, w_ref[...])
    bd384 = bd384_ref[...]
    qa_ref[...] = _bf(_head_rms(proj[:, :ATTN_W], bd384, qg_ref[...]))
    ka_out_ref[...] = _bf(_head_rms(proj[:, ATTN_W:ATTN_W + KV_W], bd128_ref[...], kg_ref[...]))
    va_ref[...] = _bf(proj[:, ATTN_W + KV_W:PB_OFF])
    qm_ref[...] = _bf(_head_rms(proj[:, QM_OFF:QM_OFF + MEM_W], bd256_ref[...], mg_ref[...]))

    @pl.when(pl.program_id(0) % tiles_per_batch == 0)
    def _():
        pbs_ref[0:8, :] = jnp.zeros((8, pbs_ref.shape[1]), F32)

    pbs_ref[8:tm + 8, 0:SHIFT_W] = proj[:, PB_OFF:PB_OFF + SHIFT_W]
    if has_vres:
        pbs_ref[8:tm + 8, SHIFT_W:SHIFT_W + LORA_W] = _dot(hn, wv_ref[...])
    cur = pbs_ref[8:tm + 8, :]
    prev = pbs_ref[7:tm + 7, :]
    sh = cur + mu_ref[...] * (prev - cur)
    pbs_ref[0:8, :] = pbs_ref[tm:tm + 8, :]

    r = sh[:, 0:RWKV_W]
    k = sh[:, RWKV_W:2 * RWKV_W]
    v = sh[:, 2 * RWKV_W:3 * RWKV_W]
    z = sh[:, 3 * RWKV_W:SHIFT_W]
    t = w0_ref[...] + _dot(_bf(jnp.tanh(z)), w2_ref[...])
    lw_ref[...] = -math.exp(-0.5) * jax.nn.sigmoid(t)
    a = jax.nn.sigmoid(a0_ref[...] + _dot(_bf(z), a2_ref[...]))
    gate_ref[...] = _dot(_bf(jax.nn.sigmoid(z)), g2_ref[...])
    if has_vres:
        vd = sh[:, SHIFT_W:SHIFT_W + LORA_W]
        v = v + (vfirst_ref[...] - v) * jax.nn.sigmoid(v0_ref[...] + _dot(_bf(vd), v2_ref[...]))
    kk = k * kk_ref[...]
    kk = kk / jnp.maximum(jnp.sqrt(_head_sum(kk * kk, bd384)), L2_EPS)
    r_ref[...] = r
    k_ref[...] = k * (1.0 + (a - 1.0) * ka_ref[...])
    v_ref[...] = v
    a_ref[...] = -kk
    b_ref[...] = kk * a


def _pad_rows(w, rows, at):
    out = jnp.zeros((rows, w.shape[1]), w.dtype)
    return lax.dynamic_update_slice(out, w, (at, 0))


def _proj_call(x, p, l, v_first, *, nb):
    T, D = x.shape
    tm = PROJ_TM
    assert T % tm == 0 and (T // nb) % tm == 0
    has_vres = l > 0
    row = lambda a: a.reshape(1, -1).astype(F32)
    tile6 = lambda a: jnp.tile(a, ATTN_HEADS)
    scale = HEAD_DIM ** -0.5
    mu = p["rwkv_mu"][l]
    if has_vres:
        mu = jnp.concatenate([mu, p["rwkv_mu_vres"][l - 1], jnp.zeros((LORA_W - 16,), F32)])
    ins = [x, row(p["mix_norm_g"][l]), _bf(p["w_in"][l])]
    if has_vres:
        ins.append(_bf(jnp.pad(p["w_in_vres"][l - 1], ((0, 0), (0, LORA_W - 16)))))
    ins += [row(tile6(p["attn_q_norm"][l]) * scale), row(jnp.tile(p["attn_k_norm"][l], ATTN_KV_HEADS)),
            row(jnp.tile(p["mem_q_norm"][l], MEM_HEADS) * scale), row(mu),
            row(p["rwkv_w0"][l]), row(p["rwkv_a0"][l]), row(p["rwkv_k_k"][l]), row(p["rwkv_k_a"][l]),
            _bf(_pad_rows(p["rwkv_w2"][l], LORA_W, 0)), _bf(_pad_rows(p["rwkv_a2"][l], LORA_W, 32)),
            _bf(_pad_rows(p["rwkv_g2"][l], LORA_W, 64)),
            _block_diag_ones(ATTN_W), _block_diag_ones(KV_W), _block_diag_ones(MEM_W)]
    if has_vres:
        ins += [v_first, row(p["rwkv_v0"][l - 1]), _bf(_pad_rows(p["rwkv_v2"][l - 1], LORA_W, 0))]

    def spec(a):
        if a.shape[0] == T:
            return pl.BlockSpec((tm, a.shape[1]), lambda i: (i, 0))
        return pl.BlockSpec(a.shape, lambda i: (0, 0))

    out_shapes = [jax.ShapeDtypeStruct((T, ATTN_W), BF16), jax.ShapeDtypeStruct((T, KV_W), BF16),
                  jax.ShapeDtypeStruct((T, KV_W), BF16), jax.ShapeDtypeStruct((T, MEM_W), BF16)]
    out_shapes += [jax.ShapeDtypeStruct((T, RWKV_W), F32)] * 7
    sw = SHIFT_W + (LORA_W if has_vres else 0)
    return pl.pallas_call(
        functools.partial(_proj_kernel, tm=tm, tiles_per_batch=(T // nb) // tm, has_vres=has_vres),
        grid=(T // tm,),
        in_specs=[spec(a) for a in ins],
        out_specs=[pl.BlockSpec((tm, s.shape[1]), lambda i: (i, 0)) for s in out_shapes],
        out_shape=out_shapes,
        scratch_shapes=[pltpu.VMEM((tm + 8, sw), F32)],
        compiler_params=pltpu.CompilerParams(dimension_semantics=("arbitrary",), vmem_limit_bytes=VMEM_LIMIT_BYTES),
        name=f"proj_l{l}",
    )(*ins)


def _bucket_table():
    qi = np.arange(BLOCK_Q)[:, None]
    kj = np.arange(2 * BLOCK_Q)[None, :]
    dist = qi + BLOCK_Q - kj
    in_band = (dist >= 0) & (dist < WINDOW)
    d = np.maximum(dist, 1).astype(np.float32)
    large = MAX_EXACT + (np.log(d / np.float32(MAX_EXACT)) / np.float32(math.log(WINDOW / MAX_EXACT))
                         * np.float32(N_BUCKETS - MAX_EXACT)).astype(np.int32)
    large = np.minimum(large, N_BUCKETS - 1)
    bucket = np.where(dist < MAX_EXACT, np.maximum(dist, 0), large)
    return np.where(in_band, bucket, -1).astype(np.int32)


def _bias_kernel(rb_ref, bucket_ref, out_ref):
    bucket = bucket_ref[...]
    for h in range(ATTN_HEADS):
        acc = jnp.full(bucket.shape, -jnp.inf, F32)
        for j in range(N_BUCKETS):
            acc = jnp.where(bucket == j, rb_ref[j, h], acc)
        out_ref[h] = acc


def _bias_call(rel_bias):
    tab = pl.pallas_call(
        _bias_kernel,
        in_specs=[pl.BlockSpec(memory_space=pltpu.SMEM), pl.BlockSpec(memory_space=pltpu.VMEM)],
        out_specs=pl.BlockSpec(memory_space=pltpu.VMEM),
        out_shape=jax.ShapeDtypeStruct((ATTN_HEADS, BLOCK_Q, 2 * BLOCK_Q), F32),
        name="rel_bias_table",
    )(rel_bias.astype(F32), jnp.asarray(_bucket_table()))
    return tab.reshape(ATTN_KV_HEADS, ATTN_GROUP * BLOCK_Q, 2 * BLOCK_Q)


def _memkv_kernel(mem_ref, g_ref, w_ref, kg_ref, bd_ref, mk_ref, mv_ref):
    hn = _bf(_rms_rows(mem_ref[...], g_ref[...]))
    kv = _dot(hn, w_ref[...])
    mk_ref[...] = _bf(_head_rms(kv[:, :MEM_W], bd_ref[...], kg_ref[...]))
    mv_ref[...] = _bf(kv[:, MEM_W:])


def _memkv_call(mem2d, p, l):
    rows = mem2d.shape[0]
    vm = pl.BlockSpec(memory_space=pltpu.VMEM)
    return pl.pallas_call(
        _memkv_kernel,
        in_specs=[vm] * 5,
        out_specs=[vm, vm],
        out_shape=[jax.ShapeDtypeStruct((rows, MEM_W), BF16)] * 2,
        compiler_params=pltpu.CompilerParams(vmem_limit_bytes=VMEM_LIMIT_BYTES),
        name=f"mem_kv_l{l}",
    )(mem2d, p["mem_norm_g"][l].reshape(1, -1), _bf(p["w_mem_kv"][l]),
      jnp.tile(p["mem_k_norm"][l], MEM_HEADS).reshape(1, -1), _block_diag_ones(MEM_W))


def _mix_kernel(x_ref, qa_ref, kc_ref, kp_ref, vc_ref, vp_ref, bias_ref, sink_ref, qm_ref, mk_ref, mv_ref,
                y_ref, r_ref, k_ref, v_ref, gate_ref, lnw_ref, lnb_ref, rk_ref, bd_ref, wout_ref, out_ref,
                *, tq, tiles_per_batch):
    N, BQ = HEAD_DIM, BLOCK_Q
    seq_start = pl.program_id(0) % tiles_per_batch == 0
    nqb = tq // BQ
    qa = qa_ref[...]
    kall = jnp.concatenate([kp_ref[...], kc_ref[...]], axis=0)
    vall = jnp.concatenate([vp_ref[...], vc_ref[...]], axis=0)
    before_seq = lax.broadcasted_iota(jnp.int32, (ATTN_GROUP * BQ, 2 * BQ), 1) < BQ

    qm = qm_ref[...]
    mk = mk_ref[0]
    mv = mv_ref[0]

    swa = [(j, g) for j in range(nqb) for g in range(ATTN_KV_HEADS)]
    logits, values, sinks = [], [], []
    for j, g in swa:
        qg = jnp.concatenate(
            [qa[j * BQ:(j + 1) * BQ, (ATTN_GROUP * g + i) * N:(ATTN_GROUP * g + i + 1) * N] for i in range(ATTN_GROUP)],
            axis=0)
        lg = _dot_nt(qg, kall[j * BQ:(j + 2) * BQ, g * N:(g + 1) * N]) + bias_ref[g]
        if j == 0:
            lg = jnp.where(jnp.logical_and(seq_start, before_seq), -jnp.inf, lg)
        logits.append(lg)
        values.append(vall[j * BQ:(j + 2) * BQ, g * N:(g + 1) * N])
        sinks.append(sink_ref[g])
    for h in range(MEM_HEADS):
        hs = slice(h * N, (h + 1) * N)
        logits.append(_dot_nt(qm[:, hs], mk[:, hs]))
        values.append(mv[:, hs])
        sinks.append(None)
    row_max = [jnp.max(lg, axis=-1, keepdims=True) for lg in logits]
    m = [rm if s is None else jnp.maximum(rm, s) for rm, s in zip(row_max, sinks)]
    e = [jnp.exp(lg - mm) for lg, mm in zip(logits, m)]
    denom = [jnp.sum(ee, axis=-1, keepdims=True) for ee in e]
    denom = [d if s is None else d + jnp.exp(s - mm) for d, s, mm in zip(denom, sinks, m)]
    outs = [_dot(_bf(ee), vv) / d for ee, vv, d in zip(e, values, denom)]

    head_rows = [[None] * nqb for _ in range(ATTN_HEADS)]
    for (j, g), o in zip(swa, outs):
        for i in range(ATTN_GROUP):
            head_rows[ATTN_GROUP * g + i][j] = o[i * BQ:(i + 1) * BQ]
    out_a = jnp.concatenate([jnp.concatenate(rows, axis=0) for rows in head_rows], axis=-1)
    out_m = jnp.concatenate(outs[len(swa):], axis=-1)

    bd = bd_ref[...]
    y = y_ref[...]
    d = y - _head_sum(y, bd) * (1.0 / N)
    var = _head_sum(d * d, bd) * (1.0 / N)
    yn = d * lax.rsqrt(var + GN_EPS) * lnw_ref[...] + lnb_ref[...]
    bonus = _head_sum(r_ref[...] * k_ref[...] * rk_ref[...], bd) * v_ref[...]
    out_b = (yn + bonus) * gate_ref[...]

    mixed = _bf(jnp.concatenate([out_a, out_b, out_m], axis=-1))
    out_ref[...] = x_ref[...] + _dot(mixed, wout_ref[...])


def _mix_call(x, qa, ka, va, qm, mk, mv, y, r, k, v, gate, bias, p, l, *, nb):
    T, D = x.shape
    tq = MIX_TQ
    S = T // nb
    assert S % tq == 0 and tq % BLOCK_Q == 0
    tpb = S // tq
    qpb = tq // BLOCK_Q
    mem_tokens = mk.shape[0] // nb
    row = lambda a: a.reshape(1, -1).astype(F32)
    sink = jnp.repeat(p["attn_sinks"][l].astype(F32), BLOCK_Q).reshape(ATTN_KV_HEADS, ATTN_GROUP * BLOCK_Q, 1)
    tile = lambda w: pl.BlockSpec((tq, w), lambda i: (i, 0))
    prev = pl.BlockSpec((BLOCK_Q, KV_W), lambda i: (jnp.maximum(i * qpb - 1, 0), 0))
    full = lambda a: pl.BlockSpec(a.shape, lambda i: (0,) * a.ndim)
    memspec = pl.BlockSpec((1, mem_tokens, MEM_W), lambda i: (i // tpb, 0, 0))
    lnw, lnb, rk = row(p["rwkv_ln_w"][l]), row(p["rwkv_ln_b"][l]), row(p["rwkv_r_k"][l])
    bd = _block_diag_ones(RWKV_W)
    wout = _bf(p["w_out"][l])
    mk3 = mk.reshape(nb, mem_tokens, MEM_W)
    mv3 = mv.reshape(nb, mem_tokens, MEM_W)
    return pl.pallas_call(
        functools.partial(_mix_kernel, tq=tq, tiles_per_batch=tpb),
        grid=(T // tq,),
        in_specs=[tile(D), tile(ATTN_W), tile(KV_W), prev, tile(KV_W), prev, full(bias), full(sink), tile(MEM_W),
                  memspec, memspec] + [tile(RWKV_W)] * 5 + [full(lnw), full(lnb), full(rk), full(bd), full(wout)],
        out_specs=tile(D),
        out_shape=jax.ShapeDtypeStruct((T, D), F32),
        compiler_params=pltpu.CompilerParams(dimension_semantics=("arbitrary",), vmem_limit_bytes=VMEM_LIMIT_BYTES),
        name=f"mix_l{l}",
    )(x, qa, ka, ka, va, va, bias, sink, qm, mk3, mv3, y, r, k, v, gate, lnw, lnb, rk, bd, wout)


SUBLANES = 8


def _slabs_to_rows(xs, width):
    return jnp.concatenate([xs[:, j * width:(j + 1) * width] for j in range(xs.shape[1] // width)], axis=0)


def _shift_rows(u, before, steps):
    first = lax.broadcasted_iota(jnp.int32, (SUBLANES, u.shape[1]), 0) == 0
    tm = u.shape[0]
    heads = []
    for i in range(steps):
        lo = tm - (steps - i) * SUBLANES
        wrapped = pltpu.roll(u[lo:lo + SUBLANES], 1, axis=0)
        prior = jnp.broadcast_to(before[SUBLANES - steps + i:SUBLANES - steps + i + 1], wrapped.shape)
        heads.append(jnp.where(first, prior, wrapped))
    return jnp.concatenate(heads + [u[:tm - steps * SUBLANES]], axis=0)


def _ffn_kernel(x_ref, xh_ref, g_ref, wup_ref, cw_ref, cb_ref, wdn_ref, out_ref, act_ref, *, tm, tiles_per_batch):
    D = g_ref.shape[1]
    seq_start = pl.program_id(0) % tiles_per_batch == 0
    g = g_ref[...]
    x = _slabs_to_rows(x_ref[...], D)
    xh = xh_ref[...]
    before = jnp.concatenate([xh[SUBLANES - 1:, :D], xh[SUBLANES - 1:, D:]], axis=0)
    before = jnp.where(seq_start, 0.0, _rms_rows(before, g))
    halo = jnp.concatenate([jnp.zeros((SUBLANES - 2, D), F32), before], axis=0)
    h_ext = _bf(jnp.concatenate([halo, _rms_rows(x, g)], axis=0))

    n_chunks = D_FF // FFN_FC
    gate_cols = lambda c: slice(c * FFN_FC, (c + 1) * FFN_FC)
    val_cols = lambda c: slice(D_FF + c * FFN_FC, D_FF + (c + 1) * FFN_FC)

    def conv(u_ext, cols):
        w = cw_ref[:, cols]
        u = u_ext[SUBLANES:]
        return (cb_ref[:, cols] + w[0:1] * _shift_rows(u, u_ext[:SUBLANES], 2)
                + w[1:2] * _shift_rows(u, u_ext[:SUBLANES], 1) + w[2:3] * u)

    up = lambda c: (_dot(h_ext, wup_ref[:, gate_cols(c)]), _dot(h_ext, wup_ref[:, val_cols(c)]))
    nxt = up(0)
    for c in range(n_chunks):
        ug, uv = nxt
        if c + 1 < n_chunks:
            nxt = up(c + 1)
        gt = conv(ug, gate_cols(c))
        act_ref[:, gate_cols(c)] = _bf(gt * jax.nn.sigmoid(gt) * conv(uv, val_cols(c)))
    res = x + _dot(act_ref[...], wdn_ref[...])
    for j in range(tm // SUBLANES):
        out_ref[:, j * D:(j + 1) * D] = res[j * SUBLANES:(j + 1) * SUBLANES]


def _ffn_call(x, p, l, *, nb):
    T, D = x.shape
    tm = FFN_TM
    S = T // nb
    q = tm // SUBLANES
    assert S % tm == 0 and tm % SUBLANES == 0 and D_FF % FFN_FC == 0
    tpb = S // tm
    g = p["ffn_norm_g"][l].reshape(1, -1)
    wup, wdn = _bf(p["w_up"][l]), _bf(p["w_down"][l])
    cw, cb = p["conv_w"][l], p["conv_b"][l].reshape(1, -1)
    full = lambda a: pl.BlockSpec(a.shape, lambda i: (0,) * a.ndim)
    xv = x.reshape(T // q, q * D)
    out = pl.pallas_call(
        functools.partial(_ffn_kernel, tm=tm, tiles_per_batch=tpb),
        grid=(T // tm,),
        in_specs=[pl.BlockSpec((SUBLANES, q * D), lambda i: (i, 0)),
                  pl.BlockSpec((SUBLANES, 2 * D), lambda i: (jnp.maximum(i - 1, 0), q // 2 - 1)),
                  full(g), full(wup), full(cw), full(cb), full(wdn)],
        out_specs=pl.BlockSpec((SUBLANES, q * D), lambda i: (i, 0)),
        out_shape=jax.ShapeDtypeStruct((T // q, q * D), F32),
        scratch_shapes=[pltpu.VMEM((tm, D_FF), BF16)],
        compiler_params=pltpu.CompilerParams(dimension_semantics=("arbitrary",), vmem_limit_bytes=VMEM_LIMIT_BYTES),
        name=f"ffn_l{l}",
    )(xv, xv, g, wup, cw, cb, wdn)
    return out.reshape(T, D)


_PARAM_NAMES = (
    "rel_bias", "mix_norm_g", "w_in", "w_in_vres", "attn_q_norm", "attn_k_norm", "attn_sinks", "rwkv_mu",
    "rwkv_mu_vres", "rwkv_w0", "rwkv_w2", "rwkv_a0", "rwkv_a2", "rwkv_v0", "rwkv_v2", "rwkv_g2", "rwkv_k_k",
    "rwkv_k_a", "rwkv_r_k", "rwkv_ln_w", "rwkv_ln_b", "mem_norm_g", "w_mem_kv", "mem_q_norm", "mem_k_norm",
    "w_out", "ffn_norm_g", "w_up", "conv_w", "conv_b", "w_down")


def kernel(x, mem, rel_bias, mix_norm_g, w_in, w_in_vres, attn_q_norm, attn_k_norm, attn_sinks, rwkv_mu,
           rwkv_mu_vres, rwkv_w0, rwkv_w2, rwkv_a0, rwkv_a2, rwkv_v0, rwkv_v2, rwkv_g2, rwkv_k_k, rwkv_k_a,
           rwkv_r_k, rwkv_ln_w, rwkv_ln_b, mem_norm_g, w_mem_kv, mem_q_norm, mem_k_norm, w_out, ffn_norm_g,
           w_up, conv_w, conv_b, w_down):
    p = dict(zip(_PARAM_NAMES, (
        rel_bias, mix_norm_g, w_in, w_in_vres, attn_q_norm, attn_k_norm, attn_sinks, rwkv_mu, rwkv_mu_vres,
        rwkv_w0, rwkv_w2, rwkv_a0, rwkv_a2, rwkv_v0, rwkv_v2, rwkv_g2, rwkv_k_k, rwkv_k_a, rwkv_r_k, rwkv_ln_w,
        rwkv_ln_b, mem_norm_g, w_mem_kv, mem_q_norm, mem_k_norm, w_out, ffn_norm_g, w_up, conv_w, conv_b, w_down)))
    nb, S, D = x.shape
    xt = x.reshape(nb * S, D)
    mem2d = mem.reshape(nb * mem.shape[1], D)
    bias = _bias_call(rel_bias)
    v_first = None
    for l in range(w_in.shape[0]):
        qa, ka, va, qm, r, lw, k, v, a, b, gate = _proj_call(xt, p, l, v_first, nb=nb)
        if l == 0:
            v_first = v
        y = _rwkv_scan(r, lw, k, v, a, b, nb=nb)
        mk, mv = _memkv_call(mem2d, p, l)
        xt = _mix_call(xt, qa, ka, va, qm, mk, mv, y, r, k, v, gate, bias, p, l, nb=nb)
        xt = _ffn_call(xt, p, l, nb=nb)
    return xt.reshape(nb, S, D)
```

```python
import functools
import math

import jax
import jax.numpy as jnp
import numpy as np
from jax import lax
from jax.experimental import pallas as pl
from jax.experimental.pallas import tpu as pltpu

F32 = jnp.float32
BF16 = jnp.bfloat16

HEAD_DIM = 64
SCAN_CHUNK = 64
VMEM_LIMIT_BYTES = 56 * 1024 * 1024


def _dot(a, b):
    return jnp.dot(a, b, preferred_element_type=F32)


def _dot_nt(a, b):
    return lax.dot_general(a, b, (((1,), (1,)), ((), ())), preferred_element_type=F32)


def _dot_tn(a, b):
    return lax.dot_general(a, b, (((0,), (0,)), ((), ())), preferred_element_type=F32)


def _bf(x):
    return x.astype(BF16)


def _scan_kernel(r_ref, lw_ref, k_ref, v_ref, a_ref, b_ref, y_ref, h_ref, *, nb, nh):
    C, N = SCAN_CHUNK, HEAD_DIM

    @pl.when(pl.program_id(0) == 0)
    def _():
        h_ref[...] = jnp.zeros_like(h_ref)

    row = lax.broadcasted_iota(jnp.int32, (C, C), 0)
    col = lax.broadcasted_iota(jnp.int32, (C, C), 1)
    incl = row >= col
    strict = row > col
    tri = incl.astype(BF16)
    eye = (row == col).astype(F32)

    pre = []
    for b in range(nb):
        lw = lw_ref[b]
        l1 = _bf(lw)
        e1 = lw - l1.astype(F32)
        l2 = _bf(e1)
        l3 = _bf(e1 - l2.astype(F32))
        cum = _dot(tri, l1) + _dot(tri, l2) + _dot(tri, l3)
        cum_last = cum[C - 1:C, :]
        g_inv = jnp.exp(-cum)
        g_out = jnp.exp(cum_last - cum)
        kk = k_ref[b]
        bb = b_ref[b]
        pre.append(dict(
            a_t=a_ref[b] * jnp.exp(cum - lw), r_t=r_ref[b] * jnp.exp(cum), b_t=bb * g_inv, k_t=kk * g_inv,
            b_h=bb * g_out, k_h=kk * g_out, g_c=jnp.broadcast_to(jnp.exp(cum_last), cum.shape), v=v_ref[b]))

    heads = [(b, h) for b in range(nb) for h in range(nh)]
    hs = lambda name, b, h: pre[b][name][:, h * N:(h + 1) * N]
    a_b = [_bf(hs("a_t", b, h)) for b, h in heads]
    r_b = [_bf(hs("r_t", b, h)) for b, h in heads]
    v_b = [_bf(hs("v", b, h)) for b, h in heads]
    ar = [jnp.concatenate([x, y], axis=0) for x, y in zip(a_b, r_b)]
    pb = [_dot_nt(ar[i], _bf(hs("b_t", b, h))) for i, (b, h) in enumerate(heads)]
    pk = [_dot_nt(ar[i], _bf(hs("k_t", b, h))) for i, (b, h) in enumerate(heads)]
    l_ab = [jnp.where(strict, p[:C], 0.0) for p in pb]
    m_rb = [_bf(jnp.where(incl, p[C:], 0.0)) for p in pb]
    l_ak = [_bf(jnp.where(strict, p[:C], 0.0)) for p in pk]
    m_rk = [_bf(jnp.where(incl, p[C:], 0.0)) for p in pk]
    t_inv = [eye + l for l in l_ab]
    pw = l_ab
    for _ in range(int(math.log2(C)) - 1):
        pw = [_dot(_bf(p), _bf(p)) for p in pw]
        t_inv = [t + _dot(_bf(t), _bf(p)) for t, p in zip(t_inv, pw)]
    t_b = [_bf(t) for t in t_inv]
    w1 = [_bf(_dot(t, x)) for t, x in zip(t_b, a_b)]
    lv = [_bf(_dot(l, x)) for l, x in zip(l_ak, v_b)]
    w2 = [_dot(t, x) for t, x in zip(t_b, lv)]
    y_v = [_dot(m, x) for m, x in zip(m_rk, v_b)]
    kv = [_dot_tn(_bf(hs("k_h", b, h)), v_b[i]) for i, (b, h) in enumerate(heads)]
    g_col = [hs("g_c", b, h).T for b, h in heads]
    h0 = [h_ref[i] for i in range(len(heads))]
    h0_b = [_bf(x) for x in h0]
    u = [_dot(w, x) + z for w, x, z in zip(w1, h0_b, w2)]
    u_b = [_bf(x) for x in u]
    y = [_dot(r_b[i], h0_b[i]) + _dot(m_rb[i], u_b[i]) + y_v[i] for i in range(len(heads))]
    for i, (b, h) in enumerate(heads):
        h_ref[i] = g_col[i] * h0[i] + _dot_tn(_bf(hs("b_h", b, h)), u_b[i]) + kv[i]
        y_ref[b, :, h * N:(h + 1) * N] = y[i]


def _rwkv_scan(r, lw, k, v, a, b, *, nb):
    T, W = r.shape
    S = T // nb
    nh = W // HEAD_DIM
    C = SCAN_CHUNK
    assert S % C == 0
    spec = pl.BlockSpec((nb, C, W), lambda c: (0, c, 0))
    args = [t.reshape(nb, S, W) for t in (r, lw, k, v, a, b)]
    y = pl.pallas_call(
        functools.partial(_scan_kernel, nb=nb, nh=nh),
        grid=(S // C,),
        in_specs=[spec] * 6,
        out_specs=spec,
        out_shape=jax.ShapeDtypeStruct((nb, S, W), F32),
        scratch_shapes=[pltpu.VMEM((nb * nh, HEAD_DIM, HEAD_DIM), F32)],
        compiler_params=pltpu.CompilerParams(dimension_semantics=("arbitrary",), vmem_limit_bytes=VMEM_LIMIT_BYTES),
        name="rwkv_scan",
    )(*args)
    return y.reshape(T, W)


D_MODEL = 1024
ATTN_HEADS = 6
ATTN_KV_HEADS = 2
ATTN_GROUP = ATTN_HEADS // ATTN_KV_HEADS
ATTN_W = ATTN_HEADS * HEAD_DIM
KV_W = ATTN_KV_HEADS * HEAD_DIM
BLOCK_Q = 128
WINDOW = 128
N_BUCKETS = 32
MAX_EXACT = N_BUCKETS // 2
RWKV_W = 6 * HEAD_DIM
LORA_W = 128
SHIFT_W = 3 * RWKV_W + LORA_W
MEM_HEADS = 4
MEM_W = MEM_HEADS * HEAD_DIM
IN_BASE = ATTN_W + 2 * KV_W + SHIFT_W + MEM_W
PB_OFF = ATTN_W + 2 * KV_W
QM_OFF = PB_OFF + SHIFT_W
D_FF = 2816
EPS = 1e-6
GN_EPS = 64e-5
L2_EPS = 1e-12
HALO = 16

PROJ_TM = 512
MIX_TQ = 512
FFN_TM = 512
FFN_FC = 256


def _block_diag_ones(width):
    idx = np.arange(width) // HEAD_DIM
    return jnp.asarray((idx[:, None] == idx[None, :]).astype(np.float32), dtype=BF16)


def _head_sum(t, bd):
    return _dot(_bf(t), bd)


def _head_rms(t, bd, gain):
    ms = _head_sum(t * t, bd) * (1.0 / HEAD_DIM)
    return t * lax.rsqrt(ms + EPS) * gain


def _rms_rows(x, g):
    ms = jnp.mean(x * x, axis=-1, keepdims=True)
    return x * lax.rsqrt(ms + EPS) * g


def _proj_kernel(*refs, tm, tiles_per_batch, has_vres):
    it = iter(refs)
    x_ref, g_ref, w_ref = next(it), next(it), next(it)
    wv_ref = next(it) if has_vres else None
    qg_ref, kg_ref, mg_ref, mu_ref = next(it), next(it), next(it), next(it)
    w0_ref, a0_ref, kk_ref, ka_ref = next(it), next(it), next(it), next(it)
    w2_ref, a2_ref, g2_ref = next(it), next(it), next(it)
    bd384_ref, bd128_ref, bd256_ref = next(it), next(it), next(it)
    if has_vres:
        vfirst_ref, v0_ref, v2_ref = next(it), next(it), next(it)
    qa_ref, ka_out_ref, va_ref, qm_ref = next(it), next(it), next(it), next(it)
    r_ref, lw_ref, k_ref, v_ref, a_ref, b_ref, gate_ref = (next(it) for _ in range(7))
    pbs_ref = next(it)

    x = x_ref[...]
    hn = _bf(_rms_rows(x, g_ref[...]))
    proj = _dot(hn, w_ref[...])
    bd384 = bd384_ref[...]
    qa_ref[...] = _bf(_head_rms(proj[:, :ATTN_W], bd384, qg_ref[...]))
    ka_out_ref[...] = _bf(_head_rms(proj[:, ATTN_W:ATTN_W + KV_W], bd128_ref[...], kg_ref[...]))
    va_ref[...] = _bf(proj[:, ATTN_W + KV_W:PB_OFF])
    qm_ref[...] = _bf(_head_rms(proj[:, QM_OFF:QM_OFF + MEM_W], bd256_ref[...], mg_ref[...]))

    @pl.when(pl.program_id(0) % tiles_per_batch == 0)
    def _():
        pbs_ref[0:8, :] = jnp.zeros((8, pbs_ref.shape[1]), F32)

    pbs_ref[8:tm + 8, 0:SHIFT_W] = proj[:, PB_OFF:PB_OFF + SHIFT_W]
    if has_vres:
        pbs_ref[8:tm + 8, SHIFT_W:SHIFT_W + LORA_W] = _dot(hn, wv_ref[...])
    cur = pbs_ref[8:tm + 8, :]
    prev = pbs_ref[7:tm + 7, :]
    sh = cur + mu_ref[...] * (prev - cur)
    pbs_ref[0:8, :] = pbs_ref[tm:tm + 8, :]

    r = sh[:, 0:RWKV_W]
    k = sh[:, RWKV_W:2 * RWKV_W]
    v = sh[:, 2 * RWKV_W:3 * RWKV_W]
    z = sh[:, 3 * RWKV_W:SHIFT_W]
    t = w0_ref[...] + _dot(_bf(jnp.tanh(z)), w2_ref[...])
    lw_ref[...] = -math.exp(-0.5) * jax.nn.sigmoid(t)
    a = jax.nn.sigmoid(a0_ref[...] + _dot(_bf(z), a2_ref[...]))
    gate_ref[...] = _dot(_bf(jax.nn.sigmoid(z)), g2_ref[...])
    if has_vres:
        vd = sh[:, SHIFT_W:SHIFT_W + LORA_W]
        v = v + (vfirst_ref[...] - v) * jax.nn.sigmoid(v0_ref[...] + _dot(_bf(vd), v2_ref[...]))
    kk = k * kk_ref[...]
    kk = kk / jnp.maximum(jnp.sqrt(_head_sum(kk * kk, bd384)), L2_EPS)
    r_ref[...] = r
    k_ref[...] = k * (1.0 + (a - 1.0) * ka_ref[...])
    v_ref[...] = v
    a_ref[...] = -kk
    b_ref[...] = kk * a


def _pad_rows(w, rows, at):
    out = jnp.zeros((rows, w.shape[1]), w.dtype)
    return lax.dynamic_update_slice(out, w, (at, 0))


def _proj_call(x, p, l, v_first, *, nb):
    T, D = x.shape
    tm = PROJ_TM
    assert T % tm == 0 and (T // nb) % tm == 0
    has_vres = l > 0
    row = lambda a: a.reshape(1, -1).astype(F32)
    tile6 = lambda a: jnp.tile(a, ATTN_HEADS)
    scale = HEAD_DIM ** -0.5
    mu = p["rwkv_mu"][l]
    if has_vres:
        mu = jnp.concatenate([mu, p["rwkv_mu_vres"][l - 1], jnp.zeros((LORA_W - 16,), F32)])
    ins = [x, row(p["mix_norm_g"][l]), _bf(p["w_in"][l])]
    if has_vres:
        ins.append(_bf(jnp.pad(p["w_in_vres"][l - 1], ((0, 0), (0, LORA_W - 16)))))
    ins += [row(tile6(p["attn_q_norm"][l]) * scale), row(jnp.tile(p["attn_k_norm"][l], ATTN_KV_HEADS)),
            row(jnp.tile(p["mem_q_norm"][l], MEM_HEADS) * scale), row(mu),
            row(p["rwkv_w0"][l]), row(p["rwkv_a0"][l]), row(p["rwkv_k_k"][l]), row(p["rwkv_k_a"][l]),
            _bf(_pad_rows(p["rwkv_w2"][l], LORA_W, 0)), _bf(_pad_rows(p["rwkv_a2"][l], LORA_W, 32)),
            _bf(_pad_rows(p["rwkv_g2"][l], LORA_W, 64)),
            _block_diag_ones(ATTN_W), _block_diag_ones(KV_W), _block_diag_ones(MEM_W)]
    if has_vres:
        ins += [v_first, row(p["rwkv_v0"][l - 1]), _bf(_pad_rows(p["rwkv_v2"][l - 1], LORA_W, 0))]

    def spec(a):
        if a.shape[0] == T:
            return pl.BlockSpec((tm, a.shape[1]), lambda i: (i, 0))
        return pl.BlockSpec(a.shape, lambda i: (0, 0))

    out_shapes = [jax.ShapeDtypeStruct((T, ATTN_W), BF16), jax.ShapeDtypeStruct((T, KV_W), BF16),
                  jax.ShapeDtypeStruct((T, KV_W), BF16), jax.ShapeDtypeStruct((T, MEM_W), BF16)]
    out_shapes += [jax.ShapeDtypeStruct((T, RWKV_W), F32)] * 7
    sw = SHIFT_W + (LORA_W if has_vres else 0)
    return pl.pallas_call(
        functools.partial(_proj_kernel, tm=tm, tiles_per_batch=(T // nb) // tm, has_vres=has_vres),
        grid=(T // tm,),
        in_specs=[spec(a) for a in ins],
        out_specs=[pl.BlockSpec((tm, s.shape[1]), lambda i: (i, 0)) for s in out_shapes],
        out_shape=out_shapes,
        scratch_shapes=[pltpu.VMEM((tm + 8, sw), F32)],
        compiler_params=pltpu.CompilerParams(dimension_semantics=("arbitrary",), vmem_limit_bytes=VMEM_LIMIT_BYTES),
        name=f"proj_l{l}",
    )(*ins)


def _bucket_table():
    qi = np.arange(BLOCK_Q)[:, None]
    kj = np.arange(2 * BLOCK_Q)[None, :]
    dist = qi + BLOCK_Q - kj
    in_band = (dist >= 0) & (dist < WINDOW)
    d = np.maximum(dist, 1).astype(np.float32)
    large = MAX_EXACT + (np.log(d / np.float32(MAX_EXACT)) / np.float32(math.log(WINDOW / MAX_EXACT))
                         * np.float32(N_BUCKETS - MAX_EXACT)).astype(np.int32)
    large = np.minimum(large, N_BUCKETS - 1)
    bucket = np.where(dist < MAX_EXACT, np.maximum(dist, 0), large)
    return np.where(in_band, bucket, -1).astype(np.int32)


def _bias_kernel(rb_ref, bucket_ref, out_ref):
    bucket = bucket_ref[...]
    for h in range(ATTN_HEADS):
        acc = jnp.full(bucket.shape, -jnp.inf, F32)
        for j in range(N_BUCKETS):
            acc = jnp.where(bucket == j, rb_ref[j, h], acc)
        out_ref[h] = acc


def _bias_call(rel_bias):
    tab = pl.pallas_call(
        _bias_kernel,
        in_specs=[pl.BlockSpec(memory_space=pltpu.SMEM), pl.BlockSpec(memory_space=pltpu.VMEM)],
        out_specs=pl.BlockSpec(memory_space=pltpu.VMEM),
        out_shape=jax.ShapeDtypeStruct((ATTN_HEADS, BLOCK_Q, 2 * BLOCK_Q), F32),
        name="rel_bias_table",
    )(rel_bias.astype(F32), jnp.asarray(_bucket_table()))
    return tab.reshape(ATTN_KV_HEADS, ATTN_GROUP * BLOCK_Q, 2 * BLOCK_Q)


def _memkv_kernel(mem_ref, g_ref, w_ref, kg_ref, bd_ref, mk_ref, mv_ref):
    hn = _bf(_rms_rows(mem_ref[...], g_ref[...]))
    kv = _dot(hn, w_ref[...])
    mk_ref[...] = _bf(_head_rms(kv[:, :MEM_W], bd_ref[...], kg_ref[...]))
    mv_ref[...] = _bf(kv[:, MEM_W:])


def _memkv_call(mem2d, p, l):
    rows = mem2d.shape[0]
    vm = pl.BlockSpec(memory_space=pltpu.VMEM)
    return pl.pallas_call(
        _memkv_kernel,
        in_specs=[vm] * 5,
        out_specs=[vm, vm],
        out_shape=[jax.ShapeDtypeStruct((rows, MEM_W), BF16)] * 2,
        compiler_params=pltpu.CompilerParams(vmem_limit_bytes=VMEM_LIMIT_BYTES),
        name=f"mem_kv_l{l}",
    )(mem2d, p["mem_norm_g"][l].reshape(1, -1), _bf(p["w_mem_kv"][l]),
      jnp.tile(p["mem_k_norm"][l], MEM_HEADS).reshape(1, -1), _block_diag_ones(MEM_W))


def _mix_kernel(x_ref, qa_ref, kc_ref, kp_ref, vc_ref, vp_ref, bias_ref, sink_ref, qm_ref, mk_ref, mv_ref,
                y_ref, r_ref, k_ref, v_ref, gate_ref, lnw_ref, lnb_ref, rk_ref, bd_ref, wout_ref, out_ref,
                *, tq, tiles_per_batch):
    N, BQ = HEAD_DIM, BLOCK_Q
    seq_start = pl.program_id(0) % tiles_per_batch == 0
    nqb = tq // BQ
    qa = qa_ref[...]
    kall = jnp.concatenate([kp_ref[...], kc_ref[...]], axis=0)
    vall = jnp.concatenate([vp_ref[...], vc_ref[...]], axis=0)
    before_seq = lax.broadcasted_iota(jnp.int32, (ATTN_GROUP * BQ, 2 * BQ), 1) < BQ

    qm = qm_ref[...]
    mk = mk_ref[0]
    mv = mv_ref[0]

    swa = [(j, g) for j in range(nqb) for g in range(ATTN_KV_HEADS)]
    logits, values, sinks = [], [], []
    for j, g in swa:
        qg = jnp.concatenate(
            [qa[j * BQ:(j + 1) * BQ, (ATTN_GROUP * g + i) * N:(ATTN_GROUP * g + i + 1) * N] for i in range(ATTN_GROUP)],
            axis=0)
        lg = _dot_nt(qg, kall[j * BQ:(j + 2) * BQ, g * N:(g + 1) * N]) + bias_ref[g]
        if j == 0:
            lg = jnp.where(jnp.logical_and(seq_start, before_seq), -jnp.inf, lg)
        logits.append(lg)
        values.append(vall[j * BQ:(j + 2) * BQ, g * N:(g + 1) * N])
        sinks.append(sink_ref[g])
    for h in range(MEM_HEADS):
        hs = slice(h * N, (h + 1) * N)
        logits.append(_dot_nt(qm[:, hs], mk[:, hs]))
        values.append(mv[:, hs])
        sinks.append(None)
    row_max = [jnp.max(lg, axis=-1, keepdims=True) for lg in logits]
    m = [rm if s is None else jnp.maximum(rm, s) for rm, s in zip(row_max, sinks)]
    e = [jnp.exp(lg - mm) for lg, mm in zip(logits, m)]
    denom = [jnp.sum(ee, axis=-1, keepdims=True) for ee in e]
    denom = [d if s is None else d + jnp.exp(s - mm) for d, s, mm in zip(denom, sinks, m)]
    outs = [_dot(_bf(ee), vv) / d for ee, vv, d in zip(e, values, denom)]

    head_rows = [[None] * nqb for _ in range(ATTN_HEADS)]
    for (j, g), o in zip(swa, outs):
        for i in range(ATTN_GROUP):
            head_rows[ATTN_GROUP * g + i][j] = o[i * BQ:(i + 1) * BQ]
    out_a = jnp.concatenate([jnp.concatenate(rows, axis=0) for rows in head_rows], axis=-1)
    out_m = jnp.concatenate(outs[len(swa):], axis=-1)

    bd = bd_ref[...]
    y = y_ref[...]
    d = y - _head_sum(y, bd) * (1.0 / N)
    var = _head_sum(d * d, bd) * (1.0 / N)
    yn = d * lax.rsqrt(var + GN_EPS) * lnw_ref[...] + lnb_ref[...]
    bonus = _head_sum(r_ref[...] * k_ref[...] * rk_ref[...], bd) * v_ref[...]
    out_b = (yn + bonus) * gate_ref[...]

    mixed = _bf(jnp.concatenate([out_a, out_b, out_m], axis=-1))
    out_ref[...] = x_ref[...] + _dot(mixed, wout_ref[...])


def _mix_call(x, qa, ka, va, qm, mk, mv, y, r, k, v, gate, bias, p, l, *, nb):
    T, D = x.shape
    tq = MIX_TQ
    S = T // nb
    assert S % tq == 0 and tq % BLOCK_Q == 0
    tpb = S // tq
    qpb = tq // BLOCK_Q
    mem_tokens = mk.shape[0] // nb
    row = lambda a: a.reshape(1, -1).astype(F32)
    sink = jnp.repeat(p["attn_sinks"][l].astype(F32), BLOCK_Q).reshape(ATTN_KV_HEADS, ATTN_GROUP * BLOCK_Q, 1)
    tile = lambda w: pl.BlockSpec((tq, w), lambda i: (i, 0))
    prev = pl.BlockSpec((BLOCK_Q, KV_W), lambda i: (jnp.maximum(i * qpb - 1, 0), 0))
    full = lambda a: pl.BlockSpec(a.shape, lambda i: (0,) * a.ndim)
    memspec = pl.BlockSpec((1, mem_tokens, MEM_W), lambda i: (i // tpb, 0, 0))
    lnw, lnb, rk = row(p["rwkv_ln_w"][l]), row(p["rwkv_ln_b"][l]), row(p["rwkv_r_k"][l])
    bd = _block_diag_ones(RWKV_W)
    wout = _bf(p["w_out"][l])
    mk3 = mk.reshape(nb, mem_tokens, MEM_W)
    mv3 = mv.reshape(nb, mem_tokens, MEM_W)
    return pl.pallas_call(
        functools.partial(_mix_kernel, tq=tq, tiles_per_batch=tpb),
        grid=(T // tq,),
        in_specs=[tile(D), tile(ATTN_W), tile(KV_W), prev, tile(KV_W), prev, full(bias), full(sink), tile(MEM_W),
                  memspec, memspec] + [tile(RWKV_W)] * 5 + [full(lnw), full(lnb), full(rk), full(bd), full(wout)],
        out_specs=tile(D),
        out_shape=jax.ShapeDtypeStruct((T, D), F32),
        compiler_params=pltpu.CompilerParams(dimension_semantics=("arbitrary",), vmem_limit_bytes=VMEM_LIMIT_BYTES),
        name=f"mix_l{l}",
    )(x, qa, ka, ka, va, va, bias, sink, qm, mk3, mv3, y, r, k, v, gate, lnw, lnb, rk, bd, wout)


SUBLANES = 8


def _tile_copies(hbm, buf, sem, tile, slot, to_vmem):
    copies = []
    for s in range(SUBLANES):
        src, dst = hbm.at[tile, s], buf.at[slot, :, s, :]
        if not to_vmem:
            src, dst = dst, src
        copies.append(pltpu.make_async_copy(src, dst, sem.at[slot, s]))
    return copies


def _shift_rows(u, before, steps):
    first = lax.broadcasted_iota(jnp.int32, (SUBLANES, u.shape[1]), 0) == 0
    tm = u.shape[0]
    heads = []
    for i in range(steps):
        lo = tm - (steps - i) * SUBLANES
        wrapped = pltpu.roll(u[lo:lo + SUBLANES], 1, axis=0)
        prior = jnp.broadcast_to(before[SUBLANES - steps + i:SUBLANES - steps + i + 1], wrapped.shape)
        heads.append(jnp.where(first, prior, wrapped))
    return jnp.concatenate(heads + [u[:tm - steps * SUBLANES]], axis=0)


def _ffn_kernel(x_hbm, g_ref, wup_ref, cw_ref, cb_ref, wdn_ref, o_hbm, xbuf, obuf, carry_ref, act_ref,
                in_sem, out_sem, *, tm, tiles_per_batch, n_tiles):
    D = g_ref.shape[1]
    q = tm // SUBLANES
    i = pl.program_id(0)
    slot = i % 2

    @pl.when(i == 0)
    def _():
        carry_ref[...] = jnp.zeros_like(carry_ref)
        for cp in _tile_copies(x_hbm, xbuf, in_sem, 0, 0, True):
            cp.start()

    @pl.when(i + 1 < n_tiles)
    def _():
        for cp in _tile_copies(x_hbm, xbuf, in_sem, i + 1, 1 - slot, True):
            cp.start()

    for cp in _tile_copies(x_hbm, xbuf, in_sem, i, slot, True):
        cp.wait()

    @pl.when(i >= 2)
    def _():
        for cp in _tile_copies(o_hbm, obuf, out_sem, i - 2, slot, False):
            cp.wait()

    seq_start = i % tiles_per_batch == 0
    g = g_ref[...]
    x = xbuf[slot].reshape(tm, D)
    before = jnp.where(seq_start, 0.0, _rms_rows(carry_ref[...], g))
    for r, grp in ((SUBLANES - 2, q - 2), (SUBLANES - 1, q - 1)):
        row = grp * SUBLANES + SUBLANES - 1
        carry_ref[r:r + 1, :] = x[row:row + 1]
    h_ext = _bf(jnp.concatenate([before, _rms_rows(x, g)], axis=0))

    n_chunks = D_FF // FFN_FC
    gate_cols = lambda c: slice(c * FFN_FC, (c + 1) * FFN_FC)
    val_cols = lambda c: slice(D_FF + c * FFN_FC, D_FF + (c + 1) * FFN_FC)

    def conv(u_ext, cols):
        w = cw_ref[:, cols]
        u = u_ext[SUBLANES:]
        return (cb_ref[:, cols] + w[0:1] * _shift_rows(u, u_ext[:SUBLANES], 2)
                + w[1:2] * _shift_rows(u, u_ext[:SUBLANES], 1) + w[2:3] * u)

    up = lambda c: (_dot(h_ext, wup_ref[:, gate_cols(c)]), _dot(h_ext, wup_ref[:, val_cols(c)]))
    nxt = up(0)
    for c in range(n_chunks):
        ug, uv = nxt
        if c + 1 < n_chunks:
            nxt = up(c + 1)
        gt = conv(ug, gate_cols(c))
        act_ref[:, gate_cols(c)] = _bf(gt * jax.nn.sigmoid(gt) * conv(uv, val_cols(c)))
    obuf[slot] = (x + _dot(act_ref[...], wdn_ref[...])).reshape(q, SUBLANES, D)

    for cp in _tile_copies(o_hbm, obuf, out_sem, i, slot, False):
        cp.start()

    @pl.when(i == n_tiles - 1)
    def _():
        if n_tiles >= 2:
            for cp in _tile_copies(o_hbm, obuf, out_sem, i - 1, 1 - slot, False):
                cp.wait()
        for cp in _tile_copies(o_hbm, obuf, out_sem, i, slot, False):
            cp.wait()


def _ffn_call(x, p, l, *, nb):
    T, D = x.shape
    tm = FFN_TM
    S = T // nb
    q = tm // SUBLANES
    assert S % tm == 0 and q % SUBLANES == 0 and D_FF % FFN_FC == 0
    n_tiles = T // tm
    g = p["ffn_norm_g"][l].reshape(1, -1)
    wup, wdn = _bf(p["w_up"][l]), _bf(p["w_down"][l])
    cw, cb = p["conv_w"][l], p["conv_b"][l].reshape(1, -1)
    full = lambda a: pl.BlockSpec(a.shape, lambda i: (0,) * a.ndim)
    once = lambda a: pl.BlockSpec(a.shape, lambda i: (0,) * a.ndim, pipeline_mode=pl.Buffered(1))
    out = pl.pallas_call(
        functools.partial(_ffn_kernel, tm=tm, tiles_per_batch=S // tm, n_tiles=n_tiles),
        grid=(n_tiles,),
        in_specs=[pl.BlockSpec(memory_space=pl.ANY), full(g), once(wup), full(cw), full(cb), once(wdn)],
        out_specs=pl.BlockSpec(memory_space=pl.ANY),
        out_shape=jax.ShapeDtypeStruct((n_tiles, SUBLANES, q, D), F32),
        scratch_shapes=[pltpu.VMEM((2, q, SUBLANES, D), F32), pltpu.VMEM((2, q, SUBLANES, D), F32),
                        pltpu.VMEM((SUBLANES, D), F32), pltpu.VMEM((tm, D_FF), BF16),
                        pltpu.SemaphoreType.DMA((2, SUBLANES)), pltpu.SemaphoreType.DMA((2, SUBLANES))],
        compiler_params=pltpu.CompilerParams(dimension_semantics=("arbitrary",), vmem_limit_bytes=VMEM_LIMIT_BYTES),
        name=f"ffn_l{l}",
    )(x.reshape(n_tiles, SUBLANES, q, D), g, wup, cw, cb, wdn)
    return out.reshape(T, D)


_PARAM_NAMES = (
    "rel_bias", "mix_norm_g", "w_in", "w_in_vres", "attn_q_norm", "attn_k_norm", "attn_sinks", "rwkv_mu",
    "rwkv_mu_vres", "rwkv_w0", "rwkv_w2", "rwkv_a0", "rwkv_a2", "rwkv_v0", "rwkv_v2", "rwkv_g2", "rwkv_k_k",
    "rwkv_k_a", "rwkv_r_k", "rwkv_ln_w", "rwkv_ln_b", "mem_norm_g", "w_mem_kv", "mem_q_norm", "mem_k_norm",
    "w_out", "ffn_norm_g", "w_up", "conv_w", "conv_b", "w_down")


def kernel(x, mem, rel_bias, mix_norm_g, w_in, w_in_vres, attn_q_norm, attn_k_norm, attn_sinks, rwkv_mu,
           rwkv_mu_vres, rwkv_w0, rwkv_w2, rwkv_a0, rwkv_a2, rwkv_v0, rwkv_v2, rwkv_g2, rwkv_k_k, rwkv_k_a,
           rwkv_r_k, rwkv_ln_w, rwkv_ln_b, mem_norm_g, w_mem_kv, mem_q_norm, mem_k_norm, w_out, ffn_norm_g,
           w_up, conv_w, conv_b, w_down):
    p = dict(zip(_PARAM_NAMES, (
        rel_bias, mix_norm_g, w_in, w_in_vres, attn_q_norm, attn_k_norm, attn_sinks, rwkv_mu, rwkv_mu_vres,
        rwkv_w0, rwkv_w2, rwkv_a0, rwkv_a2, rwkv_v0, rwkv_v2, rwkv_g2, rwkv_k_k, rwkv_k_a, rwkv_r_k, rwkv_ln_w,
        rwkv_ln_b, mem_norm_g, w_mem_kv, mem_q_norm, mem_k_norm, w_out, ffn_norm_g, w_up, conv_w, conv_b, w_down)))
    nb, S, D = x.shape
    xt = x.reshape(nb * S, D)
    mem2d = mem.reshape(nb * mem.shape[1], D)
    bias = _bias_call(rel_bias)
    v_first = None
    for l in range(w_in.shape[0]):
        qa, ka, va, qm, r, lw, k, v, a, b, gate = _proj_call(xt, p, l, v_first, nb=nb)
        if l == 0:
            v_first = v
        y = _rwkv_scan(r, lw, k, v, a, b, nb=nb)
        mk, mv = _memkv_call(mem2d, p, l)
        xt = _mix_call(xt, qa, ka, va, qm, mk, mv, y, r, k, v, gate, bias, p, l, nb=nb)
        xt = _ffn_call(xt, p, l, nb=nb)
    return xt.reshape(nb, S, D)
```

```python
import functools
import math

import jax
import jax.numpy as jnp
import numpy as np
from jax import lax
from jax.experimental import pallas as pl
from jax.experimental.pallas import tpu as pltpu

F32 = jnp.float32
BF16 = jnp.bfloat16

HEAD_DIM = 64
SCAN_CHUNK = 64
VMEM_LIMIT_BYTES = 56 * 1024 * 1024


def _dot(a, b):
    return jnp.dot(a, b, preferred_element_type=F32)


def _dot_nt(a, b):
    return lax.dot_general(a, b, (((1,), (1,)), ((), ())), preferred_element_type=F32)


def _dot_tn(a, b):
    return lax.dot_general(a, b, (((0,), (0,)), ((), ())), preferred_element_type=F32)


def _bf(x):
    return x.astype(BF16)


SCAN_GROUP_HEADS = 4
SCAN_GROUP_W = SCAN_GROUP_HEADS * HEAD_DIM
SCAN_CHUNKS_PER_STEP = 4


def _scan_kernel(r_ref, lw_ref, k_ref, v_ref, a_ref, b_ref, bdm_ref, y_ref, h_ref, *, nb, nch):
    C, N, GW, HPG = SCAN_CHUNK, HEAD_DIM, SCAN_GROUP_W, SCAN_GROUP_HEADS
    ngroups = h_ref.shape[0]

    @pl.when(pl.program_id(0) == 0)
    def _():
        h_ref[...] = jnp.zeros_like(h_ref)

    trow = lax.broadcasted_iota(jnp.int32, (C, C), 0)
    tcol = lax.broadcasted_iota(jnp.int32, (C, C), 1)
    tri = (trow >= tcol).astype(BF16)
    grow = lax.broadcasted_iota(jnp.int32, (C, GW), 0)
    gcol = lax.broadcasted_iota(jnp.int32, (C, GW), 1) % N
    incl = grow >= gcol
    strict = grow > gcol
    eye = (grow == gcol).astype(F32)
    bdm = bdm_ref[...]

    def bd(x):
        return jnp.concatenate([_bf(x)] * HPG, axis=0) * bdm

    def split2(x):
        hi = _bf(x)
        return hi, _bf(x - hi.astype(F32))

    units = []
    for c in range(nch):
        rows = slice(c * C, (c + 1) * C)
        per_batch = []
        for b in range(nb):
            lw = lw_ref[b, rows, :]
            l1 = _bf(lw)
            e1 = lw - l1.astype(F32)
            l2 = _bf(e1)
            l3 = _bf(e1 - l2.astype(F32))
            cum = _dot(tri, l1) + _dot(tri, l2) + _dot(tri, l3)
            cum_last = cum[C - 1:C, :]
            g_inv = jnp.exp(-cum)
            g_out = jnp.exp(cum_last - cum)
            kk = k_ref[b, rows, :]
            bb = b_ref[b, rows, :]
            per_batch.append(dict(
                a_t=a_ref[b, rows, :] * jnp.exp(cum - lw), r_t=r_ref[b, rows, :] * jnp.exp(cum),
                b_t=bb * g_inv, k_t=kk * g_inv, b_h=bb * g_out, k_h=kk * g_out,
                g_c=jnp.broadcast_to(jnp.exp(cum_last), cum.shape), v=v_ref[b, rows, :]))
        cat = {n: jnp.concatenate([pb[n] for pb in per_batch], axis=1) for n in per_batch[0]}
        for g in range(ngroups):
            units.append({n: t[:, g * GW:(g + 1) * GW] for n, t in cat.items()})

    nu = len(units)
    ar = [jnp.concatenate([_bf(u["a_t"]), _bf(u["r_t"])], axis=0) for u in units]
    pb = [_dot_nt(ar[i], bd(units[i]["b_t"])) for i in range(nu)]
    pk = [_dot_nt(ar[i], bd(units[i]["k_t"])) for i in range(nu)]
    l_ab = [jnp.where(strict, p[:C], 0.0) for p in pb]
    m_rb = [_bf(jnp.where(incl, p[C:], 0.0)) for p in pb]
    tril2 = jnp.concatenate([strict, incl], axis=0)
    lm_k = [_bf(jnp.where(tril2, p, 0.0)) for p in pk]
    t_inv = [eye + l for l in l_ab]
    pw = [_dot(_bf(l), bd(l)) for l in l_ab]
    n_rounds = int(math.log2(C)) - 1
    for rnd in range(n_rounds):
        if rnd + 1 < n_rounds:
            z = [_dot(jnp.concatenate([_bf(p), _bf(t)], axis=0), bd(p)) for p, t in zip(pw, t_inv)]
            pw = [zz[:C] for zz in z]
            t_inv = [t + zz[C:] for t, zz in zip(t_inv, z)]
        else:
            t_inv = [t + _dot(_bf(t), bd(p)) for t, p in zip(t_inv, pw)]
    t_b = [_bf(t) for t in t_inv]
    w1 = [_dot(t, bd(u["a_t"])) for t, u in zip(t_b, units)]
    lmv = [_dot(m, bd(u["v"])) for m, u in zip(lm_k, units)]
    w2 = [_dot(t, bd(x[:C])) for t, x in zip(t_b, lmv)]
    y_v = [x[C:] for x in lmv]
    w1r = [jnp.concatenate([_bf(w), _bf(u["r_t"])], axis=0) for w, u in zip(w1, units)]
    bk = [jnp.concatenate([_bf(u["b_h"]), _bf(u["k_h"])], axis=0) for u in units]
    parts = [p for u in units for p in split2(u["g_c"] * eye)]
    sums = _dot(jnp.concatenate(parts, axis=0), bdm)
    g_row = [sums[2 * i * C:(2 * i + 1) * C] + sums[(2 * i + 1) * C:(2 * i + 2) * C] for i in range(nu)]

    state = [h_ref[g] for g in range(ngroups)]
    for c in range(nch):
        ids = [c * ngroups + g for g in range(ngroups)]
        z = [_dot(w1r[i], bd(state[g])) for g, i in enumerate(ids)]
        u = [zz[:C] + w2[i] for zz, i in zip(z, ids)]
        bd_u = [bd(x) for x in u]
        y = [zz[C:] + _dot(m_rb[i], x) + y_v[i] for zz, i, x in zip(z, ids, bd_u)]
        uv = [jnp.concatenate([_bf(x), _bf(units[i]["v"])], axis=0) for x, i in zip(u, ids)]
        full = [_dot_tn(bk[i], x) * bdm.astype(F32) for i, x in zip(ids, uv)]
        upd = [sum(f[h * N:(h + 1) * N] for h in range(HPG)) for f in full]
        state = [g_row[i] * s + d for i, s, d in zip(ids, state, upd)]
        ycat = jnp.concatenate(y, axis=1)
        for b in range(nb):
            w = y_ref.shape[2]
            y_ref[b, c * C:(c + 1) * C, :] = ycat[:, b * w:(b + 1) * w]
    for g in range(ngroups):
        h_ref[g] = state[g]


def _rwkv_scan(r, lw, k, v, a, b, *, nb):
    T, W = r.shape
    S = T // nb
    C, nch, GW = SCAN_CHUNK, SCAN_CHUNKS_PER_STEP, SCAN_GROUP_W
    assert S % (C * nch) == 0 and (nb * W) % GW == 0 and W % 128 == 0
    ngroups = nb * W // GW
    head = np.arange(GW) // HEAD_DIM
    bdm = jnp.asarray((head[:, None] == head[None, :]).astype(np.float32), dtype=BF16)
    spec = pl.BlockSpec((nb, C * nch, W), lambda c: (0, c, 0))
    args = [t.reshape(nb, S, W) for t in (r, lw, k, v, a, b)]
    y = pl.pallas_call(
        functools.partial(_scan_kernel, nb=nb, nch=nch),
        grid=(S // (C * nch),),
        in_specs=[spec] * 6 + [pl.BlockSpec((GW, GW), lambda c: (0, 0))],
        out_specs=spec,
        out_shape=jax.ShapeDtypeStruct((nb, S, W), F32),
        scratch_shapes=[pltpu.VMEM((ngroups, HEAD_DIM, GW), F32)],
        compiler_params=pltpu.CompilerParams(dimension_semantics=("arbitrary",), vmem_limit_bytes=VMEM_LIMIT_BYTES),
        name="rwkv_scan",
    )(*args, bdm)
    return y.reshape(T, W)


D_MODEL = 1024
ATTN_HEADS = 6
ATTN_KV_HEADS = 2
ATTN_GROUP = ATTN_HEADS // ATTN_KV_HEADS
ATTN_W = ATTN_HEADS * HEAD_DIM
KV_W = ATTN_KV_HEADS * HEAD_DIM
BLOCK_Q = 128
WINDOW = 128
N_BUCKETS = 32
MAX_EXACT = N_BUCKETS // 2
RWKV_W = 6 * HEAD_DIM
LORA_W = 128
SHIFT_W = 3 * RWKV_W + LORA_W
MEM_HEADS = 4
MEM_W = MEM_HEADS * HEAD_DIM
IN_BASE = ATTN_W + 2 * KV_W + SHIFT_W + MEM_W
PB_OFF = ATTN_W + 2 * KV_W
QM_OFF = PB_OFF + SHIFT_W
D_FF = 2816
EPS = 1e-6
GN_EPS = 64e-5
L2_EPS = 1e-12
HALO = 16

PROJ_TM = 512
MIX_TQ = 512
FFN_TM = 512
FFN_FC = 256


def _block_diag_ones(width):
    idx = np.arange(width) // HEAD_DIM
    return jnp.asarray((idx[:, None] == idx[None, :]).astype(np.float32), dtype=BF16)


def _head_sum(t, bd):
    return _dot(_bf(t), bd)


def _head_rms(t, bd, gain):
    ms = _head_sum(t * t, bd) * (1.0 / HEAD_DIM)
    return t * lax.rsqrt(ms + EPS) * gain


def _rms_rows(x, g):
    ms = jnp.mean(x * x, axis=-1, keepdims=True)
    return x * lax.rsqrt(ms + EPS) * g


def _proj_kernel(*refs, tm, tiles_per_batch, has_vres):
    it = iter(refs)
    x_ref, g_ref, w_ref = next(it), next(it), next(it)
    wv_ref = next(it) if has_vres else None
    qg_ref, kg_ref, mg_ref, mu_ref = next(it), next(it), next(it), next(it)
    w0_ref, a0_ref, kk_ref, ka_ref = next(it), next(it), next(it), next(it)
    w2_ref, a2_ref, g2_ref = next(it), next(it), next(it)
    bd384_ref, bd128_ref, bd256_ref = next(it), next(it), next(it)
    if has_vres:
        vfirst_ref, v0_ref, v2_ref = next(it), next(it), next(it)
    qa_ref, ka_out_ref, va_ref, qm_ref = next(it), next(it), next(it), next(it)
    r_ref, lw_ref, k_ref, v_ref, a_ref, b_ref, gate_ref = (next(it) for _ in range(7))
    pbs_ref = next(it)

    x = x_ref[...]
    hn = _bf(_rms_rows(x, g_ref[...]))
    proj = _dot(hn, w_ref[...])
    bd384 = bd384_ref[...]
    qa_ref[...] = _bf(_head_rms(proj[:, :ATTN_W], bd384, qg_ref[...]))
    ka_out_ref[...] = _bf(_head_rms(proj[:, ATTN_W:ATTN_W + KV_W], bd128_ref[...], kg_ref[...]))
    va_ref[...] = _bf(proj[:, ATTN_W + KV_W:PB_OFF])
    qm_ref[...] = _bf(_head_rms(proj[:, QM_OFF:QM_OFF + MEM_W], bd256_ref[...], mg_ref[...]))

    @pl.when(pl.program_id(0) % tiles_per_batch == 0)
    def _():
        pbs_ref[0:8, :] = jnp.zeros((8, pbs_ref.shape[1]), F32)

    pbs_ref[8:tm + 8, 0:SHIFT_W] = proj[:, PB_OFF:PB_OFF + SHIFT_W]
    if has_vres:
        pbs_ref[8:tm + 8, SHIFT_W:SHIFT_W + LORA_W] = _dot(hn, wv_ref[...])
    cur = pbs_ref[8:tm + 8, :]
    prev = pbs_ref[7:tm + 7, :]
    sh = cur + mu_ref[...] * (prev - cur)
    pbs_ref[0:8, :] = pbs_ref[tm:tm + 8, :]

    r = sh[:, 0:RWKV_W]
    k = sh[:, RWKV_W:2 * RWKV_W]
    v = sh[:, 2 * RWKV_W:3 * RWKV_W]
    z = sh[:, 3 * RWKV_W:SHIFT_W]
    t = w0_ref[...] + _dot(_bf(jnp.tanh(z)), w2_ref[...])
    lw_ref[...] = -math.exp(-0.5) * jax.nn.sigmoid(t)
    a = jax.nn.sigmoid(a0_ref[...] + _dot(_bf(z), a2_ref[...]))
    gate_ref[...] = _dot(_bf(jax.nn.sigmoid(z)), g2_ref[...])
    if has_vres:
        vd = sh[:, SHIFT_W:SHIFT_W + LORA_W]
        v = v + (vfirst_ref[...] - v) * jax.nn.sigmoid(v0_ref[...] + _dot(_bf(vd), v2_ref[...]))
    kk = k * kk_ref[...]
    kk = kk / jnp.maximum(jnp.sqrt(_head_sum(kk * kk, bd384)), L2_EPS)
    r_ref[...] = r
    k_ref[...] = k * (1.0 + (a - 1.0) * ka_ref[...])
    v_ref[...] = v
    a_ref[...] = -kk
    b_ref[...] = kk * a


def _pad_rows(w, rows, at):
    out = jnp.zeros((rows, w.shape[1]), w.dtype)
    return lax.dynamic_update_slice(out, w, (at, 0))


def _proj_call(x, p, l, v_first, *, nb):
    T, D = x.shape
    tm = PROJ_TM
    assert T % tm == 0 and (T // nb) % tm == 0
    has_vres = l > 0
    row = lambda a: a.reshape(1, -1).astype(F32)
    tile6 = lambda a: jnp.tile(a, ATTN_HEADS)
    scale = HEAD_DIM ** -0.5
    mu = p["rwkv_mu"][l]
    if has_vres:
        mu = jnp.concatenate([mu, p["rwkv_mu_vres"][l - 1], jnp.zeros((LORA_W - 16,), F32)])
    ins = [x, row(p["mix_norm_g"][l]), _bf(p["w_in"][l])]
    if has_vres:
        ins.append(_bf(jnp.pad(p["w_in_vres"][l - 1], ((0, 0), (0, LORA_W - 16)))))
    ins += [row(tile6(p["attn_q_norm"][l]) * scale), row(jnp.tile(p["attn_k_norm"][l], ATTN_KV_HEADS)),
            row(jnp.tile(p["mem_q_norm"][l], MEM_HEADS) * scale), row(mu),
            row(p["rwkv_w0"][l]), row(p["rwkv_a0"][l]), row(p["rwkv_k_k"][l]), row(p["rwkv_k_a"][l]),
            _bf(_pad_rows(p["rwkv_w2"][l], LORA_W, 0)), _bf(_pad_rows(p["rwkv_a2"][l], LORA_W, 32)),
            _bf(_pad_rows(p["rwkv_g2"][l], LORA_W, 64)),
            _block_diag_ones(ATTN_W), _block_diag_ones(KV_W), _block_diag_ones(MEM_W)]
    if has_vres:
        ins += [v_first, row(p["rwkv_v0"][l - 1]), _bf(_pad_rows(p["rwkv_v2"][l - 1], LORA_W, 0))]

    def spec(a):
        if a.shape[0] == T:
            return pl.BlockSpec((tm, a.shape[1]), lambda i: (i, 0))
        return pl.BlockSpec(a.shape, lambda i: (0, 0))

    out_shapes = [jax.ShapeDtypeStruct((T, ATTN_W), BF16), jax.ShapeDtypeStruct((T, KV_W), BF16),
                  jax.ShapeDtypeStruct((T, KV_W), BF16), jax.ShapeDtypeStruct((T, MEM_W), BF16)]
    out_shapes += [jax.ShapeDtypeStruct((T, RWKV_W), F32)] * 7
    sw = SHIFT_W + (LORA_W if has_vres else 0)
    return pl.pallas_call(
        functools.partial(_proj_kernel, tm=tm, tiles_per_batch=(T // nb) // tm, has_vres=has_vres),
        grid=(T // tm,),
        in_specs=[spec(a) for a in ins],
        out_specs=[pl.BlockSpec((tm, s.shape[1]), lambda i: (i, 0)) for s in out_shapes],
        out_shape=out_shapes,
        scratch_shapes=[pltpu.VMEM((tm + 8, sw), F32)],
        compiler_params=pltpu.CompilerParams(dimension_semantics=("arbitrary",), vmem_limit_bytes=VMEM_LIMIT_BYTES),
        name=f"proj_l{l}",
    )(*ins)


def _bucket_table():
    qi = np.arange(BLOCK_Q)[:, None]
    kj = np.arange(2 * BLOCK_Q)[None, :]
    dist = qi + BLOCK_Q - kj
    in_band = (dist >= 0) & (dist < WINDOW)
    d = np.maximum(dist, 1).astype(np.float32)
    large = MAX_EXACT + (np.log(d / np.float32(MAX_EXACT)) / np.float32(math.log(WINDOW / MAX_EXACT))
                         * np.float32(N_BUCKETS - MAX_EXACT)).astype(np.int32)
    large = np.minimum(large, N_BUCKETS - 1)
    bucket = np.where(dist < MAX_EXACT, np.maximum(dist, 0), large)
    return np.where(in_band, bucket, -1).astype(np.int32)


def _bias_kernel(rb_ref, bucket_ref, out_ref):
    bucket = bucket_ref[...]
    for h in range(ATTN_HEADS):
        acc = jnp.full(bucket.shape, -jnp.inf, F32)
        for j in range(N_BUCKETS):
            acc = jnp.where(bucket == j, rb_ref[j, h], acc)
        out_ref[h] = acc


def _bias_call(rel_bias):
    tab = pl.pallas_call(
        _bias_kernel,
        in_specs=[pl.BlockSpec(memory_space=pltpu.SMEM), pl.BlockSpec(memory_space=pltpu.VMEM)],
        out_specs=pl.BlockSpec(memory_space=pltpu.VMEM),
        out_shape=jax.ShapeDtypeStruct((ATTN_HEADS, BLOCK_Q, 2 * BLOCK_Q), F32),
        name="rel_bias_table",
    )(rel_bias.astype(F32), jnp.asarray(_bucket_table()))
    return tab.reshape(ATTN_KV_HEADS, ATTN_GROUP * BLOCK_Q, 2 * BLOCK_Q)


def _memkv_kernel(mem_ref, g_ref, w_ref, kg_ref, bd_ref, mk_ref, mv_ref):
    hn = _bf(_rms_rows(mem_ref[...], g_ref[...]))
    kv = _dot(hn, w_ref[...])
    mk_ref[...] = _bf(_head_rms(kv[:, :MEM_W], bd_ref[...], kg_ref[...]))
    mv_ref[...] = _bf(kv[:, MEM_W:])


def _memkv_call(mem2d, p, l):
    rows = mem2d.shape[0]
    vm = pl.BlockSpec(memory_space=pltpu.VMEM)
    return pl.pallas_call(
        _memkv_kernel,
        in_specs=[vm] * 5,
        out_specs=[vm, vm],
        out_shape=[jax.ShapeDtypeStruct((rows, MEM_W), BF16)] * 2,
        compiler_params=pltpu.CompilerParams(vmem_limit_bytes=VMEM_LIMIT_BYTES),
        name=f"mem_kv_l{l}",
    )(mem2d, p["mem_norm_g"][l].reshape(1, -1), _bf(p["w_mem_kv"][l]),
      jnp.tile(p["mem_k_norm"][l], MEM_HEADS).reshape(1, -1), _block_diag_ones(MEM_W))


def _mix_kernel(x_ref, qa_ref, kc_ref, kp_ref, vc_ref, vp_ref, bias_ref, sink_ref, qm_ref, mk_ref, mv_ref,
                y_ref, r_ref, k_ref, v_ref, gate_ref, lnw_ref, lnb_ref, rk_ref, bd_ref, wout_ref, out_ref,
                *, tq, tiles_per_batch):
    N, BQ = HEAD_DIM, BLOCK_Q
    seq_start = pl.program_id(0) % tiles_per_batch == 0
    nqb = tq // BQ
    qa = qa_ref[...]
    kall = jnp.concatenate([kp_ref[...], kc_ref[...]], axis=0)
    vall = jnp.concatenate([vp_ref[...], vc_ref[...]], axis=0)
    before_seq = lax.broadcasted_iota(jnp.int32, (ATTN_GROUP * BQ, 2 * BQ), 1) < BQ

    qm = qm_ref[...]
    mk = mk_ref[0]
    mv = mv_ref[0]

    swa = [(j, g) for j in range(nqb) for g in range(ATTN_KV_HEADS)]
    logits, values, sinks = [], [], []
    for j, g in swa:
        qg = jnp.concatenate(
            [qa[j * BQ:(j + 1) * BQ, (ATTN_GROUP * g + i) * N:(ATTN_GROUP * g + i + 1) * N] for i in range(ATTN_GROUP)],
            axis=0)
        lg = _dot_nt(qg, kall[j * BQ:(j + 2) * BQ, g * N:(g + 1) * N]) + bias_ref[g]
        if j == 0:
            lg = jnp.where(jnp.logical_and(seq_start, before_seq), -jnp.inf, lg)
        logits.append(lg)
        values.append(vall[j * BQ:(j + 2) * BQ, g * N:(g + 1) * N])
        sinks.append(sink_ref[g])
    for h in range(MEM_HEADS):
        hs = slice(h * N, (h + 1) * N)
        logits.append(_dot_nt(qm[:, hs], mk[:, hs]))
        values.append(mv[:, hs])
        sinks.append(None)
    row_max = [jnp.max(lg, axis=-1, keepdims=True) for lg in logits]
    m = [rm if s is None else jnp.maximum(rm, s) for rm, s in zip(row_max, sinks)]
    e = [jnp.exp(lg - mm) for lg, mm in zip(logits, m)]
    denom = [jnp.sum(ee, axis=-1, keepdims=True) for ee in e]
    denom = [d if s is None else d + jnp.exp(s - mm) for d, s, mm in zip(denom, sinks, m)]
    outs = [_dot(_bf(ee), vv) / d for ee, vv, d in zip(e, values, denom)]

    head_rows = [[None] * nqb for _ in range(ATTN_HEADS)]
    for (j, g), o in zip(swa, outs):
        for i in range(ATTN_GROUP):
            head_rows[ATTN_GROUP * g + i][j] = o[i * BQ:(i + 1) * BQ]
    out_a = jnp.concatenate([jnp.concatenate(rows, axis=0) for rows in head_rows], axis=-1)
    out_m = jnp.concatenate(outs[len(swa):], axis=-1)

    bd = bd_ref[...]
    y = y_ref[...]
    d = y - _head_sum(y, bd) * (1.0 / N)
    var = _head_sum(d * d, bd) * (1.0 / N)
    yn = d * lax.rsqrt(var + GN_EPS) * lnw_ref[...] + lnb_ref[...]
    bonus = _head_sum(r_ref[...] * k_ref[...] * rk_ref[...], bd) * v_ref[...]
    out_b = (yn + bonus) * gate_ref[...]

    mixed = _bf(jnp.concatenate([out_a, out_b, out_m], axis=-1))
    out_ref[...] = x_ref[...] + _dot(mixed, wout_ref[...])


def _mix_call(x, qa, ka, va, qm, mk, mv, y, r, k, v, gate, bias, p, l, *, nb):
    T, D = x.shape
    tq = MIX_TQ
    S = T // nb
    assert S % tq == 0 and tq % BLOCK_Q == 0
    tpb = S // tq
    qpb = tq // BLOCK_Q
    mem_tokens = mk.shape[0] // nb
    row = lambda a: a.reshape(1, -1).astype(F32)
    sink = jnp.repeat(p["attn_sinks"][l].astype(F32), BLOCK_Q).reshape(ATTN_KV_HEADS, ATTN_GROUP * BLOCK_Q, 1)
    tile = lambda w: pl.BlockSpec((tq, w), lambda i: (i, 0))
    prev = pl.BlockSpec((BLOCK_Q, KV_W), lambda i: (jnp.maximum(i * qpb - 1, 0), 0))
    full = lambda a: pl.BlockSpec(a.shape, lambda i: (0,) * a.ndim)
    memspec = pl.BlockSpec((1, mem_tokens, MEM_W), lambda i: (i // tpb, 0, 0))
    lnw, lnb, rk = row(p["rwkv_ln_w"][l]), row(p["rwkv_ln_b"][l]), row(p["rwkv_r_k"][l])
    bd = _block_diag_ones(RWKV_W)
    wout = _bf(p["w_out"][l])
    mk3 = mk.reshape(nb, mem_tokens, MEM_W)
    mv3 = mv.reshape(nb, mem_tokens, MEM_W)
    return pl.pallas_call(
        functools.partial(_mix_kernel, tq=tq, tiles_per_batch=tpb),
        grid=(T // tq,),
        in_specs=[tile(D), tile(ATTN_W), tile(KV_W), prev, tile(KV_W), prev, full(bias), full(sink), tile(MEM_W),
                  memspec, memspec] + [tile(RWKV_W)] * 5 + [full(lnw), full(lnb), full(rk), full(bd), full(wout)],
        out_specs=tile(D),
        out_shape=jax.ShapeDtypeStruct((T, D), F32),
        compiler_params=pltpu.CompilerParams(dimension_semantics=("arbitrary",), vmem_limit_bytes=VMEM_LIMIT_BYTES),
        name=f"mix_l{l}",
    )(x, qa, ka, ka, va, va, bias, sink, qm, mk3, mv3, y, r, k, v, gate, lnw, lnb, rk, bd, wout)


SUBLANES = 8


def _tile_copies(hbm, buf, sem, tile, slot, to_vmem):
    copies = []
    for s in range(SUBLANES):
        src, dst = hbm.at[tile, s], buf.at[slot, :, s, :]
        if not to_vmem:
            src, dst = dst, src
        copies.append(pltpu.make_async_copy(src, dst, sem.at[slot, s]))
    return copies


def _shift_rows(u, before, steps):
    first = lax.broadcasted_iota(jnp.int32, (SUBLANES, u.shape[1]), 0) == 0
    tm = u.shape[0]
    heads = []
    for i in range(steps):
        lo = tm - (steps - i) * SUBLANES
        wrapped = pltpu.roll(u[lo:lo + SUBLANES], 1, axis=0)
        prior = jnp.broadcast_to(before[SUBLANES - steps + i:SUBLANES - steps + i + 1], wrapped.shape)
        heads.append(jnp.where(first, prior, wrapped))
    return jnp.concatenate(heads + [u[:tm - steps * SUBLANES]], axis=0)


def _ffn_kernel(x_hbm, g_ref, wup_ref, cw_ref, cb_ref, wdn_ref, o_hbm, xbuf, obuf, carry_ref, act_ref,
                in_sem, out_sem, *, tm, tiles_per_batch, n_tiles):
    D = g_ref.shape[1]
    q = tm // SUBLANES
    i = pl.program_id(0)
    slot = i % 2

    @pl.when(i == 0)
    def _():
        carry_ref[...] = jnp.zeros_like(carry_ref)
        for cp in _tile_copies(x_hbm, xbuf, in_sem, 0, 0, True):
            cp.start()

    @pl.when(i + 1 < n_tiles)
    def _():
        for cp in _tile_copies(x_hbm, xbuf, in_sem, i + 1, 1 - slot, True):
            cp.start()

    for cp in _tile_copies(x_hbm, xbuf, in_sem, i, slot, True):
        cp.wait()

    @pl.when(i >= 2)
    def _():
        for cp in _tile_copies(o_hbm, obuf, out_sem, i - 2, slot, False):
            cp.wait()

    seq_start = i % tiles_per_batch == 0
    g = g_ref[...]
    x = xbuf[slot].reshape(tm, D)
    before = jnp.where(seq_start, 0.0, _rms_rows(carry_ref[...], g))
    for r, grp in ((SUBLANES - 2, q - 2), (SUBLANES - 1, q - 1)):
        row = grp * SUBLANES + SUBLANES - 1
        carry_ref[r:r + 1, :] = x[row:row + 1]
    h_ext = _bf(jnp.concatenate([before, _rms_rows(x, g)], axis=0))

    n_chunks = D_FF // FFN_FC
    gate_cols = lambda c: slice(c * FFN_FC, (c + 1) * FFN_FC)
    val_cols = lambda c: slice(D_FF + c * FFN_FC, D_FF + (c + 1) * FFN_FC)

    def conv(u_ext, cols):
        w = cw_ref[:, cols]
        u = u_ext[SUBLANES:]
        return (cb_ref[:, cols] + w[0:1] * _shift_rows(u, u_ext[:SUBLANES], 2)
                + w[1:2] * _shift_rows(u, u_ext[:SUBLANES], 1) + w[2:3] * u)

    up = lambda c: (_dot(h_ext, wup_ref[:, gate_cols(c)]), _dot(h_ext, wup_ref[:, val_cols(c)]))
    nxt = up(0)
    for c in range(n_chunks):
        ug, uv = nxt
        if c + 1 < n_chunks:
            nxt = up(c + 1)
        gt = conv(ug, gate_cols(c))
        act_ref[:, gate_cols(c)] = _bf(gt * jax.nn.sigmoid(gt) * conv(uv, val_cols(c)))
    obuf[slot] = (x + _dot(act_ref[...], wdn_ref[...])).reshape(q, SUBLANES, D)

    for cp in _tile_copies(o_hbm, obuf, out_sem, i, slot, False):
        cp.start()

    @pl.when(i == n_tiles - 1)
    def _():
        if n_tiles >= 2:
            for cp in _tile_copies(o_hbm, obuf, out_sem, i - 1, 1 - slot, False):
                cp.wait()
        for cp in _tile_copies(o_hbm, obuf, out_sem, i, slot, False):
            cp.wait()


def _ffn_call(x, p, l, *, nb):
    T, D = x.shape
    tm = FFN_TM
    S = T // nb
    q = tm // SUBLANES
    assert S % tm == 0 and q % SUBLANES == 0 and D_FF % FFN_FC == 0
    n_tiles = T // tm
    g = p["ffn_norm_g"][l].reshape(1, -1)
    wup, wdn = _bf(p["w_up"][l]), _bf(p["w_down"][l])
    cw, cb = p["conv_w"][l], p["conv_b"][l].reshape(1, -1)
    full = lambda a: pl.BlockSpec(a.shape, lambda i: (0,) * a.ndim)
    once = lambda a: pl.BlockSpec(a.shape, lambda i: (0,) * a.ndim, pipeline_mode=pl.Buffered(1))
    out = pl.pallas_call(
        functools.partial(_ffn_kernel, tm=tm, tiles_per_batch=S // tm, n_tiles=n_tiles),
        grid=(n_tiles,),
        in_specs=[pl.BlockSpec(memory_space=pl.ANY), full(g), once(wup), full(cw), full(cb), once(wdn)],
        out_specs=pl.BlockSpec(memory_space=pl.ANY),
        out_shape=jax.ShapeDtypeStruct((n_tiles, SUBLANES, q, D), F32),
        scratch_shapes=[pltpu.VMEM((2, q, SUBLANES, D), F32), pltpu.VMEM((2, q, SUBLANES, D), F32),
                        pltpu.VMEM((SUBLANES, D), F32), pltpu.VMEM((tm, D_FF), BF16),
                        pltpu.SemaphoreType.DMA((2, SUBLANES)), pltpu.SemaphoreType.DMA((2, SUBLANES))],
        compiler_params=pltpu.CompilerParams(dimension_semantics=("arbitrary",), vmem_limit_bytes=VMEM_LIMIT_BYTES),
        name=f"ffn_l{l}",
    )(x.reshape(n_tiles, SUBLANES, q, D), g, wup, cw, cb, wdn)
    return out.reshape(T, D)


_PARAM_NAMES = (
    "rel_bias", "mix_norm_g", "w_in", "w_in_vres", "attn_q_norm", "attn_k_norm", "attn_sinks", "rwkv_mu",
    "rwkv_mu_vres", "rwkv_w0", "rwkv_w2", "rwkv_a0", "rwkv_a2", "rwkv_v0", "rwkv_v2", "rwkv_g2", "rwkv_k_k",
    "rwkv_k_a", "rwkv_r_k", "rwkv_ln_w", "rwkv_ln_b", "mem_norm_g", "w_mem_kv", "mem_q_norm", "mem_k_norm",
    "w_out", "ffn_norm_g", "w_up", "conv_w", "conv_b", "w_down")


def kernel(x, mem, rel_bias, mix_norm_g, w_in, w_in_vres, attn_q_norm, attn_k_norm, attn_sinks, rwkv_mu,
           rwkv_mu_vres, rwkv_w0, rwkv_w2, rwkv_a0, rwkv_a2, rwkv_v0, rwkv_v2, rwkv_g2, rwkv_k_k, rwkv_k_a,
           rwkv_r_k, rwkv_ln_w, rwkv_ln_b, mem_norm_g, w_mem_kv, mem_q_norm, mem_k_norm, w_out, ffn_norm_g,
           w_up, conv_w, conv_b, w_down):
    p = dict(zip(_PARAM_NAMES, (
        rel_bias, mix_norm_g, w_in, w_in_vres, attn_q_norm, attn_k_norm, attn_sinks, rwkv_mu, rwkv_mu_vres,
        rwkv_w0, rwkv_w2, rwkv_a0, rwkv_a2, rwkv_v0, rwkv_v2, rwkv_g2, rwkv_k_k, rwkv_k_a, rwkv_r_k, rwkv_ln_w,
        rwkv_ln_b, mem_norm_g, w_mem_kv, mem_q_norm, mem_k_norm, w_out, ffn_norm_g, w_up, conv_w, conv_b, w_down)))
    nb, S, D = x.shape
    xt = x.reshape(nb * S, D)
    mem2d = mem.reshape(nb * mem.shape[1], D)
    bias = _bias_call(rel_bias)
    v_first = None
    for l in range(w_in.shape[0]):
        qa, ka, va, qm, r, lw, k, v, a, b, gate = _proj_call(xt, p, l, v_first, nb=nb)
        if l == 0:
            v_first = v
        y = _rwkv_scan(r, lw, k, v, a, b, nb=nb)
        mk, mv = _memkv_call(mem2d, p, l)
        xt = _mix_call(xt, qa, ka, va, qm, mk, mv, y, r, k, v, gate, bias, p, l, nb=nb)
        xt = _ffn_call(xt, p, l, nb=nb)
    return xt.reshape(nb, S, D)
```

```python
import functools
import math

import jax
import jax.numpy as jnp
import numpy as np
from jax import lax
from jax.experimental import pallas as pl
from jax.experimental.pallas import tpu as pltpu

F32 = jnp.float32
BF16 = jnp.bfloat16

HEAD_DIM = 64
SCAN_CHUNK = 64
VMEM_LIMIT_BYTES = 56 * 1024 * 1024


def _dot(a, b):
    return jnp.dot(a, b, preferred_element_type=F32)


def _dot_nt(a, b):
    return lax.dot_general(a, b, (((1,), (1,)), ((), ())), preferred_element_type=F32)


def _dot_tn(a, b):
    return lax.dot_general(a, b, (((0,), (0,)), ((), ())), preferred_element_type=F32)


def _bf(x):
    return x.astype(BF16)


SCAN_GROUP_HEADS = 4
SCAN_GROUP_W = SCAN_GROUP_HEADS * HEAD_DIM
SCAN_CHUNKS_PER_STEP = 4


def _scan_kernel(r_ref, lw_ref, k_ref, v_ref, a_ref, b_ref, bdm_ref, y_ref, h_ref, *, nb, nch):
    C, N, GW, HPG = SCAN_CHUNK, HEAD_DIM, SCAN_GROUP_W, SCAN_GROUP_HEADS
    ngroups = h_ref.shape[0]

    @pl.when(pl.program_id(0) == 0)
    def _():
        h_ref[...] = jnp.zeros_like(h_ref)

    trow = lax.broadcasted_iota(jnp.int32, (C, C), 0)
    tcol = lax.broadcasted_iota(jnp.int32, (C, C), 1)
    tri = (trow >= tcol).astype(BF16)
    grow = lax.broadcasted_iota(jnp.int32, (C, GW), 0)
    gcol = lax.broadcasted_iota(jnp.int32, (C, GW), 1) % N
    incl = grow >= gcol
    strict = grow > gcol
    eye = (grow == gcol).astype(F32)
    bdm = bdm_ref[...]

    def bd(x):
        return jnp.concatenate([_bf(x)] * HPG, axis=0) * bdm

    def split2(x):
        hi = _bf(x)
        return hi, _bf(x - hi.astype(F32))

    ready = {}

    def chunk_units(c):
        rows = slice(c * C, (c + 1) * C)
        per_batch = []
        for b in range(nb):
            lw = lw_ref[b, rows, :]
            l1 = _bf(lw)
            e1 = lw - l1.astype(F32)
            l2 = _bf(e1)
            l3 = _bf(e1 - l2.astype(F32))
            cum = _dot(tri, l1) + _dot(tri, l2) + _dot(tri, l3)
            cum_last = cum[C - 1:C, :]
            g_inv = jnp.exp(-cum)
            g_out = jnp.exp(cum_last - cum)
            kk = k_ref[b, rows, :]
            bb = b_ref[b, rows, :]
            per_batch.append(dict(
                a_t=a_ref[b, rows, :] * jnp.exp(cum - lw), r_t=r_ref[b, rows, :] * jnp.exp(cum),
                b_t=bb * g_inv, k_t=kk * g_inv, b_h=bb * g_out, k_h=kk * g_out,
                g_c=jnp.broadcast_to(jnp.exp(cum_last), cum.shape), v=v_ref[b, rows, :]))
        cat = {n: jnp.concatenate([pb[n] for pb in per_batch], axis=1) for n in per_batch[0]}
        return [{n: t[:, g * GW:(g + 1) * GW] for n, t in cat.items()} for g in range(ngroups)]

    def prepare(chunks):
        units = [u for c in chunks for u in chunk_units(c)]

        nu = len(units)
        ar = [jnp.concatenate([_bf(u["a_t"]), _bf(u["r_t"])], axis=0) for u in units]
        pb = [_dot_nt(ar[i], bd(units[i]["b_t"])) for i in range(nu)]
        pk = [_dot_nt(ar[i], bd(units[i]["k_t"])) for i in range(nu)]
        l_ab = [jnp.where(strict, p[:C], 0.0) for p in pb]
        m_rb = [_bf(jnp.where(incl, p[C:], 0.0)) for p in pb]
        tril2 = jnp.concatenate([strict, incl], axis=0)
        lm_k = [_bf(jnp.where(tril2, p, 0.0)) for p in pk]
        yield
        t_inv = [eye + l for l in l_ab]
        pw = [_dot(_bf(l), bd(l)) for l in l_ab]
        yield
        n_rounds = int(math.log2(C)) - 1
        for rnd in range(n_rounds):
            if rnd + 1 < n_rounds:
                z = [_dot(jnp.concatenate([_bf(p), _bf(t)], axis=0), bd(p)) for p, t in zip(pw, t_inv)]
                pw = [zz[:C] for zz in z]
                t_inv = [t + zz[C:] for t, zz in zip(t_inv, z)]
            else:
                t_inv = [t + _dot(_bf(t), bd(p)) for t, p in zip(t_inv, pw)]
            yield
        t_b = [_bf(t) for t in t_inv]
        w1 = [_dot(t, bd(u["a_t"])) for t, u in zip(t_b, units)]
        lmv = [_dot(m, bd(u["v"])) for m, u in zip(lm_k, units)]
        yield
        w2 = [_dot(t, bd(x[:C])) for t, x in zip(t_b, lmv)]
        parts = [p for u in units for p in split2(u["g_c"] * eye)]
        sums = _dot(jnp.concatenate(parts, axis=0), bdm)
        done = dict(
            w2=w2, y_v=[x[C:] for x in lmv], m_rb=m_rb, v=[u["v"] for u in units],
            w1r=[jnp.concatenate([_bf(w), _bf(u["r_t"])], axis=0) for w, u in zip(w1, units)],
            bk=[jnp.concatenate([_bf(u["b_h"]), _bf(u["k_h"])], axis=0) for u in units],
            g_row=[sums[2 * i * C:(2 * i + 1) * C] + sums[(2 * i + 1) * C:(2 * i + 2) * C] for i in range(nu)])
        for j, c in enumerate(chunks):
            ready[c] = {n: t[j * ngroups:(j + 1) * ngroups] for n, t in done.items()}
        yield

    state = [h_ref[g] for g in range(ngroups)]

    def advance(c):
        p = ready.pop(c)
        z = [_dot(w, bd(s)) for w, s in zip(p["w1r"], state)]
        yield
        u = [zz[:C] + w for zz, w in zip(z, p["w2"])]
        y = [zz[C:] + _dot(m, bd(x)) + yv for zz, m, x, yv in zip(z, p["m_rb"], u, p["y_v"])]
        uv = [jnp.concatenate([_bf(x), _bf(v)], axis=0) for x, v in zip(u, p["v"])]
        full = [_dot_tn(b, x) * bdm.astype(F32) for b, x in zip(p["bk"], uv)]
        yield
        upd = [sum(f[h * N:(h + 1) * N] for h in range(HPG)) for f in full]
        state[:] = [g * s + d for g, s, d in zip(p["g_row"], state, upd)]
        ycat = jnp.concatenate(y, axis=1)
        w = y_ref.shape[2]
        for b in range(nb):
            y_ref[b, c * C:(c + 1) * C, :] = ycat[:, b * w:(b + 1) * w]
        yield

    def run(gen, stages):
        for _ in range(stages):
            next(gen, None)

    n_prep = 4 + int(math.log2(C)) - 1
    run(prepare(list(range(nch))), n_prep)
    for c in range(nch):
        run(advance(c), 3)
    for g in range(ngroups):
        h_ref[g] = state[g]


def _rwkv_scan(r, lw, k, v, a, b, *, nb):
    T, W = r.shape
    S = T // nb
    C, nch, GW = SCAN_CHUNK, SCAN_CHUNKS_PER_STEP, SCAN_GROUP_W
    assert S % (C * nch) == 0 and (nb * W) % GW == 0 and W % 128 == 0
    ngroups = nb * W // GW
    head = np.arange(GW) // HEAD_DIM
    bdm = jnp.asarray((head[:, None] == head[None, :]).astype(np.float32), dtype=BF16)
    spec = pl.BlockSpec((nb, C * nch, W), lambda c: (0, c, 0))
    args = [t.reshape(nb, S, W) for t in (r, lw, k, v, a, b)]
    y = pl.pallas_call(
        functools.partial(_scan_kernel, nb=nb, nch=nch),
        grid=(S // (C * nch),),
        in_specs=[spec] * 6 + [pl.BlockSpec((GW, GW), lambda c: (0, 0))],
        out_specs=spec,
        out_shape=jax.ShapeDtypeStruct((nb, S, W), F32),
        scratch_shapes=[pltpu.VMEM((ngroups, HEAD_DIM, GW), F32)],
        compiler_params=pltpu.CompilerParams(dimension_semantics=("arbitrary",), vmem_limit_bytes=VMEM_LIMIT_BYTES),
        name="rwkv_scan",
    )(*args, bdm)
    return y.reshape(T, W)


D_MODEL = 1024
ATTN_HEADS = 6
ATTN_KV_HEADS = 2
ATTN_GROUP = ATTN_HEADS // ATTN_KV_HEADS
ATTN_W = ATTN_HEADS * HEAD_DIM
KV_W = ATTN_KV_HEADS * HEAD_DIM
BLOCK_Q = 128
WINDOW = 128
N_BUCKETS = 32
MAX_EXACT = N_BUCKETS // 2
RWKV_W = 6 * HEAD_DIM
LORA_W = 128
SHIFT_W = 3 * RWKV_W + LORA_W
MEM_HEADS = 4
MEM_W = MEM_HEADS * HEAD_DIM
IN_BASE = ATTN_W + 2 * KV_W + SHIFT_W + MEM_W
PB_OFF = ATTN_W + 2 * KV_W
QM_OFF = PB_OFF + SHIFT_W
D_FF = 2816
EPS = 1e-6
GN_EPS = 64e-5
L2_EPS = 1e-12
HALO = 16

PROJ_TM = 1024
PROJ_SUB = 256
MIX_TQ = 512
FFN_TM = 512
FFN_FC = 256


def _block_diag_ones(width):
    idx = np.arange(width) // HEAD_DIM
    return jnp.asarray((idx[:, None] == idx[None, :]).astype(np.float32), dtype=BF16)


def _head_sum(t, bd):
    return _dot(_bf(t), bd)


def _head_rms(t, bd, gain):
    ms = _head_sum(t * t, bd) * (1.0 / HEAD_DIM)
    return t * lax.rsqrt(ms + EPS) * gain


def _rms_rows(x, g):
    ms = jnp.mean(x * x, axis=-1, keepdims=True)
    return x * lax.rsqrt(ms + EPS) * g


def _proj_kernel(*refs, tm, sub, tiles_per_batch, has_vres):
    it = iter(refs)
    x_ref, g_ref, w_ref = next(it), next(it), next(it)
    qg_ref, kg_ref, mg_ref, mu_ref = next(it), next(it), next(it), next(it)
    w0_ref, a0_ref, kk_ref, ka_ref = next(it), next(it), next(it), next(it)
    w2_ref, a2_ref, g2_ref = next(it), next(it), next(it)
    bd384_ref, bd128_ref, bd256_ref = next(it), next(it), next(it)
    if has_vres:
        vfirst_ref, v0_ref, v2_ref = next(it), next(it), next(it)
    qa_ref, ka_out_ref, va_ref, qm_ref = next(it), next(it), next(it), next(it)
    r_ref, lw_ref, k_ref, v_ref, a_ref, b_ref, gate_ref = (next(it) for _ in range(7))
    pbs_ref = next(it)
    sw = pbs_ref.shape[1]

    @pl.when(pl.program_id(0) % tiles_per_batch == 0)
    def _():
        pbs_ref[0:8, :] = jnp.zeros((8, sw), F32)

    def project(i):
        rows = slice(i * sub, (i + 1) * sub)
        return _dot(_bf(_rms_rows(x_ref[rows, :], g_ref[...])), w_ref[...])

    def attention_outputs(i, proj):
        rows = slice(i * sub, (i + 1) * sub)
        qa_ref[rows, :] = _bf(_head_rms(proj[:, :ATTN_W], bd384_ref[...], qg_ref[...]))
        ka_out_ref[rows, :] = _bf(_head_rms(proj[:, ATTN_W:ATTN_W + KV_W], bd128_ref[...], kg_ref[...]))
        va_ref[rows, :] = _bf(proj[:, ATTN_W + KV_W:PB_OFF])
        qm_ref[rows, :] = _bf(_head_rms(proj[:, QM_OFF:QM_OFF + MEM_W], bd256_ref[...], mg_ref[...]))

    def rwkv_outputs(i, proj):
        rows = slice(i * sub, (i + 1) * sub)
        lo = 8 + i * sub
        pbs_ref[lo:lo + sub, 0:SHIFT_W] = proj[:, PB_OFF:PB_OFF + SHIFT_W]
        if has_vres:
            pbs_ref[lo:lo + sub, SHIFT_W:sw] = proj[:, IN_BASE:IN_BASE + LORA_W]
        cur = pbs_ref[lo:lo + sub, :]
        prev = pbs_ref[lo - 1:lo - 1 + sub, :]
        sh = cur + mu_ref[...] * (prev - cur)
        r = sh[:, 0:RWKV_W]
        k = sh[:, RWKV_W:2 * RWKV_W]
        v = sh[:, 2 * RWKV_W:3 * RWKV_W]
        z = sh[:, 3 * RWKV_W:SHIFT_W]
        t = w0_ref[...] + _dot(_bf(jnp.tanh(z)), w2_ref[...])
        lw_ref[rows, :] = -math.exp(-0.5) * jax.nn.sigmoid(t)
        a = jax.nn.sigmoid(a0_ref[...] + _dot(_bf(z), a2_ref[...]))
        gate_ref[rows, :] = _dot(_bf(jax.nn.sigmoid(z)), g2_ref[...])
        if has_vres:
            vd = sh[:, SHIFT_W:sw]
            v = v + (vfirst_ref[rows, :] - v) * jax.nn.sigmoid(v0_ref[...] + _dot(_bf(vd), v2_ref[...]))
        kk = k * kk_ref[...]
        kk = kk / jnp.maximum(jnp.sqrt(_head_sum(kk * kk, bd384_ref[...])), L2_EPS)
        r_ref[rows, :] = r
        k_ref[rows, :] = k * (1.0 + (a - 1.0) * ka_ref[...])
        v_ref[rows, :] = v
        a_ref[rows, :] = -kk
        b_ref[rows, :] = kk * a

    n_sub = tm // sub
    proj = project(0)
    for i in range(n_sub):
        attention_outputs(i, proj)
        nxt = project(i + 1) if i + 1 < n_sub else None
        rwkv_outputs(i, proj)
        proj = nxt
    pbs_ref[0:8, :] = pbs_ref[tm:tm + 8, :]


def _layer_spec(stacked, l):
    return pl.BlockSpec((None,) + stacked.shape[1:], lambda i: (l, 0, 0), pipeline_mode=pl.Buffered(1))


def _pad_rows(w, rows, at):
    out = jnp.zeros((rows, w.shape[1]), w.dtype)
    return lax.dynamic_update_slice(out, w, (at, 0))


def _proj_call(x, p, l, v_first, *, nb):
    T, D = x.shape
    tm = PROJ_TM
    assert T % tm == 0 and (T // nb) % tm == 0
    has_vres = l > 0
    row = lambda a: a.reshape(1, -1).astype(F32)
    tile6 = lambda a: jnp.tile(a, ATTN_HEADS)
    scale = HEAD_DIM ** -0.5
    mu = p["rwkv_mu"][l]
    if has_vres:
        mu = jnp.concatenate([mu, p["rwkv_mu_vres"][l - 1], jnp.zeros((LORA_W - 16,), F32)])
    w = p["w_in"]
    if has_vres:
        w = jnp.concatenate([w[l], _bf(jnp.pad(p["w_in_vres"][l - 1], ((0, 0), (0, LORA_W - 16))))], axis=1)
    ins = [x, row(p["mix_norm_g"][l]), w]
    ins += [row(tile6(p["attn_q_norm"][l]) * scale), row(jnp.tile(p["attn_k_norm"][l], ATTN_KV_HEADS)),
            row(jnp.tile(p["mem_q_norm"][l], MEM_HEADS) * scale), row(mu),
            row(p["rwkv_w0"][l]), row(p["rwkv_a0"][l]), row(p["rwkv_k_k"][l]), row(p["rwkv_k_a"][l]),
            _bf(_pad_rows(p["rwkv_w2"][l], LORA_W, 0)), _bf(_pad_rows(p["rwkv_a2"][l], LORA_W, 32)),
            _bf(_pad_rows(p["rwkv_g2"][l], LORA_W, 64)),
            _block_diag_ones(ATTN_W), _block_diag_ones(KV_W), _block_diag_ones(MEM_W)]
    if has_vres:
        ins += [v_first, row(p["rwkv_v0"][l - 1]), _bf(_pad_rows(p["rwkv_v2"][l - 1], LORA_W, 0))]

    def spec(a):
        if a.ndim == 3:
            return _layer_spec(a, l)
        if a.shape[0] == T:
            return pl.BlockSpec((tm, a.shape[1]), lambda i: (i, 0))
        return pl.BlockSpec(a.shape, lambda i: (0, 0), pipeline_mode=pl.Buffered(1))

    out_shapes = [jax.ShapeDtypeStruct((T, ATTN_W), BF16), jax.ShapeDtypeStruct((T, KV_W), BF16),
                  jax.ShapeDtypeStruct((T, KV_W), BF16), jax.ShapeDtypeStruct((T, MEM_W), BF16)]
    out_shapes += [jax.ShapeDtypeStruct((T, RWKV_W), F32)] * 7
    sw = SHIFT_W + (LORA_W if has_vres else 0)
    return pl.pallas_call(
        functools.partial(_proj_kernel, tm=tm, sub=PROJ_SUB, tiles_per_batch=(T // nb) // tm, has_vres=has_vres),
        grid=(T // tm,),
        in_specs=[spec(a) for a in ins],
        out_specs=[pl.BlockSpec((tm, s.shape[1]), lambda i: (i, 0)) for s in out_shapes],
        out_shape=out_shapes,
        scratch_shapes=[pltpu.VMEM((tm + 8, sw), F32)],
        compiler_params=pltpu.CompilerParams(dimension_semantics=("arbitrary",), vmem_limit_bytes=VMEM_LIMIT_BYTES),
        name=f"proj_l{l}",
    )(*ins)


def _bucket_table():
    qi = np.arange(BLOCK_Q)[:, None]
    kj = np.arange(2 * BLOCK_Q)[None, :]
    dist = qi + BLOCK_Q - kj
    in_band = (dist >= 0) & (dist < WINDOW)
    d = np.maximum(dist, 1).astype(np.float32)
    large = MAX_EXACT + (np.log(d / np.float32(MAX_EXACT)) / np.float32(math.log(WINDOW / MAX_EXACT))
                         * np.float32(N_BUCKETS - MAX_EXACT)).astype(np.int32)
    large = np.minimum(large, N_BUCKETS - 1)
    bucket = np.where(dist < MAX_EXACT, np.maximum(dist, 0), large)
    return np.where(in_band, bucket, -1).astype(np.int32)


def _bias_kernel(rb_ref, bucket_ref, out_ref):
    bucket = bucket_ref[...]
    for h in range(ATTN_HEADS):
        acc = jnp.full(bucket.shape, -jnp.inf, F32)
        for j in range(N_BUCKETS):
            acc = jnp.where(bucket == j, rb_ref[j, h], acc)
        out_ref[h] = acc


def _bias_call(rel_bias):
    tab = pl.pallas_call(
        _bias_kernel,
        in_specs=[pl.BlockSpec(memory_space=pltpu.SMEM), pl.BlockSpec(memory_space=pltpu.VMEM)],
        out_specs=pl.BlockSpec(memory_space=pltpu.VMEM),
        out_shape=jax.ShapeDtypeStruct((ATTN_HEADS, BLOCK_Q, 2 * BLOCK_Q), F32),
        name="rel_bias_table",
    )(rel_bias.astype(F32), jnp.asarray(_bucket_table()))
    return tab.reshape(ATTN_KV_HEADS, ATTN_GROUP * BLOCK_Q, 2 * BLOCK_Q)


def _memkv_kernel(mem_ref, g_ref, w_ref, kg_ref, bd_ref, mk_ref, mv_ref):
    hn = _bf(_rms_rows(mem_ref[...], g_ref[...]))
    kv = _dot(hn, w_ref[...])
    mk_ref[...] = _bf(_head_rms(kv[:, :MEM_W], bd_ref[...], kg_ref[...]))
    mv_ref[...] = _bf(kv[:, MEM_W:])


def _memkv_call(mem2d, p, l):
    rows = mem2d.shape[0]
    vm = pl.BlockSpec(memory_space=pltpu.VMEM)
    return pl.pallas_call(
        _memkv_kernel,
        in_specs=[vm] * 5,
        out_specs=[vm, vm],
        out_shape=[jax.ShapeDtypeStruct((rows, MEM_W), BF16)] * 2,
        compiler_params=pltpu.CompilerParams(vmem_limit_bytes=VMEM_LIMIT_BYTES),
        name=f"mem_kv_l{l}",
    )(mem2d, p["mem_norm_g"][l].reshape(1, -1), _bf(p["w_mem_kv"][l]),
      jnp.tile(p["mem_k_norm"][l], MEM_HEADS).reshape(1, -1), _block_diag_ones(MEM_W))


def _mix_kernel(x_ref, qa_ref, kc_ref, kp_ref, vc_ref, vp_ref, bias_ref, sink_ref, qm_ref, mk_ref, mv_ref,
                y_ref, r_ref, k_ref, v_ref, gate_ref, lnw_ref, lnb_ref, rk_ref, bd_ref, wout_ref, out_ref,
                *, tq, tiles_per_batch):
    N, BQ = HEAD_DIM, BLOCK_Q
    seq_start = pl.program_id(0) % tiles_per_batch == 0
    nqb = tq // BQ
    qa = qa_ref[...]
    kall = jnp.concatenate([kp_ref[...], kc_ref[...]], axis=0)
    vall = jnp.concatenate([vp_ref[...], vc_ref[...]], axis=0)
    before_seq = lax.broadcasted_iota(jnp.int32, (ATTN_GROUP * BQ, 2 * BQ), 1) < BQ

    qm = qm_ref[...]
    mk = mk_ref[0]
    mv = mv_ref[0]

    swa = [(j, g) for j in range(nqb) for g in range(ATTN_KV_HEADS)]
    logits, values, sinks = [], [], []
    for j, g in swa:
        qg = jnp.concatenate(
            [qa[j * BQ:(j + 1) * BQ, (ATTN_GROUP * g + i) * N:(ATTN_GROUP * g + i + 1) * N] for i in range(ATTN_GROUP)],
            axis=0)
        lg = _dot_nt(qg, kall[j * BQ:(j + 2) * BQ, g * N:(g + 1) * N]) + bias_ref[g]
        if j == 0:
            lg = jnp.where(jnp.logical_and(seq_start, before_seq), -jnp.inf, lg)
        logits.append(lg)
        values.append(vall[j * BQ:(j + 2) * BQ, g * N:(g + 1) * N])
        sinks.append(sink_ref[g])
    for h in range(MEM_HEADS):
        hs = slice(h * N, (h + 1) * N)
        logits.append(_dot_nt(qm[:, hs], mk[:, hs]))
        values.append(mv[:, hs])
        sinks.append(None)
    row_max = [jnp.max(lg, axis=-1, keepdims=True) for lg in logits]
    m = [rm if s is None else jnp.maximum(rm, s) for rm, s in zip(row_max, sinks)]
    e = [jnp.exp(lg - mm) for lg, mm in zip(logits, m)]
    denom = [jnp.sum(ee, axis=-1, keepdims=True) for ee in e]
    denom = [d if s is None else d + jnp.exp(s - mm) for d, s, mm in zip(denom, sinks, m)]
    outs = [_dot(_bf(ee), vv) / d for ee, vv, d in zip(e, values, denom)]

    head_rows = [[None] * nqb for _ in range(ATTN_HEADS)]
    for (j, g), o in zip(swa, outs):
        for i in range(ATTN_GROUP):
            head_rows[ATTN_GROUP * g + i][j] = o[i * BQ:(i + 1) * BQ]
    out_a = jnp.concatenate([jnp.concatenate(rows, axis=0) for rows in head_rows], axis=-1)
    out_m = jnp.concatenate(outs[len(swa):], axis=-1)

    bd = bd_ref[...]
    y = y_ref[...]
    d = y - _head_sum(y, bd) * (1.0 / N)
    var = _head_sum(d * d, bd) * (1.0 / N)
    yn = d * lax.rsqrt(var + GN_EPS) * lnw_ref[...] + lnb_ref[...]
    bonus = _head_sum(r_ref[...] * k_ref[...] * rk_ref[...], bd) * v_ref[...]
    out_b = (yn + bonus) * gate_ref[...]

    mixed = _bf(jnp.concatenate([out_a, out_b, out_m], axis=-1))
    out_ref[...] = x_ref[...] + _dot(mixed, wout_ref[...])


def _mix_call(x, qa, ka, va, qm, mk, mv, y, r, k, v, gate, bias, p, l, *, nb):
    T, D = x.shape
    tq = MIX_TQ
    S = T // nb
    assert S % tq == 0 and tq % BLOCK_Q == 0
    tpb = S // tq
    qpb = tq // BLOCK_Q
    mem_tokens = mk.shape[0] // nb
    row = lambda a: a.reshape(1, -1).astype(F32)
    sink = jnp.repeat(p["attn_sinks"][l].astype(F32), BLOCK_Q).reshape(ATTN_KV_HEADS, ATTN_GROUP * BLOCK_Q, 1)
    tile = lambda w: pl.BlockSpec((tq, w), lambda i: (i, 0))
    prev = pl.BlockSpec((BLOCK_Q, KV_W), lambda i: (jnp.maximum(i * qpb - 1, 0), 0))
    full = lambda a: pl.BlockSpec(a.shape, lambda i: (0,) * a.ndim)
    memspec = pl.BlockSpec((1, mem_tokens, MEM_W), lambda i: (i // tpb, 0, 0))
    lnw, lnb, rk = row(p["rwkv_ln_w"][l]), row(p["rwkv_ln_b"][l]), row(p["rwkv_r_k"][l])
    bd = _block_diag_ones(RWKV_W)
    wout = p["w_out"]
    mk3 = mk.reshape(nb, mem_tokens, MEM_W)
    mv3 = mv.reshape(nb, mem_tokens, MEM_W)
    return pl.pallas_call(
        functools.partial(_mix_kernel, tq=tq, tiles_per_batch=tpb),
        grid=(T // tq,),
        in_specs=[tile(D), tile(ATTN_W), tile(KV_W), prev, tile(KV_W), prev, full(bias), full(sink), tile(MEM_W),
                  memspec, memspec] + [tile(RWKV_W)] * 5
                 + [full(lnw), full(lnb), full(rk), full(bd), _layer_spec(wout, l)],
        out_specs=tile(D),
        out_shape=jax.ShapeDtypeStruct((T, D), F32),
        compiler_params=pltpu.CompilerParams(dimension_semantics=("arbitrary",), vmem_limit_bytes=VMEM_LIMIT_BYTES),
        name=f"mix_l{l}",
    )(x, qa, ka, ka, va, va, bias, sink, qm, mk3, mv3, y, r, k, v, gate, lnw, lnb, rk, bd, wout)


SUBLANES = 8


def _tile_copies(hbm, buf, sem, tile, slot, to_vmem):
    copies = []
    for s in range(SUBLANES):
        src, dst = hbm.at[tile, s], buf.at[slot, :, s, :]
        if not to_vmem:
            src, dst = dst, src
        copies.append(pltpu.make_async_copy(src, dst, sem.at[slot, s]))
    return copies


def _shift_rows(u, before, steps):
    first = lax.broadcasted_iota(jnp.int32, (SUBLANES, u.shape[1]), 0) == 0
    tm = u.shape[0]
    heads = []
    for i in range(steps):
        lo = tm - (steps - i) * SUBLANES
        wrapped = pltpu.roll(u[lo:lo + SUBLANES], 1, axis=0)
        prior = jnp.broadcast_to(before[SUBLANES - steps + i:SUBLANES - steps + i + 1], wrapped.shape)
        heads.append(jnp.where(first, prior, wrapped))
    return jnp.concatenate(heads + [u[:tm - steps * SUBLANES]], axis=0)


def _ffn_kernel(x_hbm, g_ref, wup_ref, cw_ref, cb_ref, wdn_ref, o_hbm, xbuf, obuf, carry_ref, act_ref,
                in_sem, out_sem, *, tm, tiles_per_batch, n_tiles):
    D = g_ref.shape[1]
    q = tm // SUBLANES
    i = pl.program_id(0)
    slot = i % 2

    @pl.when(i == 0)
    def _():
        carry_ref[...] = jnp.zeros_like(carry_ref)
        for cp in _tile_copies(x_hbm, xbuf, in_sem, 0, 0, True):
            cp.start()

    @pl.when(i + 1 < n_tiles)
    def _():
        for cp in _tile_copies(x_hbm, xbuf, in_sem, i + 1, 1 - slot, True):
            cp.start()

    for cp in _tile_copies(x_hbm, xbuf, in_sem, i, slot, True):
        cp.wait()

    @pl.when(i >= 2)
    def _():
        for cp in _tile_copies(o_hbm, obuf, out_sem, i - 2, slot, False):
            cp.wait()

    seq_start = i % tiles_per_batch == 0
    g = g_ref[...]
    x = xbuf[slot].reshape(tm, D)
    before = jnp.where(seq_start, 0.0, _rms_rows(carry_ref[...], g))
    for r, grp in ((SUBLANES - 2, q - 2), (SUBLANES - 1, q - 1)):
        row = grp * SUBLANES + SUBLANES - 1
        carry_ref[r:r + 1, :] = x[row:row + 1]
    h_ext = _bf(jnp.concatenate([before, _rms_rows(x, g)], axis=0))

    n_chunks = D_FF // FFN_FC
    gate_cols = lambda c: slice(c * FFN_FC, (c + 1) * FFN_FC)
    val_cols = lambda c: slice(D_FF + c * FFN_FC, D_FF + (c + 1) * FFN_FC)

    def conv(u_ext, cols):
        w = cw_ref[:, cols]
        u = u_ext[SUBLANES:]
        return (cb_ref[:, cols] + w[0:1] * _shift_rows(u, u_ext[:SUBLANES], 2)
                + w[1:2] * _shift_rows(u, u_ext[:SUBLANES], 1) + w[2:3] * u)

    up = lambda c: (_dot(h_ext, wup_ref[:, gate_cols(c)]), _dot(h_ext, wup_ref[:, val_cols(c)]))
    nxt = up(0)
    for c in range(n_chunks):
        ug, uv = nxt
        if c + 1 < n_chunks:
            nxt = up(c + 1)
        gt = conv(ug, gate_cols(c))
        act_ref[:, gate_cols(c)] = _bf(gt * jax.nn.sigmoid(gt) * conv(uv, val_cols(c)))
    obuf[slot] = (x + _dot(act_ref[...], wdn_ref[...])).reshape(q, SUBLANES, D)

    for cp in _tile_copies(o_hbm, obuf, out_sem, i, slot, False):
        cp.start()

    @pl.when(i == n_tiles - 1)
    def _():
        if n_tiles >= 2:
            for cp in _tile_copies(o_hbm, obuf, out_sem, i - 1, 1 - slot, False):
                cp.wait()
        for cp in _tile_copies(o_hbm, obuf, out_sem, i, slot, False):
            cp.wait()


def _ffn_call(x, p, l, *, nb):
    T, D = x.shape
    tm = FFN_TM
    S = T // nb
    q = tm // SUBLANES
    assert S % tm == 0 and q % SUBLANES == 0 and D_FF % FFN_FC == 0
    n_tiles = T // tm
    g = p["ffn_norm_g"][l].reshape(1, -1)
    wup, wdn = p["w_up"], p["w_down"]
    cw, cb = p["conv_w"][l], p["conv_b"][l].reshape(1, -1)
    full = lambda a: pl.BlockSpec(a.shape, lambda i: (0,) * a.ndim)
    out = pl.pallas_call(
        functools.partial(_ffn_kernel, tm=tm, tiles_per_batch=S // tm, n_tiles=n_tiles),
        grid=(n_tiles,),
        in_specs=[pl.BlockSpec(memory_space=pl.ANY), full(g), _layer_spec(wup, l), full(cw), full(cb),
                  _layer_spec(wdn, l)],
        out_specs=pl.BlockSpec(memory_space=pl.ANY),
        out_shape=jax.ShapeDtypeStruct((n_tiles, SUBLANES, q, D), F32),
        scratch_shapes=[pltpu.VMEM((2, q, SUBLANES, D), F32), pltpu.VMEM((2, q, SUBLANES, D), F32),
                        pltpu.VMEM((SUBLANES, D), F32), pltpu.VMEM((tm, D_FF), BF16),
                        pltpu.SemaphoreType.DMA((2, SUBLANES)), pltpu.SemaphoreType.DMA((2, SUBLANES))],
        compiler_params=pltpu.CompilerParams(dimension_semantics=("arbitrary",), vmem_limit_bytes=VMEM_LIMIT_BYTES),
        name=f"ffn_l{l}",
    )(x.reshape(n_tiles, SUBLANES, q, D), g, wup, cw, cb, wdn)
    return out.reshape(T, D)


_PARAM_NAMES = (
    "rel_bias", "mix_norm_g", "w_in", "w_in_vres", "attn_q_norm", "attn_k_norm", "attn_sinks", "rwkv_mu",
    "rwkv_mu_vres", "rwkv_w0", "rwkv_w2", "rwkv_a0", "rwkv_a2", "rwkv_v0", "rwkv_v2", "rwkv_g2", "rwkv_k_k",
    "rwkv_k_a", "rwkv_r_k", "rwkv_ln_w", "rwkv_ln_b", "mem_norm_g", "w_mem_kv", "mem_q_norm", "mem_k_norm",
    "w_out", "ffn_norm_g", "w_up", "conv_w", "conv_b", "w_down")


def kernel(x, mem, rel_bias, mix_norm_g, w_in, w_in_vres, attn_q_norm, attn_k_norm, attn_sinks, rwkv_mu,
           rwkv_mu_vres, rwkv_w0, rwkv_w2, rwkv_a0, rwkv_a2, rwkv_v0, rwkv_v2, rwkv_g2, rwkv_k_k, rwkv_k_a,
           rwkv_r_k, rwkv_ln_w, rwkv_ln_b, mem_norm_g, w_mem_kv, mem_q_norm, mem_k_norm, w_out, ffn_norm_g,
           w_up, conv_w, conv_b, w_down):
    p = dict(zip(_PARAM_NAMES, (
        rel_bias, mix_norm_g, w_in, w_in_vres, attn_q_norm, attn_k_norm, attn_sinks, rwkv_mu, rwkv_mu_vres,
        rwkv_w0, rwkv_w2, rwkv_a0, rwkv_a2, rwkv_v0, rwkv_v2, rwkv_g2, rwkv_k_k, rwkv_k_a, rwkv_r_k, rwkv_ln_w,
        rwkv_ln_b, mem_norm_g, w_mem_kv, mem_q_norm, mem_k_norm, w_out, ffn_norm_g, w_up, conv_w, conv_b, w_down)))
    for name in ("w_in", "w_mem_kv", "w_out", "w_up", "w_down"):
        p[name] = _bf(p[name])
    nb, S, D = x.shape
    xt = x.reshape(nb * S, D)
    mem2d = mem.reshape(nb * mem.shape[1], D)
    bias = _bias_call(rel_bias)
    v_first = None
    for l in range(w_in.shape[0]):
        qa, ka, va, qm, r, lw, k, v, a, b, gate = _proj_call(xt, p, l, v_first, nb=nb)
        if l == 0:
            v_first = v
        y = _rwkv_scan(r, lw, k, v, a, b, nb=nb)
        mk, mv = _memkv_call(mem2d, p, l)
        xt = _mix_call(xt, qa, ka, va, qm, mk, mv, y, r, k, v, gate, bias, p, l, nb=nb)
        xt = _ffn_call(xt, p, l, nb=nb)
    return xt.reshape(nb, S, D)
```

```python
import functools
import math

import jax
import jax.numpy as jnp
import numpy as np
from jax import lax
from jax.experimental import pallas as pl
from jax.experimental.pallas import tpu as pltpu

F32 = jnp.float32
BF16 = jnp.bfloat16

HEAD_DIM = 64
SCAN_CHUNK = 64
VMEM_LIMIT_BYTES = 56 * 1024 * 1024


def _dot(a, b):
    return jnp.dot(a, b, preferred_element_type=F32)


def _dot_nt(a, b):
    return lax.dot_general(a, b, (((1,), (1,)), ((), ())), preferred_element_type=F32)


def _dot_tn(a, b):
    return lax.dot_general(a, b, (((0,), (0,)), ((), ())), preferred_element_type=F32)


def _bf(x):
    return x.astype(BF16)


SCAN_GROUP_HEADS = 4
SCAN_GROUP_W = SCAN_GROUP_HEADS * HEAD_DIM
SCAN_CHUNKS_PER_STEP = 4


_SCAN_STASH = (("w2", 1, F32), ("y_v", 1, F32), ("m_rb", 1, F32), ("v", 1, F32), ("w1r", 2, F32),
               ("bk", 2, F32), ("g_row", 1, F32))


def _scan_kernel(r_ref, lw_ref, k_ref, v_ref, a_ref, b_ref, bdm_ref, y_ref, h_ref, *stash_refs, nb, nch):
    C, N, GW, HPG = SCAN_CHUNK, HEAD_DIM, SCAN_GROUP_W, SCAN_GROUP_HEADS
    ngroups = h_ref.shape[0]
    stash = {name: ref for (name, _, _), ref in zip(_SCAN_STASH, stash_refs)}

    @pl.when(pl.program_id(0) == 0)
    def _():
        h_ref[...] = jnp.zeros_like(h_ref)
        for ref in stash_refs:
            ref[...] = jnp.zeros_like(ref)

    trow = lax.broadcasted_iota(jnp.int32, (C, C), 0)
    tcol = lax.broadcasted_iota(jnp.int32, (C, C), 1)
    tri = (trow >= tcol).astype(BF16)
    grow = lax.broadcasted_iota(jnp.int32, (C, GW), 0)
    gcol = lax.broadcasted_iota(jnp.int32, (C, GW), 1) % N
    incl = grow >= gcol
    strict = grow > gcol
    eye = (grow == gcol).astype(F32)
    bdm = bdm_ref[...]

    def bd(x):
        return jnp.concatenate([_bf(x)] * HPG, axis=0) * bdm

    def split2(x):
        hi = _bf(x)
        return hi, _bf(x - hi.astype(F32))

    def chunk_units(c):
        rows = slice(c * C, (c + 1) * C)
        per_batch = []
        for b in range(nb):
            lw = lw_ref[b, rows, :]
            l1 = _bf(lw)
            e1 = lw - l1.astype(F32)
            l2 = _bf(e1)
            l3 = _bf(e1 - l2.astype(F32))
            cum = _dot(tri, l1) + _dot(tri, l2) + _dot(tri, l3)
            cum_last = cum[C - 1:C, :]
            g_inv = jnp.exp(-cum)
            g_out = jnp.exp(cum_last - cum)
            kk = k_ref[b, rows, :]
            bb = b_ref[b, rows, :]
            per_batch.append(dict(
                a_t=a_ref[b, rows, :] * jnp.exp(cum - lw), r_t=r_ref[b, rows, :] * jnp.exp(cum),
                b_t=bb * g_inv, k_t=kk * g_inv, b_h=bb * g_out, k_h=kk * g_out,
                g_c=jnp.broadcast_to(jnp.exp(cum_last), cum.shape), v=v_ref[b, rows, :]))
        cat = {n: jnp.concatenate([pb[n] for pb in per_batch], axis=1) for n in per_batch[0]}
        return [{n: t[:, g * GW:(g + 1) * GW] for n, t in cat.items()} for g in range(ngroups)]

    def prepare(chunks):
        units = [u for c in chunks for u in chunk_units(c)]

        nu = len(units)
        ar = [jnp.concatenate([_bf(u["a_t"]), _bf(u["r_t"])], axis=0) for u in units]
        pb = [_dot_nt(ar[i], bd(units[i]["b_t"])) for i in range(nu)]
        pk = [_dot_nt(ar[i], bd(units[i]["k_t"])) for i in range(nu)]
        l_ab = [jnp.where(strict, p[:C], 0.0) for p in pb]
        m_rb = [_bf(jnp.where(incl, p[C:], 0.0)) for p in pb]
        tril2 = jnp.concatenate([strict, incl], axis=0)
        lm_k = [_bf(jnp.where(tril2, p, 0.0)) for p in pk]
        yield
        t_inv = [eye + l for l in l_ab]
        pw = [_dot(_bf(l), bd(l)) for l in l_ab]
        yield
        n_rounds = int(math.log2(C)) - 1
        for rnd in range(n_rounds):
            if rnd + 1 < n_rounds:
                z = [_dot(jnp.concatenate([_bf(p), _bf(t)], axis=0), bd(p)) for p, t in zip(pw, t_inv)]
                pw = [zz[:C] for zz in z]
                t_inv = [t + zz[C:] for t, zz in zip(t_inv, z)]
            else:
                t_inv = [t + _dot(_bf(t), bd(p)) for t, p in zip(t_inv, pw)]
            yield
        t_b = [_bf(t) for t in t_inv]
        w1 = [_dot(t, bd(u["a_t"])) for t, u in zip(t_b, units)]
        lmv = [_dot(m, bd(u["v"])) for m, u in zip(lm_k, units)]
        yield
        w2 = [_dot(t, bd(x[:C])) for t, x in zip(t_b, lmv)]
        parts = [p for u in units for p in split2(u["g_c"] * eye)]
        sums = _dot(jnp.concatenate(parts, axis=0), bdm)
        yield
        for i, u in enumerate(units):
            put = lambda name, val: stash[name].__setitem__(i, val.astype(stash[name].dtype))
            put("w2", w2[i])
            put("y_v", lmv[i][C:])
            put("m_rb", m_rb[i])
            put("v", u["v"])
            put("w1r", jnp.concatenate([w1[i], u["r_t"]], axis=0))
            put("bk", jnp.concatenate([u["b_h"], u["k_h"]], axis=0))
            put("g_row", sums[2 * i * C:(2 * i + 1) * C] + sums[(2 * i + 1) * C:(2 * i + 2) * C])
        yield

    state = [h_ref[g] for g in range(ngroups)]

    def advance(c):
        p = {n: [ref[c * ngroups + g] for g in range(ngroups)] for n, ref in stash.items()}
        for n in ("m_rb", "v", "w1r", "bk"):
            p[n] = [_bf(x) for x in p[n]]
        z = [_dot(w, bd(s)) for w, s in zip(p["w1r"], state)]
        yield
        u = [zz[:C] + w for zz, w in zip(z, p["w2"])]
        y = [zz[C:] + _dot(m, bd(x)) + yv for zz, m, x, yv in zip(z, p["m_rb"], u, p["y_v"])]
        uv = [jnp.concatenate([_bf(x), v], axis=0) for x, v in zip(u, p["v"])]
        full = [_dot_tn(b, x) * bdm.astype(F32) for b, x in zip(p["bk"], uv)]
        yield
        upd = [sum(f[h * N:(h + 1) * N] for h in range(HPG)) for f in full]
        state[:] = [g * s + d for g, s, d in zip(p["g_row"], state, upd)]
        ycat = jnp.concatenate(y, axis=1)
        w = y_ref.shape[2]
        for b in range(nb):
            y_ref[b, c * C:(c + 1) * C, :] = ycat[:, b * w:(b + 1) * w]
        yield

    def run(gen, stages):
        for _ in range(stages):
            next(gen, None)

    prep = prepare(list(range(nch)))
    n_prep = 5 + int(math.log2(C)) - 1
    for c in range(nch):
        seq = advance(c)
        run(prep, 1)
        run(seq, 1)
        run(prep, 1)
        run(seq, 2)
    run(prep, n_prep)
    for g in range(ngroups):
        h_ref[g] = state[g]


def _rwkv_scan(r, lw, k, v, a, b, *, nb):
    T, W = r.shape
    S = T // nb
    C, nch, GW = SCAN_CHUNK, SCAN_CHUNKS_PER_STEP, SCAN_GROUP_W
    assert S % (C * nch) == 0 and (nb * W) % GW == 0 and W % 128 == 0
    ngroups = nb * W // GW
    head = np.arange(GW) // HEAD_DIM
    bdm = jnp.asarray((head[:, None] == head[None, :]).astype(np.float32), dtype=BF16)
    n_blocks = S // (C * nch)
    in_spec = pl.BlockSpec((nb, C * nch, W), lambda s: (0, jnp.minimum(s, n_blocks - 1), 0))
    out_spec = pl.BlockSpec((nb, C * nch, W), lambda s: (0, jnp.maximum(s - 1, 0), 0))
    args = [t.reshape(nb, S, W) for t in (r, lw, k, v, a, b)]
    y = pl.pallas_call(
        functools.partial(_scan_kernel, nb=nb, nch=nch),
        grid=(n_blocks + 1,),
        in_specs=[in_spec] * 6 + [pl.BlockSpec((GW, GW), lambda s: (0, 0))],
        out_specs=out_spec,
        out_shape=jax.ShapeDtypeStruct((nb, S, W), F32),
        scratch_shapes=[pltpu.VMEM((ngroups, HEAD_DIM, GW), F32)]
        + [pltpu.VMEM((nch * ngroups, rows * C, GW), dt) for _, rows, dt in _SCAN_STASH],
        compiler_params=pltpu.CompilerParams(dimension_semantics=("arbitrary",), vmem_limit_bytes=VMEM_LIMIT_BYTES),
        name="rwkv_scan",
    )(*args, bdm)
    return y.reshape(T, W)


D_MODEL = 1024
ATTN_HEADS = 6
ATTN_KV_HEADS = 2
ATTN_GROUP = ATTN_HEADS // ATTN_KV_HEADS
ATTN_W = ATTN_HEADS * HEAD_DIM
KV_W = ATTN_KV_HEADS * HEAD_DIM
BLOCK_Q = 128
WINDOW = 128
N_BUCKETS = 32
MAX_EXACT = N_BUCKETS // 2
RWKV_W = 6 * HEAD_DIM
LORA_W = 128
SHIFT_W = 3 * RWKV_W + LORA_W
MEM_HEADS = 4
MEM_W = MEM_HEADS * HEAD_DIM
IN_BASE = ATTN_W + 2 * KV_W + SHIFT_W + MEM_W
PB_OFF = ATTN_W + 2 * KV_W
QM_OFF = PB_OFF + SHIFT_W
D_FF = 2816
EPS = 1e-6
GN_EPS = 64e-5
L2_EPS = 1e-12
MXU_WIDTH = 256

PROJ_TM = 1024
PROJ_SUB = 256
MIX_TQ = 512
FFN_TM = 512
FFN_FC = 256


def _block_diag_ones(width):
    idx = np.arange(width) // HEAD_DIM
    return jnp.asarray((idx[:, None] == idx[None, :]).astype(np.float32), dtype=BF16)


def _head_sum(t, bd):
    return _dot(_bf(t), bd)


def _head_rms(t, bd, gain):
    ms = _head_sum(t * t, bd) * (1.0 / HEAD_DIM)
    return t * lax.rsqrt(ms + EPS) * gain


def _rms_rows(x, g):
    ms = jnp.mean(x * x, axis=-1, keepdims=True)
    return x * lax.rsqrt(ms + EPS) * g


def _proj_kernel(*refs, tm, sub, tiles_per_batch, has_vres):
    it = iter(refs)
    x_ref, g_ref, w_ref = next(it), next(it), next(it)
    qg_ref, kg_ref, mg_ref, mu_ref = next(it), next(it), next(it), next(it)
    w0_ref, a0_ref, kk_ref, ka_ref = next(it), next(it), next(it), next(it)
    w2_ref, a2_ref, g2_ref = next(it), next(it), next(it)
    bd384_ref, bd128_ref, bd256_ref = next(it), next(it), next(it)
    if has_vres:
        vfirst_ref, v0_ref, v2_ref = next(it), next(it), next(it)
    qa_ref, ka_out_ref, va_ref, qm_ref = next(it), next(it), next(it), next(it)
    r_ref, lw_ref, k_ref, v_ref, a_ref, b_ref, gate_ref = (next(it) for _ in range(7))
    pbs_ref = next(it)
    sw = pbs_ref.shape[1]

    @pl.when(pl.program_id(0) % tiles_per_batch == 0)
    def _():
        pbs_ref[0:8, :] = jnp.zeros((8, sw), F32)

    def project(i):
        rows = slice(i * sub, (i + 1) * sub)
        return _dot(_bf(_rms_rows(x_ref[rows, :], g_ref[...])), w_ref[...])

    def attention_outputs(i, proj):
        rows = slice(i * sub, (i + 1) * sub)
        qa_ref[rows, :] = _bf(_head_rms(proj[:, :ATTN_W], bd384_ref[...], qg_ref[...]))
        ka_out_ref[rows, :] = _bf(_head_rms(proj[:, ATTN_W:ATTN_W + KV_W], bd128_ref[...], kg_ref[...]))
        va_ref[rows, :] = _bf(proj[:, ATTN_W + KV_W:PB_OFF])
        qm_ref[rows, :] = _bf(_head_rms(proj[:, QM_OFF:QM_OFF + MEM_W], bd256_ref[...], mg_ref[...]))

    def rwkv_outputs(i, proj):
        rows = slice(i * sub, (i + 1) * sub)
        lo = 8 + i * sub
        pbs_ref[lo:lo + sub, 0:SHIFT_W] = proj[:, PB_OFF:PB_OFF + SHIFT_W]
        if has_vres:
            pbs_ref[lo:lo + sub, SHIFT_W:sw] = proj[:, IN_BASE:IN_BASE + LORA_W]
        cur = pbs_ref[lo:lo + sub, :]
        prev = pbs_ref[lo - 1:lo - 1 + sub, :]
        sh = cur + mu_ref[...] * (prev - cur)
        r = sh[:, 0:RWKV_W]
        k = sh[:, RWKV_W:2 * RWKV_W]
        v = sh[:, 2 * RWKV_W:3 * RWKV_W]
        z = sh[:, 3 * RWKV_W:SHIFT_W]
        t = w0_ref[...] + _dot(_bf(jnp.tanh(z)), w2_ref[...])
        lw_ref[rows, :] = -math.exp(-0.5) * jax.nn.sigmoid(t)
        a = jax.nn.sigmoid(a0_ref[...] + _dot(_bf(z), a2_ref[...]))
        gate_ref[rows, :] = _dot(_bf(jax.nn.sigmoid(z)), g2_ref[...])
        if has_vres:
            vd = sh[:, SHIFT_W:sw]
            v = v + (vfirst_ref[rows, :] - v) * jax.nn.sigmoid(v0_ref[...] + _dot(_bf(vd), v2_ref[...]))
        kk = k * kk_ref[...]
        kk = kk / jnp.maximum(jnp.sqrt(_head_sum(kk * kk, bd384_ref[...])), L2_EPS)
        r_ref[rows, :] = r
        k_ref[rows, :] = k * (1.0 + (a - 1.0) * ka_ref[...])
        v_ref[rows, :] = v
        a_ref[rows, :] = -kk
        b_ref[rows, :] = kk * a

    n_sub = tm // sub
    proj = project(0)
    for i in range(n_sub):
        attention_outputs(i, proj)
        nxt = project(i + 1) if i + 1 < n_sub else None
        rwkv_outputs(i, proj)
        proj = nxt
    pbs_ref[0:8, :] = pbs_ref[tm:tm + 8, :]


def _layer_spec(stacked, l):
    return pl.BlockSpec((None,) + stacked.shape[1:], lambda i: (l, 0, 0), pipeline_mode=pl.Buffered(1))


def _pad_rows(w, rows, at):
    out = jnp.zeros((rows, w.shape[1]), w.dtype)
    return lax.dynamic_update_slice(out, w, (at, 0))


def _proj_call(x, p, l, v_first, *, nb):
    T, D = x.shape
    tm = PROJ_TM
    assert T % tm == 0 and (T // nb) % tm == 0
    has_vres = l > 0
    row = lambda a: a.reshape(1, -1).astype(F32)
    tile6 = lambda a: jnp.tile(a, ATTN_HEADS)
    scale = HEAD_DIM ** -0.5
    mu = p["rwkv_mu"][l]
    if has_vres:
        mu = jnp.concatenate([mu, p["rwkv_mu_vres"][l - 1], jnp.zeros((LORA_W - 16,), F32)])
    w = p["w_in"]
    if has_vres:
        w = jnp.concatenate([w[l], _bf(jnp.pad(p["w_in_vres"][l - 1], ((0, 0), (0, LORA_W - 16))))], axis=1)
    ins = [x, row(p["mix_norm_g"][l]), w]
    ins += [row(tile6(p["attn_q_norm"][l]) * scale), row(jnp.tile(p["attn_k_norm"][l], ATTN_KV_HEADS)),
            row(jnp.tile(p["mem_q_norm"][l], MEM_HEADS) * scale), row(mu),
            row(p["rwkv_w0"][l]), row(p["rwkv_a0"][l]), row(p["rwkv_k_k"][l]), row(p["rwkv_k_a"][l]),
            _bf(_pad_rows(p["rwkv_w2"][l], LORA_W, 0)), _bf(_pad_rows(p["rwkv_a2"][l], LORA_W, 32)),
            _bf(_pad_rows(p["rwkv_g2"][l], LORA_W, 64)),
            _block_diag_ones(ATTN_W), _block_diag_ones(KV_W), _block_diag_ones(MEM_W)]
    if has_vres:
        ins += [v_first, row(p["rwkv_v0"][l - 1]), _bf(_pad_rows(p["rwkv_v2"][l - 1], LORA_W, 0))]

    def spec(a):
        if a.ndim == 3:
            return _layer_spec(a, l)
        if a.shape[0] == T:
            return pl.BlockSpec((tm, a.shape[1]), lambda i: (i, 0))
        return pl.BlockSpec(a.shape, lambda i: (0, 0), pipeline_mode=pl.Buffered(1))

    out_shapes = [jax.ShapeDtypeStruct((T, ATTN_W), BF16), jax.ShapeDtypeStruct((T, KV_W), BF16),
                  jax.ShapeDtypeStruct((T, KV_W), BF16), jax.ShapeDtypeStruct((T, MEM_W), BF16)]
    out_shapes += [jax.ShapeDtypeStruct((T, RWKV_W), F32)] * 7
    sw = SHIFT_W + (LORA_W if has_vres else 0)
    return pl.pallas_call(
        functools.partial(_proj_kernel, tm=tm, sub=PROJ_SUB, tiles_per_batch=(T // nb) // tm, has_vres=has_vres),
        grid=(T // tm,),
        in_specs=[spec(a) for a in ins],
        out_specs=[pl.BlockSpec((tm, s.shape[1]), lambda i: (i, 0)) for s in out_shapes],
        out_shape=out_shapes,
        scratch_shapes=[pltpu.VMEM((tm + 8, sw), F32)],
        compiler_params=pltpu.CompilerParams(dimension_semantics=("arbitrary",), vmem_limit_bytes=VMEM_LIMIT_BYTES),
        name=f"proj_l{l}",
    )(*ins)


def _bucket_table():
    qi = np.arange(BLOCK_Q)[:, None]
    kj = np.arange(2 * BLOCK_Q)[None, :]
    dist = qi + BLOCK_Q - kj
    in_band = (dist >= 0) & (dist < WINDOW)
    d = np.maximum(dist, 1).astype(np.float32)
    large = MAX_EXACT + (np.log(d / np.float32(MAX_EXACT)) / np.float32(math.log(WINDOW / MAX_EXACT))
                         * np.float32(N_BUCKETS - MAX_EXACT)).astype(np.int32)
    large = np.minimum(large, N_BUCKETS - 1)
    bucket = np.where(dist < MAX_EXACT, np.maximum(dist, 0), large)
    return np.where(in_band, bucket, -1).astype(np.int32)


def _bias_kernel(rb_ref, bucket_ref, out_ref):
    bucket = bucket_ref[...]
    for h in range(ATTN_HEADS):
        acc = jnp.full(bucket.shape, -jnp.inf, F32)
        for j in range(N_BUCKETS):
            acc = jnp.where(bucket == j, rb_ref[j, h], acc)
        out_ref[h] = acc


def _bias_call(rel_bias):
    tab = pl.pallas_call(
        _bias_kernel,
        in_specs=[pl.BlockSpec(memory_space=pltpu.SMEM), pl.BlockSpec(memory_space=pltpu.VMEM)],
        out_specs=pl.BlockSpec(memory_space=pltpu.VMEM),
        out_shape=jax.ShapeDtypeStruct((ATTN_HEADS, BLOCK_Q, 2 * BLOCK_Q), F32),
        name="rel_bias_table",
    )(rel_bias.astype(F32), jnp.asarray(_bucket_table()))
    return tab.reshape(ATTN_KV_HEADS, ATTN_GROUP * BLOCK_Q, 2 * BLOCK_Q)


def _memkv_kernel(mem_ref, g_ref, w_ref, kg_ref, bd_ref, mk_ref, mv_ref):
    hn = _bf(_rms_rows(mem_ref[...], g_ref[...]))
    kv = _dot(hn, w_ref[...])
    mk_ref[...] = _bf(_head_rms(kv[:, :MEM_W], bd_ref[...], kg_ref[...]))
    mv_ref[...] = _bf(kv[:, MEM_W:])


def _memkv_call(mem2d, p, l):
    rows = mem2d.shape[0]
    vm = pl.BlockSpec(memory_space=pltpu.VMEM)
    return pl.pallas_call(
        _memkv_kernel,
        in_specs=[vm] * 5,
        out_specs=[vm, vm],
        out_shape=[jax.ShapeDtypeStruct((rows, MEM_W), BF16)] * 2,
        compiler_params=pltpu.CompilerParams(vmem_limit_bytes=VMEM_LIMIT_BYTES),
        name=f"mem_kv_l{l}",
    )(mem2d, p["mem_norm_g"][l].reshape(1, -1), _bf(p["w_mem_kv"][l]),
      jnp.tile(p["mem_k_norm"][l], MEM_HEADS).reshape(1, -1), _block_diag_ones(MEM_W))


def _mix_kernel(x_ref, qa_ref, kc_ref, kp_ref, vc_ref, vp_ref, bias_ref, sink_ref, qm_ref, mk_ref, mv_ref,
                y_ref, r_ref, k_ref, v_ref, gate_ref, lnw_ref, lnb_ref, rk_ref, bd_ref, wout_ref, out_ref,
                *, tq, tiles_per_batch):
    N, BQ = HEAD_DIM, BLOCK_Q
    seq_start = pl.program_id(0) % tiles_per_batch == 0
    nqb = tq // BQ
    qa = qa_ref[...]
    kall = jnp.concatenate([kp_ref[...], kc_ref[...]], axis=0)
    vall = jnp.concatenate([vp_ref[...], vc_ref[...]], axis=0)
    before_seq = lax.broadcasted_iota(jnp.int32, (ATTN_GROUP * BQ, 2 * BQ), 1) < BQ

    qm = qm_ref[...]
    mk = mk_ref[0]
    mv = mv_ref[0]

    swa = [(j, g) for j in range(nqb) for g in range(ATTN_KV_HEADS)]
    logits, values, sinks = [], [], []
    for j, g in swa:
        qg = jnp.concatenate(
            [qa[j * BQ:(j + 1) * BQ, (ATTN_GROUP * g + i) * N:(ATTN_GROUP * g + i + 1) * N] for i in range(ATTN_GROUP)],
            axis=0)
        lg = _dot_nt(qg, kall[j * BQ:(j + 2) * BQ, g * N:(g + 1) * N]) + bias_ref[g]
        if j == 0:
            lg = jnp.where(jnp.logical_and(seq_start, before_seq), -jnp.inf, lg)
        logits.append(lg)
        values.append(vall[j * BQ:(j + 2) * BQ, g * N:(g + 1) * N])
        sinks.append(sink_ref[g])
    for h in range(MEM_HEADS):
        hs = slice(h * N, (h + 1) * N)
        logits.append(_dot_nt(qm[:, hs], mk[:, hs]))
        values.append(mv[:, hs])
        sinks.append(None)
    row_max = [jnp.max(lg, axis=-1, keepdims=True) for lg in logits]
    m = [rm if s is None else jnp.maximum(rm, s) for rm, s in zip(row_max, sinks)]
    e = [jnp.exp(lg - mm) for lg, mm in zip(logits, m)]
    denom = [jnp.sum(ee, axis=-1, keepdims=True) for ee in e]
    denom = [d if s is None else d + jnp.exp(s - mm) for d, s, mm in zip(denom, sinks, m)]
    outs = [_dot(_bf(ee), vv) / d for ee, vv, d in zip(e, values, denom)]

    head_rows = [[None] * nqb for _ in range(ATTN_HEADS)]
    for (j, g), o in zip(swa, outs):
        for i in range(ATTN_GROUP):
            head_rows[ATTN_GROUP * g + i][j] = o[i * BQ:(i + 1) * BQ]
    out_a = jnp.concatenate([jnp.concatenate(rows, axis=0) for rows in head_rows], axis=-1)
    out_m = jnp.concatenate(outs[len(swa):], axis=-1)

    bd = bd_ref[...]
    y = y_ref[...]
    d = y - _head_sum(y, bd) * (1.0 / N)
    var = _head_sum(d * d, bd) * (1.0 / N)
    yn = d * lax.rsqrt(var + GN_EPS) * lnw_ref[...] + lnb_ref[...]
    bonus = _head_sum(r_ref[...] * k_ref[...] * rk_ref[...], bd) * v_ref[...]
    out_b = (yn + bonus) * gate_ref[...]

    mixed = _bf(jnp.concatenate([out_a, out_b, out_m], axis=-1))
    out_ref[...] = x_ref[...] + _dot(mixed, wout_ref[...])


def _mix_call(x, qa, ka, va, qm, mk, mv, y, r, k, v, gate, bias, p, l, *, nb):
    T, D = x.shape
    tq = MIX_TQ
    S = T // nb
    assert S % tq == 0 and tq % BLOCK_Q == 0
    tpb = S // tq
    qpb = tq // BLOCK_Q
    mem_tokens = mk.shape[0] // nb
    row = lambda a: a.reshape(1, -1).astype(F32)
    sink = jnp.repeat(p["attn_sinks"][l].astype(F32), BLOCK_Q).reshape(ATTN_KV_HEADS, ATTN_GROUP * BLOCK_Q, 1)
    tile = lambda w: pl.BlockSpec((tq, w), lambda i: (i, 0))
    prev = pl.BlockSpec((BLOCK_Q, KV_W), lambda i: (jnp.maximum(i * qpb - 1, 0), 0))
    full = lambda a: pl.BlockSpec(a.shape, lambda i: (0,) * a.ndim)
    memspec = pl.BlockSpec((1, mem_tokens, MEM_W), lambda i: (i // tpb, 0, 0))
    lnw, lnb, rk = row(p["rwkv_ln_w"][l]), row(p["rwkv_ln_b"][l]), row(p["rwkv_r_k"][l])
    bd = _block_diag_ones(RWKV_W)
    wout = p["w_out"]
    mk3 = mk.reshape(nb, mem_tokens, MEM_W)
    mv3 = mv.reshape(nb, mem_tokens, MEM_W)
    return pl.pallas_call(
        functools.partial(_mix_kernel, tq=tq, tiles_per_batch=tpb),
        grid=(T // tq,),
        in_specs=[tile(D), tile(ATTN_W), tile(KV_W), prev, tile(KV_W), prev, full(bias), full(sink), tile(MEM_W),
                  memspec, memspec] + [tile(RWKV_W)] * 5
                 + [full(lnw), full(lnb), full(rk), full(bd), _layer_spec(wout, l)],
        out_specs=tile(D),
        out_shape=jax.ShapeDtypeStruct((T, D), F32),
        compiler_params=pltpu.CompilerParams(dimension_semantics=("arbitrary",), vmem_limit_bytes=VMEM_LIMIT_BYTES),
        name=f"mix_l{l}",
    )(x, qa, ka, ka, va, va, bias, sink, qm, mk3, mv3, y, r, k, v, gate, lnw, lnb, rk, bd, wout)


SUBLANES = 8


def _tile_copies(hbm, buf, sem, tile, slot, to_vmem):
    copies = []
    for s in range(SUBLANES):
        src, dst = hbm.at[tile, s], buf.at[slot, :, s, :]
        if not to_vmem:
            src, dst = dst, src
        copies.append(pltpu.make_async_copy(src, dst, sem.at[slot, s]))
    return copies


def _shift_rows(u, before, steps):
    first = lax.broadcasted_iota(jnp.int32, (SUBLANES, u.shape[1]), 0) == 0
    tm = u.shape[0]
    heads = []
    for i in range(steps):
        lo = tm - (steps - i) * SUBLANES
        wrapped = pltpu.roll(u[lo:lo + SUBLANES], 1, axis=0)
        prior = jnp.broadcast_to(before[SUBLANES - steps + i:SUBLANES - steps + i + 1], wrapped.shape)
        heads.append(jnp.where(first, prior, wrapped))
    return jnp.concatenate(heads + [u[:tm - steps * SUBLANES]], axis=0)


def _ffn_kernel(x_hbm, g_ref, wup_ref, cw_ref, cb_ref, wdn_ref, o_hbm, xbuf, obuf, carry_ref, act_ref,
                in_sem, out_sem, *, tm, tiles_per_batch, n_tiles):
    D = g_ref.shape[1]
    q = tm // SUBLANES
    i = pl.program_id(0)
    slot = i % 2

    @pl.when(i == 0)
    def _():
        carry_ref[...] = jnp.zeros_like(carry_ref)
        for cp in _tile_copies(x_hbm, xbuf, in_sem, 0, 0, True):
            cp.start()

    @pl.when(i + 1 < n_tiles)
    def _():
        for cp in _tile_copies(x_hbm, xbuf, in_sem, i + 1, 1 - slot, True):
            cp.start()

    for cp in _tile_copies(x_hbm, xbuf, in_sem, i, slot, True):
        cp.wait()

    @pl.when(i >= 2)
    def _():
        for cp in _tile_copies(o_hbm, obuf, out_sem, i - 2, slot, False):
            cp.wait()

    seq_start = i % tiles_per_batch == 0
    g = g_ref[...]
    x = xbuf[slot].reshape(tm, D)
    before = jnp.where(seq_start, 0.0, _rms_rows(carry_ref[...], g))
    for r, grp in ((SUBLANES - 2, q - 2), (SUBLANES - 1, q - 1)):
        row = grp * SUBLANES + SUBLANES - 1
        carry_ref[r:r + 1, :] = x[row:row + 1]
    h_ext = _bf(jnp.concatenate([before, _rms_rows(x, g)], axis=0))

    n_chunks = D_FF // FFN_FC
    gate_cols = lambda c: slice(c * FFN_FC, (c + 1) * FFN_FC)
    val_cols = lambda c: slice(D_FF + c * FFN_FC, D_FF + (c + 1) * FFN_FC)

    def conv(u_ext, cols):
        w = cw_ref[:, cols]
        u = u_ext[SUBLANES:]
        return (cb_ref[:, cols] + w[0:1] * _shift_rows(u, u_ext[:SUBLANES], 2)
                + w[1:2] * _shift_rows(u, u_ext[:SUBLANES], 1) + w[2:3] * u)

    up = lambda c: (_dot(h_ext, wup_ref[:, gate_cols(c)]), _dot(h_ext, wup_ref[:, val_cols(c)]))
    nxt = up(0)
    for c in range(n_chunks):
        ug, uv = nxt
        if c + 1 < n_chunks:
            nxt = up(c + 1)
        gt = conv(ug, gate_cols(c))
        act_ref[:, gate_cols(c)] = _bf(gt * jax.nn.sigmoid(gt) * conv(uv, val_cols(c)))
    obuf[slot] = (x + _dot(act_ref[...], wdn_ref[...])).reshape(q, SUBLANES, D)

    for cp in _tile_copies(o_hbm, obuf, out_sem, i, slot, False):
        cp.start()

    @pl.when(i == n_tiles - 1)
    def _():
        if n_tiles >= 2:
            for cp in _tile_copies(o_hbm, obuf, out_sem, i - 1, 1 - slot, False):
                cp.wait()
        for cp in _tile_copies(o_hbm, obuf, out_sem, i, slot, False):
            cp.wait()


def _ffn_call(x, p, l, *, nb):
    T, D = x.shape
    tm = FFN_TM
    S = T // nb
    q = tm // SUBLANES
    assert S % tm == 0 and q % SUBLANES == 0 and D_FF % FFN_FC == 0
    n_tiles = T // tm
    g = p["ffn_norm_g"][l].reshape(1, -1)
    wup, wdn = p["w_up"], p["w_down"]
    cw, cb = p["conv_w"][l], p["conv_b"][l].reshape(1, -1)
    full = lambda a: pl.BlockSpec(a.shape, lambda i: (0,) * a.ndim)
    out = pl.pallas_call(
        functools.partial(_ffn_kernel, tm=tm, tiles_per_batch=S // tm, n_tiles=n_tiles),
        grid=(n_tiles,),
        in_specs=[pl.BlockSpec(memory_space=pl.ANY), full(g), _layer_spec(wup, l), full(cw), full(cb),
                  _layer_spec(wdn, l)],
        out_specs=pl.BlockSpec(memory_space=pl.ANY),
        out_shape=jax.ShapeDtypeStruct((n_tiles, SUBLANES, q, D), F32),
        scratch_shapes=[pltpu.VMEM((2, q, SUBLANES, D), F32), pltpu.VMEM((2, q, SUBLANES, D), F32),
                        pltpu.VMEM((SUBLANES, D), F32), pltpu.VMEM((tm, D_FF), BF16),
                        pltpu.SemaphoreType.DMA((2, SUBLANES)), pltpu.SemaphoreType.DMA((2, SUBLANES))],
        compiler_params=pltpu.CompilerParams(dimension_semantics=("arbitrary",), vmem_limit_bytes=VMEM_LIMIT_BYTES),
        name=f"ffn_l{l}",
    )(x.reshape(n_tiles, SUBLANES, q, D), g, wup, cw, cb, wdn)
    return out.reshape(T, D)


_PARAM_NAMES = (
    "rel_bias", "mix_norm_g", "w_in", "w_in_vres", "attn_q_norm", "attn_k_norm", "attn_sinks", "rwkv_mu",
    "rwkv_mu_vres", "rwkv_w0", "rwkv_w2", "rwkv_a0", "rwkv_a2", "rwkv_v0", "rwkv_v2", "rwkv_g2", "rwkv_k_k",
    "rwkv_k_a", "rwkv_r_k", "rwkv_ln_w", "rwkv_ln_b", "mem_norm_g", "w_mem_kv", "mem_q_norm", "mem_k_norm",
    "w_out", "ffn_norm_g", "w_up", "conv_w", "conv_b", "w_down")


def kernel(x, mem, rel_bias, mix_norm_g, w_in, w_in_vres, attn_q_norm, attn_k_norm, attn_sinks, rwkv_mu,
           rwkv_mu_vres, rwkv_w0, rwkv_w2, rwkv_a0, rwkv_a2, rwkv_v0, rwkv_v2, rwkv_g2, rwkv_k_k, rwkv_k_a,
           rwkv_r_k, rwkv_ln_w, rwkv_ln_b, mem_norm_g, w_mem_kv, mem_q_norm, mem_k_norm, w_out, ffn_norm_g,
           w_up, conv_w, conv_b, w_down):
    p = dict(zip(_PARAM_NAMES, (
        rel_bias, mix_norm_g, w_in, w_in_vres, attn_q_norm, attn_k_norm, attn_sinks, rwkv_mu, rwkv_mu_vres,
        rwkv_w0, rwkv_w2, rwkv_a0, rwkv_a2, rwkv_v0, rwkv_v2, rwkv_g2, rwkv_k_k, rwkv_k_a, rwkv_r_k, rwkv_ln_w,
        rwkv_ln_b, mem_norm_g, w_mem_kv, mem_q_norm, mem_k_norm, w_out, ffn_norm_g, w_up, conv_w, conv_b, w_down)))
    for name in ("w_in", "w_mem_kv", "w_out", "w_up", "w_down"):
        p[name] = _bf(p[name])
    nb, S, D = x.shape
    xt = x.reshape(nb * S, D)
    mem2d = mem.reshape(nb * mem.shape[1], D)
    bias = _bias_call(rel_bias)
    v_first = None
    for l in range(w_in.shape[0]):
        qa, ka, va, qm, r, lw, k, v, a, b, gate = _proj_call(xt, p, l, v_first, nb=nb)
        if l == 0:
            v_first = v
        y = _rwkv_scan(r, lw, k, v, a, b, nb=nb)
        mk, mv = _memkv_call(mem2d, p, l)
        xt = _mix_call(xt, qa, ka, va, qm, mk, mv, y, r, k, v, gate, bias, p, l, nb=nb)
        xt = _ffn_call(xt, p, l, nb=nb)
    return xt.reshape(nb, S, D)
```

```python
import functools
import math

import jax
import jax.numpy as jnp
import numpy as np
from jax import lax
from jax.experimental import pallas as pl
from jax.experimental.pallas import tpu as pltpu

F32 = jnp.float32
BF16 = jnp.bfloat16

HEAD_DIM = 64
SCAN_CHUNK = 64
VMEM_LIMIT_BYTES = 56 * 1024 * 1024


def _dot(a, b):
    return jnp.dot(a, b, preferred_element_type=F32)


def _dot_nt(a, b):
    return lax.dot_general(a, b, (((1,), (1,)), ((), ())), preferred_element_type=F32)


def _dot_tn(a, b):
    return lax.dot_general(a, b, (((0,), (0,)), ((), ())), preferred_element_type=F32)


def _bf(x):
    return x.astype(BF16)


SCAN_GROUP_HEADS = 4
SCAN_GROUP_W = SCAN_GROUP_HEADS * HEAD_DIM
SCAN_CHUNKS_PER_STEP = 4


_SCAN_STASH = (("w2", 1, F32), ("y_v", 1, F32), ("m_rb", 1, F32), ("v", 1, F32), ("w1r", 2, F32),
               ("bk", 2, F32), ("g_row", 1, F32))


def _scan_kernel(r_ref, lw_ref, k_ref, v_ref, a_ref, b_ref, bdm_ref, y_ref, h_ref, *stash_refs, nb, nch):
    C, N, GW, HPG = SCAN_CHUNK, HEAD_DIM, SCAN_GROUP_W, SCAN_GROUP_HEADS
    ngroups = h_ref.shape[0]
    stash = {name: ref for (name, _, _), ref in zip(_SCAN_STASH, stash_refs)}

    @pl.when(pl.program_id(0) == 0)
    def _():
        h_ref[...] = jnp.zeros_like(h_ref)
        for ref in stash_refs:
            ref[...] = jnp.zeros_like(ref)

    trow = lax.broadcasted_iota(jnp.int32, (C, C), 0)
    tcol = lax.broadcasted_iota(jnp.int32, (C, C), 1)
    tri = (trow >= tcol).astype(BF16)
    grow = lax.broadcasted_iota(jnp.int32, (C, GW), 0)
    gcol = lax.broadcasted_iota(jnp.int32, (C, GW), 1) % N
    incl = grow >= gcol
    strict = grow > gcol
    eye = (grow == gcol).astype(F32)
    bdm = bdm_ref[...]

    def bd(x):
        return jnp.concatenate([_bf(x)] * HPG, axis=0) * bdm

    def split2(x):
        hi = _bf(x)
        return hi, _bf(x - hi.astype(F32))

    def chunk_units(c):
        rows = slice(c * C, (c + 1) * C)
        per_batch = []
        for b in range(nb):
            lw = lw_ref[b, rows, :]
            l1 = _bf(lw)
            e1 = lw - l1.astype(F32)
            l2 = _bf(e1)
            l3 = _bf(e1 - l2.astype(F32))
            cum = _dot(tri, l1) + _dot(tri, l2) + _dot(tri, l3)
            cum_last = cum[C - 1:C, :]
            g_inv = jnp.exp(-cum)
            g_out = jnp.exp(cum_last - cum)
            kk = k_ref[b, rows, :]
            bb = b_ref[b, rows, :]
            per_batch.append(dict(
                a_t=a_ref[b, rows, :] * jnp.exp(cum - lw), r_t=r_ref[b, rows, :] * jnp.exp(cum),
                b_t=bb * g_inv, k_t=kk * g_inv, b_h=bb * g_out, k_h=kk * g_out,
                g_c=jnp.broadcast_to(jnp.exp(cum_last), cum.shape), v=v_ref[b, rows, :]))
        cat = {n: jnp.concatenate([pb[n] for pb in per_batch], axis=1) for n in per_batch[0]}
        return [{n: t[:, g * GW:(g + 1) * GW] for n, t in cat.items()} for g in range(ngroups)]

    def prepare(chunks):
        units = [u for c in chunks for u in chunk_units(c)]

        nu = len(units)
        ar = [jnp.concatenate([_bf(u["a_t"]), _bf(u["r_t"])], axis=0) for u in units]
        pb = [_dot_nt(ar[i], bd(units[i]["b_t"])) for i in range(nu)]
        pk = [_dot_nt(ar[i], bd(units[i]["k_t"])) for i in range(nu)]
        l_ab = [jnp.where(strict, p[:C], 0.0) for p in pb]
        m_rb = [_bf(jnp.where(incl, p[C:], 0.0)) for p in pb]
        tril2 = jnp.concatenate([strict, incl], axis=0)
        lm_k = [_bf(jnp.where(tril2, p, 0.0)) for p in pk]
        yield
        t_inv = [eye + l for l in l_ab]
        pw = [_dot(_bf(l), bd(l)) for l in l_ab]
        yield
        n_rounds = int(math.log2(C)) - 1
        for rnd in range(n_rounds):
            if rnd + 1 < n_rounds:
                z = [_dot(jnp.concatenate([_bf(p), _bf(t)], axis=0), bd(p)) for p, t in zip(pw, t_inv)]
                pw = [zz[:C] for zz in z]
                t_inv = [t + zz[C:] for t, zz in zip(t_inv, z)]
            else:
                t_inv = [t + _dot(_bf(t), bd(p)) for t, p in zip(t_inv, pw)]
            yield
        t_b = [_bf(t) for t in t_inv]
        w1 = [_dot(t, bd(u["a_t"])) for t, u in zip(t_b, units)]
        lmv = [_dot(m, bd(u["v"])) for m, u in zip(lm_k, units)]
        yield
        w2 = [_dot(t, bd(x[:C])) for t, x in zip(t_b, lmv)]
        parts = [p for u in units for p in split2(u["g_c"] * eye)]
        sums = _dot(jnp.concatenate(parts, axis=0), bdm)
        yield
        for i, u in enumerate(units):
            put = lambda name, val: stash[name].__setitem__(i, val.astype(stash[name].dtype))
            put("w2", w2[i])
            put("y_v", lmv[i][C:])
            put("m_rb", m_rb[i])
            put("v", u["v"])
            put("w1r", jnp.concatenate([w1[i], u["r_t"]], axis=0))
            put("bk", jnp.concatenate([u["b_h"], u["k_h"]], axis=0))
            put("g_row", sums[2 * i * C:(2 * i + 1) * C] + sums[(2 * i + 1) * C:(2 * i + 2) * C])
        yield

    state = [h_ref[g] for g in range(ngroups)]

    def advance(c):
        p = {n: [ref[c * ngroups + g] for g in range(ngroups)] for n, ref in stash.items()}
        for n in ("m_rb", "v", "w1r", "bk"):
            p[n] = [_bf(x) for x in p[n]]
        z = [_dot(w, bd(s)) for w, s in zip(p["w1r"], state)]
        yield
        u = [zz[:C] + w for zz, w in zip(z, p["w2"])]
        y = [zz[C:] + _dot(m, bd(x)) + yv for zz, m, x, yv in zip(z, p["m_rb"], u, p["y_v"])]
        uv = [jnp.concatenate([_bf(x), v], axis=0) for x, v in zip(u, p["v"])]
        full = [_dot_tn(b, x) * bdm.astype(F32) for b, x in zip(p["bk"], uv)]
        yield
        upd = [sum(f[h * N:(h + 1) * N] for h in range(HPG)) for f in full]
        state[:] = [g * s + d for g, s, d in zip(p["g_row"], state, upd)]
        ycat = jnp.concatenate(y, axis=1)
        w = y_ref.shape[2]
        for b in range(nb):
            y_ref[b, c * C:(c + 1) * C, :] = ycat[:, b * w:(b + 1) * w]
        yield

    def run(gen, stages):
        for _ in range(stages):
            next(gen, None)

    prep = prepare(list(range(nch)))
    n_prep = 5 + int(math.log2(C)) - 1
    for c in range(nch):
        seq = advance(c)
        run(prep, 1)
        run(seq, 1)
        run(prep, 1)
        run(seq, 2)
    run(prep, n_prep)
    for g in range(ngroups):
        h_ref[g] = state[g]


def _rwkv_scan(r, lw, k, v, a, b, *, nb):
    T, W = r.shape
    S = T // nb
    C, nch, GW = SCAN_CHUNK, SCAN_CHUNKS_PER_STEP, SCAN_GROUP_W
    assert S % (C * nch) == 0 and (nb * W) % GW == 0 and W % 128 == 0
    ngroups = nb * W // GW
    head = np.arange(GW) // HEAD_DIM
    bdm = jnp.asarray((head[:, None] == head[None, :]).astype(np.float32), dtype=BF16)
    n_blocks = S // (C * nch)
    in_spec = pl.BlockSpec((nb, C * nch, W), lambda s: (0, jnp.minimum(s, n_blocks - 1), 0))
    out_spec = pl.BlockSpec((nb, C * nch, W), lambda s: (0, jnp.maximum(s - 1, 0), 0))
    args = [t.reshape(nb, S, W) for t in (r, lw, k, v, a, b)]
    y = pl.pallas_call(
        functools.partial(_scan_kernel, nb=nb, nch=nch),
        grid=(n_blocks + 1,),
        in_specs=[in_spec] * 6 + [pl.BlockSpec((GW, GW), lambda s: (0, 0))],
        out_specs=out_spec,
        out_shape=jax.ShapeDtypeStruct((nb, S, W), F32),
        scratch_shapes=[pltpu.VMEM((ngroups, HEAD_DIM, GW), F32)]
        + [pltpu.VMEM((nch * ngroups, rows * C, GW), dt) for _, rows, dt in _SCAN_STASH],
        compiler_params=pltpu.CompilerParams(dimension_semantics=("arbitrary",), vmem_limit_bytes=VMEM_LIMIT_BYTES),
        name="rwkv_scan",
    )(*args, bdm)
    return y.reshape(T, W)


D_MODEL = 1024
ATTN_HEADS = 6
ATTN_KV_HEADS = 2
ATTN_GROUP = ATTN_HEADS // ATTN_KV_HEADS
ATTN_W = ATTN_HEADS * HEAD_DIM
KV_W = ATTN_KV_HEADS * HEAD_DIM
BLOCK_Q = 128
WINDOW = 128
N_BUCKETS = 32
MAX_EXACT = N_BUCKETS // 2
RWKV_W = 6 * HEAD_DIM
LORA_W = 128
SHIFT_W = 3 * RWKV_W + LORA_W
MEM_HEADS = 4
MEM_W = MEM_HEADS * HEAD_DIM
IN_BASE = ATTN_W + 2 * KV_W + SHIFT_W + MEM_W
PB_OFF = ATTN_W + 2 * KV_W
QM_OFF = PB_OFF + SHIFT_W
D_FF = 2816
EPS = 1e-6
GN_EPS = 64e-5
L2_EPS = 1e-12
MXU_WIDTH = 256

PROJ_TM = 1024
PROJ_SUB = 256
MIX_TQ = 512
FFN_TM = 512
FFN_FC = 256


def _block_diag_ones(width):
    idx = np.arange(width) // HEAD_DIM
    return jnp.asarray((idx[:, None] == idx[None, :]).astype(np.float32), dtype=BF16)


def _head_sum(t, bd):
    return _dot(_bf(t), bd)


def _head_rms(t, bd, gain):
    ms = _head_sum(t * t, bd) * (1.0 / HEAD_DIM)
    return t * lax.rsqrt(ms + EPS) * gain


def _rms_rows(x, g):
    ms = jnp.mean(x * x, axis=-1, keepdims=True)
    return x * lax.rsqrt(ms + EPS) * g


def _proj_kernel(*refs, tm, sub, tiles_per_batch, has_vres):
    it = iter(refs)
    x_ref, g_ref, w_ref = next(it), next(it), next(it)
    qg_ref, kg_ref, mg_ref, mu_ref = next(it), next(it), next(it), next(it)
    w0_ref, a0_ref, kk_ref, ka_ref = next(it), next(it), next(it), next(it)
    w2_ref, a2_ref, g2_ref = next(it), next(it), next(it)
    bd384_ref, bd128_ref, bd256_ref = next(it), next(it), next(it)
    if has_vres:
        vfirst_ref, v0_ref, v2_ref = next(it), next(it), next(it)
    qa_ref, ka_out_ref, va_ref, qm_ref = next(it), next(it), next(it), next(it)
    r_ref, lw_ref, k_ref, v_ref, a_ref, b_ref, gate_ref = (next(it) for _ in range(7))
    pbs_ref = next(it)
    sw = pbs_ref.shape[1]

    @pl.when(pl.program_id(0) % tiles_per_batch == 0)
    def _():
        pbs_ref[0:8, :] = jnp.zeros((8, sw), F32)

    def project(i):
        rows = slice(i * sub, (i + 1) * sub)
        return _dot(_bf(_rms_rows(x_ref[rows, :], g_ref[...])), w_ref[...])

    def attention_outputs(i, proj):
        rows = slice(i * sub, (i + 1) * sub)
        qa_ref[rows, :] = _bf(_head_rms(proj[:, :ATTN_W], bd384_ref[...], qg_ref[...]))
        ka_out_ref[rows, :] = _bf(_head_rms(proj[:, ATTN_W:ATTN_W + KV_W], bd128_ref[...], kg_ref[...]))
        va_ref[rows, :] = _bf(proj[:, ATTN_W + KV_W:PB_OFF])
        qm_ref[rows, :] = _bf(_head_rms(proj[:, QM_OFF:QM_OFF + MEM_W], bd256_ref[...], mg_ref[...]))

    def rwkv_outputs(i, proj):
        rows = slice(i * sub, (i + 1) * sub)
        lo = 8 + i * sub
        pbs_ref[lo:lo + sub, 0:SHIFT_W] = proj[:, PB_OFF:PB_OFF + SHIFT_W]
        if has_vres:
            pbs_ref[lo:lo + sub, SHIFT_W:sw] = proj[:, IN_BASE:IN_BASE + LORA_W]
        cur = pbs_ref[lo:lo + sub, :]
        prev = pbs_ref[lo - 1:lo - 1 + sub, :]
        sh = cur + mu_ref[...] * (prev - cur)
        r = sh[:, 0:RWKV_W]
        k = sh[:, RWKV_W:2 * RWKV_W]
        v = sh[:, 2 * RWKV_W:3 * RWKV_W]
        z = sh[:, 3 * RWKV_W:SHIFT_W]
        t = w0_ref[...] + _dot(_bf(jnp.tanh(z)), w2_ref[...])
        lw_ref[rows, :] = -math.exp(-0.5) * jax.nn.sigmoid(t)
        a = jax.nn.sigmoid(a0_ref[...] + _dot(_bf(z), a2_ref[...]))
        gate_ref[rows, :] = _dot(_bf(jax.nn.sigmoid(z)), g2_ref[...])
        if has_vres:
            vd = sh[:, SHIFT_W:sw]
            v = v + (vfirst_ref[rows, :] - v) * jax.nn.sigmoid(v0_ref[...] + _dot(_bf(vd), v2_ref[...]))
        kk = k * kk_ref[...]
        kk = kk / jnp.maximum(jnp.sqrt(_head_sum(kk * kk, bd384_ref[...])), L2_EPS)
        r_ref[rows, :] = r
        k_ref[rows, :] = k * (1.0 + (a - 1.0) * ka_ref[...])
        v_ref[rows, :] = v
        a_ref[rows, :] = -kk
        b_ref[rows, :] = kk * a

    n_sub = tm // sub
    proj = project(0)
    for i in range(n_sub):
        attention_outputs(i, proj)
        nxt = project(i + 1) if i + 1 < n_sub else None
        rwkv_outputs(i, proj)
        proj = nxt
    pbs_ref[0:8, :] = pbs_ref[tm:tm + 8, :]


def _layer_spec(stacked, l):
    return pl.BlockSpec((None,) + stacked.shape[1:], lambda i: (l, 0, 0), pipeline_mode=pl.Buffered(1))


def _pad_rows(w, rows, at):
    out = jnp.zeros((rows, w.shape[1]), w.dtype)
    return lax.dynamic_update_slice(out, w, (at, 0))


def _proj_call(x, p, l, v_first, *, nb):
    T, D = x.shape
    tm = PROJ_TM
    assert T % tm == 0 and (T // nb) % tm == 0
    has_vres = l > 0
    row = lambda a: a.reshape(1, -1).astype(F32)
    tile6 = lambda a: jnp.tile(a, ATTN_HEADS)
    scale = HEAD_DIM ** -0.5
    mu = p["rwkv_mu"][l]
    if has_vres:
        mu = jnp.concatenate([mu, p["rwkv_mu_vres"][l - 1], jnp.zeros((LORA_W - 16,), F32)])
    w = p["w_in"]
    if has_vres:
        w = jnp.concatenate([w[l], _bf(jnp.pad(p["w_in_vres"][l - 1], ((0, 0), (0, LORA_W - 16))))], axis=1)
    ins = [x, row(p["mix_norm_g"][l]), w]
    ins += [row(tile6(p["attn_q_norm"][l]) * scale), row(jnp.tile(p["attn_k_norm"][l], ATTN_KV_HEADS)),
            row(jnp.tile(p["mem_q_norm"][l], MEM_HEADS) * scale), row(mu),
            row(p["rwkv_w0"][l]), row(p["rwkv_a0"][l]), row(p["rwkv_k_k"][l]), row(p["rwkv_k_a"][l]),
            _bf(_pad_rows(p["rwkv_w2"][l], LORA_W, 0)), _bf(_pad_rows(p["rwkv_a2"][l], LORA_W, 32)),
            _bf(_pad_rows(p["rwkv_g2"][l], LORA_W, 64)),
            _block_diag_ones(ATTN_W), _block_diag_ones(KV_W), _block_diag_ones(MEM_W)]
    if has_vres:
        ins += [v_first, row(p["rwkv_v0"][l - 1]), _bf(_pad_rows(p["rwkv_v2"][l - 1], LORA_W, 0))]

    def spec(a):
        if a.ndim == 3:
            return _layer_spec(a, l)
        if a.shape[0] == T:
            return pl.BlockSpec((tm, a.shape[1]), lambda i: (i, 0))
        return pl.BlockSpec(a.shape, lambda i: (0, 0), pipeline_mode=pl.Buffered(1))

    out_shapes = [jax.ShapeDtypeStruct((T, ATTN_W), BF16), jax.ShapeDtypeStruct((T, KV_W), BF16),
                  jax.ShapeDtypeStruct((T, KV_W), BF16), jax.ShapeDtypeStruct((T, MEM_W), BF16)]
    out_shapes += [jax.ShapeDtypeStruct((T, RWKV_W), F32)] * 7
    sw = SHIFT_W + (LORA_W if has_vres else 0)
    return pl.pallas_call(
        functools.partial(_proj_kernel, tm=tm, sub=PROJ_SUB, tiles_per_batch=(T // nb) // tm, has_vres=has_vres),
        grid=(T // tm,),
        in_specs=[spec(a) for a in ins],
        out_specs=[pl.BlockSpec((tm, s.shape[1]), lambda i: (i, 0)) for s in out_shapes],
        out_shape=out_shapes,
        scratch_shapes=[pltpu.VMEM((tm + 8, sw), F32)],
        compiler_params=pltpu.CompilerParams(dimension_semantics=("arbitrary",), vmem_limit_bytes=VMEM_LIMIT_BYTES),
        name=f"proj_l{l}",
    )(*ins)


def _bucket_table():
    qi = np.arange(BLOCK_Q)[:, None]
    kj = np.arange(2 * BLOCK_Q)[None, :]
    dist = qi + BLOCK_Q - kj
    in_band = (dist >= 0) & (dist < WINDOW)
    d = np.maximum(dist, 1).astype(np.float32)
    large = MAX_EXACT + (np.log(d / np.float32(MAX_EXACT)) / np.float32(math.log(WINDOW / MAX_EXACT))
                         * np.float32(N_BUCKETS - MAX_EXACT)).astype(np.int32)
    large = np.minimum(large, N_BUCKETS - 1)
    bucket = np.where(dist < MAX_EXACT, np.maximum(dist, 0), large)
    return np.where(in_band, bucket, -1).astype(np.int32)


def _bias_kernel(rb_ref, bucket_ref, out_ref):
    bucket = bucket_ref[...]
    for h in range(ATTN_HEADS):
        acc = jnp.full(bucket.shape, -jnp.inf, F32)
        for j in range(N_BUCKETS):
            acc = jnp.where(bucket == j, rb_ref[j, h], acc)
        out_ref[h] = acc


def _bias_call(rel_bias):
    tab = pl.pallas_call(
        _bias_kernel,
        in_specs=[pl.BlockSpec(memory_space=pltpu.SMEM), pl.BlockSpec(memory_space=pltpu.VMEM)],
        out_specs=pl.BlockSpec(memory_space=pltpu.VMEM),
        out_shape=jax.ShapeDtypeStruct((ATTN_HEADS, BLOCK_Q, 2 * BLOCK_Q), F32),
        name="rel_bias_table",
    )(rel_bias.astype(F32), jnp.asarray(_bucket_table()))
    return tab.reshape(ATTN_KV_HEADS, ATTN_GROUP * BLOCK_Q, 2 * BLOCK_Q)


def _memkv_kernel(mem_ref, g_ref, w_ref, kg_ref, bd_ref, mk_ref, mv_ref):
    hn = _bf(_rms_rows(mem_ref[...], g_ref[...]))
    kv = _dot(hn, w_ref[...])
    mk_ref[...] = _bf(_head_rms(kv[:, :MEM_W], bd_ref[...], kg_ref[...]))
    mv_ref[...] = _bf(kv[:, MEM_W:])


def _memkv_call(mem2d, p, l):
    rows = mem2d.shape[0]
    vm = pl.BlockSpec(memory_space=pltpu.VMEM)
    return pl.pallas_call(
        _memkv_kernel,
        in_specs=[vm] * 5,
        out_specs=[vm, vm],
        out_shape=[jax.ShapeDtypeStruct((rows, MEM_W), BF16)] * 2,
        compiler_params=pltpu.CompilerParams(vmem_limit_bytes=VMEM_LIMIT_BYTES),
        name=f"mem_kv_l{l}",
    )(mem2d, p["mem_norm_g"][l].reshape(1, -1), _bf(p["w_mem_kv"][l]),
      jnp.tile(p["mem_k_norm"][l], MEM_HEADS).reshape(1, -1), _block_diag_ones(MEM_W))


def _mix_kernel(x_ref, qa_ref, kc_ref, kp_ref, vc_ref, vp_ref, bias_ref, sink_ref, qm_ref, mk_ref, mv_ref,
                y_ref, r_ref, k_ref, v_ref, gate_ref, lnw_ref, lnb_ref, rk_ref, bd_ref, wout_ref, out_ref, att_ref,
                *, tq, tiles_per_batch):
    N, BQ = HEAD_DIM, BLOCK_Q
    seq_start = pl.program_id(0) % tiles_per_batch == 0
    nqb = tq // BQ

    @pl.when(pl.program_id(0) == 0)
    def _():
        att_ref[...] = jnp.zeros_like(att_ref)

    qa = qa_ref[...]
    kall = jnp.concatenate([kp_ref[...], kc_ref[...]], axis=0)
    vall = jnp.concatenate([vp_ref[...], vc_ref[...]], axis=0)
    before_seq = lax.broadcasted_iota(jnp.int32, (ATTN_GROUP * BQ, 2 * BQ), 1) < BQ

    qm = qm_ref[...]
    mk = mk_ref[0]
    mv = mv_ref[0]

    swa = [(j, g) for j in range(nqb) for g in range(ATTN_KV_HEADS)]
    logits, values, sinks = [], [], []
    for j, g in swa:
        qg = jnp.concatenate(
            [qa[j * BQ:(j + 1) * BQ, (ATTN_GROUP * g + i) * N:(ATTN_GROUP * g + i + 1) * N] for i in range(ATTN_GROUP)],
            axis=0)
        lg = _dot_nt(qg, kall[j * BQ:(j + 2) * BQ, g * N:(g + 1) * N]) + bias_ref[g]
        if j == 0:
            lg = jnp.where(jnp.logical_and(seq_start, before_seq), -jnp.inf, lg)
        logits.append(lg)
        values.append(vall[j * BQ:(j + 2) * BQ, g * N:(g + 1) * N])
        sinks.append(sink_ref[g])
    for h in range(MEM_HEADS):
        hs = slice(h * N, (h + 1) * N)
        logits.append(_dot_nt(qm[:, hs], mk[:, hs]))
        values.append(mv[:, hs])
        sinks.append(None)
    bd = bd_ref[...]
    y = y_ref[...]
    d = y - _head_sum(y, bd) * (1.0 / N)
    var = _head_sum(d * d, bd) * (1.0 / N)
    yn = d * lax.rsqrt(var + GN_EPS) * lnw_ref[...] + lnb_ref[...]
    bonus = _head_sum(r_ref[...] * k_ref[...] * rk_ref[...], bd) * v_ref[...]
    out_b = (yn + bonus) * gate_ref[...]
    mixed = _bf(jnp.concatenate([att_ref[:, :ATTN_W], out_b, att_ref[:, ATTN_W:]], axis=-1))
    out_ref[...] = x_ref[...] + _dot(mixed, wout_ref[...])

    row_max = [jnp.max(lg, axis=-1, keepdims=True) for lg in logits]
    m = [rm if s is None else jnp.maximum(rm, s) for rm, s in zip(row_max, sinks)]
    e = [jnp.exp(lg - mm) for lg, mm in zip(logits, m)]
    denom = [jnp.sum(ee, axis=-1, keepdims=True) for ee in e]
    denom = [d if s is None else d + jnp.exp(s - mm) for d, s, mm in zip(denom, sinks, m)]
    outs = [_dot(_bf(ee), vv) / d for ee, vv, d in zip(e, values, denom)]

    head_rows = [[None] * nqb for _ in range(ATTN_HEADS)]
    for (j, g), o in zip(swa, outs):
        for i in range(ATTN_GROUP):
            head_rows[ATTN_GROUP * g + i][j] = o[i * BQ:(i + 1) * BQ]
    att_ref[:, :ATTN_W] = jnp.concatenate([jnp.concatenate(rows, axis=0) for rows in head_rows], axis=-1)
    att_ref[:, ATTN_W:] = jnp.concatenate(outs[len(swa):], axis=-1)


def _mix_call(x, qa, ka, va, qm, mk, mv, y, r, k, v, gate, bias, p, l, *, nb):
    T, D = x.shape
    tq = MIX_TQ
    S = T // nb
    assert S % tq == 0 and tq % BLOCK_Q == 0
    tpb = S // tq
    qpb = tq // BLOCK_Q
    mem_tokens = mk.shape[0] // nb
    row = lambda a: a.reshape(1, -1).astype(F32)
    sink = jnp.repeat(p["attn_sinks"][l].astype(F32), BLOCK_Q).reshape(ATTN_KV_HEADS, ATTN_GROUP * BLOCK_Q, 1)
    n_tiles = T // tq
    att = lambda i: jnp.minimum(i, n_tiles - 1)
    fin = lambda i: jnp.maximum(i - 1, 0)
    att_tile = lambda w: pl.BlockSpec((tq, w), lambda i: (att(i), 0))
    tile = lambda w: pl.BlockSpec((tq, w), lambda i: (fin(i), 0))
    prev = pl.BlockSpec((BLOCK_Q, KV_W), lambda i: (jnp.maximum(att(i) * qpb - 1, 0), 0))
    full = lambda a: pl.BlockSpec(a.shape, lambda i: (0,) * a.ndim)
    memspec = pl.BlockSpec((1, mem_tokens, MEM_W), lambda i: (att(i) // tpb, 0, 0))
    lnw, lnb, rk = row(p["rwkv_ln_w"][l]), row(p["rwkv_ln_b"][l]), row(p["rwkv_r_k"][l])
    bd = _block_diag_ones(RWKV_W)
    wout = p["w_out"]
    mk3 = mk.reshape(nb, mem_tokens, MEM_W)
    mv3 = mv.reshape(nb, mem_tokens, MEM_W)
    return pl.pallas_call(
        functools.partial(_mix_kernel, tq=tq, tiles_per_batch=tpb),
        grid=(n_tiles + 1,),
        in_specs=[tile(D), att_tile(ATTN_W), att_tile(KV_W), prev, att_tile(KV_W), prev, full(bias), full(sink),
                  att_tile(MEM_W), memspec, memspec] + [tile(RWKV_W)] * 5
                 + [full(lnw), full(lnb), full(rk), full(bd), _layer_spec(wout, l)],
        out_specs=tile(D),
        out_shape=jax.ShapeDtypeStruct((T, D), F32),
        scratch_shapes=[pltpu.VMEM((tq, ATTN_W + MEM_W), F32)],
        compiler_params=pltpu.CompilerParams(dimension_semantics=("arbitrary",), vmem_limit_bytes=VMEM_LIMIT_BYTES),
        name=f"mix_l{l}",
    )(x, qa, ka, ka, va, va, bias, sink, qm, mk3, mv3, y, r, k, v, gate, lnw, lnb, rk, bd, wout)


SUBLANES = 8


def _tile_copies(hbm, buf, sem, tile, slot, to_vmem):
    copies = []
    for s in range(SUBLANES):
        src, dst = hbm.at[tile, s], buf.at[slot, :, s, :]
        if not to_vmem:
            src, dst = dst, src
        copies.append(pltpu.make_async_copy(src, dst, sem.at[slot, s]))
    return copies


def _shift_rows(u, before, steps):
    first = lax.broadcasted_iota(jnp.int32, (SUBLANES, u.shape[1]), 0) == 0
    tm = u.shape[0]
    heads = []
    for i in range(steps):
        lo = tm - (steps - i) * SUBLANES
        wrapped = pltpu.roll(u[lo:lo + SUBLANES], 1, axis=0)
        prior = jnp.broadcast_to(before[SUBLANES - steps + i:SUBLANES - steps + i + 1], wrapped.shape)
        heads.append(jnp.where(first, prior, wrapped))
    return jnp.concatenate(heads + [u[:tm - steps * SUBLANES]], axis=0)


def _ffn_kernel(x_hbm, g_ref, wup_ref, cw_ref, cb_ref, wdn_ref, o_hbm, xbuf, obuf, carry_ref, act_ref,
                in_sem, out_sem, *, tm, tiles_per_batch, n_tiles):
    D = g_ref.shape[1]
    q = tm // SUBLANES
    i = pl.program_id(0)
    slot = i % 2

    @pl.when(i == 0)
    def _():
        carry_ref[...] = jnp.zeros_like(carry_ref)
        for cp in _tile_copies(x_hbm, xbuf, in_sem, 0, 0, True):
            cp.start()

    @pl.when(i + 1 < n_tiles)
    def _():
        for cp in _tile_copies(x_hbm, xbuf, in_sem, i + 1, 1 - slot, True):
            cp.start()

    for cp in _tile_copies(x_hbm, xbuf, in_sem, i, slot, True):
        cp.wait()

    @pl.when(i >= 2)
    def _():
        for cp in _tile_copies(o_hbm, obuf, out_sem, i - 2, slot, False):
            cp.wait()

    seq_start = i % tiles_per_batch == 0
    g = g_ref[...]
    x = xbuf[slot].reshape(tm, D)
    before = jnp.where(seq_start, 0.0, _rms_rows(carry_ref[...], g))
    for r, grp in ((SUBLANES - 2, q - 2), (SUBLANES - 1, q - 1)):
        row = grp * SUBLANES + SUBLANES - 1
        carry_ref[r:r + 1, :] = x[row:row + 1]
    h_ext = _bf(jnp.concatenate([before, _rms_rows(x, g)], axis=0))

    n_chunks = D_FF // FFN_FC
    gate_cols = lambda c: slice(c * FFN_FC, (c + 1) * FFN_FC)
    val_cols = lambda c: slice(D_FF + c * FFN_FC, D_FF + (c + 1) * FFN_FC)

    def conv(u_ext, cols):
        w = cw_ref[:, cols]
        u = u_ext[SUBLANES:]
        return (cb_ref[:, cols] + w[0:1] * _shift_rows(u, u_ext[:SUBLANES], 2)
                + w[1:2] * _shift_rows(u, u_ext[:SUBLANES], 1) + w[2:3] * u)

    up = lambda c: (_dot(h_ext, wup_ref[:, gate_cols(c)]), _dot(h_ext, wup_ref[:, val_cols(c)]))
    nxt = up(0)
    for c in range(n_chunks):
        ug, uv = nxt
        if c + 1 < n_chunks:
            nxt = up(c + 1)
        gt = conv(ug, gate_cols(c))
        act_ref[:, gate_cols(c)] = _bf(gt * jax.nn.sigmoid(gt) * conv(uv, val_cols(c)))
    obuf[slot] = (x + _dot(act_ref[...], wdn_ref[...])).reshape(q, SUBLANES, D)

    for cp in _tile_copies(o_hbm, obuf, out_sem, i, slot, False):
        cp.start()

    @pl.when(i == n_tiles - 1)
    def _():
        if n_tiles >= 2:
            for cp in _tile_copies(o_hbm, obuf, out_sem, i - 1, 1 - slot, False):
                cp.wait()
        for cp in _tile_copies(o_hbm, obuf, out_sem, i, slot, False):
            cp.wait()


def _ffn_call(x, p, l, *, nb):
    T, D = x.shape
    tm = FFN_TM
    S = T // nb
    q = tm // SUBLANES
    assert S % tm == 0 and q % SUBLANES == 0 and D_FF % FFN_FC == 0
    n_tiles = T // tm
    g = p["ffn_norm_g"][l].reshape(1, -1)
    wup, wdn = p["w_up"], p["w_down"]
    cw, cb = p["conv_w"][l], p["conv_b"][l].reshape(1, -1)
    full = lambda a: pl.BlockSpec(a.shape, lambda i: (0,) * a.ndim)
    out = pl.pallas_call(
        functools.partial(_ffn_kernel, tm=tm, tiles_per_batch=S // tm, n_tiles=n_tiles),
        grid=(n_tiles,),
        in_specs=[pl.BlockSpec(memory_space=pl.ANY), full(g), _layer_spec(wup, l), full(cw), full(cb),
                  _layer_spec(wdn, l)],
        out_specs=pl.BlockSpec(memory_space=pl.ANY),
        out_shape=jax.ShapeDtypeStruct((n_tiles, SUBLANES, q, D), F32),
        scratch_shapes=[pltpu.VMEM((2, q, SUBLANES, D), F32), pltpu.VMEM((2, q, SUBLANES, D), F32),
                        pltpu.VMEM((SUBLANES, D), F32), pltpu.VMEM((tm, D_FF), BF16),
                        pltpu.SemaphoreType.DMA((2, SUBLANES)), pltpu.SemaphoreType.DMA((2, SUBLANES))],
        compiler_params=pltpu.CompilerParams(dimension_semantics=("arbitrary",), vmem_limit_bytes=VMEM_LIMIT_BYTES),
        name=f"ffn_l{l}",
    )(x.reshape(n_tiles, SUBLANES, q, D), g, wup, cw, cb, wdn)
    return out.reshape(T, D)


_PARAM_NAMES = (
    "rel_bias", "mix_norm_g", "w_in", "w_in_vres", "attn_q_norm", "attn_k_norm", "attn_sinks", "rwkv_mu",
    "rwkv_mu_vres", "rwkv_w0", "rwkv_w2", "rwkv_a0", "rwkv_a2", "rwkv_v0", "rwkv_v2", "rwkv_g2", "rwkv_k_k",
    "rwkv_k_a", "rwkv_r_k", "rwkv_ln_w", "rwkv_ln_b", "mem_norm_g", "w_mem_kv", "mem_q_norm", "mem_k_norm",
    "w_out", "ffn_norm_g", "w_up", "conv_w", "conv_b", "w_down")


def kernel(x, mem, rel_bias, mix_norm_g, w_in, w_in_vres, attn_q_norm, attn_k_norm, attn_sinks, rwkv_mu,
           rwkv_mu_vres, rwkv_w0, rwkv_w2, rwkv_a0, rwkv_a2, rwkv_v0, rwkv_v2, rwkv_g2, rwkv_k_k, rwkv_k_a,
           rwkv_r_k, rwkv_ln_w, rwkv_ln_b, mem_norm_g, w_mem_kv, mem_q_norm, mem_k_norm, w_out, ffn_norm_g,
           w_up, conv_w, conv_b, w_down):
    p = dict(zip(_PARAM_NAMES, (
        rel_bias, mix_norm_g, w_in, w_in_vres, attn_q_norm, attn_k_norm, attn_sinks, rwkv_mu, rwkv_mu_vres,
        rwkv_w0, rwkv_w2, rwkv_a0, rwkv_a2, rwkv_v0, rwkv_v2, rwkv_g2, rwkv_k_k, rwkv_k_a, rwkv_r_k, rwkv_ln_w,
        rwkv_ln_b, mem_norm_g, w_mem_kv, mem_q_norm, mem_k_norm, w_out, ffn_norm_g, w_up, conv_w, conv_b, w_down)))
    for name in ("w_in", "w_mem_kv", "w_out", "w_up", "w_down"):
        p[name] = _bf(p[name])
    nb, S, D = x.shape
    xt = x.reshape(nb * S, D)
    mem2d = mem.reshape(nb * mem.shape[1], D)
    bias = _bias_call(rel_bias)
    v_first = None
    for l in range(w_in.shape[0]):
        qa, ka, va, qm, r, lw, k, v, a, b, gate = _proj_call(xt, p, l, v_first, nb=nb)
        if l == 0:
            v_first = v
        y = _rwkv_scan(r, lw, k, v, a, b, nb=nb)
        mk, mv = _memkv_call(mem2d, p, l)
        xt = _mix_call(xt, qa, ka, va, qm, mk, mv, y, r, k, v, gate, bias, p, l, nb=nb)
        xt = _ffn_call(xt, p, l, nb=nb)
    return xt.reshape(nb, S, D)
```

```python
import functools
import math

import jax
import jax.numpy as jnp
import numpy as np
from jax import lax
from jax.experimental import pallas as pl
from jax.experimental.pallas import tpu as pltpu

F32 = jnp.float32
BF16 = jnp.bfloat16

HEAD_DIM = 64
SCAN_CHUNK = 64
VMEM_LIMIT_BYTES = 56 * 1024 * 1024


def _dot(a, b):
    return jnp.dot(a, b, preferred_element_type=F32)


def _dot_nt(a, b):
    return lax.dot_general(a, b, (((1,), (1,)), ((), ())), preferred_element_type=F32)


def _dot_tn(a, b):
    return lax.dot_general(a, b, (((0,), (0,)), ((), ())), preferred_element_type=F32)


def _bf(x):
    return x.astype(BF16)


SCAN_GROUP_HEADS = 4
SCAN_GROUP_W = SCAN_GROUP_HEADS * HEAD_DIM
SCAN_CHUNKS_PER_STEP = 4


_SCAN_STASH = (("w2", 1, F32), ("y_v", 1, F32), ("m_rb", 1, F32), ("v", 1, F32), ("w1r", 2, F32),
               ("bk", 2, F32), ("g_row", 1, F32))


def _scan_kernel(r_ref, lw_ref, k_ref, v_ref, a_ref, b_ref, bdm_ref, y_ref, h_ref, *stash_refs, nb, nch):
    C, N, GW, HPG = SCAN_CHUNK, HEAD_DIM, SCAN_GROUP_W, SCAN_GROUP_HEADS
    ngroups = h_ref.shape[0]
    stash = {name: ref for (name, _, _), ref in zip(_SCAN_STASH, stash_refs)}

    @pl.when(pl.program_id(0) == 0)
    def _():
        h_ref[...] = jnp.zeros_like(h_ref)
        for ref in stash_refs:
            ref[...] = jnp.zeros_like(ref)

    trow = lax.broadcasted_iota(jnp.int32, (C, C), 0)
    tcol = lax.broadcasted_iota(jnp.int32, (C, C), 1)
    tri = (trow >= tcol).astype(BF16)
    grow = lax.broadcasted_iota(jnp.int32, (C, GW), 0)
    gcol = lax.broadcasted_iota(jnp.int32, (C, GW), 1) % N
    incl = grow >= gcol
    strict = grow > gcol
    eye = (grow == gcol).astype(F32)
    bdm = bdm_ref[...]

    def bd(x):
        return jnp.concatenate([_bf(x)] * HPG, axis=0) * bdm

    def split2(x):
        hi = _bf(x)
        return hi, _bf(x - hi.astype(F32))

    def chunk_units(c):
        rows = slice(c * C, (c + 1) * C)
        per_batch = []
        for b in range(nb):
            lw = lw_ref[b, rows, :]
            l1 = _bf(lw)
            e1 = lw - l1.astype(F32)
            l2 = _bf(e1)
            l3 = _bf(e1 - l2.astype(F32))
            cum = _dot(tri, l1) + _dot(tri, l2) + _dot(tri, l3)
            cum_last = cum[C - 1:C, :]
            g_inv = jnp.exp(-cum)
            g_out = jnp.exp(cum_last - cum)
            kk = k_ref[b, rows, :]
            bb = b_ref[b, rows, :]
            per_batch.append(dict(
                a_t=a_ref[b, rows, :] * jnp.exp(cum - lw), r_t=r_ref[b, rows, :] * jnp.exp(cum),
                b_t=bb * g_inv, k_t=kk * g_inv, b_h=bb * g_out, k_h=kk * g_out,
                g_c=jnp.broadcast_to(jnp.exp(cum_last), cum.shape), v=v_ref[b, rows, :]))
        cat = {n: jnp.concatenate([pb[n] for pb in per_batch], axis=1) for n in per_batch[0]}
        return [{n: t[:, g * GW:(g + 1) * GW] for n, t in cat.items()} for g in range(ngroups)]

    def prepare(chunks):
        units = [u for c in chunks for u in chunk_units(c)]

        nu = len(units)
        ar = [jnp.concatenate([_bf(u["a_t"]), _bf(u["r_t"])], axis=0) for u in units]
        pb = [_dot_nt(ar[i], bd(units[i]["b_t"])) for i in range(nu)]
        pk = [_dot_nt(ar[i], bd(units[i]["k_t"])) for i in range(nu)]
        l_ab = [jnp.where(strict, p[:C], 0.0) for p in pb]
        m_rb = [_bf(jnp.where(incl, p[C:], 0.0)) for p in pb]
        tril2 = jnp.concatenate([strict, incl], axis=0)
        lm_k = [_bf(jnp.where(tril2, p, 0.0)) for p in pk]
        yield
        t_inv = [eye + l for l in l_ab]
        pw = [_dot(_bf(l), bd(l)) for l in l_ab]
        yield
        n_rounds = int(math.log2(C)) - 1
        for rnd in range(n_rounds):
            if rnd + 1 < n_rounds:
                z = [_dot(jnp.concatenate([_bf(p), _bf(t)], axis=0), bd(p)) for p, t in zip(pw, t_inv)]
                pw = [zz[:C] for zz in z]
                t_inv = [t + zz[C:] for t, zz in zip(t_inv, z)]
            else:
                t_inv = [t + _dot(_bf(t), bd(p)) for t, p in zip(t_inv, pw)]
            yield
        t_b = [_bf(t) for t in t_inv]
        w1 = [_dot(t, bd(u["a_t"])) for t, u in zip(t_b, units)]
        lmv = [_dot(m, bd(u["v"])) for m, u in zip(lm_k, units)]
        yield
        w2 = [_dot(t, bd(x[:C])) for t, x in zip(t_b, lmv)]
        parts = [p for u in units for p in split2(u["g_c"] * eye)]
        sums = _dot(jnp.concatenate(parts, axis=0), bdm)
        yield
        for i, u in enumerate(units):
            put = lambda name, val: stash[name].__setitem__(i, val.astype(stash[name].dtype))
            put("w2", w2[i])
            put("y_v", lmv[i][C:])
            put("m_rb", m_rb[i])
            put("v", u["v"])
            put("w1r", jnp.concatenate([w1[i], u["r_t"]], axis=0))
            put("bk", jnp.concatenate([u["b_h"], u["k_h"]], axis=0))
            put("g_row", sums[2 * i * C:(2 * i + 1) * C] + sums[(2 * i + 1) * C:(2 * i + 2) * C])
        yield

    state = [h_ref[g] for g in range(ngroups)]

    def advance(c):
        p = {n: [ref[c * ngroups + g] for g in range(ngroups)] for n, ref in stash.items()}
        for n in ("m_rb", "v", "w1r", "bk"):
            p[n] = [_bf(x) for x in p[n]]
        z = [_dot(w, bd(s)) for w, s in zip(p["w1r"], state)]
        yield
        u = [zz[:C] + w for zz, w in zip(z, p["w2"])]
        y = [zz[C:] + _dot(m, bd(x)) + yv for zz, m, x, yv in zip(z, p["m_rb"], u, p["y_v"])]
        uv = [jnp.concatenate([_bf(x), v], axis=0) for x, v in zip(u, p["v"])]
        full = [_dot_tn(b, x) * bdm.astype(F32) for b, x in zip(p["bk"], uv)]
        yield
        upd = [sum(f[h * N:(h + 1) * N] for h in range(HPG)) for f in full]
        state[:] = [g * s + d for g, s, d in zip(p["g_row"], state, upd)]
        ycat = jnp.concatenate(y, axis=1)
        w = y_ref.shape[2]
        for b in range(nb):
            y_ref[b, c * C:(c + 1) * C, :] = ycat[:, b * w:(b + 1) * w]
        yield

    def run(gen, stages):
        for _ in range(stages):
            next(gen, None)

    prep = prepare(list(range(nch)))
    n_prep = 5 + int(math.log2(C)) - 1
    for c in range(nch):
        seq = advance(c)
        run(prep, 1)
        run(seq, 1)
        run(prep, 1)
        run(seq, 2)
    run(prep, n_prep)
    for g in range(ngroups):
        h_ref[g] = state[g]


def _rwkv_scan(r, lw, k, v, a, b, *, nb):
    T, W = r.shape
    S = T // nb
    C, nch, GW = SCAN_CHUNK, SCAN_CHUNKS_PER_STEP, SCAN_GROUP_W
    assert S % (C * nch) == 0 and (nb * W) % GW == 0 and W % 128 == 0
    ngroups = nb * W // GW
    head = np.arange(GW) // HEAD_DIM
    bdm = jnp.asarray((head[:, None] == head[None, :]).astype(np.float32), dtype=BF16)
    n_blocks = S // (C * nch)
    in_spec = pl.BlockSpec((nb, C * nch, W), lambda s: (0, jnp.minimum(s, n_blocks - 1), 0))
    out_spec = pl.BlockSpec((nb, C * nch, W), lambda s: (0, jnp.maximum(s - 1, 0), 0))
    args = [t.reshape(nb, S, W) for t in (r, lw, k, v, a, b)]
    y = pl.pallas_call(
        functools.partial(_scan_kernel, nb=nb, nch=nch),
        grid=(n_blocks + 1,),
        in_specs=[in_spec] * 6 + [pl.BlockSpec((GW, GW), lambda s: (0, 0))],
        out_specs=out_spec,
        out_shape=jax.ShapeDtypeStruct((nb, S, W), F32),
        scratch_shapes=[pltpu.VMEM((ngroups, HEAD_DIM, GW), F32)]
        + [pltpu.VMEM((nch * ngroups, rows * C, GW), dt) for _, rows, dt in _SCAN_STASH],
        compiler_params=pltpu.CompilerParams(dimension_semantics=("arbitrary",), vmem_limit_bytes=VMEM_LIMIT_BYTES),
        name="rwkv_scan",
    )(*args, bdm)
    return y.reshape(T, W)


D_MODEL = 1024
ATTN_HEADS = 6
ATTN_KV_HEADS = 2
ATTN_GROUP = ATTN_HEADS // ATTN_KV_HEADS
ATTN_W = ATTN_HEADS * HEAD_DIM
KV_W = ATTN_KV_HEADS * HEAD_DIM
BLOCK_Q = 128
WINDOW = 128
N_BUCKETS = 32
MAX_EXACT = N_BUCKETS // 2
RWKV_W = 6 * HEAD_DIM
LORA_W = 128
SHIFT_W = 3 * RWKV_W + LORA_W
MEM_HEADS = 4
MEM_W = MEM_HEADS * HEAD_DIM
IN_BASE = ATTN_W + 2 * KV_W + SHIFT_W + MEM_W
PB_OFF = ATTN_W + 2 * KV_W
QM_OFF = PB_OFF + SHIFT_W
D_FF = 2816
EPS = 1e-6
GN_EPS = 64e-5
L2_EPS = 1e-12
MXU_WIDTH = 256

PROJ_TM = 1024
PROJ_SUB = 256
MIX_TQ = 512
FFN_TM = 512
FFN_FC = 256


def _block_diag_ones(width):
    idx = np.arange(width) // HEAD_DIM
    return jnp.asarray((idx[:, None] == idx[None, :]).astype(np.float32), dtype=BF16)


def _head_sum(t, bd):
    return _dot(_bf(t), bd)


def _head_rms(t, bd, gain):
    ms = _head_sum(t * t, bd) * (1.0 / HEAD_DIM)
    return t * lax.rsqrt(ms + EPS) * gain


def _rms_rows(x, g):
    ms = jnp.mean(x * x, axis=-1, keepdims=True)
    return x * lax.rsqrt(ms + EPS) * g


class _Row:
    def __init__(self, table_ref, row, width):
        self.table_ref, self.row, self.width = table_ref, row, width

    def __getitem__(self, _):
        return self.table_ref[self.row:self.row + 1, 0:self.width]


def _proj_kernel(*refs, tm, sub, tiles_per_batch, has_vres):
    it = iter(refs)
    x_ref, vec_ref, w_ref, lora_ref = next(it), next(it), next(it), next(it)
    bd384_ref, bd128_ref, bd256_ref = next(it), next(it), next(it)
    if has_vres:
        vfirst_ref = next(it)
    qa_ref, ka_out_ref, va_ref, qm_ref = next(it), next(it), next(it), next(it)
    r_ref, lw_ref, k_ref, v_ref, a_ref, b_ref, gate_ref = (next(it) for _ in range(7))
    pbs_ref = next(it)
    sw = pbs_ref.shape[1]
    g_ref, qg_ref, kg_ref, mg_ref = (_Row(vec_ref, VEC_ROWS[n], w) for n, w in (
        ("mix_norm_g", D_MODEL), ("q_gain", ATTN_W), ("k_gain", KV_W), ("mq_gain", MEM_W)))
    mu_ref = _Row(vec_ref, VEC_ROWS["mu"], sw)
    w0_ref, a0_ref, kk_ref, ka_ref, v0_ref = (_Row(vec_ref, VEC_ROWS[n], RWKV_W) for n in ("w0", "a0", "k_k", "k_a", "v0"))
    w2_ref, a2_ref, g2_ref, v2_ref = (lora_ref.at[j] for j in range(4))

    @pl.when(pl.program_id(0) % tiles_per_batch == 0)
    def _():
        pbs_ref[0:8, :] = jnp.zeros((8, sw), F32)

    def project(i):
        rows = slice(i * sub, (i + 1) * sub)
        return _dot(_bf(_rms_rows(x_ref[rows, :], g_ref[...])), w_ref[...])

    def attention_outputs(i, proj):
        rows = slice(i * sub, (i + 1) * sub)
        qa_ref[rows, :] = _bf(_head_rms(proj[:, :ATTN_W], bd384_ref[...], qg_ref[...]))
        ka_out_ref[rows, :] = _bf(_head_rms(proj[:, ATTN_W:ATTN_W + KV_W], bd128_ref[...], kg_ref[...]))
        va_ref[rows, :] = _bf(proj[:, ATTN_W + KV_W:PB_OFF])
        qm_ref[rows, :] = _bf(_head_rms(proj[:, QM_OFF:QM_OFF + MEM_W], bd256_ref[...], mg_ref[...]))

    def rwkv_outputs(i, proj):
        rows = slice(i * sub, (i + 1) * sub)
        lo = 8 + i * sub
        pbs_ref[lo:lo + sub, 0:SHIFT_W] = proj[:, PB_OFF:PB_OFF + SHIFT_W]
        if has_vres:
            pbs_ref[lo:lo + sub, SHIFT_W:sw] = proj[:, IN_BASE:IN_BASE + LORA_W]
        cur = pbs_ref[lo:lo + sub, :]
        prev = pbs_ref[lo - 1:lo - 1 + sub, :]
        sh = cur + mu_ref[...] * (prev - cur)
        r = sh[:, 0:RWKV_W]
        k = sh[:, RWKV_W:2 * RWKV_W]
        v = sh[:, 2 * RWKV_W:3 * RWKV_W]
        z = sh[:, 3 * RWKV_W:SHIFT_W]
        t = w0_ref[...] + _dot(_bf(jnp.tanh(z)), w2_ref[...])
        lw_ref[rows, :] = -math.exp(-0.5) * jax.nn.sigmoid(t)
        a = jax.nn.sigmoid(a0_ref[...] + _dot(_bf(z), a2_ref[...]))
        gate_ref[rows, :] = _dot(_bf(jax.nn.sigmoid(z)), g2_ref[...])
        if has_vres:
            vd = sh[:, SHIFT_W:sw]
            v = v + (vfirst_ref[rows, :] - v) * jax.nn.sigmoid(v0_ref[...] + _dot(_bf(vd), v2_ref[...]))
        kk = k * kk_ref[...]
        kk = kk / jnp.maximum(jnp.sqrt(_head_sum(kk * kk, bd384_ref[...])), L2_EPS)
        r_ref[rows, :] = r
        k_ref[rows, :] = k * (1.0 + (a - 1.0) * ka_ref[...])
        v_ref[rows, :] = v
        a_ref[rows, :] = -kk
        b_ref[rows, :] = kk * a

    n_sub = tm // sub
    proj = project(0)
    for i in range(n_sub):
        attention_outputs(i, proj)
        nxt = project(i + 1) if i + 1 < n_sub else None
        rwkv_outputs(i, proj)
        proj = nxt
    pbs_ref[0:8, :] = pbs_ref[tm:tm + 8, :]


def _layer_spec(stacked, l):
    return pl.BlockSpec((None,) + stacked.shape[1:], lambda i: (l,) + (0,) * (stacked.ndim - 1),
                        pipeline_mode=pl.Buffered(1))


VEC_W = SHIFT_W + LORA_W
VEC_ROWS = {n: i for i, n in enumerate((
    "mix_norm_g", "q_gain", "k_gain", "mq_gain", "mu", "w0", "a0", "k_k", "k_a", "v0", "ln_w", "ln_b", "r_k",
    "ffn_norm_g"))}
VEC_TABLE_ROWS = 16


def _pack_params(p):
    layers = p["w_in"].shape[0]
    scale = HEAD_DIM ** -0.5
    zeros = jnp.zeros((RWKV_W,), F32)
    tables, loras = [], []
    for l in range(layers):
        mu = p["rwkv_mu"][l]
        if l > 0:
            mu = jnp.concatenate([mu, p["rwkv_mu_vres"][l - 1]])
        vecs = dict(
            mix_norm_g=p["mix_norm_g"][l], q_gain=jnp.tile(p["attn_q_norm"][l], ATTN_HEADS) * scale,
            k_gain=jnp.tile(p["attn_k_norm"][l], ATTN_KV_HEADS), mq_gain=jnp.tile(p["mem_q_norm"][l], MEM_HEADS) * scale,
            mu=mu, w0=p["rwkv_w0"][l], a0=p["rwkv_a0"][l], k_k=p["rwkv_k_k"][l], k_a=p["rwkv_k_a"][l],
            v0=p["rwkv_v0"][l - 1] if l > 0 else zeros, ln_w=p["rwkv_ln_w"][l], ln_b=p["rwkv_ln_b"][l],
            r_k=p["rwkv_r_k"][l].reshape(-1), ffn_norm_g=p["ffn_norm_g"][l])
        rows = [jnp.pad(vecs[n].astype(F32), (0, VEC_W - vecs[n].shape[0])) for n in VEC_ROWS]
        rows += [jnp.zeros((VEC_W,), F32)] * (VEC_TABLE_ROWS - len(rows))
        tables.append(jnp.stack(rows))
        pad = lambda w, at: jnp.pad(w, ((at, LORA_W - at - w.shape[0]), (0, 0)))
        v2 = p["rwkv_v2"][l - 1] if l > 0 else jnp.zeros((16, RWKV_W), F32)
        loras.append(jnp.stack([pad(p["rwkv_w2"][l], 0), pad(p["rwkv_a2"][l], 32), pad(p["rwkv_g2"][l], 64), pad(v2, 0)]))
    return jnp.stack(tables), _bf(jnp.stack(loras))


def _proj_call(x, p, l, v_first, *, nb):
    T, D = x.shape
    tm = PROJ_TM
    assert T % tm == 0 and (T // nb) % tm == 0
    has_vres = l > 0
    w = p["w_in"]
    if has_vres:
        w = jnp.concatenate([w[l], _bf(jnp.pad(p["w_in_vres"][l - 1], ((0, 0), (0, LORA_W - 16))))], axis=1)
    ins = [x, p["vectors"], w, p["lora"], _block_diag_ones(ATTN_W), _block_diag_ones(KV_W), _block_diag_ones(MEM_W)]
    if has_vres:
        ins.append(v_first)

    def spec(a):
        if a.ndim >= 3:
            return _layer_spec(a, l)
        if a.shape[0] == T:
            return pl.BlockSpec((tm, a.shape[1]), lambda i: (i, 0))
        return pl.BlockSpec(a.shape, lambda i: (0, 0), pipeline_mode=pl.Buffered(1))

    out_shapes = [jax.ShapeDtypeStruct((T, ATTN_W), BF16), jax.ShapeDtypeStruct((T, KV_W), BF16),
                  jax.ShapeDtypeStruct((T, KV_W), BF16), jax.ShapeDtypeStruct((T, MEM_W), BF16)]
    out_shapes += [jax.ShapeDtypeStruct((T, RWKV_W), F32)] * 7
    sw = SHIFT_W + (LORA_W if has_vres else 0)
    return pl.pallas_call(
        functools.partial(_proj_kernel, tm=tm, sub=PROJ_SUB, tiles_per_batch=(T // nb) // tm, has_vres=has_vres),
        grid=(T // tm,),
        in_specs=[spec(a) for a in ins],
        out_specs=[pl.BlockSpec((tm, s.shape[1]), lambda i: (i, 0)) for s in out_shapes],
        out_shape=out_shapes,
        scratch_shapes=[pltpu.VMEM((tm + 8, sw), F32)],
        compiler_params=pltpu.CompilerParams(dimension_semantics=("arbitrary",), vmem_limit_bytes=VMEM_LIMIT_BYTES),
        name=f"proj_l{l}",
    )(*ins)


def _bucket_table():
    qi = np.arange(BLOCK_Q)[:, None]
    kj = np.arange(2 * BLOCK_Q)[None, :]
    dist = qi + BLOCK_Q - kj
    in_band = (dist >= 0) & (dist < WINDOW)
    d = np.maximum(dist, 1).astype(np.float32)
    large = MAX_EXACT + (np.log(d / np.float32(MAX_EXACT)) / np.float32(math.log(WINDOW / MAX_EXACT))
                         * np.float32(N_BUCKETS - MAX_EXACT)).astype(np.int32)
    large = np.minimum(large, N_BUCKETS - 1)
    bucket = np.where(dist < MAX_EXACT, np.maximum(dist, 0), large)
    return np.where(in_band, bucket, -1).astype(np.int32)


def _bias_kernel(rb_ref, bucket_ref, out_ref):
    bucket = bucket_ref[...]
    for h in range(ATTN_HEADS):
        acc = jnp.full(bucket.shape, -jnp.inf, F32)
        for j in range(N_BUCKETS):
            acc = jnp.where(bucket == j, rb_ref[j, h], acc)
        out_ref[h] = acc


def _bias_call(rel_bias):
    tab = pl.pallas_call(
        _bias_kernel,
        in_specs=[pl.BlockSpec(memory_space=pltpu.SMEM), pl.BlockSpec(memory_space=pltpu.VMEM)],
        out_specs=pl.BlockSpec(memory_space=pltpu.VMEM),
        out_shape=jax.ShapeDtypeStruct((ATTN_HEADS, BLOCK_Q, 2 * BLOCK_Q), F32),
        name="rel_bias_table",
    )(rel_bias.astype(F32), jnp.asarray(_bucket_table()))
    return tab.reshape(ATTN_KV_HEADS, ATTN_GROUP * BLOCK_Q, 2 * BLOCK_Q)


def _memkv_kernel(mem_ref, g_ref, w_ref, kg_ref, bd_ref, mk_ref, mv_ref):
    hn = _bf(_rms_rows(mem_ref[...], g_ref[...]))
    kv = _dot(hn, w_ref[...])
    mk_ref[...] = _bf(_head_rms(kv[:, :MEM_W], bd_ref[...], kg_ref[...]))
    mv_ref[...] = _bf(kv[:, MEM_W:])


def _memkv_call(mem2d, p, l):
    rows = mem2d.shape[0]
    vm = pl.BlockSpec(memory_space=pltpu.VMEM)
    return pl.pallas_call(
        _memkv_kernel,
        in_specs=[vm] * 5,
        out_specs=[vm, vm],
        out_shape=[jax.ShapeDtypeStruct((rows, MEM_W), BF16)] * 2,
        compiler_params=pltpu.CompilerParams(vmem_limit_bytes=VMEM_LIMIT_BYTES),
        name=f"mem_kv_l{l}",
    )(mem2d, p["mem_norm_g"][l].reshape(1, -1), _bf(p["w_mem_kv"][l]),
      jnp.tile(p["mem_k_norm"][l], MEM_HEADS).reshape(1, -1), _block_diag_ones(MEM_W))


def _mix_kernel(x_ref, qa_ref, kc_ref, kp_ref, vc_ref, vp_ref, bias_ref, sink_ref, qm_ref, mk_ref, mv_ref,
                y_ref, r_ref, k_ref, v_ref, gate_ref, vec_ref, bd_ref, wout_ref, out_ref, att_ref,
                *, tq, tiles_per_batch):
    lnw_ref, lnb_ref, rk_ref = (_Row(vec_ref, VEC_ROWS[n], RWKV_W) for n in ("ln_w", "ln_b", "r_k"))
    N, BQ = HEAD_DIM, BLOCK_Q
    seq_start = pl.program_id(0) % tiles_per_batch == 0
    nqb = tq // BQ

    @pl.when(pl.program_id(0) == 0)
    def _():
        att_ref[...] = jnp.zeros_like(att_ref)

    qa = qa_ref[...]
    kall = jnp.concatenate([kp_ref[...], kc_ref[...]], axis=0)
    vall = jnp.concatenate([vp_ref[...], vc_ref[...]], axis=0)
    before_seq = lax.broadcasted_iota(jnp.int32, (ATTN_GROUP * BQ, 2 * BQ), 1) < BQ

    qm = qm_ref[...]
    mk = mk_ref[0]
    mv = mv_ref[0]

    swa = [(j, g) for j in range(nqb) for g in range(ATTN_KV_HEADS)]
    logits, values, sinks = [], [], []
    for j, g in swa:
        qg = jnp.concatenate(
            [qa[j * BQ:(j + 1) * BQ, (ATTN_GROUP * g + i) * N:(ATTN_GROUP * g + i + 1) * N] for i in range(ATTN_GROUP)],
            axis=0)
        lg = _dot_nt(qg, kall[j * BQ:(j + 2) * BQ, g * N:(g + 1) * N]) + bias_ref[g]
        if j == 0:
            lg = jnp.where(jnp.logical_and(seq_start, before_seq), -jnp.inf, lg)
        logits.append(lg)
        values.append(vall[j * BQ:(j + 2) * BQ, g * N:(g + 1) * N])
        sinks.append(sink_ref[g])
    for h in range(MEM_HEADS):
        hs = slice(h * N, (h + 1) * N)
        logits.append(_dot_nt(qm[:, hs], mk[:, hs]))
        values.append(mv[:, hs])
        sinks.append(None)
    bd = bd_ref[...]
    y = y_ref[...]
    d = y - _head_sum(y, bd) * (1.0 / N)
    var = _head_sum(d * d, bd) * (1.0 / N)
    yn = d * lax.rsqrt(var + GN_EPS) * lnw_ref[...] + lnb_ref[...]
    bonus = _head_sum(r_ref[...] * k_ref[...] * rk_ref[...], bd) * v_ref[...]
    out_b = (yn + bonus) * gate_ref[...]
    mixed = _bf(jnp.concatenate([att_ref[:, :ATTN_W], out_b, att_ref[:, ATTN_W:]], axis=-1))
    out_ref[...] = x_ref[...] + _dot(mixed, wout_ref[...])

    row_max = [jnp.max(lg, axis=-1, keepdims=True) for lg in logits]
    m = [rm if s is None else jnp.maximum(rm, s) for rm, s in zip(row_max, sinks)]
    e = [jnp.exp(lg - mm) for lg, mm in zip(logits, m)]
    denom = [jnp.sum(ee, axis=-1, keepdims=True) for ee in e]
    denom = [d if s is None else d + jnp.exp(s - mm) for d, s, mm in zip(denom, sinks, m)]
    outs = [_dot(_bf(ee), vv) / d for ee, vv, d in zip(e, values, denom)]

    head_rows = [[None] * nqb for _ in range(ATTN_HEADS)]
    for (j, g), o in zip(swa, outs):
        for i in range(ATTN_GROUP):
            head_rows[ATTN_GROUP * g + i][j] = o[i * BQ:(i + 1) * BQ]
    att_ref[:, :ATTN_W] = jnp.concatenate([jnp.concatenate(rows, axis=0) for rows in head_rows], axis=-1)
    att_ref[:, ATTN_W:] = jnp.concatenate(outs[len(swa):], axis=-1)


def _mix_call(x, qa, ka, va, qm, mk, mv, y, r, k, v, gate, bias, p, l, *, nb):
    T, D = x.shape
    tq = MIX_TQ
    S = T // nb
    assert S % tq == 0 and tq % BLOCK_Q == 0
    tpb = S // tq
    qpb = tq // BLOCK_Q
    mem_tokens = mk.shape[0] // nb
    sink = jnp.repeat(p["attn_sinks"][l].astype(F32), BLOCK_Q).reshape(ATTN_KV_HEADS, ATTN_GROUP * BLOCK_Q, 1)
    n_tiles = T // tq
    att = lambda i: jnp.minimum(i, n_tiles - 1)
    fin = lambda i: jnp.maximum(i - 1, 0)
    att_tile = lambda w: pl.BlockSpec((tq, w), lambda i: (att(i), 0))
    tile = lambda w: pl.BlockSpec((tq, w), lambda i: (fin(i), 0))
    prev = pl.BlockSpec((BLOCK_Q, KV_W), lambda i: (jnp.maximum(att(i) * qpb - 1, 0), 0))
    full = lambda a: pl.BlockSpec(a.shape, lambda i: (0,) * a.ndim)
    memspec = pl.BlockSpec((1, mem_tokens, MEM_W), lambda i: (att(i) // tpb, 0, 0))
    bd = _block_diag_ones(RWKV_W)
    wout = p["w_out"]
    mk3 = mk.reshape(nb, mem_tokens, MEM_W)
    mv3 = mv.reshape(nb, mem_tokens, MEM_W)
    return pl.pallas_call(
        functools.partial(_mix_kernel, tq=tq, tiles_per_batch=tpb),
        grid=(n_tiles + 1,),
        in_specs=[tile(D), att_tile(ATTN_W), att_tile(KV_W), prev, att_tile(KV_W), prev, full(bias), full(sink),
                  att_tile(MEM_W), memspec, memspec] + [tile(RWKV_W)] * 5
                 + [_layer_spec(p["vectors"], l), full(bd), _layer_spec(wout, l)],
        out_specs=tile(D),
        out_shape=jax.ShapeDtypeStruct((T, D), F32),
        scratch_shapes=[pltpu.VMEM((tq, ATTN_W + MEM_W), F32)],
        compiler_params=pltpu.CompilerParams(dimension_semantics=("arbitrary",), vmem_limit_bytes=VMEM_LIMIT_BYTES),
        name=f"mix_l{l}",
    )(x, qa, ka, ka, va, va, bias, sink, qm, mk3, mv3, y, r, k, v, gate, p["vectors"], bd, wout)


SUBLANES = 8


def _tile_copies(hbm, buf, sem, tile, slot, to_vmem):
    copies = []
    for s in range(SUBLANES):
        src, dst = hbm.at[tile, s], buf.at[slot, :, s, :]
        if not to_vmem:
            src, dst = dst, src
        copies.append(pltpu.make_async_copy(src, dst, sem.at[slot, s]))
    return copies


def _shift_rows(u, before, steps):
    first = lax.broadcasted_iota(jnp.int32, (SUBLANES, u.shape[1]), 0) == 0
    tm = u.shape[0]
    heads = []
    for i in range(steps):
        lo = tm - (steps - i) * SUBLANES
        wrapped = pltpu.roll(u[lo:lo + SUBLANES], 1, axis=0)
        prior = jnp.broadcast_to(before[SUBLANES - steps + i:SUBLANES - steps + i + 1], wrapped.shape)
        heads.append(jnp.where(first, prior, wrapped))
    return jnp.concatenate(heads + [u[:tm - steps * SUBLANES]], axis=0)


def _ffn_kernel(x_hbm, vec_ref, wup_ref, cw_ref, cb_ref, wdn_ref, o_hbm, xbuf, obuf, carry_ref, act_ref,
                in_sem, out_sem, *, tm, tiles_per_batch, n_tiles):
    D = xbuf.shape[-1]
    g_ref = _Row(vec_ref, VEC_ROWS["ffn_norm_g"], D)
    q = tm // SUBLANES
    i = pl.program_id(0)
    slot = i % 2

    @pl.when(i == 0)
    def _():
        carry_ref[...] = jnp.zeros_like(carry_ref)
        for cp in _tile_copies(x_hbm, xbuf, in_sem, 0, 0, True):
            cp.start()

    @pl.when(i + 1 < n_tiles)
    def _():
        for cp in _tile_copies(x_hbm, xbuf, in_sem, i + 1, 1 - slot, True):
            cp.start()

    for cp in _tile_copies(x_hbm, xbuf, in_sem, i, slot, True):
        cp.wait()

    @pl.when(i >= 2)
    def _():
        for cp in _tile_copies(o_hbm, obuf, out_sem, i - 2, slot, False):
            cp.wait()

    seq_start = i % tiles_per_batch == 0
    g = g_ref[...]
    x = xbuf[slot].reshape(tm, D)
    before = jnp.where(seq_start, 0.0, _rms_rows(carry_ref[...], g))
    for r, grp in ((SUBLANES - 2, q - 2), (SUBLANES - 1, q - 1)):
        row = grp * SUBLANES + SUBLANES - 1
        carry_ref[r:r + 1, :] = x[row:row + 1]
    h_ext = _bf(jnp.concatenate([before, _rms_rows(x, g)], axis=0))

    n_chunks = D_FF // FFN_FC
    gate_cols = lambda c: slice(c * FFN_FC, (c + 1) * FFN_FC)
    val_cols = lambda c: slice(D_FF + c * FFN_FC, D_FF + (c + 1) * FFN_FC)

    def conv(u_ext, cols):
        w = cw_ref[:, cols]
        u = u_ext[SUBLANES:]
        return (cb_ref[:, cols] + w[0:1] * _shift_rows(u, u_ext[:SUBLANES], 2)
                + w[1:2] * _shift_rows(u, u_ext[:SUBLANES], 1) + w[2:3] * u)

    up = lambda c: (_dot(h_ext, wup_ref[:, gate_cols(c)]), _dot(h_ext, wup_ref[:, val_cols(c)]))
    nxt = up(0)
    for c in range(n_chunks):
        ug, uv = nxt
        if c + 1 < n_chunks:
            nxt = up(c + 1)
        gt = conv(ug, gate_cols(c))
        act_ref[:, gate_cols(c)] = _bf(gt * jax.nn.sigmoid(gt) * conv(uv, val_cols(c)))
    obuf[slot] = (x + _dot(act_ref[...], wdn_ref[...])).reshape(q, SUBLANES, D)

    for cp in _tile_copies(o_hbm, obuf, out_sem, i, slot, False):
        cp.start()

    @pl.when(i == n_tiles - 1)
    def _():
        if n_tiles >= 2:
            for cp in _tile_copies(o_hbm, obuf, out_sem, i - 1, 1 - slot, False):
                cp.wait()
        for cp in _tile_copies(o_hbm, obuf, out_sem, i, slot, False):
            cp.wait()


def _ffn_call(x, p, l, *, nb):
    T, D = x.shape
    tm = FFN_TM
    S = T // nb
    q = tm // SUBLANES
    assert S % tm == 0 and q % SUBLANES == 0 and D_FF % FFN_FC == 0
    n_tiles = T // tm
    wup, wdn = p["w_up"], p["w_down"]
    cw, cb = p["conv_w"], p["conv_b"][:, None, :]
    out = pl.pallas_call(
        functools.partial(_ffn_kernel, tm=tm, tiles_per_batch=S // tm, n_tiles=n_tiles),
        grid=(n_tiles,),
        in_specs=[pl.BlockSpec(memory_space=pl.ANY)] + [_layer_spec(a, l) for a in (p["vectors"], wup, cw, cb, wdn)],
        out_specs=pl.BlockSpec(memory_space=pl.ANY),
        out_shape=jax.ShapeDtypeStruct((n_tiles, SUBLANES, q, D), F32),
        scratch_shapes=[pltpu.VMEM((2, q, SUBLANES, D), F32), pltpu.VMEM((2, q, SUBLANES, D), F32),
                        pltpu.VMEM((SUBLANES, D), F32), pltpu.VMEM((tm, D_FF), BF16),
                        pltpu.SemaphoreType.DMA((2, SUBLANES)), pltpu.SemaphoreType.DMA((2, SUBLANES))],
        compiler_params=pltpu.CompilerParams(dimension_semantics=("arbitrary",), vmem_limit_bytes=VMEM_LIMIT_BYTES),
        name=f"ffn_l{l}",
    )(x.reshape(n_tiles, SUBLANES, q, D), p["vectors"], wup, cw, cb, wdn)
    return out.reshape(T, D)


_PARAM_NAMES = (
    "rel_bias", "mix_norm_g", "w_in", "w_in_vres", "attn_q_norm", "attn_k_norm", "attn_sinks", "rwkv_mu",
    "rwkv_mu_vres", "rwkv_w0", "rwkv_w2", "rwkv_a0", "rwkv_a2", "rwkv_v0", "rwkv_v2", "rwkv_g2", "rwkv_k_k",
    "rwkv_k_a", "rwkv_r_k", "rwkv_ln_w", "rwkv_ln_b", "mem_norm_g", "w_mem_kv", "mem_q_norm", "mem_k_norm",
    "w_out", "ffn_norm_g", "w_up", "conv_w", "conv_b", "w_down")


def kernel(x, mem, rel_bias, mix_norm_g, w_in, w_in_vres, attn_q_norm, attn_k_norm, attn_sinks, rwkv_mu,
           rwkv_mu_vres, rwkv_w0, rwkv_w2, rwkv_a0, rwkv_a2, rwkv_v0, rwkv_v2, rwkv_g2, rwkv_k_k, rwkv_k_a,
           rwkv_r_k, rwkv_ln_w, rwkv_ln_b, mem_norm_g, w_mem_kv, mem_q_norm, mem_k_norm, w_out, ffn_norm_g,
           w_up, conv_w, conv_b, w_down):
    p = dict(zip(_PARAM_NAMES, (
        rel_bias, mix_norm_g, w_in, w_in_vres, attn_q_norm, attn_k_norm, attn_sinks, rwkv_mu, rwkv_mu_vres,
        rwkv_w0, rwkv_w2, rwkv_a0, rwkv_a2, rwkv_v0, rwkv_v2, rwkv_g2, rwkv_k_k, rwkv_k_a, rwkv_r_k, rwkv_ln_w,
        rwkv_ln_b, mem_norm_g, w_mem_kv, mem_q_norm, mem_k_norm, w_out, ffn_norm_g, w_up, conv_w, conv_b, w_down)))
    p["vectors"], p["lora"] = _pack_params(p)
    for name in ("w_in", "w_mem_kv", "w_out", "w_up", "w_down"):
        p[name] = _bf(p[name])
    nb, S, D = x.shape
    xt = x.reshape(nb * S, D)
    mem2d = mem.reshape(nb * mem.shape[1], D)
    bias = _bias_call(rel_bias)
    v_first = None
    for l in range(w_in.shape[0]):
        qa, ka, va, qm, r, lw, k, v, a, b, gate = _proj_call(xt, p, l, v_first, nb=nb)
        if l == 0:
            v_first = v
        y = _rwkv_scan(r, lw, k, v, a, b, nb=nb)
        mk, mv = _memkv_call(mem2d, p, l)
        xt = _mix_call(xt, qa, ka, va, qm, mk, mv, y, r, k, v, gate, bias, p, l, nb=nb)
        xt = _ffn_call(xt, p, l, nb=nb)
    return xt.reshape(nb, S, D)
```

```python
import functools
import math

import jax
import jax.numpy as jnp
import numpy as np
from jax import lax
from jax.experimental import pallas as pl
from jax.experimental.pallas import tpu as pltpu

F32 = jnp.float32
BF16 = jnp.bfloat16

HEAD_DIM = 64
SCAN_CHUNK = 64
VMEM_LIMIT_BYTES = 56 * 1024 * 1024


def _dot(a, b):
    return jnp.dot(a, b, preferred_element_type=F32)


def _dot_nt(a, b):
    return lax.dot_general(a, b, (((1,), (1,)), ((), ())), preferred_element_type=F32)


def _dot_tn(a, b):
    return lax.dot_general(a, b, (((0,), (0,)), ((), ())), preferred_element_type=F32)


def _bf(x):
    return x.astype(BF16)


SCAN_GROUP_HEADS = 4
SCAN_GROUP_W = SCAN_GROUP_HEADS * HEAD_DIM
SCAN_CHUNKS_PER_STEP = 4


_SCAN_STASH = (("w2", 1, F32), ("y_v", 1, F32), ("m_rb", 1, F32), ("v", 1, F32), ("w1r", 2, F32),
               ("bk", 2, F32), ("g_row", 1, F32))


def _scan_kernel(r_ref, lw_ref, k_ref, v_ref, a_ref, b_ref, bdm_ref, y_ref, h_ref, *stash_refs, nb, nch):
    C, N, GW, HPG = SCAN_CHUNK, HEAD_DIM, SCAN_GROUP_W, SCAN_GROUP_HEADS
    ngroups = h_ref.shape[0]
    stash = {name: ref for (name, _, _), ref in zip(_SCAN_STASH, stash_refs)}

    @pl.when(pl.program_id(0) == 0)
    def _():
        h_ref[...] = jnp.zeros_like(h_ref)
        for ref in stash_refs:
            ref[...] = jnp.zeros_like(ref)

    trow = lax.broadcasted_iota(jnp.int32, (C, C), 0)
    tcol = lax.broadcasted_iota(jnp.int32, (C, C), 1)
    tri = (trow >= tcol).astype(BF16)
    grow = lax.broadcasted_iota(jnp.int32, (C, GW), 0)
    gcol = lax.broadcasted_iota(jnp.int32, (C, GW), 1) % N
    incl = grow >= gcol
    strict = grow > gcol
    eye = (grow == gcol).astype(F32)
    bdm = bdm_ref[...]

    def bd(x):
        return jnp.concatenate([_bf(x)] * HPG, axis=0) * bdm

    def split2(x):
        hi = _bf(x)
        return hi, _bf(x - hi.astype(F32))

    def chunk_units(c):
        rows = slice(c * C, (c + 1) * C)
        per_batch = []
        for b in range(nb):
            lw = lw_ref[b, rows, :]
            l1 = _bf(lw)
            e1 = lw - l1.astype(F32)
            l2 = _bf(e1)
            l3 = _bf(e1 - l2.astype(F32))
            cum = _dot(tri, l1) + _dot(tri, l2) + _dot(tri, l3)
            cum_last = cum[C - 1:C, :]
            g_inv = jnp.exp(-cum)
            g_out = jnp.exp(cum_last - cum)
            kk = k_ref[b, rows, :]
            bb = b_ref[b, rows, :]
            per_batch.append(dict(
                a_t=a_ref[b, rows, :] * jnp.exp(cum - lw), r_t=r_ref[b, rows, :] * jnp.exp(cum),
                b_t=bb * g_inv, k_t=kk * g_inv, b_h=bb * g_out, k_h=kk * g_out,
                g_c=jnp.broadcast_to(jnp.exp(cum_last), cum.shape), v=v_ref[b, rows, :]))
        cat = {n: jnp.concatenate([pb[n] for pb in per_batch], axis=1) for n in per_batch[0]}
        return [{n: t[:, g * GW:(g + 1) * GW] for n, t in cat.items()} for g in range(ngroups)]

    def prepare(chunks):
        units = [u for c in chunks for u in chunk_units(c)]

        nu = len(units)
        ar = [jnp.concatenate([_bf(u["a_t"]), _bf(u["r_t"])], axis=0) for u in units]
        pb = [_dot_nt(ar[i], bd(units[i]["b_t"])) for i in range(nu)]
        pk = [_dot_nt(ar[i], bd(units[i]["k_t"])) for i in range(nu)]
        l_ab = [jnp.where(strict, p[:C], 0.0) for p in pb]
        m_rb = [_bf(jnp.where(incl, p[C:], 0.0)) for p in pb]
        tril2 = jnp.concatenate([strict, incl], axis=0)
        lm_k = [_bf(jnp.where(tril2, p, 0.0)) for p in pk]
        yield
        t_inv = [eye + l for l in l_ab]
        pw = [_dot(_bf(l), bd(l)) for l in l_ab]
        yield
        n_rounds = int(math.log2(C)) - 1
        for rnd in range(n_rounds):
            if rnd + 1 < n_rounds:
                z = [_dot(jnp.concatenate([_bf(p), _bf(t)], axis=0), bd(p)) for p, t in zip(pw, t_inv)]
                pw = [zz[:C] for zz in z]
                t_inv = [t + zz[C:] for t, zz in zip(t_inv, z)]
            else:
                t_inv = [t + _dot(_bf(t), bd(p)) for t, p in zip(t_inv, pw)]
            yield
        t_b = [_bf(t) for t in t_inv]
        w1 = [_dot(t, bd(u["a_t"])) for t, u in zip(t_b, units)]
        lmv = [_dot(m, bd(u["v"])) for m, u in zip(lm_k, units)]
        yield
        w2 = [_dot(t, bd(x[:C])) for t, x in zip(t_b, lmv)]
        parts = [p for u in units for p in split2(u["g_c"] * eye)]
        sums = _dot(jnp.concatenate(parts, axis=0), bdm)
        yield
        for i, u in enumerate(units):
            put = lambda name, val: stash[name].__setitem__(i, val.astype(stash[name].dtype))
            put("w2", w2[i])
            put("y_v", lmv[i][C:])
            put("m_rb", m_rb[i])
            put("v", u["v"])
            put("w1r", jnp.concatenate([w1[i], u["r_t"]], axis=0))
            put("bk", jnp.concatenate([u["b_h"], u["k_h"]], axis=0))
            put("g_row", sums[2 * i * C:(2 * i + 1) * C] + sums[(2 * i + 1) * C:(2 * i + 2) * C])
        yield

    state = [h_ref[g] for g in range(ngroups)]

    def advance(c):
        p = {n: [ref[c * ngroups + g] for g in range(ngroups)] for n, ref in stash.items()}
        for n in ("m_rb", "v", "w1r", "bk"):
            p[n] = [_bf(x) for x in p[n]]
        z = [_dot(w, bd(s)) for w, s in zip(p["w1r"], state)]
        yield
        u = [zz[:C] + w for zz, w in zip(z, p["w2"])]
        y = [zz[C:] + _dot(m, bd(x)) + yv for zz, m, x, yv in zip(z, p["m_rb"], u, p["y_v"])]
        uv = [jnp.concatenate([_bf(x), v], axis=0) for x, v in zip(u, p["v"])]
        full = [_dot_tn(b, x) * bdm.astype(F32) for b, x in zip(p["bk"], uv)]
        yield
        upd = [sum(f[h * N:(h + 1) * N] for h in range(HPG)) for f in full]
        state[:] = [g * s + d for g, s, d in zip(p["g_row"], state, upd)]
        ycat = jnp.concatenate(y, axis=1)
        w = y_ref.shape[2]
        for b in range(nb):
            y_ref[b, c * C:(c + 1) * C, :] = ycat[:, b * w:(b + 1) * w]
        yield

    def run(gen, stages):
        for _ in range(stages):
            next(gen, None)

    prep = prepare(list(range(nch)))
    n_prep = 5 + int(math.log2(C)) - 1
    for c in range(nch):
        seq = advance(c)
        run(prep, 1)
        run(seq, 1)
        run(prep, 1)
        run(seq, 2)
    run(prep, n_prep)
    for g in range(ngroups):
        h_ref[g] = state[g]


def _rwkv_scan(r, lw, k, v, a, b, *, nb):
    T, W = r.shape
    S = T // nb
    C, nch, GW = SCAN_CHUNK, SCAN_CHUNKS_PER_STEP, SCAN_GROUP_W
    assert S % (C * nch) == 0 and (nb * W) % GW == 0 and W % 128 == 0
    ngroups = nb * W // GW
    head = np.arange(GW) // HEAD_DIM
    bdm = jnp.asarray((head[:, None] == head[None, :]).astype(np.float32), dtype=BF16)
    n_blocks = S // (C * nch)
    in_spec = pl.BlockSpec((nb, C * nch, W), lambda s: (0, jnp.minimum(s, n_blocks - 1), 0))
    out_spec = pl.BlockSpec((nb, C * nch, W), lambda s: (0, jnp.maximum(s - 1, 0), 0))
    args = [t.reshape(nb, S, W) for t in (r, lw, k, v, a, b)]
    y = pl.pallas_call(
        functools.partial(_scan_kernel, nb=nb, nch=nch),
        grid=(n_blocks + 1,),
        in_specs=[in_spec] * 6 + [pl.BlockSpec((GW, GW), lambda s: (0, 0))],
        out_specs=out_spec,
        out_shape=jax.ShapeDtypeStruct((nb, S, W), F32),
        scratch_shapes=[pltpu.VMEM((ngroups, HEAD_DIM, GW), F32)]
        + [pltpu.VMEM((nch * ngroups, rows * C, GW), dt) for _, rows, dt in _SCAN_STASH],
        compiler_params=pltpu.CompilerParams(dimension_semantics=("arbitrary",), vmem_limit_bytes=VMEM_LIMIT_BYTES),
        name="rwkv_scan",
    )(*args, bdm)
    return y.reshape(T, W)


D_MODEL = 1024
ATTN_HEADS = 6
ATTN_KV_HEADS = 2
ATTN_GROUP = ATTN_HEADS // ATTN_KV_HEADS
ATTN_W = ATTN_HEADS * HEAD_DIM
KV_W = ATTN_KV_HEADS * HEAD_DIM
BLOCK_Q = 128
WINDOW = 128
N_BUCKETS = 32
MAX_EXACT = N_BUCKETS // 2
RWKV_W = 6 * HEAD_DIM
LORA_W = 128
SHIFT_W = 3 * RWKV_W + LORA_W
MEM_HEADS = 4
MEM_W = MEM_HEADS * HEAD_DIM
IN_BASE = ATTN_W + 2 * KV_W + SHIFT_W + MEM_W
PB_OFF = ATTN_W + 2 * KV_W
QM_OFF = PB_OFF + SHIFT_W
D_FF = 2816
EPS = 1e-6
GN_EPS = 64e-5
L2_EPS = 1e-12
MXU_WIDTH = 256

PROJ_TM = 1024
PROJ_SUB = 256
MIX_TQ = 512
FFN_TM = 512
FFN_FC = 256
FFN_STAGE_CHUNKS = 8


def _block_diag_ones(width):
    idx = np.arange(width) // HEAD_DIM
    return jnp.asarray((idx[:, None] == idx[None, :]).astype(np.float32), dtype=BF16)


def _head_sum(t, bd):
    return _dot(_bf(t), bd)


def _head_rms(t, bd, gain):
    ms = _head_sum(t * t, bd) * (1.0 / HEAD_DIM)
    return t * lax.rsqrt(ms + EPS) * gain


def _rms_rows(x, g):
    ms = jnp.mean(x * x, axis=-1, keepdims=True)
    return x * lax.rsqrt(ms + EPS) * g


class _Row:
    def __init__(self, table_ref, row, width):
        self.table_ref, self.row, self.width = table_ref, row, width

    def __getitem__(self, _):
        return self.table_ref[self.row:self.row + 1, 0:self.width]


def _proj_kernel(*refs, tm, sub, tiles_per_batch, has_vres):
    it = iter(refs)
    x_ref, vec_ref, w_ref, lora_ref = next(it), next(it), next(it), next(it)
    bd384_ref, bd128_ref, bd256_ref = next(it), next(it), next(it)
    if has_vres:
        vfirst_ref = next(it)
    qa_ref, ka_out_ref, va_ref, qm_ref = next(it), next(it), next(it), next(it)
    r_ref, lw_ref, k_ref, v_ref, a_ref, b_ref, gate_ref = (next(it) for _ in range(7))
    pbs_ref = next(it)
    sw = pbs_ref.shape[1]
    g_ref, qg_ref, kg_ref, mg_ref = (_Row(vec_ref, VEC_ROWS[n], w) for n, w in (
        ("mix_norm_g", D_MODEL), ("q_gain", ATTN_W), ("k_gain", KV_W), ("mq_gain", MEM_W)))
    mu_ref = _Row(vec_ref, VEC_ROWS["mu"], sw)
    w0_ref, a0_ref, kk_ref, ka_ref, v0_ref = (_Row(vec_ref, VEC_ROWS[n], RWKV_W) for n in ("w0", "a0", "k_k", "k_a", "v0"))
    w2_ref, a2_ref, g2_ref, v2_ref = (lora_ref.at[j] for j in range(4))

    @pl.when(pl.program_id(0) % tiles_per_batch == 0)
    def _():
        pbs_ref[0:8, :] = jnp.zeros((8, sw), F32)

    def project(i):
        rows = slice(i * sub, (i + 1) * sub)
        return _dot(_bf(_rms_rows(x_ref[rows, :], g_ref[...])), w_ref[...])

    def attention_outputs(i, proj):
        rows = slice(i * sub, (i + 1) * sub)
        qa_ref[rows, :] = _bf(_head_rms(proj[:, :ATTN_W], bd384_ref[...], qg_ref[...]))
        ka_out_ref[rows, :] = _bf(_head_rms(proj[:, ATTN_W:ATTN_W + KV_W], bd128_ref[...], kg_ref[...]))
        va_ref[rows, :] = _bf(proj[:, ATTN_W + KV_W:PB_OFF])
        qm_ref[rows, :] = _bf(_head_rms(proj[:, QM_OFF:QM_OFF + MEM_W], bd256_ref[...], mg_ref[...]))

    def rwkv_outputs(i, proj):
        rows = slice(i * sub, (i + 1) * sub)
        lo = 8 + i * sub
        pbs_ref[lo:lo + sub, 0:SHIFT_W] = proj[:, PB_OFF:PB_OFF + SHIFT_W]
        if has_vres:
            pbs_ref[lo:lo + sub, SHIFT_W:sw] = proj[:, IN_BASE:IN_BASE + LORA_W]
        cur = pbs_ref[lo:lo + sub, :]
        prev = pbs_ref[lo - 1:lo - 1 + sub, :]
        sh = cur + mu_ref[...] * (prev - cur)
        r = sh[:, 0:RWKV_W]
        k = sh[:, RWKV_W:2 * RWKV_W]
        v = sh[:, 2 * RWKV_W:3 * RWKV_W]
        z = sh[:, 3 * RWKV_W:SHIFT_W]
        t = w0_ref[...] + _dot(_bf(jnp.tanh(z)), w2_ref[...])
        lw_ref[rows, :] = -math.exp(-0.5) * jax.nn.sigmoid(t)
        a = jax.nn.sigmoid(a0_ref[...] + _dot(_bf(z), a2_ref[...]))
        gate_ref[rows, :] = _dot(_bf(jax.nn.sigmoid(z)), g2_ref[...])
        if has_vres:
            vd = sh[:, SHIFT_W:sw]
            v = v + (vfirst_ref[rows, :] - v) * jax.nn.sigmoid(v0_ref[...] + _dot(_bf(vd), v2_ref[...]))
        kk = k * kk_ref[...]
        kk = kk / jnp.maximum(jnp.sqrt(_head_sum(kk * kk, bd384_ref[...])), L2_EPS)
        r_ref[rows, :] = r
        k_ref[rows, :] = k * (1.0 + (a - 1.0) * ka_ref[...])
        v_ref[rows, :] = v
        a_ref[rows, :] = -kk
        b_ref[rows, :] = kk * a

    n_sub = tm // sub
    proj = project(0)
    for i in range(n_sub):
        attention_outputs(i, proj)
        nxt = project(i + 1) if i + 1 < n_sub else None
        rwkv_outputs(i, proj)
        proj = nxt
    pbs_ref[0:8, :] = pbs_ref[tm:tm + 8, :]


def _layer_spec(stacked, l):
    return pl.BlockSpec((None,) + stacked.shape[1:], lambda i: (l,) + (0,) * (stacked.ndim - 1),
                        pipeline_mode=pl.Buffered(1))


VEC_W = SHIFT_W + LORA_W
VEC_ROWS = {n: i for i, n in enumerate((
    "mix_norm_g", "q_gain", "k_gain", "mq_gain", "mu", "w0", "a0", "k_k", "k_a", "v0", "ln_w", "ln_b", "r_k",
    "ffn_norm_g"))}
VEC_TABLE_ROWS = 16


def _pack_params(p):
    layers = p["w_in"].shape[0]
    scale = HEAD_DIM ** -0.5
    zeros = jnp.zeros((RWKV_W,), F32)
    tables, loras = [], []
    for l in range(layers):
        mu = p["rwkv_mu"][l]
        if l > 0:
            mu = jnp.concatenate([mu, p["rwkv_mu_vres"][l - 1]])
        vecs = dict(
            mix_norm_g=p["mix_norm_g"][l], q_gain=jnp.tile(p["attn_q_norm"][l], ATTN_HEADS) * scale,
            k_gain=jnp.tile(p["attn_k_norm"][l], ATTN_KV_HEADS), mq_gain=jnp.tile(p["mem_q_norm"][l], MEM_HEADS) * scale,
            mu=mu, w0=p["rwkv_w0"][l], a0=p["rwkv_a0"][l], k_k=p["rwkv_k_k"][l], k_a=p["rwkv_k_a"][l],
            v0=p["rwkv_v0"][l - 1] if l > 0 else zeros, ln_w=p["rwkv_ln_w"][l], ln_b=p["rwkv_ln_b"][l],
            r_k=p["rwkv_r_k"][l].reshape(-1), ffn_norm_g=p["ffn_norm_g"][l])
        rows = [jnp.pad(vecs[n].astype(F32), (0, VEC_W - vecs[n].shape[0])) for n in VEC_ROWS]
        rows += [jnp.zeros((VEC_W,), F32)] * (VEC_TABLE_ROWS - len(rows))
        tables.append(jnp.stack(rows))
        pad = lambda w, at: jnp.pad(w, ((at, LORA_W - at - w.shape[0]), (0, 0)))
        v2 = p["rwkv_v2"][l - 1] if l > 0 else jnp.zeros((16, RWKV_W), F32)
        loras.append(jnp.stack([pad(p["rwkv_w2"][l], 0), pad(p["rwkv_a2"][l], 32), pad(p["rwkv_g2"][l], 64), pad(v2, 0)]))
    return jnp.stack(tables), _bf(jnp.stack(loras))


def _proj_call(x, p, l, v_first, *, nb):
    T, D = x.shape
    tm = PROJ_TM
    assert T % tm == 0 and (T // nb) % tm == 0
    has_vres = l > 0
    w = p["w_in"]
    if has_vres:
        w = jnp.concatenate([w[l], _bf(jnp.pad(p["w_in_vres"][l - 1], ((0, 0), (0, LORA_W - 16))))], axis=1)
    ins = [x, p["vectors"], w, p["lora"], _block_diag_ones(ATTN_W), _block_diag_ones(KV_W), _block_diag_ones(MEM_W)]
    if has_vres:
        ins.append(v_first)

    def spec(a):
        if a.ndim >= 3:
            return _layer_spec(a, l)
        if a.shape[0] == T:
            return pl.BlockSpec((tm, a.shape[1]), lambda i: (i, 0))
        return pl.BlockSpec(a.shape, lambda i: (0, 0), pipeline_mode=pl.Buffered(1))

    out_shapes = [jax.ShapeDtypeStruct((T, ATTN_W), BF16), jax.ShapeDtypeStruct((T, KV_W), BF16),
                  jax.ShapeDtypeStruct((T, KV_W), BF16), jax.ShapeDtypeStruct((T, MEM_W), BF16)]
    out_shapes += [jax.ShapeDtypeStruct((T, RWKV_W), F32)] * 7
    sw = SHIFT_W + (LORA_W if has_vres else 0)
    return pl.pallas_call(
        functools.partial(_proj_kernel, tm=tm, sub=PROJ_SUB, tiles_per_batch=(T // nb) // tm, has_vres=has_vres),
        grid=(T // tm,),
        in_specs=[spec(a) for a in ins],
        out_specs=[pl.BlockSpec((tm, s.shape[1]), lambda i: (i, 0)) for s in out_shapes],
        out_shape=out_shapes,
        scratch_shapes=[pltpu.VMEM((tm + 8, sw), F32)],
        compiler_params=pltpu.CompilerParams(dimension_semantics=("arbitrary",), vmem_limit_bytes=VMEM_LIMIT_BYTES),
        name=f"proj_l{l}",
    )(*ins)


def _bucket_table():
    qi = np.arange(BLOCK_Q)[:, None]
    kj = np.arange(2 * BLOCK_Q)[None, :]
    dist = qi + BLOCK_Q - kj
    in_band = (dist >= 0) & (dist < WINDOW)
    d = np.maximum(dist, 1).astype(np.float32)
    large = MAX_EXACT + (np.log(d / np.float32(MAX_EXACT)) / np.float32(math.log(WINDOW / MAX_EXACT))
                         * np.float32(N_BUCKETS - MAX_EXACT)).astype(np.int32)
    large = np.minimum(large, N_BUCKETS - 1)
    bucket = np.where(dist < MAX_EXACT, np.maximum(dist, 0), large)
    return np.where(in_band, bucket, -1).astype(np.int32)


def _bias_kernel(rb_ref, bucket_ref, out_ref):
    bucket = bucket_ref[...]
    for h in range(ATTN_HEADS):
        acc = jnp.full(bucket.shape, -jnp.inf, F32)
        for j in range(N_BUCKETS):
            acc = jnp.where(bucket == j, rb_ref[j, h], acc)
        out_ref[h] = acc


def _bias_call(rel_bias):
    tab = pl.pallas_call(
        _bias_kernel,
        in_specs=[pl.BlockSpec(memory_space=pltpu.SMEM), pl.BlockSpec(memory_space=pltpu.VMEM)],
        out_specs=pl.BlockSpec(memory_space=pltpu.VMEM),
        out_shape=jax.ShapeDtypeStruct((ATTN_HEADS, BLOCK_Q, 2 * BLOCK_Q), F32),
        name="rel_bias_table",
    )(rel_bias.astype(F32), jnp.asarray(_bucket_table()))
    return tab.reshape(ATTN_KV_HEADS, ATTN_GROUP * BLOCK_Q, 2 * BLOCK_Q)


def _memkv_kernel(mem_ref, g_ref, w_ref, kg_ref, bd_ref, mk_ref, mv_ref):
    hn = _bf(_rms_rows(mem_ref[...], g_ref[...]))
    kv = _dot(hn, w_ref[...])
    mk_ref[...] = _bf(_head_rms(kv[:, :MEM_W], bd_ref[...], kg_ref[...]))
    mv_ref[...] = _bf(kv[:, MEM_W:])


def _memkv_call(mem2d, p, l):
    rows = mem2d.shape[0]
    vm = pl.BlockSpec(memory_space=pltpu.VMEM)
    return pl.pallas_call(
        _memkv_kernel,
        in_specs=[vm] * 5,
        out_specs=[vm, vm],
        out_shape=[jax.ShapeDtypeStruct((rows, MEM_W), BF16)] * 2,
        compiler_params=pltpu.CompilerParams(vmem_limit_bytes=VMEM_LIMIT_BYTES),
        name=f"mem_kv_l{l}",
    )(mem2d, p["mem_norm_g"][l].reshape(1, -1), _bf(p["w_mem_kv"][l]),
      jnp.tile(p["mem_k_norm"][l], MEM_HEADS).reshape(1, -1), _block_diag_ones(MEM_W))


def _mix_kernel(x_ref, qa_ref, kc_ref, kp_ref, vc_ref, vp_ref, bias_ref, sink_ref, qm_ref, mk_ref, mv_ref,
                y_ref, r_ref, k_ref, v_ref, gate_ref, vec_ref, bd_ref, wout_ref, out_ref, att_ref,
                *, tq, tiles_per_batch):
    lnw_ref, lnb_ref, rk_ref = (_Row(vec_ref, VEC_ROWS[n], RWKV_W) for n in ("ln_w", "ln_b", "r_k"))
    N, BQ = HEAD_DIM, BLOCK_Q
    seq_start = pl.program_id(0) % tiles_per_batch == 0
    nqb = tq // BQ

    @pl.when(pl.program_id(0) == 0)
    def _():
        att_ref[...] = jnp.zeros_like(att_ref)

    qa = qa_ref[...]
    kall = jnp.concatenate([kp_ref[...], kc_ref[...]], axis=0)
    vall = jnp.concatenate([vp_ref[...], vc_ref[...]], axis=0)
    before_seq = lax.broadcasted_iota(jnp.int32, (ATTN_GROUP * BQ, 2 * BQ), 1) < BQ

    qm = qm_ref[...]
    mk = mk_ref[0]
    mv = mv_ref[0]

    swa = [(j, g) for j in range(nqb) for g in range(ATTN_KV_HEADS)]
    logits, values, sinks = [], [], []
    for j, g in swa:
        qg = jnp.concatenate(
            [qa[j * BQ:(j + 1) * BQ, (ATTN_GROUP * g + i) * N:(ATTN_GROUP * g + i + 1) * N] for i in range(ATTN_GROUP)],
            axis=0)
        lg = _dot_nt(qg, kall[j * BQ:(j + 2) * BQ, g * N:(g + 1) * N]) + bias_ref[g]
        if j == 0:
            lg = jnp.where(jnp.logical_and(seq_start, before_seq), -jnp.inf, lg)
        logits.append(lg)
        values.append(vall[j * BQ:(j + 2) * BQ, g * N:(g + 1) * N])
        sinks.append(sink_ref[g])
    for h in range(MEM_HEADS):
        hs = slice(h * N, (h + 1) * N)
        logits.append(_dot_nt(qm[:, hs], mk[:, hs]))
        values.append(mv[:, hs])
        sinks.append(None)
    bd = bd_ref[...]
    y = y_ref[...]
    d = y - _head_sum(y, bd) * (1.0 / N)
    var = _head_sum(d * d, bd) * (1.0 / N)
    yn = d * lax.rsqrt(var + GN_EPS) * lnw_ref[...] + lnb_ref[...]
    bonus = _head_sum(r_ref[...] * k_ref[...] * rk_ref[...], bd) * v_ref[...]
    out_b = (yn + bonus) * gate_ref[...]
    mixed = _bf(jnp.concatenate([att_ref[:, :ATTN_W], out_b, att_ref[:, ATTN_W:]], axis=-1))
    out_ref[...] = x_ref[...] + _dot(mixed, wout_ref[...])

    row_max = [jnp.max(lg, axis=-1, keepdims=True) for lg in logits]
    m = [rm if s is None else jnp.maximum(rm, s) for rm, s in zip(row_max, sinks)]
    e = [jnp.exp(lg - mm) for lg, mm in zip(logits, m)]
    denom = [jnp.sum(ee, axis=-1, keepdims=True) for ee in e]
    denom = [d if s is None else d + jnp.exp(s - mm) for d, s, mm in zip(denom, sinks, m)]
    outs = [_dot(_bf(ee), vv) / d for ee, vv, d in zip(e, values, denom)]

    head_rows = [[None] * nqb for _ in range(ATTN_HEADS)]
    for (j, g), o in zip(swa, outs):
        for i in range(ATTN_GROUP):
            head_rows[ATTN_GROUP * g + i][j] = o[i * BQ:(i + 1) * BQ]
    att_ref[:, :ATTN_W] = jnp.concatenate([jnp.concatenate(rows, axis=0) for rows in head_rows], axis=-1)
    att_ref[:, ATTN_W:] = jnp.concatenate(outs[len(swa):], axis=-1)


def _mix_call(x, qa, ka, va, qm, mk, mv, y, r, k, v, gate, bias, p, l, *, nb):
    T, D = x.shape
    tq = MIX_TQ
    S = T // nb
    assert S % tq == 0 and tq % BLOCK_Q == 0
    tpb = S // tq
    qpb = tq // BLOCK_Q
    mem_tokens = mk.shape[0] // nb
    sink = jnp.repeat(p["attn_sinks"][l].astype(F32), BLOCK_Q).reshape(ATTN_KV_HEADS, ATTN_GROUP * BLOCK_Q, 1)
    n_tiles = T // tq
    att = lambda i: jnp.minimum(i, n_tiles - 1)
    fin = lambda i: jnp.maximum(i - 1, 0)
    att_tile = lambda w: pl.BlockSpec((tq, w), lambda i: (att(i), 0))
    tile = lambda w: pl.BlockSpec((tq, w), lambda i: (fin(i), 0))
    prev = pl.BlockSpec((BLOCK_Q, KV_W), lambda i: (jnp.maximum(att(i) * qpb - 1, 0), 0))
    full = lambda a: pl.BlockSpec(a.shape, lambda i: (0,) * a.ndim)
    memspec = pl.BlockSpec((1, mem_tokens, MEM_W), lambda i: (att(i) // tpb, 0, 0))
    bd = _block_diag_ones(RWKV_W)
    wout = p["w_out"]
    mk3 = mk.reshape(nb, mem_tokens, MEM_W)
    mv3 = mv.reshape(nb, mem_tokens, MEM_W)
    return pl.pallas_call(
        functools.partial(_mix_kernel, tq=tq, tiles_per_batch=tpb),
        grid=(n_tiles + 1,),
        in_specs=[tile(D), att_tile(ATTN_W), att_tile(KV_W), prev, att_tile(KV_W), prev, full(bias), full(sink),
                  att_tile(MEM_W), memspec, memspec] + [tile(RWKV_W)] * 5
                 + [_layer_spec(p["vectors"], l), full(bd), _layer_spec(wout, l)],
        out_specs=tile(D),
        out_shape=jax.ShapeDtypeStruct((T, D), F32),
        scratch_shapes=[pltpu.VMEM((tq, ATTN_W + MEM_W), F32)],
        compiler_params=pltpu.CompilerParams(dimension_semantics=("arbitrary",), vmem_limit_bytes=VMEM_LIMIT_BYTES),
        name=f"mix_l{l}",
    )(x, qa, ka, ka, va, va, bias, sink, qm, mk3, mv3, y, r, k, v, gate, p["vectors"], bd, wout)


SUBLANES = 8


def _tile_copies(hbm, buf, sem, tile, slot, to_vmem):
    copies = []
    for s in range(SUBLANES):
        src, dst = hbm.at[tile, s], buf.at[slot, :, s, :]
        if not to_vmem:
            src, dst = dst, src
        copies.append(pltpu.make_async_copy(src, dst, sem.at[slot, s]))
    return copies


def _shift_rows(u, before, steps):
    first = lax.broadcasted_iota(jnp.int32, (SUBLANES, u.shape[1]), 0) == 0
    tm = u.shape[0]
    heads = []
    for i in range(steps):
        lo = tm - (steps - i) * SUBLANES
        wrapped = pltpu.roll(u[lo:lo + SUBLANES], 1, axis=0)
        prior = jnp.broadcast_to(before[SUBLANES - steps + i:SUBLANES - steps + i + 1], wrapped.shape)
        heads.append(jnp.where(first, prior, wrapped))
    return jnp.concatenate(heads + [u[:tm - steps * SUBLANES]], axis=0)


def _stage_weights(w_hbm, layer, dst_ref, stage_ref, sem):
    rows = stage_ref.shape[1]
    n = dst_ref.shape[0] // rows
    copy = lambda c: pltpu.make_async_copy(w_hbm.at[layer, pl.ds(c * rows, rows), :], stage_ref.at[c % 2], sem.at[c % 2])
    copy(0).start()
    for c in range(n):
        if c + 1 < n:
            copy(c + 1).start()
        copy(c).wait()
        dst_ref[c * rows:(c + 1) * rows, :] = _bf(stage_ref[c % 2])


def _ffn_kernel(x_hbm, vec_ref, wup_hbm, cw_ref, cb_ref, wdn_hbm, o_hbm, xbuf, obuf, carry_ref, act_ref,
                wup_ref, wdn_ref, up_stage, dn_stage, in_sem, out_sem, w_sem, *, tm, tiles_per_batch, n_tiles, layer):
    D = xbuf.shape[-1]
    g_ref = _Row(vec_ref, VEC_ROWS["ffn_norm_g"], D)
    q = tm // SUBLANES
    i = pl.program_id(0)
    slot = i % 2

    @pl.when(i == 0)
    def _():
        carry_ref[...] = jnp.zeros_like(carry_ref)
        for cp in _tile_copies(x_hbm, xbuf, in_sem, 0, 0, True):
            cp.start()
        _stage_weights(wup_hbm, layer, wup_ref, up_stage, w_sem)
        _stage_weights(wdn_hbm, layer, wdn_ref, dn_stage, w_sem)

    @pl.when(i + 1 < n_tiles)
    def _():
        for cp in _tile_copies(x_hbm, xbuf, in_sem, i + 1, 1 - slot, True):
            cp.start()

    for cp in _tile_copies(x_hbm, xbuf, in_sem, i, slot, True):
        cp.wait()

    @pl.when(i >= 2)
    def _():
        for cp in _tile_copies(o_hbm, obuf, out_sem, i - 2, slot, False):
            cp.wait()

    seq_start = i % tiles_per_batch == 0
    g = g_ref[...]
    x = xbuf[slot].reshape(tm, D)
    before = jnp.where(seq_start, 0.0, _rms_rows(carry_ref[...], g))
    for r, grp in ((SUBLANES - 2, q - 2), (SUBLANES - 1, q - 1)):
        row = grp * SUBLANES + SUBLANES - 1
        carry_ref[r:r + 1, :] = x[row:row + 1]
    h_ext = _bf(jnp.concatenate([before, _rms_rows(x, g)], axis=0))

    n_chunks = D_FF // FFN_FC
    gate_cols = lambda c: slice(c * FFN_FC, (c + 1) * FFN_FC)
    val_cols = lambda c: slice(D_FF + c * FFN_FC, D_FF + (c + 1) * FFN_FC)

    def conv(u_ext, cols):
        w = cw_ref[:, cols]
        u = u_ext[SUBLANES:]
        return (cb_ref[:, cols] + w[0:1] * _shift_rows(u, u_ext[:SUBLANES], 2)
                + w[1:2] * _shift_rows(u, u_ext[:SUBLANES], 1) + w[2:3] * u)

    up = lambda c: (_dot(h_ext, wup_ref[:, gate_cols(c)]), _dot(h_ext, wup_ref[:, val_cols(c)]))
    nxt = up(0)
    for c in range(n_chunks):
        ug, uv = nxt
        if c + 1 < n_chunks:
            nxt = up(c + 1)
        gt = conv(ug, gate_cols(c))
        act_ref[:, gate_cols(c)] = _bf(gt * jax.nn.sigmoid(gt) * conv(uv, val_cols(c)))
    obuf[slot] = (x + _dot(act_ref[...], wdn_ref[...])).reshape(q, SUBLANES, D)

    for cp in _tile_copies(o_hbm, obuf, out_sem, i, slot, False):
        cp.start()

    @pl.when(i == n_tiles - 1)
    def _():
        if n_tiles >= 2:
            for cp in _tile_copies(o_hbm, obuf, out_sem, i - 1, 1 - slot, False):
                cp.wait()
        for cp in _tile_copies(o_hbm, obuf, out_sem, i, slot, False):
            cp.wait()


def _ffn_call(x, p, l, *, nb):
    T, D = x.shape
    tm = FFN_TM
    S = T // nb
    q = tm // SUBLANES
    assert S % tm == 0 and q % SUBLANES == 0 and D_FF % FFN_FC == 0
    n_tiles = T // tm
    wup, wdn = p["w_up"], p["w_down"]
    cw, cb = p["conv_w"], p["conv_b"][:, None, :]
    hbm = pl.BlockSpec(memory_space=pl.ANY)
    assert D % FFN_STAGE_CHUNKS == 0 and D_FF % FFN_STAGE_CHUNKS == 0
    out = pl.pallas_call(
        functools.partial(_ffn_kernel, tm=tm, tiles_per_batch=S // tm, n_tiles=n_tiles, layer=l),
        grid=(n_tiles,),
        in_specs=[hbm, _layer_spec(p["vectors"], l), hbm, _layer_spec(cw, l), _layer_spec(cb, l), hbm],
        out_specs=hbm,
        out_shape=jax.ShapeDtypeStruct((n_tiles, SUBLANES, q, D), F32),
        scratch_shapes=[pltpu.VMEM((2, q, SUBLANES, D), F32), pltpu.VMEM((2, q, SUBLANES, D), F32),
                        pltpu.VMEM((SUBLANES, D), F32), pltpu.VMEM((tm, D_FF), BF16),
                        pltpu.VMEM(wup.shape[1:], BF16), pltpu.VMEM(wdn.shape[1:], BF16),
                        pltpu.VMEM((2, D // FFN_STAGE_CHUNKS, 2 * D_FF), F32),
                        pltpu.VMEM((2, D_FF // FFN_STAGE_CHUNKS, D), F32),
                        pltpu.SemaphoreType.DMA((2, SUBLANES)), pltpu.SemaphoreType.DMA((2, SUBLANES)),
                        pltpu.SemaphoreType.DMA((2,))],
        compiler_params=pltpu.CompilerParams(dimension_semantics=("arbitrary",), vmem_limit_bytes=VMEM_LIMIT_BYTES),
        name=f"ffn_l{l}",
    )(x.reshape(n_tiles, SUBLANES, q, D), p["vectors"], wup, cw, cb, wdn)
    return out.reshape(T, D)


_PARAM_NAMES = (
    "rel_bias", "mix_norm_g", "w_in", "w_in_vres", "attn_q_norm", "attn_k_norm", "attn_sinks", "rwkv_mu",
    "rwkv_mu_vres", "rwkv_w0", "rwkv_w2", "rwkv_a0", "rwkv_a2", "rwkv_v0", "rwkv_v2", "rwkv_g2", "rwkv_k_k",
    "rwkv_k_a", "rwkv_r_k", "rwkv_ln_w", "rwkv_ln_b", "mem_norm_g", "w_mem_kv", "mem_q_norm", "mem_k_norm",
    "w_out", "ffn_norm_g", "w_up", "conv_w", "conv_b", "w_down")


def kernel(x, mem, rel_bias, mix_norm_g, w_in, w_in_vres, attn_q_norm, attn_k_norm, attn_sinks, rwkv_mu,
           rwkv_mu_vres, rwkv_w0, rwkv_w2, rwkv_a0, rwkv_a2, rwkv_v0, rwkv_v2, rwkv_g2, rwkv_k_k, rwkv_k_a,
           rwkv_r_k, rwkv_ln_w, rwkv_ln_b, mem_norm_g, w_mem_kv, mem_q_norm, mem_k_norm, w_out, ffn_norm_g,
           w_up, conv_w, conv_b, w_down):
    p = dict(zip(_PARAM_NAMES, (
        rel_bias, mix_norm_g, w_in, w_in_vres, attn_q_norm, attn_k_norm, attn_sinks, rwkv_mu, rwkv_mu_vres,
        rwkv_w0, rwkv_w2, rwkv_a0, rwkv_a2, rwkv_v0, rwkv_v2, rwkv_g2, rwkv_k_k, rwkv_k_a, rwkv_r_k, rwkv_ln_w,
        rwkv_ln_b, mem_norm_g, w_mem_kv, mem_q_norm, mem_k_norm, w_out, ffn_norm_g, w_up, conv_w, conv_b, w_down)))
    p["vectors"], p["lora"] = _pack_params(p)
    for name in ("w_in", "w_mem_kv", "w_out"):
        p[name] = _bf(p[name])
    nb, S, D = x.shape
    xt = x.reshape(nb * S, D)
    mem2d = mem.reshape(nb * mem.shape[1], D)
    bias = _bias_call(rel_bias)
    v_first = None
    for l in range(w_in.shape[0]):
        qa, ka, va, qm, r, lw, k, v, a, b, gate = _proj_call(xt, p, l, v_first, nb=nb)
        if l == 0:
            v_first = v
        y = _rwkv_scan(r, lw, k, v, a, b, nb=nb)
        mk, mv = _memkv_call(mem2d, p, l)
        xt = _mix_call(xt, qa, ka, va, qm, mk, mv, y, r, k, v, gate, bias, p, l, nb=nb)
        xt = _ffn_call(xt, p, l, nb=nb)
    return xt.reshape(nb, S, D)
```

```python
import functools
import math

import jax
import jax.numpy as jnp
import numpy as np
from jax import lax
from jax.experimental import pallas as pl
from jax.experimental.pallas import tpu as pltpu

F32 = jnp.float32
BF16 = jnp.bfloat16

HEAD_DIM = 64
SCAN_CHUNK = 64
VMEM_LIMIT_BYTES = 56 * 1024 * 1024


def _dot(a, b):
    return jnp.dot(a, b, preferred_element_type=F32)


def _dot_nt(a, b):
    return lax.dot_general(a, b, (((1,), (1,)), ((), ())), preferred_element_type=F32)


def _dot_tn(a, b):
    return lax.dot_general(a, b, (((0,), (0,)), ((), ())), preferred_element_type=F32)


def _bf(x):
    return x.astype(BF16)


SCAN_GROUP_HEADS = 4
SCAN_GROUP_W = SCAN_GROUP_HEADS * HEAD_DIM
SCAN_CHUNKS_PER_STEP = 4


_SCAN_STASH = (("w2", 1, F32), ("y_v", 1, F32), ("m_rb", 1, F32), ("v", 1, F32), ("w1r", 2, F32),
               ("bk", 2, F32), ("g_row", 1, F32))


def _scan_kernel(r_ref, lw_ref, k_ref, v_ref, a_ref, b_ref, bdm_ref, y_ref, h_ref, *stash_refs, nb, nch):
    C, N, GW, HPG = SCAN_CHUNK, HEAD_DIM, SCAN_GROUP_W, SCAN_GROUP_HEADS
    ngroups = h_ref.shape[0]
    stash = {name: ref for (name, _, _), ref in zip(_SCAN_STASH, stash_refs)}

    @pl.when(pl.program_id(0) == 0)
    def _():
        h_ref[...] = jnp.zeros_like(h_ref)
        for ref in stash_refs:
            ref[...] = jnp.zeros_like(ref)

    trow = lax.broadcasted_iota(jnp.int32, (C, C), 0)
    tcol = lax.broadcasted_iota(jnp.int32, (C, C), 1)
    tri = (trow >= tcol).astype(BF16)
    grow = lax.broadcasted_iota(jnp.int32, (C, GW), 0)
    gcol = lax.broadcasted_iota(jnp.int32, (C, GW), 1) % N
    incl = grow >= gcol
    strict = grow > gcol
    eye = (grow == gcol).astype(F32)
    bdm = bdm_ref[...]

    def bd(x):
        return jnp.concatenate([_bf(x)] * HPG, axis=0) * bdm

    def split2(x):
        hi = _bf(x)
        return hi, _bf(x - hi.astype(F32))

    def chunk_units(c):
        rows = slice(c * C, (c + 1) * C)
        per_batch = []
        for b in range(nb):
            lw = lw_ref[b, rows, :]
            l1 = _bf(lw)
            e1 = lw - l1.astype(F32)
            l2 = _bf(e1)
            l3 = _bf(e1 - l2.astype(F32))
            cum = _dot(tri, l1) + _dot(tri, l2) + _dot(tri, l3)
            cum_last = cum[C - 1:C, :]
            g_inv = jnp.exp(-cum)
            g_out = jnp.exp(cum_last - cum)
            f32 = lambda ref: ref[b, rows, :].astype(F32)
            kk, bb = f32(k_ref), f32(b_ref)
            per_batch.append(dict(
                a_t=f32(a_ref) * jnp.exp(cum - lw), r_t=f32(r_ref) * jnp.exp(cum),
                b_t=bb * g_inv, k_t=kk * g_inv, b_h=bb * g_out, k_h=kk * g_out,
                g_c=jnp.broadcast_to(jnp.exp(cum_last), cum.shape), v=f32(v_ref)))
        cat = {n: jnp.concatenate([pb[n] for pb in per_batch], axis=1) for n in per_batch[0]}
        return [{n: t[:, g * GW:(g + 1) * GW] for n, t in cat.items()} for g in range(ngroups)]

    def prepare(chunks):
        units = [u for c in chunks for u in chunk_units(c)]

        nu = len(units)
        ar = [jnp.concatenate([_bf(u["a_t"]), _bf(u["r_t"])], axis=0) for u in units]
        pb = [_dot_nt(ar[i], bd(units[i]["b_t"])) for i in range(nu)]
        pk = [_dot_nt(ar[i], bd(units[i]["k_t"])) for i in range(nu)]
        l_ab = [jnp.where(strict, p[:C], 0.0) for p in pb]
        m_rb = [_bf(jnp.where(incl, p[C:], 0.0)) for p in pb]
        tril2 = jnp.concatenate([strict, incl], axis=0)
        lm_k = [_bf(jnp.where(tril2, p, 0.0)) for p in pk]
        yield
        t_inv = [eye + l for l in l_ab]
        pw = [_dot(_bf(l), bd(l)) for l in l_ab]
        yield
        n_rounds = int(math.log2(C)) - 1
        for rnd in range(n_rounds):
            if rnd + 1 < n_rounds:
                z = [_dot(jnp.concatenate([_bf(p), _bf(t)], axis=0), bd(p)) for p, t in zip(pw, t_inv)]
                pw = [zz[:C] for zz in z]
                t_inv = [t + zz[C:] for t, zz in zip(t_inv, z)]
            else:
                t_inv = [t + _dot(_bf(t), bd(p)) for t, p in zip(t_inv, pw)]
            yield
        t_b = [_bf(t) for t in t_inv]
        w1 = [_dot(t, bd(u["a_t"])) for t, u in zip(t_b, units)]
        lmv = [_dot(m, bd(u["v"])) for m, u in zip(lm_k, units)]
        yield
        w2 = [_dot(t, bd(x[:C])) for t, x in zip(t_b, lmv)]
        parts = [p for u in units for p in split2(u["g_c"] * eye)]
        sums = _dot(jnp.concatenate(parts, axis=0), bdm)
        yield
        for i, u in enumerate(units):
            put = lambda name, val: stash[name].__setitem__(i, val.astype(stash[name].dtype))
            put("w2", w2[i])
            put("y_v", lmv[i][C:])
            put("m_rb", m_rb[i])
            put("v", u["v"])
            put("w1r", jnp.concatenate([w1[i], u["r_t"]], axis=0))
            put("bk", jnp.concatenate([u["b_h"], u["k_h"]], axis=0))
            put("g_row", sums[2 * i * C:(2 * i + 1) * C] + sums[(2 * i + 1) * C:(2 * i + 2) * C])
        yield

    state = [h_ref[g] for g in range(ngroups)]

    def advance(c):
        p = {n: [ref[c * ngroups + g] for g in range(ngroups)] for n, ref in stash.items()}
        for n in ("m_rb", "v", "w1r", "bk"):
            p[n] = [_bf(x) for x in p[n]]
        z = [_dot(w, bd(s)) for w, s in zip(p["w1r"], state)]
        yield
        u = [zz[:C] + w for zz, w in zip(z, p["w2"])]
        y = [zz[C:] + _dot(m, bd(x)) + yv for zz, m, x, yv in zip(z, p["m_rb"], u, p["y_v"])]
        uv = [jnp.concatenate([_bf(x), v], axis=0) for x, v in zip(u, p["v"])]
        full = [_dot_tn(b, x) * bdm.astype(F32) for b, x in zip(p["bk"], uv)]
        yield
        upd = [sum(f[h * N:(h + 1) * N] for h in range(HPG)) for f in full]
        state[:] = [g * s + d for g, s, d in zip(p["g_row"], state, upd)]
        ycat = jnp.concatenate(y, axis=1)
        w = y_ref.shape[2]
        for b in range(nb):
            y_ref[b, c * C:(c + 1) * C, :] = ycat[:, b * w:(b + 1) * w]
        yield

    def run(gen, stages):
        for _ in range(stages):
            next(gen, None)

    prep = prepare(list(range(nch)))
    n_prep = 5 + int(math.log2(C)) - 1
    for c in range(nch):
        seq = advance(c)
        run(prep, 1)
        run(seq, 1)
        run(prep, 1)
        run(seq, 2)
    run(prep, n_prep)
    for g in range(ngroups):
        h_ref[g] = state[g]


def _rwkv_scan(r, lw, k, v, a, b, *, nb):
    T, W = r.shape
    S = T // nb
    C, nch, GW = SCAN_CHUNK, SCAN_CHUNKS_PER_STEP, SCAN_GROUP_W
    assert S % (C * nch) == 0 and (nb * W) % GW == 0 and W % 128 == 0
    ngroups = nb * W // GW
    head = np.arange(GW) // HEAD_DIM
    bdm = jnp.asarray((head[:, None] == head[None, :]).astype(np.float32), dtype=BF16)
    n_blocks = S // (C * nch)
    in_spec = pl.BlockSpec((nb, C * nch, W), lambda s: (0, jnp.minimum(s, n_blocks - 1), 0))
    out_spec = pl.BlockSpec((nb, C * nch, W), lambda s: (0, jnp.maximum(s - 1, 0), 0))
    args = [t.reshape(nb, S, W) for t in (r, lw, k, v, a, b)]
    y = pl.pallas_call(
        functools.partial(_scan_kernel, nb=nb, nch=nch),
        grid=(n_blocks + 1,),
        in_specs=[in_spec] * 6 + [pl.BlockSpec((GW, GW), lambda s: (0, 0))],
        out_specs=out_spec,
        out_shape=jax.ShapeDtypeStruct((nb, S, W), F32),
        scratch_shapes=[pltpu.VMEM((ngroups, HEAD_DIM, GW), F32)]
        + [pltpu.VMEM((nch * ngroups, rows * C, GW), dt) for _, rows, dt in _SCAN_STASH],
        compiler_params=pltpu.CompilerParams(dimension_semantics=("arbitrary",), vmem_limit_bytes=VMEM_LIMIT_BYTES),
        name="rwkv_scan",
    )(*args, bdm)
    return y.reshape(T, W)


D_MODEL = 1024
ATTN_HEADS = 6
ATTN_KV_HEADS = 2
ATTN_GROUP = ATTN_HEADS // ATTN_KV_HEADS
ATTN_W = ATTN_HEADS * HEAD_DIM
KV_W = ATTN_KV_HEADS * HEAD_DIM
BLOCK_Q = 128
WINDOW = 128
N_BUCKETS = 32
MAX_EXACT = N_BUCKETS // 2
RWKV_W = 6 * HEAD_DIM
LORA_W = 128
SHIFT_W = 3 * RWKV_W + LORA_W
MEM_HEADS = 4
MEM_W = MEM_HEADS * HEAD_DIM
IN_BASE = ATTN_W + 2 * KV_W + SHIFT_W + MEM_W
PB_OFF = ATTN_W + 2 * KV_W
QM_OFF = PB_OFF + SHIFT_W
D_FF = 2816
EPS = 1e-6
GN_EPS = 64e-5
L2_EPS = 1e-12
MXU_WIDTH = 256

PROJ_TM = 1024
PROJ_SUB = 256
MIX_TQ = 512
FFN_TM = 512
FFN_FC = 256
FFN_STAGE_CHUNKS = 8


def _block_diag_ones(width):
    idx = np.arange(width) // HEAD_DIM
    return jnp.asarray((idx[:, None] == idx[None, :]).astype(np.float32), dtype=BF16)


def _head_sum(t, bd):
    return _dot(_bf(t), bd)


def _head_rms(t, bd, gain):
    ms = _head_sum(t * t, bd) * (1.0 / HEAD_DIM)
    return t * lax.rsqrt(ms + EPS) * gain


def _rms_rows(x, g):
    ms = jnp.mean(x * x, axis=-1, keepdims=True)
    return x * lax.rsqrt(ms + EPS) * g


class _Row:
    def __init__(self, table_ref, row, width):
        self.table_ref, self.row, self.width = table_ref, row, width

    def __getitem__(self, _):
        return self.table_ref[self.row:self.row + 1, 0:self.width]


def _proj_kernel(*refs, tm, sub, tiles_per_batch, has_vres):
    it = iter(refs)
    x_ref, vec_ref, w_ref, lora_ref = next(it), next(it), next(it), next(it)
    bd384_ref, bd128_ref, bd256_ref = next(it), next(it), next(it)
    if has_vres:
        vfirst_ref = next(it)
    qa_ref, ka_out_ref, va_ref, qm_ref = next(it), next(it), next(it), next(it)
    r_ref, lw_ref, k_ref, v_ref, a_ref, b_ref, gate_ref = (next(it) for _ in range(7))
    pbs_ref = next(it)
    sw = pbs_ref.shape[1]
    g_ref, qg_ref, kg_ref, mg_ref = (_Row(vec_ref, VEC_ROWS[n], w) for n, w in (
        ("mix_norm_g", D_MODEL), ("q_gain", ATTN_W), ("k_gain", KV_W), ("mq_gain", MEM_W)))
    mu_ref = _Row(vec_ref, VEC_ROWS["mu"], sw)
    w0_ref, a0_ref, kk_ref, ka_ref, v0_ref = (_Row(vec_ref, VEC_ROWS[n], RWKV_W) for n in ("w0", "a0", "k_k", "k_a", "v0"))
    w2_ref, a2_ref, g2_ref, v2_ref = (lora_ref.at[j] for j in range(4))

    @pl.when(pl.program_id(0) % tiles_per_batch == 0)
    def _():
        pbs_ref[0:8, :] = jnp.zeros((8, sw), F32)

    def project(i):
        rows = slice(i * sub, (i + 1) * sub)
        return _dot(_bf(_rms_rows(x_ref[rows, :], g_ref[...])), w_ref[...])

    def attention_outputs(i, proj):
        rows = slice(i * sub, (i + 1) * sub)
        qa_ref[rows, :] = _bf(_head_rms(proj[:, :ATTN_W], bd384_ref[...], qg_ref[...]))
        ka_out_ref[rows, :] = _bf(_head_rms(proj[:, ATTN_W:ATTN_W + KV_W], bd128_ref[...], kg_ref[...]))
        va_ref[rows, :] = _bf(proj[:, ATTN_W + KV_W:PB_OFF])
        qm_ref[rows, :] = _bf(_head_rms(proj[:, QM_OFF:QM_OFF + MEM_W], bd256_ref[...], mg_ref[...]))

    def rwkv_outputs(i, proj):
        rows = slice(i * sub, (i + 1) * sub)
        lo = 8 + i * sub
        pbs_ref[lo:lo + sub, 0:SHIFT_W] = proj[:, PB_OFF:PB_OFF + SHIFT_W]
        if has_vres:
            pbs_ref[lo:lo + sub, SHIFT_W:sw] = proj[:, IN_BASE:IN_BASE + LORA_W]
        cur = pbs_ref[lo:lo + sub, :]
        prev = pbs_ref[lo - 1:lo - 1 + sub, :]
        sh = cur + mu_ref[...] * (prev - cur)
        r = sh[:, 0:RWKV_W]
        k = sh[:, RWKV_W:2 * RWKV_W]
        v = sh[:, 2 * RWKV_W:3 * RWKV_W]
        z = sh[:, 3 * RWKV_W:SHIFT_W]
        t = w0_ref[...] + _dot(_bf(jnp.tanh(z)), w2_ref[...])
        lw_ref[rows, :] = -math.exp(-0.5) * jax.nn.sigmoid(t)
        a = jax.nn.sigmoid(a0_ref[...] + _dot(_bf(z), a2_ref[...]))
        gate_ref[rows, :] = _bf(_dot(_bf(jax.nn.sigmoid(z)), g2_ref[...]))
        if has_vres:
            vd = sh[:, SHIFT_W:sw]
            v = v + (vfirst_ref[rows, :].astype(F32) - v) * jax.nn.sigmoid(v0_ref[...] + _dot(_bf(vd), v2_ref[...]))
        kk = k * kk_ref[...]
        kk = kk / jnp.maximum(jnp.sqrt(_head_sum(kk * kk, bd384_ref[...])), L2_EPS)
        r_ref[rows, :] = _bf(r)
        k_ref[rows, :] = _bf(k * (1.0 + (a - 1.0) * ka_ref[...]))
        v_ref[rows, :] = _bf(v)
        a_ref[rows, :] = _bf(-kk)
        b_ref[rows, :] = _bf(kk * a)

    n_sub = tm // sub
    proj = project(0)
    for i in range(n_sub):
        attention_outputs(i, proj)
        nxt = project(i + 1) if i + 1 < n_sub else None
        rwkv_outputs(i, proj)
        proj = nxt
    pbs_ref[0:8, :] = pbs_ref[tm:tm + 8, :]


def _layer_spec(stacked, l):
    return pl.BlockSpec((None,) + stacked.shape[1:], lambda i: (l,) + (0,) * (stacked.ndim - 1),
                        pipeline_mode=pl.Buffered(1))


VEC_W = SHIFT_W + LORA_W
VEC_ROWS = {n: i for i, n in enumerate((
    "mix_norm_g", "q_gain", "k_gain", "mq_gain", "mu", "w0", "a0", "k_k", "k_a", "v0", "ln_w", "ln_b", "r_k",
    "ffn_norm_g"))}
VEC_TABLE_ROWS = 16


def _pack_params(p):
    layers = p["w_in"].shape[0]
    scale = HEAD_DIM ** -0.5
    zeros = jnp.zeros((RWKV_W,), F32)
    vec_pieces, lora_pieces = [], []
    for l in range(layers):
        vecs = dict(
            mix_norm_g=p["mix_norm_g"][l], q_gain=jnp.tile(p["attn_q_norm"][l], ATTN_HEADS) * scale,
            k_gain=jnp.tile(p["attn_k_norm"][l], ATTN_KV_HEADS), mq_gain=jnp.tile(p["mem_q_norm"][l], MEM_HEADS) * scale,
            mu=p["rwkv_mu"][l], w0=p["rwkv_w0"][l], a0=p["rwkv_a0"][l], k_k=p["rwkv_k_k"][l], k_a=p["rwkv_k_a"][l],
            v0=p["rwkv_v0"][l - 1] if l > 0 else zeros, ln_w=p["rwkv_ln_w"][l], ln_b=p["rwkv_ln_b"][l],
            r_k=p["rwkv_r_k"][l].reshape(-1), ffn_norm_g=p["ffn_norm_g"][l])
        for n in VEC_ROWS:
            row = [vecs[n].astype(F32)]
            if n == "mu" and l > 0:
                row.append(p["rwkv_mu_vres"][l - 1].astype(F32))
            row.append(jnp.zeros((VEC_W - sum(t.shape[0] for t in row),), F32))
            vec_pieces += row
        vec_pieces.append(jnp.zeros(((VEC_TABLE_ROWS - len(VEC_ROWS)) * VEC_W,), F32))
        v2 = p["rwkv_v2"][l - 1] if l > 0 else jnp.zeros((16, RWKV_W), F32)
        for w, at in ((p["rwkv_w2"][l], 0), (p["rwkv_a2"][l], 32), (p["rwkv_g2"][l], 64), (v2, 0)):
            lora_pieces += [jnp.zeros((at, RWKV_W), F32), w.astype(F32),
                            jnp.zeros((LORA_W - at - w.shape[0], RWKV_W), F32)]
    tables = jnp.concatenate(vec_pieces).reshape(layers, VEC_TABLE_ROWS, VEC_W)
    loras = jnp.concatenate([t for t in lora_pieces if t.shape[0]], axis=0).reshape(layers, 4, LORA_W, RWKV_W)
    return tables, _bf(loras)


def _proj_call(x, p, l, v_first, *, nb):
    T, D = x.shape
    tm = PROJ_TM
    assert T % tm == 0 and (T // nb) % tm == 0
    has_vres = l > 0
    w = p["w_in"]
    if has_vres:
        w = jnp.concatenate([w[l], _bf(jnp.pad(p["w_in_vres"][l - 1], ((0, 0), (0, LORA_W - 16))))], axis=1)
    ins = [x, p["vectors"], w, p["lora"], _block_diag_ones(ATTN_W), _block_diag_ones(KV_W), _block_diag_ones(MEM_W)]
    if has_vres:
        ins.append(v_first)

    def spec(a):
        if a.ndim >= 3:
            return _layer_spec(a, l)
        if a.shape[0] == T:
            return pl.BlockSpec((tm, a.shape[1]), lambda i: (i, 0))
        return pl.BlockSpec(a.shape, lambda i: (0, 0), pipeline_mode=pl.Buffered(1))

    out_shapes = [jax.ShapeDtypeStruct((T, ATTN_W), BF16), jax.ShapeDtypeStruct((T, KV_W), BF16),
                  jax.ShapeDtypeStruct((T, KV_W), BF16), jax.ShapeDtypeStruct((T, MEM_W), BF16)]
    out_shapes += [jax.ShapeDtypeStruct((T, RWKV_W), F32 if j == 1 else BF16) for j in range(7)]
    sw = SHIFT_W + (LORA_W if has_vres else 0)
    return pl.pallas_call(
        functools.partial(_proj_kernel, tm=tm, sub=PROJ_SUB, tiles_per_batch=(T // nb) // tm, has_vres=has_vres),
        grid=(T // tm,),
        in_specs=[spec(a) for a in ins],
        out_specs=[pl.BlockSpec((tm, s.shape[1]), lambda i: (i, 0)) for s in out_shapes],
        out_shape=out_shapes,
        scratch_shapes=[pltpu.VMEM((tm + 8, sw), F32)],
        compiler_params=pltpu.CompilerParams(dimension_semantics=("arbitrary",), vmem_limit_bytes=VMEM_LIMIT_BYTES),
        name=f"proj_l{l}",
    )(*ins)


def _bucket_table():
    qi = np.arange(BLOCK_Q)[:, None]
    kj = np.arange(2 * BLOCK_Q)[None, :]
    dist = qi + BLOCK_Q - kj
    in_band = (dist >= 0) & (dist < WINDOW)
    d = np.maximum(dist, 1).astype(np.float32)
    large = MAX_EXACT + (np.log(d / np.float32(MAX_EXACT)) / np.float32(math.log(WINDOW / MAX_EXACT))
                         * np.float32(N_BUCKETS - MAX_EXACT)).astype(np.int32)
    large = np.minimum(large, N_BUCKETS - 1)
    bucket = np.where(dist < MAX_EXACT, np.maximum(dist, 0), large)
    return np.where(in_band, bucket, -1).astype(np.int32)


def _bias_kernel(rb_ref, bucket_ref, out_ref):
    bucket = bucket_ref[...]
    for h in range(ATTN_HEADS):
        acc = jnp.full(bucket.shape, -jnp.inf, F32)
        for j in range(N_BUCKETS):
            acc = jnp.where(bucket == j, rb_ref[j, h], acc)
        out_ref[h] = acc


def _bias_call(rel_bias):
    tab = pl.pallas_call(
        _bias_kernel,
        in_specs=[pl.BlockSpec(memory_space=pltpu.SMEM), pl.BlockSpec(memory_space=pltpu.VMEM)],
        out_specs=pl.BlockSpec(memory_space=pltpu.VMEM),
        out_shape=jax.ShapeDtypeStruct((ATTN_HEADS, BLOCK_Q, 2 * BLOCK_Q), F32),
        name="rel_bias_table",
    )(rel_bias.astype(F32), jnp.asarray(_bucket_table()))
    return tab.reshape(ATTN_KV_HEADS, ATTN_GROUP * BLOCK_Q, 2 * BLOCK_Q)


def _memkv_kernel(mem_ref, g_ref, w_ref, kg_ref, bd_ref, mk_ref, mv_ref):
    hn = _bf(_rms_rows(mem_ref[...], g_ref[...]))
    kv = _dot(hn, w_ref[...])
    mk_ref[...] = _bf(_head_rms(kv[:, :MEM_W], bd_ref[...], kg_ref[...]))
    mv_ref[...] = _bf(kv[:, MEM_W:])


def _memkv_call(mem2d, p, l):
    rows = mem2d.shape[0]
    vm = pl.BlockSpec(memory_space=pltpu.VMEM)
    return pl.pallas_call(
        _memkv_kernel,
        in_specs=[vm] * 5,
        out_specs=[vm, vm],
        out_shape=[jax.ShapeDtypeStruct((rows, MEM_W), BF16)] * 2,
        compiler_params=pltpu.CompilerParams(vmem_limit_bytes=VMEM_LIMIT_BYTES),
        name=f"mem_kv_l{l}",
    )(mem2d, p["mem_norm_g"][l].reshape(1, -1), _bf(p["w_mem_kv"][l]),
      jnp.tile(p["mem_k_norm"][l], MEM_HEADS).reshape(1, -1), _block_diag_ones(MEM_W))


def _mix_kernel(x_ref, qa_ref, kc_ref, kp_ref, vc_ref, vp_ref, bias_ref, sink_ref, qm_ref, mk_ref, mv_ref,
                y_ref, r_ref, k_ref, v_ref, gate_ref, vec_ref, bd_ref, wout_ref, out_ref, att_ref,
                *, tq, tiles_per_batch):
    lnw_ref, lnb_ref, rk_ref = (_Row(vec_ref, VEC_ROWS[n], RWKV_W) for n in ("ln_w", "ln_b", "r_k"))
    N, BQ = HEAD_DIM, BLOCK_Q
    seq_start = pl.program_id(0) % tiles_per_batch == 0
    nqb = tq // BQ

    @pl.when(pl.program_id(0) == 0)
    def _():
        att_ref[...] = jnp.zeros_like(att_ref)

    qa = qa_ref[...]
    kall = jnp.concatenate([kp_ref[...], kc_ref[...]], axis=0)
    vall = jnp.concatenate([vp_ref[...], vc_ref[...]], axis=0)
    before_seq = lax.broadcasted_iota(jnp.int32, (ATTN_GROUP * BQ, 2 * BQ), 1) < BQ

    qm = qm_ref[...]
    mk = mk_ref[0]
    mv = mv_ref[0]

    swa = [(j, g) for j in range(nqb) for g in range(ATTN_KV_HEADS)]
    logits, values, sinks = [], [], []
    for j, g in swa:
        qg = jnp.concatenate(
            [qa[j * BQ:(j + 1) * BQ, (ATTN_GROUP * g + i) * N:(ATTN_GROUP * g + i + 1) * N] for i in range(ATTN_GROUP)],
            axis=0)
        lg = _dot_nt(qg, kall[j * BQ:(j + 2) * BQ, g * N:(g + 1) * N]) + bias_ref[g]
        if j == 0:
            lg = jnp.where(jnp.logical_and(seq_start, before_seq), -jnp.inf, lg)
        logits.append(lg)
        values.append(vall[j * BQ:(j + 2) * BQ, g * N:(g + 1) * N])
        sinks.append(sink_ref[g])
    for h in range(MEM_HEADS):
        hs = slice(h * N, (h + 1) * N)
        logits.append(_dot_nt(qm[:, hs], mk[:, hs]))
        values.append(mv[:, hs])
        sinks.append(None)
    bd = bd_ref[...]
    y = y_ref[...]
    d = y - _head_sum(y, bd) * (1.0 / N)
    var = _head_sum(d * d, bd) * (1.0 / N)
    yn = d * lax.rsqrt(var + GN_EPS) * lnw_ref[...] + lnb_ref[...]
    f32 = lambda ref: ref[...].astype(F32)
    bonus = _head_sum(f32(r_ref) * f32(k_ref) * rk_ref[...], bd) * f32(v_ref)
    out_b = (yn + bonus) * f32(gate_ref)
    mixed = _bf(jnp.concatenate([att_ref[:, :ATTN_W], out_b, att_ref[:, ATTN_W:]], axis=-1))
    out_ref[...] = x_ref[...] + _dot(mixed, wout_ref[...])

    row_max = [jnp.max(lg, axis=-1, keepdims=True) for lg in logits]
    m = [rm if s is None else jnp.maximum(rm, s) for rm, s in zip(row_max, sinks)]
    e = [jnp.exp(lg - mm) for lg, mm in zip(logits, m)]
    denom = [jnp.sum(ee, axis=-1, keepdims=True) for ee in e]
    denom = [d if s is None else d + jnp.exp(s - mm) for d, s, mm in zip(denom, sinks, m)]
    outs = [_dot(_bf(ee), vv) / d for ee, vv, d in zip(e, values, denom)]

    head_rows = [[None] * nqb for _ in range(ATTN_HEADS)]
    for (j, g), o in zip(swa, outs):
        for i in range(ATTN_GROUP):
            head_rows[ATTN_GROUP * g + i][j] = o[i * BQ:(i + 1) * BQ]
    att_ref[:, :ATTN_W] = jnp.concatenate([jnp.concatenate(rows, axis=0) for rows in head_rows], axis=-1)
    att_ref[:, ATTN_W:] = jnp.concatenate(outs[len(swa):], axis=-1)


def _mix_call(x, qa, ka, va, qm, mk, mv, y, r, k, v, gate, bias, p, l, *, nb):
    T, D = x.shape
    tq = MIX_TQ
    S = T // nb
    assert S % tq == 0 and tq % BLOCK_Q == 0
    tpb = S // tq
    qpb = tq // BLOCK_Q
    mem_tokens = mk.shape[0] // nb
    sink = jnp.repeat(p["attn_sinks"][l].astype(F32), BLOCK_Q).reshape(ATTN_KV_HEADS, ATTN_GROUP * BLOCK_Q, 1)
    n_tiles = T // tq
    att = lambda i: jnp.minimum(i, n_tiles - 1)
    fin = lambda i: jnp.maximum(i - 1, 0)
    att_tile = lambda w: pl.BlockSpec((tq, w), lambda i: (att(i), 0))
    tile = lambda w: pl.BlockSpec((tq, w), lambda i: (fin(i), 0))
    prev = pl.BlockSpec((BLOCK_Q, KV_W), lambda i: (jnp.maximum(att(i) * qpb - 1, 0), 0))
    full = lambda a: pl.BlockSpec(a.shape, lambda i: (0,) * a.ndim)
    memspec = pl.BlockSpec((1, mem_tokens, MEM_W), lambda i: (att(i) // tpb, 0, 0))
    bd = _block_diag_ones(RWKV_W)
    wout = p["w_out"]
    mk3 = mk.reshape(nb, mem_tokens, MEM_W)
    mv3 = mv.reshape(nb, mem_tokens, MEM_W)
    return pl.pallas_call(
        functools.partial(_mix_kernel, tq=tq, tiles_per_batch=tpb),
        grid=(n_tiles + 1,),
        in_specs=[tile(D), att_tile(ATTN_W), att_tile(KV_W), prev, att_tile(KV_W), prev, full(bias), full(sink),
                  att_tile(MEM_W), memspec, memspec] + [tile(RWKV_W)] * 5
                 + [_layer_spec(p["vectors"], l), full(bd), _layer_spec(wout, l)],
        out_specs=tile(D),
        out_shape=jax.ShapeDtypeStruct((T, D), F32),
        scratch_shapes=[pltpu.VMEM((tq, ATTN_W + MEM_W), F32)],
        compiler_params=pltpu.CompilerParams(dimension_semantics=("arbitrary",), vmem_limit_bytes=VMEM_LIMIT_BYTES),
        name=f"mix_l{l}",
    )(x, qa, ka, ka, va, va, bias, sink, qm, mk3, mv3, y, r, k, v, gate, p["vectors"], bd, wout)


SUBLANES = 8


def _tile_copies(hbm, buf, sem, tile, slot, to_vmem):
    copies = []
    for s in range(SUBLANES):
        src, dst = hbm.at[tile, s], buf.at[slot, :, s, :]
        if not to_vmem:
            src, dst = dst, src
        copies.append(pltpu.make_async_copy(src, dst, sem.at[slot, s]))
    return copies


def _shift_rows(u, before, steps):
    first = lax.broadcasted_iota(jnp.int32, (SUBLANES, u.shape[1]), 0) == 0
    tm = u.shape[0]
    heads = []
    for i in range(steps):
        lo = tm - (steps - i) * SUBLANES
        wrapped = pltpu.roll(u[lo:lo + SUBLANES], 1, axis=0)
        prior = jnp.broadcast_to(before[SUBLANES - steps + i:SUBLANES - steps + i + 1], wrapped.shape)
        heads.append(jnp.where(first, prior, wrapped))
    return jnp.concatenate(heads + [u[:tm - steps * SUBLANES]], axis=0)


def _stage_weights(w_hbm, layer, dst_ref, stage_ref, sem):
    rows = stage_ref.shape[1]
    n = dst_ref.shape[0] // rows
    copy = lambda c: pltpu.make_async_copy(w_hbm.at[layer, pl.ds(c * rows, rows), :], stage_ref.at[c % 2], sem.at[c % 2])
    copy(0).start()
    for c in range(n):
        if c + 1 < n:
            copy(c + 1).start()
        copy(c).wait()
        dst_ref[c * rows:(c + 1) * rows, :] = _bf(stage_ref[c % 2])


def _ffn_kernel(x_hbm, vec_ref, wup_hbm, cw_ref, cb_ref, wdn_hbm, o_hbm, xbuf, obuf, carry_ref, act_ref,
                wup_ref, wdn_ref, up_stage, dn_stage, in_sem, out_sem, w_sem, *, tm, tiles_per_batch, n_tiles, layer):
    D = xbuf.shape[-1]
    g_ref = _Row(vec_ref, VEC_ROWS["ffn_norm_g"], D)
    q = tm // SUBLANES
    i = pl.program_id(0)
    slot = i % 2

    @pl.when(i == 0)
    def _():
        carry_ref[...] = jnp.zeros_like(carry_ref)
        for cp in _tile_copies(x_hbm, xbuf, in_sem, 0, 0, True):
            cp.start()
        _stage_weights(wup_hbm, layer, wup_ref, up_stage, w_sem)
        _stage_weights(wdn_hbm, layer, wdn_ref, dn_stage, w_sem)

    @pl.when(i + 1 < n_tiles)
    def _():
        for cp in _tile_copies(x_hbm, xbuf, in_sem, i + 1, 1 - slot, True):
            cp.start()

    for cp in _tile_copies(x_hbm, xbuf, in_sem, i, slot, True):
        cp.wait()

    @pl.when(i >= 2)
    def _():
        for cp in _tile_copies(o_hbm, obuf, out_sem, i - 2, slot, False):
            cp.wait()

    seq_start = i % tiles_per_batch == 0
    g = g_ref[...]
    x = xbuf[slot].reshape(tm, D)
    before = jnp.where(seq_start, 0.0, _rms_rows(carry_ref[...], g))
    for r, grp in ((SUBLANES - 2, q - 2), (SUBLANES - 1, q - 1)):
        row = grp * SUBLANES + SUBLANES - 1
        carry_ref[r:r + 1, :] = x[row:row + 1]
    h_ext = _bf(jnp.concatenate([before, _rms_rows(x, g)], axis=0))

    n_chunks = D_FF // FFN_FC
    gate_cols = lambda c: slice(c * FFN_FC, (c + 1) * FFN_FC)
    val_cols = lambda c: slice(D_FF + c * FFN_FC, D_FF + (c + 1) * FFN_FC)

    def conv(u_ext, cols):
        w = cw_ref[:, cols]
        u = u_ext[SUBLANES:]
        return (cb_ref[:, cols] + w[0:1] * _shift_rows(u, u_ext[:SUBLANES], 2)
                + w[1:2] * _shift_rows(u, u_ext[:SUBLANES], 1) + w[2:3] * u)

    up = lambda c: (_dot(h_ext, wup_ref[:, gate_cols(c)]), _dot(h_ext, wup_ref[:, val_cols(c)]))
    nxt = up(0)
    for c in range(n_chunks):
        ug, uv = nxt
        if c + 1 < n_chunks:
            nxt = up(c + 1)
        gt = conv(ug, gate_cols(c))
        act_ref[:, gate_cols(c)] = _bf(gt * jax.nn.sigmoid(gt) * conv(uv, val_cols(c)))
    obuf[slot] = (x + _dot(act_ref[...], wdn_ref[...])).reshape(q, SUBLANES, D)

    for cp in _tile_copies(o_hbm, obuf, out_sem, i, slot, False):
        cp.start()

    @pl.when(i == n_tiles - 1)
    def _():
        if n_tiles >= 2:
            for cp in _tile_copies(o_hbm, obuf, out_sem, i - 1, 1 - slot, False):
                cp.wait()
        for cp in _tile_copies(o_hbm, obuf, out_sem, i, slot, False):
            cp.wait()


def _ffn_call(x, p, l, *, nb):
    T, D = x.shape
    tm = FFN_TM
    S = T // nb
    q = tm // SUBLANES
    assert S % tm == 0 and q % SUBLANES == 0 and D_FF % FFN_FC == 0
    n_tiles = T // tm
    wup, wdn = p["w_up"], p["w_down"]
    cw, cb = p["conv_w"], p["conv_b"][:, None, :]
    hbm = pl.BlockSpec(memory_space=pl.ANY)
    assert D % FFN_STAGE_CHUNKS == 0 and D_FF % FFN_STAGE_CHUNKS == 0
    out = pl.pallas_call(
        functools.partial(_ffn_kernel, tm=tm, tiles_per_batch=S // tm, n_tiles=n_tiles, layer=l),
        grid=(n_tiles,),
        in_specs=[hbm, _layer_spec(p["vectors"], l), hbm, _layer_spec(cw, l), _layer_spec(cb, l), hbm],
        out_specs=hbm,
        out_shape=jax.ShapeDtypeStruct((n_tiles, SUBLANES, q, D), F32),
        scratch_shapes=[pltpu.VMEM((2, q, SUBLANES, D), F32), pltpu.VMEM((2, q, SUBLANES, D), F32),
                        pltpu.VMEM((SUBLANES, D), F32), pltpu.VMEM((tm, D_FF), BF16),
                        pltpu.VMEM(wup.shape[1:], BF16), pltpu.VMEM(wdn.shape[1:], BF16),
                        pltpu.VMEM((2, D // FFN_STAGE_CHUNKS, 2 * D_FF), F32),
                        pltpu.VMEM((2, D_FF // FFN_STAGE_CHUNKS, D), F32),
                        pltpu.SemaphoreType.DMA((2, SUBLANES)), pltpu.SemaphoreType.DMA((2, SUBLANES)),
                        pltpu.SemaphoreType.DMA((2,))],
        compiler_params=pltpu.CompilerParams(dimension_semantics=("arbitrary",), vmem_limit_bytes=VMEM_LIMIT_BYTES),
        name=f"ffn_l{l}",
    )(x.reshape(n_tiles, SUBLANES, q, D), p["vectors"], wup, cw, cb, wdn)
    return out.reshape(T, D)


_PARAM_NAMES = (
    "rel_bias", "mix_norm_g", "w_in", "w_in_vres", "attn_q_norm", "attn_k_norm", "attn_sinks", "rwkv_mu",
    "rwkv_mu_vres", "rwkv_w0", "rwkv_w2", "rwkv_a0", "rwkv_a2", "rwkv_v0", "rwkv_v2", "rwkv_g2", "rwkv_k_k",
    "rwkv_k_a", "rwkv_r_k", "rwkv_ln_w", "rwkv_ln_b", "mem_norm_g", "w_mem_kv", "mem_q_norm", "mem_k_norm",
    "w_out", "ffn_norm_g", "w_up", "conv_w", "conv_b", "w_down")


def kernel(x, mem, rel_bias, mix_norm_g, w_in, w_in_vres, attn_q_norm, attn_k_norm, attn_sinks, rwkv_mu,
           rwkv_mu_vres, rwkv_w0, rwkv_w2, rwkv_a0, rwkv_a2, rwkv_v0, rwkv_v2, rwkv_g2, rwkv_k_k, rwkv_k_a,
           rwkv_r_k, rwkv_ln_w, rwkv_ln_b, mem_norm_g, w_mem_kv, mem_q_norm, mem_k_norm, w_out, ffn_norm_g,
           w_up, conv_w, conv_b, w_down):
    p = dict(zip(_PARAM_NAMES, (
        rel_bias, mix_norm_g, w_in, w_in_vres, attn_q_norm, attn_k_norm, attn_sinks, rwkv_mu, rwkv_mu_vres,
        rwkv_w0, rwkv_w2, rwkv_a0, rwkv_a2, rwkv_v0, rwkv_v2, rwkv_g2, rwkv_k_k, rwkv_k_a, rwkv_r_k, rwkv_ln_w,
        rwkv_ln_b, mem_norm_g, w_mem_kv, mem_q_norm, mem_k_norm, w_out, ffn_norm_g, w_up, conv_w, conv_b, w_down)))
    p["vectors"], p["lora"] = _pack_params(p)
    for name in ("w_in", "w_mem_kv", "w_out"):
        p[name] = _bf(p[name])
    nb, S, D = x.shape
    xt = x.reshape(nb * S, D)
    mem2d = mem.reshape(nb * mem.shape[1], D)
    bias = _bias_call(rel_bias)
    v_first = None
    for l in range(w_in.shape[0]):
        qa, ka, va, qm, r, lw, k, v, a, b, gate = _proj_call(xt, p, l, v_first, nb=nb)
        if l == 0:
            v_first = v
        y = _rwkv_scan(r, lw, k, v, a, b, nb=nb)
        mk, mv = _memkv_call(mem2d, p, l)
        xt = _mix_call(xt, qa, ka, va, qm, mk, mv, y, r, k, v, gate, bias, p, l, nb=nb)
        xt = _ffn_call(xt, p, l, nb=nb)
    return xt.reshape(nb, S, D)
```

```python
import functools
import math

import jax
import jax.numpy as jnp
import numpy as np
from jax import lax
from jax.experimental import pallas as pl
from jax.experimental.pallas import tpu as pltpu

F32 = jnp.float32
BF16 = jnp.bfloat16

HEAD_DIM = 64
SCAN_CHUNK = 64
VMEM_LIMIT_BYTES = 56 * 1024 * 1024


def _dot(a, b):
    return jnp.dot(a, b, preferred_element_type=F32)


def _dot_nt(a, b):
    return lax.dot_general(a, b, (((1,), (1,)), ((), ())), preferred_element_type=F32)


def _dot_tn(a, b):
    return lax.dot_general(a, b, (((0,), (0,)), ((), ())), preferred_element_type=F32)


def _bf(x):
    return x.astype(BF16)


SCAN_GROUP_HEADS = 4
SCAN_GROUP_W = SCAN_GROUP_HEADS * HEAD_DIM
SCAN_CHUNKS_PER_STEP = 4


_SCAN_STASH = (("w2", 1, F32), ("y_v", 1, F32), ("m_rb", 1, F32), ("v", 1, F32), ("w1r", 2, F32),
               ("bk", 2, F32), ("g_row", 1, F32))


def _scan_kernel(r_ref, lw_ref, k_ref, v_ref, a_ref, b_ref, bdm_ref, y_ref, h_ref, *stash_refs, nb, nch):
    C, N, GW, HPG = SCAN_CHUNK, HEAD_DIM, SCAN_GROUP_W, SCAN_GROUP_HEADS
    ngroups = h_ref.shape[0]
    stash = {name: ref for (name, _, _), ref in zip(_SCAN_STASH, stash_refs)}

    @pl.when(pl.program_id(0) == 0)
    def _():
        h_ref[...] = jnp.zeros_like(h_ref)
        for ref in stash_refs:
            ref[...] = jnp.zeros_like(ref)

    trow = lax.broadcasted_iota(jnp.int32, (C, C), 0)
    tcol = lax.broadcasted_iota(jnp.int32, (C, C), 1)
    tri = (trow >= tcol).astype(BF16)
    grow = lax.broadcasted_iota(jnp.int32, (C, GW), 0)
    gcol = lax.broadcasted_iota(jnp.int32, (C, GW), 1) % N
    incl = grow >= gcol
    strict = grow > gcol
    eye = (grow == gcol).astype(F32)
    bdm = bdm_ref[...]

    def bd(x):
        return jnp.concatenate([_bf(x)] * HPG, axis=0) * bdm

    def split2(x):
        hi = _bf(x)
        return hi, _bf(x - hi.astype(F32))

    def chunk_units(c):
        rows = slice(c * C, (c + 1) * C)
        per_batch = []
        for b in range(nb):
            lw = lw_ref[b, rows, :]
            l1 = _bf(lw)
            e1 = lw - l1.astype(F32)
            l2 = _bf(e1)
            l3 = _bf(e1 - l2.astype(F32))
            cum = _dot(tri, l1) + _dot(tri, l2) + _dot(tri, l3)
            cum_last = cum[C - 1:C, :]
            g_inv = jnp.exp(-cum)
            g_out = jnp.exp(cum_last - cum)
            f32 = lambda ref: ref[b, rows, :].astype(F32)
            kk, bb = f32(k_ref), f32(b_ref)
            per_batch.append(dict(
                a_t=f32(a_ref) * jnp.exp(cum - lw), r_t=f32(r_ref) * jnp.exp(cum),
                b_t=bb * g_inv, k_t=kk * g_inv, b_h=bb * g_out, k_h=kk * g_out,
                g_c=jnp.broadcast_to(jnp.exp(cum_last), cum.shape), v=f32(v_ref)))
        cat = {n: jnp.concatenate([pb[n] for pb in per_batch], axis=1) for n in per_batch[0]}
        return [{n: t[:, g * GW:(g + 1) * GW] for n, t in cat.items()} for g in range(ngroups)]

    def prepare(chunks):
        units = [u for c in chunks for u in chunk_units(c)]

        nu = len(units)
        ar = [jnp.concatenate([_bf(u["a_t"]), _bf(u["r_t"])], axis=0) for u in units]
        pb = [_dot_nt(ar[i], bd(units[i]["b_t"])) for i in range(nu)]
        pk = [_dot_nt(ar[i], bd(units[i]["k_t"])) for i in range(nu)]
        l_ab = [jnp.where(strict, p[:C], 0.0) for p in pb]
        m_rb = [_bf(jnp.where(incl, p[C:], 0.0)) for p in pb]
        tril2 = jnp.concatenate([strict, incl], axis=0)
        lm_k = [_bf(jnp.where(tril2, p, 0.0)) for p in pk]
        yield
        t_inv = [eye + l for l in l_ab]
        pw = [_dot(_bf(l), bd(l)) for l in l_ab]
        yield
        n_rounds = int(math.log2(C)) - 1
        for rnd in range(n_rounds):
            if rnd + 1 < n_rounds:
                z = [_dot(jnp.concatenate([_bf(p), _bf(t)], axis=0), bd(p)) for p, t in zip(pw, t_inv)]
                pw = [zz[:C] for zz in z]
                t_inv = [t + zz[C:] for t, zz in zip(t_inv, z)]
            else:
                t_inv = [t + _dot(_bf(t), bd(p)) for t, p in zip(t_inv, pw)]
            yield
        t_b = [_bf(t) for t in t_inv]
        w1 = [_dot(t, bd(u["a_t"])) for t, u in zip(t_b, units)]
        lmv = [_dot(m, bd(u["v"])) for m, u in zip(lm_k, units)]
        yield
        w2 = [_dot(t, bd(x[:C])) for t, x in zip(t_b, lmv)]
        parts = [p for u in units for p in split2(u["g_c"] * eye)]
        sums = _dot(jnp.concatenate(parts, axis=0), bdm)
        yield
        for i, u in enumerate(units):
            put = lambda name, val: stash[name].__setitem__(i, val.astype(stash[name].dtype))
            put("w2", w2[i])
            put("y_v", lmv[i][C:])
            put("m_rb", m_rb[i])
            put("v", u["v"])
            put("w1r", jnp.concatenate([w1[i], u["r_t"]], axis=0))
            put("bk", jnp.concatenate([u["b_h"], u["k_h"]], axis=0))
            put("g_row", sums[2 * i * C:(2 * i + 1) * C] + sums[(2 * i + 1) * C:(2 * i + 2) * C])
        yield

    state = [h_ref[g] for g in range(ngroups)]

    def advance(c):
        p = {n: [ref[c * ngroups + g] for g in range(ngroups)] for n, ref in stash.items()}
        for n in ("m_rb", "v", "w1r", "bk"):
            p[n] = [_bf(x) for x in p[n]]
        z = [_dot(w, bd(s)) for w, s in zip(p["w1r"], state)]
        yield
        u = [zz[:C] + w for zz, w in zip(z, p["w2"])]
        y = [zz[C:] + _dot(m, bd(x)) + yv for zz, m, x, yv in zip(z, p["m_rb"], u, p["y_v"])]
        uv = [jnp.concatenate([_bf(x), v], axis=0) for x, v in zip(u, p["v"])]
        full = [_dot_tn(b, x) * bdm.astype(F32) for b, x in zip(p["bk"], uv)]
        yield
        upd = [sum(f[h * N:(h + 1) * N] for h in range(HPG)) for f in full]
        state[:] = [g * s + d for g, s, d in zip(p["g_row"], state, upd)]
        ycat = jnp.concatenate(y, axis=1)
        w = y_ref.shape[2]
        for b in range(nb):
            y_ref[b, c * C:(c + 1) * C, :] = ycat[:, b * w:(b + 1) * w]
        yield

    def run(gen, stages):
        for _ in range(stages):
            next(gen, None)

    prep = prepare(list(range(nch)))
    n_prep = 5 + int(math.log2(C)) - 1
    for c in range(nch):
        seq = advance(c)
        run(prep, 1)
        run(seq, 1)
        run(prep, 1)
        run(seq, 2)
    run(prep, n_prep)
    for g in range(ngroups):
        h_ref[g] = state[g]


def _rwkv_scan(r, lw, k, v, a, b, *, nb):
    T, W = r.shape
    S = T // nb
    C, nch, GW = SCAN_CHUNK, SCAN_CHUNKS_PER_STEP, SCAN_GROUP_W
    assert S % (C * nch) == 0 and (nb * W) % GW == 0 and W % 128 == 0
    ngroups = nb * W // GW
    head = np.arange(GW) // HEAD_DIM
    bdm = jnp.asarray((head[:, None] == head[None, :]).astype(np.float32), dtype=BF16)
    n_blocks = S // (C * nch)
    in_spec = pl.BlockSpec((nb, C * nch, W), lambda s: (0, jnp.minimum(s, n_blocks - 1), 0))
    out_spec = pl.BlockSpec((nb, C * nch, W), lambda s: (0, jnp.maximum(s - 1, 0), 0))
    args = [t.reshape(nb, S, W) for t in (r, lw, k, v, a, b)]
    y = pl.pallas_call(
        functools.partial(_scan_kernel, nb=nb, nch=nch),
        grid=(n_blocks + 1,),
        in_specs=[in_spec] * 6 + [pl.BlockSpec((GW, GW), lambda s: (0, 0))],
        out_specs=out_spec,
        out_shape=jax.ShapeDtypeStruct((nb, S, W), F32),
        scratch_shapes=[pltpu.VMEM((ngroups, HEAD_DIM, GW), F32)]
        + [pltpu.VMEM((nch * ngroups, rows * C, GW), dt) for _, rows, dt in _SCAN_STASH],
        compiler_params=pltpu.CompilerParams(dimension_semantics=("arbitrary",), vmem_limit_bytes=VMEM_LIMIT_BYTES),
        name="rwkv_scan",
    )(*args, bdm)
    return y.reshape(T, W)


D_MODEL = 1024
ATTN_HEADS = 6
ATTN_KV_HEADS = 2
ATTN_GROUP = ATTN_HEADS // ATTN_KV_HEADS
ATTN_W = ATTN_HEADS * HEAD_DIM
KV_W = ATTN_KV_HEADS * HEAD_DIM
BLOCK_Q = 128
WINDOW = 128
N_BUCKETS = 32
MAX_EXACT = N_BUCKETS // 2
RWKV_W = 6 * HEAD_DIM
LORA_W = 128
SHIFT_W = 3 * RWKV_W + LORA_W
MEM_HEADS = 4
MEM_W = MEM_HEADS * HEAD_DIM
IN_BASE = ATTN_W + 2 * KV_W + SHIFT_W + MEM_W
PB_OFF = ATTN_W + 2 * KV_W
QM_OFF = PB_OFF + SHIFT_W
D_FF = 2816
EPS = 1e-6
GN_EPS = 64e-5
L2_EPS = 1e-12
MXU_WIDTH = 256

PROJ_TM = 1024
PROJ_SUB = 256
MIX_TQ = 512
FFN_TM = 1024
FFN_FC = 256
FFN_STAGE_CHUNKS = 8


def _block_diag_ones(width):
    idx = np.arange(width) // HEAD_DIM
    return jnp.asarray((idx[:, None] == idx[None, :]).astype(np.float32), dtype=BF16)


def _head_sum(t, bd):
    return _dot(_bf(t), bd)


def _head_rms(t, bd, gain):
    ms = _head_sum(t * t, bd) * (1.0 / HEAD_DIM)
    return t * lax.rsqrt(ms + EPS) * gain


def _rms_rows(x, g):
    ms = jnp.mean(x * x, axis=-1, keepdims=True)
    return x * lax.rsqrt(ms + EPS) * g


class _Row:
    def __init__(self, table_ref, row, width):
        self.table_ref, self.row, self.width = table_ref, row, width

    def __getitem__(self, _):
        return self.table_ref[self.row:self.row + 1, 0:self.width]


def _proj_kernel(*refs, tm, sub, tiles_per_batch, has_vres):
    it = iter(refs)
    x_ref, vec_ref, w_ref, lora_ref = next(it), next(it), next(it), next(it)
    bd384_ref, bd128_ref, bd256_ref = next(it), next(it), next(it)
    if has_vres:
        vfirst_ref = next(it)
    qa_ref, ka_out_ref, va_ref, qm_ref = next(it), next(it), next(it), next(it)
    r_ref, lw_ref, k_ref, v_ref, a_ref, b_ref, gate_ref = (next(it) for _ in range(7))
    pbs_ref = next(it)
    sw = pbs_ref.shape[1]
    g_ref, qg_ref, kg_ref, mg_ref = (_Row(vec_ref, VEC_ROWS[n], w) for n, w in (
        ("mix_norm_g", D_MODEL), ("q_gain", ATTN_W), ("k_gain", KV_W), ("mq_gain", MEM_W)))
    mu_ref = _Row(vec_ref, VEC_ROWS["mu"], sw)
    w0_ref, a0_ref, kk_ref, ka_ref, v0_ref = (_Row(vec_ref, VEC_ROWS[n], RWKV_W) for n in ("w0", "a0", "k_k", "k_a", "v0"))
    w2_ref, a2_ref, g2_ref, v2_ref = (lora_ref.at[j] for j in range(4))

    @pl.when(pl.program_id(0) % tiles_per_batch == 0)
    def _():
        pbs_ref[0:8, :] = jnp.zeros((8, sw), F32)

    def project(i):
        rows = slice(i * sub, (i + 1) * sub)
        return _dot(_bf(_rms_rows(x_ref[rows, :], g_ref[...])), w_ref[...])

    def attention_outputs(i, proj):
        rows = slice(i * sub, (i + 1) * sub)
        qa_ref[rows, :] = _bf(_head_rms(proj[:, :ATTN_W], bd384_ref[...], qg_ref[...]))
        ka_out_ref[rows, :] = _bf(_head_rms(proj[:, ATTN_W:ATTN_W + KV_W], bd128_ref[...], kg_ref[...]))
        va_ref[rows, :] = _bf(proj[:, ATTN_W + KV_W:PB_OFF])
        qm_ref[rows, :] = _bf(_head_rms(proj[:, QM_OFF:QM_OFF + MEM_W], bd256_ref[...], mg_ref[...]))

    def rwkv_outputs(i, proj):
        rows = slice(i * sub, (i + 1) * sub)
        lo = 8 + i * sub
        pbs_ref[lo:lo + sub, 0:SHIFT_W] = proj[:, PB_OFF:PB_OFF + SHIFT_W]
        if has_vres:
            pbs_ref[lo:lo + sub, SHIFT_W:sw] = proj[:, IN_BASE:IN_BASE + LORA_W]
        cur = pbs_ref[lo:lo + sub, :]
        prev = pbs_ref[lo - 1:lo - 1 + sub, :]
        sh = cur + mu_ref[...] * (prev - cur)
        r = sh[:, 0:RWKV_W]
        k = sh[:, RWKV_W:2 * RWKV_W]
        v = sh[:, 2 * RWKV_W:3 * RWKV_W]
        z = sh[:, 3 * RWKV_W:SHIFT_W]
        t = w0_ref[...] + _dot(_bf(jnp.tanh(z)), w2_ref[...])
        lw_ref[rows, :] = -math.exp(-0.5) * jax.nn.sigmoid(t)
        a = jax.nn.sigmoid(a0_ref[...] + _dot(_bf(z), a2_ref[...]))
        gate_ref[rows, :] = _bf(_dot(_bf(jax.nn.sigmoid(z)), g2_ref[...]))
        if has_vres:
            vd = sh[:, SHIFT_W:sw]
            v = v + (vfirst_ref[rows, :].astype(F32) - v) * jax.nn.sigmoid(v0_ref[...] + _dot(_bf(vd), v2_ref[...]))
        kk = k * kk_ref[...]
        kk = kk / jnp.maximum(jnp.sqrt(_head_sum(kk * kk, bd384_ref[...])), L2_EPS)
        r_ref[rows, :] = _bf(r)
        k_ref[rows, :] = _bf(k * (1.0 + (a - 1.0) * ka_ref[...]))
        v_ref[rows, :] = _bf(v)
        a_ref[rows, :] = _bf(-kk)
        b_ref[rows, :] = _bf(kk * a)

    n_sub = tm // sub
    proj = project(0)
    for i in range(n_sub):
        attention_outputs(i, proj)
        nxt = project(i + 1) if i + 1 < n_sub else None
        rwkv_outputs(i, proj)
        proj = nxt
    pbs_ref[0:8, :] = pbs_ref[tm:tm + 8, :]


def _layer_spec(stacked, l):
    return pl.BlockSpec((None,) + stacked.shape[1:], lambda i: (l,) + (0,) * (stacked.ndim - 1),
                        pipeline_mode=pl.Buffered(1))


VEC_W = SHIFT_W + LORA_W
VEC_ROWS = {n: i for i, n in enumerate((
    "mix_norm_g", "q_gain", "k_gain", "mq_gain", "mu", "w0", "a0", "k_k", "k_a", "v0", "ln_w", "ln_b", "r_k",
    "ffn_norm_g"))}
VEC_TABLE_ROWS = 16


def _pack_params(p):
    layers = p["w_in"].shape[0]
    scale = HEAD_DIM ** -0.5
    zeros = jnp.zeros((RWKV_W,), F32)
    vec_pieces, lora_pieces = [], []
    for l in range(layers):
        vecs = dict(
            mix_norm_g=p["mix_norm_g"][l], q_gain=jnp.tile(p["attn_q_norm"][l], ATTN_HEADS) * scale,
            k_gain=jnp.tile(p["attn_k_norm"][l], ATTN_KV_HEADS), mq_gain=jnp.tile(p["mem_q_norm"][l], MEM_HEADS) * scale,
            mu=p["rwkv_mu"][l], w0=p["rwkv_w0"][l], a0=p["rwkv_a0"][l], k_k=p["rwkv_k_k"][l], k_a=p["rwkv_k_a"][l],
            v0=p["rwkv_v0"][l - 1] if l > 0 else zeros, ln_w=p["rwkv_ln_w"][l], ln_b=p["rwkv_ln_b"][l],
            r_k=p["rwkv_r_k"][l].reshape(-1), ffn_norm_g=p["ffn_norm_g"][l])
        for n in VEC_ROWS:
            row = [vecs[n].astype(F32)]
            if n == "mu" and l > 0:
                row.append(p["rwkv_mu_vres"][l - 1].astype(F32))
            row.append(jnp.zeros((VEC_W - sum(t.shape[0] for t in row),), F32))
            vec_pieces += row
        vec_pieces.append(jnp.zeros(((VEC_TABLE_ROWS - len(VEC_ROWS)) * VEC_W,), F32))
        v2 = p["rwkv_v2"][l - 1] if l > 0 else jnp.zeros((16, RWKV_W), F32)
        for w, at in ((p["rwkv_w2"][l], 0), (p["rwkv_a2"][l], 32), (p["rwkv_g2"][l], 64), (v2, 0)):
            lora_pieces += [jnp.zeros((at, RWKV_W), F32), w.astype(F32),
                            jnp.zeros((LORA_W - at - w.shape[0], RWKV_W), F32)]
    tables = jnp.concatenate(vec_pieces).reshape(layers, VEC_TABLE_ROWS, VEC_W)
    loras = jnp.concatenate([t for t in lora_pieces if t.shape[0]], axis=0).reshape(layers, 4, LORA_W, RWKV_W)
    return tables, _bf(loras)


def _proj_call(x, p, l, v_first, *, nb):
    T, D = x.shape
    tm = PROJ_TM
    assert T % tm == 0 and (T // nb) % tm == 0
    has_vres = l > 0
    w = p["w_in"]
    if has_vres:
        w = jnp.concatenate([w[l], _bf(jnp.pad(p["w_in_vres"][l - 1], ((0, 0), (0, LORA_W - 16))))], axis=1)
    ins = [x, p["vectors"], w, p["lora"], _block_diag_ones(ATTN_W), _block_diag_ones(KV_W), _block_diag_ones(MEM_W)]
    if has_vres:
        ins.append(v_first)

    def spec(a):
        if a.ndim >= 3:
            return _layer_spec(a, l)
        if a.shape[0] == T:
            return pl.BlockSpec((tm, a.shape[1]), lambda i: (i, 0))
        return pl.BlockSpec(a.shape, lambda i: (0, 0), pipeline_mode=pl.Buffered(1))

    out_shapes = [jax.ShapeDtypeStruct((T, ATTN_W), BF16), jax.ShapeDtypeStruct((T, KV_W), BF16),
                  jax.ShapeDtypeStruct((T, KV_W), BF16), jax.ShapeDtypeStruct((T, MEM_W), BF16)]
    out_shapes += [jax.ShapeDtypeStruct((T, RWKV_W), F32 if j == 1 else BF16) for j in range(7)]
    sw = SHIFT_W + (LORA_W if has_vres else 0)
    return pl.pallas_call(
        functools.partial(_proj_kernel, tm=tm, sub=PROJ_SUB, tiles_per_batch=(T // nb) // tm, has_vres=has_vres),
        grid=(T // tm,),
        in_specs=[spec(a) for a in ins],
        out_specs=[pl.BlockSpec((tm, s.shape[1]), lambda i: (i, 0)) for s in out_shapes],
        out_shape=out_shapes,
        scratch_shapes=[pltpu.VMEM((tm + 8, sw), F32)],
        compiler_params=pltpu.CompilerParams(dimension_semantics=("arbitrary",), vmem_limit_bytes=VMEM_LIMIT_BYTES),
        name=f"proj_l{l}",
    )(*ins)


def _bucket_table():
    qi = np.arange(BLOCK_Q)[:, None]
    kj = np.arange(2 * BLOCK_Q)[None, :]
    dist = qi + BLOCK_Q - kj
    in_band = (dist >= 0) & (dist < WINDOW)
    d = np.maximum(dist, 1).astype(np.float32)
    large = MAX_EXACT + (np.log(d / np.float32(MAX_EXACT)) / np.float32(math.log(WINDOW / MAX_EXACT))
                         * np.float32(N_BUCKETS - MAX_EXACT)).astype(np.int32)
    large = np.minimum(large, N_BUCKETS - 1)
    bucket = np.where(dist < MAX_EXACT, np.maximum(dist, 0), large)
    return np.where(in_band, bucket, -1).astype(np.int32)


def _bias_kernel(rb_ref, bucket_ref, out_ref):
    bucket = bucket_ref[...]
    for h in range(ATTN_HEADS):
        acc = jnp.full(bucket.shape, -jnp.inf, F32)
        for j in range(N_BUCKETS):
            acc = jnp.where(bucket == j, rb_ref[j, h], acc)
        out_ref[h] = acc


def _bias_call(rel_bias):
    tab = pl.pallas_call(
        _bias_kernel,
        in_specs=[pl.BlockSpec(memory_space=pltpu.SMEM), pl.BlockSpec(memory_space=pltpu.VMEM)],
        out_specs=pl.BlockSpec(memory_space=pltpu.VMEM),
        out_shape=jax.ShapeDtypeStruct((ATTN_HEADS, BLOCK_Q, 2 * BLOCK_Q), F32),
        name="rel_bias_table",
    )(rel_bias.astype(F32), jnp.asarray(_bucket_table()))
    return tab.reshape(ATTN_KV_HEADS, ATTN_GROUP * BLOCK_Q, 2 * BLOCK_Q)


def _memkv_kernel(mem_ref, g_ref, w_ref, kg_ref, bd_ref, mk_ref, mv_ref):
    hn = _bf(_rms_rows(mem_ref[...], g_ref[...]))
    kv = _dot(hn, w_ref[...])
    mk_ref[...] = _bf(_head_rms(kv[:, :MEM_W], bd_ref[...], kg_ref[...]))
    mv_ref[...] = _bf(kv[:, MEM_W:])


def _memkv_call(mem2d, p, l):
    rows = mem2d.shape[0]
    vm = pl.BlockSpec(memory_space=pltpu.VMEM)
    return pl.pallas_call(
        _memkv_kernel,
        in_specs=[vm] * 5,
        out_specs=[vm, vm],
        out_shape=[jax.ShapeDtypeStruct((rows, MEM_W), BF16)] * 2,
        compiler_params=pltpu.CompilerParams(vmem_limit_bytes=VMEM_LIMIT_BYTES),
        name=f"mem_kv_l{l}",
    )(mem2d, p["mem_norm_g"][l].reshape(1, -1), _bf(p["w_mem_kv"][l]),
      jnp.tile(p["mem_k_norm"][l], MEM_HEADS).reshape(1, -1), _block_diag_ones(MEM_W))


def _mix_kernel(x_ref, qa_ref, kc_ref, kp_ref, vc_ref, vp_ref, bias_ref, sink_ref, qm_ref, mk_ref, mv_ref,
                y_ref, r_ref, k_ref, v_ref, gate_ref, vec_ref, bd_ref, wout_ref, out_ref, att_ref,
                *, tq, tiles_per_batch):
    lnw_ref, lnb_ref, rk_ref = (_Row(vec_ref, VEC_ROWS[n], RWKV_W) for n in ("ln_w", "ln_b", "r_k"))
    N, BQ = HEAD_DIM, BLOCK_Q
    seq_start = pl.program_id(0) % tiles_per_batch == 0
    nqb = tq // BQ

    @pl.when(pl.program_id(0) == 0)
    def _():
        att_ref[...] = jnp.zeros_like(att_ref)

    qa = qa_ref[...]
    kall = jnp.concatenate([kp_ref[...], kc_ref[...]], axis=0)
    vall = jnp.concatenate([vp_ref[...], vc_ref[...]], axis=0)
    before_seq = lax.broadcasted_iota(jnp.int32, (ATTN_GROUP * BQ, 2 * BQ), 1) < BQ

    qm = qm_ref[...]
    mk = mk_ref[0]
    mv = mv_ref[0]

    swa = [(j, g) for j in range(nqb) for g in range(ATTN_KV_HEADS)]
    logits, values, sinks = [], [], []
    for j, g in swa:
        qg = jnp.concatenate(
            [qa[j * BQ:(j + 1) * BQ, (ATTN_GROUP * g + i) * N:(ATTN_GROUP * g + i + 1) * N] for i in range(ATTN_GROUP)],
            axis=0)
        lg = _dot_nt(qg, kall[j * BQ:(j + 2) * BQ, g * N:(g + 1) * N]) + bias_ref[g]
        if j == 0:
            lg = jnp.where(jnp.logical_and(seq_start, before_seq), -jnp.inf, lg)
        logits.append(lg)
        values.append(vall[j * BQ:(j + 2) * BQ, g * N:(g + 1) * N])
        sinks.append(sink_ref[g])
    for h in range(MEM_HEADS):
        hs = slice(h * N, (h + 1) * N)
        logits.append(_dot_nt(qm[:, hs], mk[:, hs]))
        values.append(mv[:, hs])
        sinks.append(None)
    bd = bd_ref[...]
    y = y_ref[...]
    d = y - _head_sum(y, bd) * (1.0 / N)
    var = _head_sum(d * d, bd) * (1.0 / N)
    yn = d * lax.rsqrt(var + GN_EPS) * lnw_ref[...] + lnb_ref[...]
    f32 = lambda ref: ref[...].astype(F32)
    bonus = _head_sum(f32(r_ref) * f32(k_ref) * rk_ref[...], bd) * f32(v_ref)
    out_b = (yn + bonus) * f32(gate_ref)
    mixed = _bf(jnp.concatenate([att_ref[:, :ATTN_W], out_b, att_ref[:, ATTN_W:]], axis=-1))
    out_ref[...] = x_ref[...] + _dot(mixed, wout_ref[...])

    row_max = [jnp.max(lg, axis=-1, keepdims=True) for lg in logits]
    m = [rm if s is None else jnp.maximum(rm, s) for rm, s in zip(row_max, sinks)]
    e = [jnp.exp(lg - mm) for lg, mm in zip(logits, m)]
    denom = [jnp.sum(ee, axis=-1, keepdims=True) for ee in e]
    denom = [d if s is None else d + jnp.exp(s - mm) for d, s, mm in zip(denom, sinks, m)]
    outs = [_dot(_bf(ee), vv) / d for ee, vv, d in zip(e, values, denom)]

    head_rows = [[None] * nqb for _ in range(ATTN_HEADS)]
    for (j, g), o in zip(swa, outs):
        for i in range(ATTN_GROUP):
            head_rows[ATTN_GROUP * g + i][j] = o[i * BQ:(i + 1) * BQ]
    att_ref[:, :ATTN_W] = jnp.concatenate([jnp.concatenate(rows, axis=0) for rows in head_rows], axis=-1)
    att_ref[:, ATTN_W:] = jnp.concatenate(outs[len(swa):], axis=-1)


def _mix_call(x, qa, ka, va, qm, mk, mv, y, r, k, v, gate, bias, p, l, *, nb):
    T, D = x.shape
    tq = MIX_TQ
    S = T // nb
    assert S % tq == 0 and tq % BLOCK_Q == 0
    tpb = S // tq
    qpb = tq // BLOCK_Q
    mem_tokens = mk.shape[0] // nb
    sink = jnp.repeat(p["attn_sinks"][l].astype(F32), BLOCK_Q).reshape(ATTN_KV_HEADS, ATTN_GROUP * BLOCK_Q, 1)
    n_tiles = T // tq
    att = lambda i: jnp.minimum(i, n_tiles - 1)
    fin = lambda i: jnp.maximum(i - 1, 0)
    att_tile = lambda w: pl.BlockSpec((tq, w), lambda i: (att(i), 0))
    tile = lambda w: pl.BlockSpec((tq, w), lambda i: (fin(i), 0))
    prev = pl.BlockSpec((BLOCK_Q, KV_W), lambda i: (jnp.maximum(att(i) * qpb - 1, 0), 0))
    full = lambda a: pl.BlockSpec(a.shape, lambda i: (0,) * a.ndim)
    memspec = pl.BlockSpec((1, mem_tokens, MEM_W), lambda i: (att(i) // tpb, 0, 0))
    bd = _block_diag_ones(RWKV_W)
    wout = p["w_out"]
    mk3 = mk.reshape(nb, mem_tokens, MEM_W)
    mv3 = mv.reshape(nb, mem_tokens, MEM_W)
    return pl.pallas_call(
        functools.partial(_mix_kernel, tq=tq, tiles_per_batch=tpb),
        grid=(n_tiles + 1,),
        in_specs=[tile(D), att_tile(ATTN_W), att_tile(KV_W), prev, att_tile(KV_W), prev, full(bias), full(sink),
                  att_tile(MEM_W), memspec, memspec] + [tile(RWKV_W)] * 5
                 + [_layer_spec(p["vectors"], l), full(bd), _layer_spec(wout, l)],
        out_specs=tile(D),
        out_shape=jax.ShapeDtypeStruct((T, D), F32),
        scratch_shapes=[pltpu.VMEM((tq, ATTN_W + MEM_W), F32)],
        compiler_params=pltpu.CompilerParams(dimension_semantics=("arbitrary",), vmem_limit_bytes=VMEM_LIMIT_BYTES),
        name=f"mix_l{l}",
    )(x, qa, ka, ka, va, va, bias, sink, qm, mk3, mv3, y, r, k, v, gate, p["vectors"], bd, wout)


SUBLANES = 8


def _tile_copies(hbm, buf, sem, tile, slot, to_vmem):
    copies = []
    for s in range(SUBLANES):
        src, dst = hbm.at[tile, s], buf.at[slot, :, s, :]
        if not to_vmem:
            src, dst = dst, src
        copies.append(pltpu.make_async_copy(src, dst, sem.at[slot, s]))
    return copies


def _shift_rows(u, before, steps):
    first = lax.broadcasted_iota(jnp.int32, (SUBLANES, u.shape[1]), 0) == 0
    tm = u.shape[0]
    heads = []
    for i in range(steps):
        lo = tm - (steps - i) * SUBLANES
        wrapped = pltpu.roll(u[lo:lo + SUBLANES], 1, axis=0)
        prior = jnp.broadcast_to(before[SUBLANES - steps + i:SUBLANES - steps + i + 1], wrapped.shape)
        heads.append(jnp.where(first, prior, wrapped))
    return jnp.concatenate(heads + [u[:tm - steps * SUBLANES]], axis=0)


def _stage_weights(w_hbm, layer, dst_ref, stage_ref, sem):
    rows = stage_ref.shape[1]
    n = dst_ref.shape[0] // rows
    copy = lambda c: pltpu.make_async_copy(w_hbm.at[layer, pl.ds(c * rows, rows), :], stage_ref.at[c % 2], sem.at[c % 2])
    copy(0).start()
    for c in range(n):
        if c + 1 < n:
            copy(c + 1).start()
        copy(c).wait()
        dst_ref[c * rows:(c + 1) * rows, :] = _bf(stage_ref[c % 2])


def _ffn_kernel(x_hbm, vec_ref, wup_hbm, cw_ref, cb_ref, wdn_hbm, o_hbm, xbuf, obuf, carry_ref, act_ref,
                wup_ref, wdn_ref, up_stage, dn_stage, in_sem, out_sem, w_sem, *, tm, tiles_per_batch, n_tiles, layer):
    D = xbuf.shape[-1]
    g_ref = _Row(vec_ref, VEC_ROWS["ffn_norm_g"], D)
    q = tm // SUBLANES
    i = pl.program_id(0)
    slot = i % 2

    @pl.when(i == 0)
    def _():
        carry_ref[...] = jnp.zeros_like(carry_ref)
        for cp in _tile_copies(x_hbm, xbuf, in_sem, 0, 0, True):
            cp.start()
        _stage_weights(wup_hbm, layer, wup_ref, up_stage, w_sem)
        _stage_weights(wdn_hbm, layer, wdn_ref, dn_stage, w_sem)

    @pl.when(i + 1 < n_tiles)
    def _():
        for cp in _tile_copies(x_hbm, xbuf, in_sem, i + 1, 1 - slot, True):
            cp.start()

    for cp in _tile_copies(x_hbm, xbuf, in_sem, i, slot, True):
        cp.wait()

    @pl.when(i >= 2)
    def _():
        for cp in _tile_copies(o_hbm, obuf, out_sem, i - 2, slot, False):
            cp.wait()

    seq_start = i % tiles_per_batch == 0
    g = g_ref[...]
    x = xbuf[slot].reshape(tm, D)
    before = jnp.where(seq_start, 0.0, _rms_rows(carry_ref[...], g))
    for r, grp in ((SUBLANES - 2, q - 2), (SUBLANES - 1, q - 1)):
        row = grp * SUBLANES + SUBLANES - 1
        carry_ref[r:r + 1, :] = x[row:row + 1]
    h_ext = _bf(jnp.concatenate([before, _rms_rows(x, g)], axis=0))

    n_chunks = D_FF // FFN_FC
    gate_cols = lambda c: slice(c * FFN_FC, (c + 1) * FFN_FC)
    val_cols = lambda c: slice(D_FF + c * FFN_FC, D_FF + (c + 1) * FFN_FC)

    def conv(u_ext, cols):
        w = cw_ref[:, cols]
        u = u_ext[SUBLANES:]
        return (cb_ref[:, cols] + w[0:1] * _shift_rows(u, u_ext[:SUBLANES], 2)
                + w[1:2] * _shift_rows(u, u_ext[:SUBLANES], 1) + w[2:3] * u)

    up = lambda c: (_dot(h_ext, wup_ref[:, gate_cols(c)]), _dot(h_ext, wup_ref[:, val_cols(c)]))
    nxt = up(0)
    for c in range(n_chunks):
        ug, uv = nxt
        if c + 1 < n_chunks:
            nxt = up(c + 1)
        gt = conv(ug, gate_cols(c))
        act_ref[:, gate_cols(c)] = _bf(gt * jax.nn.sigmoid(gt) * conv(uv, val_cols(c)))
    obuf[slot] = (x + _dot(act_ref[...], wdn_ref[...])).reshape(q, SUBLANES, D)

    for cp in _tile_copies(o_hbm, obuf, out_sem, i, slot, False):
        cp.start()

    @pl.when(i == n_tiles - 1)
    def _():
        if n_tiles >= 2:
            for cp in _tile_copies(o_hbm, obuf, out_sem, i - 1, 1 - slot, False):
                cp.wait()
        for cp in _tile_copies(o_hbm, obuf, out_sem, i, slot, False):
            cp.wait()


def _ffn_call(x, p, l, *, nb):
    T, D = x.shape
    tm = FFN_TM
    S = T // nb
    q = tm // SUBLANES
    assert S % tm == 0 and q % SUBLANES == 0 and D_FF % FFN_FC == 0
    n_tiles = T // tm
    wup, wdn = p["w_up"], p["w_down"]
    cw, cb = p["conv_w"], p["conv_b"][:, None, :]
    hbm = pl.BlockSpec(memory_space=pl.ANY)
    assert D % FFN_STAGE_CHUNKS == 0 and D_FF % FFN_STAGE_CHUNKS == 0
    out = pl.pallas_call(
        functools.partial(_ffn_kernel, tm=tm, tiles_per_batch=S // tm, n_tiles=n_tiles, layer=l),
        grid=(n_tiles,),
        in_specs=[hbm, _layer_spec(p["vectors"], l), hbm, _layer_spec(cw, l), _layer_spec(cb, l), hbm],
        out_specs=hbm,
        out_shape=jax.ShapeDtypeStruct((n_tiles, SUBLANES, q, D), F32),
        scratch_shapes=[pltpu.VMEM((2, q, SUBLANES, D), F32), pltpu.VMEM((2, q, SUBLANES, D), F32),
                        pltpu.VMEM((SUBLANES, D), F32), pltpu.VMEM((tm, D_FF), BF16),
                        pltpu.VMEM(wup.shape[1:], BF16), pltpu.VMEM(wdn.shape[1:], BF16),
                        pltpu.VMEM((2, D // FFN_STAGE_CHUNKS, 2 * D_FF), F32),
                        pltpu.VMEM((2, D_FF // FFN_STAGE_CHUNKS, D), F32),
                        pltpu.SemaphoreType.DMA((2, SUBLANES)), pltpu.SemaphoreType.DMA((2, SUBLANES)),
                        pltpu.SemaphoreType.DMA((2,))],
        compiler_params=pltpu.CompilerParams(dimension_semantics=("arbitrary",), vmem_limit_bytes=VMEM_LIMIT_BYTES),
        name=f"ffn_l{l}",
    )(x.reshape(n_tiles, SUBLANES, q, D), p["vectors"], wup, cw, cb, wdn)
    return out.reshape(T, D)


_PARAM_NAMES = (
    "rel_bias", "mix_norm_g", "w_in", "w_in_vres", "attn_q_norm", "attn_k_norm", "attn_sinks", "rwkv_mu",
    "rwkv_mu_vres", "rwkv_w0", "rwkv_w2", "rwkv_a0", "rwkv_a2", "rwkv_v0", "rwkv_v2", "rwkv_g2", "rwkv_k_k",
    "rwkv_k_a", "rwkv_r_k", "rwkv_ln_w", "rwkv_ln_b", "mem_norm_g", "w_mem_kv", "mem_q_norm", "mem_k_norm",
    "w_out", "ffn_norm_g", "w_up", "conv_w", "conv_b", "w_down")


def kernel(x, mem, rel_bias, mix_norm_g, w_in, w_in_vres, attn_q_norm, attn_k_norm, attn_sinks, rwkv_mu,
           rwkv_mu_vres, rwkv_w0, rwkv_w2, rwkv_a0, rwkv_a2, rwkv_v0, rwkv_v2, rwkv_g2, rwkv_k_k, rwkv_k_a,
           rwkv_r_k, rwkv_ln_w, rwkv_ln_b, mem_norm_g, w_mem_kv, mem_q_norm, mem_k_norm, w_out, ffn_norm_g,
           w_up, conv_w, conv_b, w_down):
    p = dict(zip(_PARAM_NAMES, (
        rel_bias, mix_norm_g, w_in, w_in_vres, attn_q_norm, attn_k_norm, attn_sinks, rwkv_mu, rwkv_mu_vres,
        rwkv_w0, rwkv_w2, rwkv_a0, rwkv_a2, rwkv_v0, rwkv_v2, rwkv_g2, rwkv_k_k, rwkv_k_a, rwkv_r_k, rwkv_ln_w,
        rwkv_ln_b, mem_norm_g, w_mem_kv, mem_q_norm, mem_k_norm, w_out, ffn_norm_g, w_up, conv_w, conv_b, w_down)))
    p["vectors"], p["lora"] = _pack_params(p)
    for name in ("w_in", "w_mem_kv", "w_out"):
        p[name] = _bf(p[name])
    nb, S, D = x.shape
    xt = x.reshape(nb * S, D)
    mem2d = mem.reshape(nb * mem.shape[1], D)
    bias = _bias_call(rel_bias)
    v_first = None
    for l in range(w_in.shape[0]):
        qa, ka, va, qm, r, lw, k, v, a, b, gate = _proj_call(xt, p, l, v_first, nb=nb)
        if l == 0:
            v_first = v
        y = _rwkv_scan(r, lw, k, v, a, b, nb=nb)
        mk, mv = _memkv_call(mem2d, p, l)
        xt = _mix_call(xt, qa, ka, va, qm, mk, mv, y, r, k, v, gate, bias, p, l, nb=nb)
        xt = _ffn_call(xt, p, l, nb=nb)
    return xt.reshape(nb, S, D)
```

```python
import functools
import math

import jax
import jax.numpy as jnp
import numpy as np
from jax import lax
from jax.experimental import pallas as pl
from jax.experimental.pallas import tpu as pltpu

F32 = jnp.float32
BF16 = jnp.bfloat16

HEAD_DIM = 64
SCAN_CHUNK = 64
VMEM_LIMIT_BYTES = 56 * 1024 * 1024


def _dot(a, b):
    return jnp.dot(a, b, preferred_element_type=F32)


def _dot_nt(a, b):
    return lax.dot_general(a, b, (((1,), (1,)), ((), ())), preferred_element_type=F32)


def _dot_tn(a, b):
    return lax.dot_general(a, b, (((0,), (0,)), ((), ())), preferred_element_type=F32)


def _bf(x):
    return x.astype(BF16)


SCAN_GROUP_HEADS = 4
SCAN_GROUP_W = SCAN_GROUP_HEADS * HEAD_DIM
SCAN_CHUNKS_PER_STEP = 4


_SCAN_STASH = (("w2", 1, F32), ("y_v", 1, F32), ("m_rb", 1, F32), ("v", 1, F32), ("w1r", 2, F32),
               ("bk", 2, F32), ("g_row", 1, F32))


def _scan_kernel(r_ref, lw_ref, k_ref, v_ref, a_ref, b_ref, bdm_ref, y_ref, h_ref, *stash_refs, nb, nch):
    C, N, GW, HPG = SCAN_CHUNK, HEAD_DIM, SCAN_GROUP_W, SCAN_GROUP_HEADS
    ngroups = h_ref.shape[0]
    stash = {name: ref for (name, _, _), ref in zip(_SCAN_STASH, stash_refs)}

    @pl.when(pl.program_id(0) == 0)
    def _():
        h_ref[...] = jnp.zeros_like(h_ref)
        for ref in stash_refs:
            ref[...] = jnp.zeros_like(ref)

    trow = lax.broadcasted_iota(jnp.int32, (C, C), 0)
    tcol = lax.broadcasted_iota(jnp.int32, (C, C), 1)
    tri = (trow >= tcol).astype(BF16)
    grow = lax.broadcasted_iota(jnp.int32, (C, GW), 0)
    gcol = lax.broadcasted_iota(jnp.int32, (C, GW), 1) % N
    incl = grow >= gcol
    strict = grow > gcol
    eye = (grow == gcol).astype(F32)
    bdm = bdm_ref[...]

    def bd(x):
        return jnp.concatenate([_bf(x)] * HPG, axis=0) * bdm

    def split2(x):
        hi = _bf(x)
        return hi, _bf(x - hi.astype(F32))

    def chunk_units(c):
        rows = slice(c * C, (c + 1) * C)
        per_batch = []
        for b in range(nb):
            lw = lw_ref[b, rows, :]
            l1 = _bf(lw)
            e1 = lw - l1.astype(F32)
            l2 = _bf(e1)
            l3 = _bf(e1 - l2.astype(F32))
            cum = _dot(tri, l1) + _dot(tri, l2) + _dot(tri, l3)
            cum_last = cum[C - 1:C, :]
            g_inv = jnp.exp(-cum)
            g_out = jnp.exp(cum_last - cum)
            f32 = lambda ref: ref[b, rows, :].astype(F32)
            kk, bb = f32(k_ref), f32(b_ref)
            per_batch.append(dict(
                a_t=f32(a_ref) * jnp.exp(cum - lw), r_t=f32(r_ref) * jnp.exp(cum),
                b_t=bb * g_inv, k_t=kk * g_inv, b_h=bb * g_out, k_h=kk * g_out,
                g_c=jnp.broadcast_to(jnp.exp(cum_last), cum.shape), v=f32(v_ref)))
        cat = {n: jnp.concatenate([pb[n] for pb in per_batch], axis=1) for n in per_batch[0]}
        return [{n: t[:, g * GW:(g + 1) * GW] for n, t in cat.items()} for g in range(ngroups)]

    def prepare(chunks):
        units = [u for c in chunks for u in chunk_units(c)]

        nu = len(units)
        ar = [jnp.concatenate([_bf(u["a_t"]), _bf(u["r_t"])], axis=0) for u in units]
        pb = [_dot_nt(ar[i], bd(units[i]["b_t"])) for i in range(nu)]
        pk = [_dot_nt(ar[i], bd(units[i]["k_t"])) for i in range(nu)]
        l_ab = [jnp.where(strict, p[:C], 0.0) for p in pb]
        m_rb = [_bf(jnp.where(incl, p[C:], 0.0)) for p in pb]
        tril2 = jnp.concatenate([strict, incl], axis=0)
        lm_k = [_bf(jnp.where(tril2, p, 0.0)) for p in pk]
        yield
        t_inv = [eye + l for l in l_ab]
        pw = [_dot(_bf(l), bd(l)) for l in l_ab]
        yield
        n_rounds = int(math.log2(C)) - 1
        for rnd in range(n_rounds):
            if rnd + 1 < n_rounds:
                z = [_dot(jnp.concatenate([_bf(p), _bf(t)], axis=0), bd(p)) for p, t in zip(pw, t_inv)]
                pw = [zz[:C] for zz in z]
                t_inv = [t + zz[C:] for t, zz in zip(t_inv, z)]
            else:
                t_inv = [t + _dot(_bf(t), bd(p)) for t, p in zip(t_inv, pw)]
            yield
        t_b = [_bf(t) for t in t_inv]
        w1 = [_dot(t, bd(u["a_t"])) for t, u in zip(t_b, units)]
        lmv = [_dot(m, bd(u["v"])) for m, u in zip(lm_k, units)]
        yield
        w2 = [_dot(t, bd(x[:C])) for t, x in zip(t_b, lmv)]
        parts = [p for u in units for p in split2(u["g_c"] * eye)]
        sums = _dot(jnp.concatenate(parts, axis=0), bdm)
        yield
        for i, u in enumerate(units):
            put = lambda name, val: stash[name].__setitem__(i, val.astype(stash[name].dtype))
            put("w2", w2[i])
            put("y_v", lmv[i][C:])
            put("m_rb", m_rb[i])
            put("v", u["v"])
            put("w1r", jnp.concatenate([w1[i], u["r_t"]], axis=0))
            put("bk", jnp.concatenate([u["b_h"], u["k_h"]], axis=0))
            put("g_row", sums[2 * i * C:(2 * i + 1) * C] + sums[(2 * i + 1) * C:(2 * i + 2) * C])
        yield

    state = [h_ref[g] for g in range(ngroups)]

    def advance(c):
        p = {n: [ref[c * ngroups + g] for g in range(ngroups)] for n, ref in stash.items()}
        for n in ("m_rb", "v", "w1r", "bk"):
            p[n] = [_bf(x) for x in p[n]]
        z = [_dot(w, bd(s)) for w, s in zip(p["w1r"], state)]
        yield
        u = [zz[:C] + w for zz, w in zip(z, p["w2"])]
        y = [zz[C:] + _dot(m, bd(x)) + yv for zz, m, x, yv in zip(z, p["m_rb"], u, p["y_v"])]
        uv = [jnp.concatenate([_bf(x), v], axis=0) for x, v in zip(u, p["v"])]
        full = [_dot_tn(b, x) * bdm.astype(F32) for b, x in zip(p["bk"], uv)]
        yield
        upd = [sum(f[h * N:(h + 1) * N] for h in range(HPG)) for f in full]
        state[:] = [g * s + d for g, s, d in zip(p["g_row"], state, upd)]
        ycat = jnp.concatenate(y, axis=1)
        w = y_ref.shape[2]
        for b in range(nb):
            y_ref[b, c * C:(c + 1) * C, :] = ycat[:, b * w:(b + 1) * w]
        yield

    def run(gen, stages):
        for _ in range(stages):
            next(gen, None)

    prep = prepare(list(range(nch)))
    n_prep = 5 + int(math.log2(C)) - 1
    for c in range(nch):
        seq = advance(c)
        run(prep, 1)
        run(seq, 1)
        run(prep, 1)
        run(seq, 2)
    run(prep, n_prep)
    for g in range(ngroups):
        h_ref[g] = state[g]


def _rwkv_scan(r, lw, k, v, a, b, *, nb):
    T, W = r.shape
    S = T // nb
    C, nch, GW = SCAN_CHUNK, SCAN_CHUNKS_PER_STEP, SCAN_GROUP_W
    assert S % (C * nch) == 0 and (nb * W) % GW == 0 and W % 128 == 0
    ngroups = nb * W // GW
    head = np.arange(GW) // HEAD_DIM
    bdm = jnp.asarray((head[:, None] == head[None, :]).astype(np.float32), dtype=BF16)
    n_blocks = S // (C * nch)
    in_spec = pl.BlockSpec((nb, C * nch, W), lambda s: (0, jnp.minimum(s, n_blocks - 1), 0))
    out_spec = pl.BlockSpec((nb, C * nch, W), lambda s: (0, jnp.maximum(s - 1, 0), 0))
    args = [t.reshape(nb, S, W) for t in (r, lw, k, v, a, b)]
    y = pl.pallas_call(
        functools.partial(_scan_kernel, nb=nb, nch=nch),
        grid=(n_blocks + 1,),
        in_specs=[in_spec] * 6 + [pl.BlockSpec((GW, GW), lambda s: (0, 0))],
        out_specs=out_spec,
        out_shape=jax.ShapeDtypeStruct((nb, S, W), F32),
        scratch_shapes=[pltpu.VMEM((ngroups, HEAD_DIM, GW), F32)]
        + [pltpu.VMEM((nch * ngroups, rows * C, GW), dt) for _, rows, dt in _SCAN_STASH],
        compiler_params=pltpu.CompilerParams(dimension_semantics=("arbitrary",), vmem_limit_bytes=VMEM_LIMIT_BYTES),
        name="rwkv_scan",
    )(*args, bdm)
    return y.reshape(T, W)


D_MODEL = 1024
ATTN_HEADS = 6
ATTN_KV_HEADS = 2
ATTN_GROUP = ATTN_HEADS // ATTN_KV_HEADS
ATTN_W = ATTN_HEADS * HEAD_DIM
KV_W = ATTN_KV_HEADS * HEAD_DIM
BLOCK_Q = 128
WINDOW = 128
N_BUCKETS = 32
MAX_EXACT = N_BUCKETS // 2
RWKV_W = 6 * HEAD_DIM
LORA_W = 128
SHIFT_W = 3 * RWKV_W + LORA_W
MEM_HEADS = 4
MEM_W = MEM_HEADS * HEAD_DIM
IN_BASE = ATTN_W + 2 * KV_W + SHIFT_W + MEM_W
PB_OFF = ATTN_W + 2 * KV_W
QM_OFF = PB_OFF + SHIFT_W
D_FF = 2816
EPS = 1e-6
GN_EPS = 64e-5
L2_EPS = 1e-12
MXU_WIDTH = 256

PROJ_TM = 1024
PROJ_SUB = 256
MIX_TQ = 512
FFN_TM = 1024
FFN_FC = 256
FFN_STAGE_CHUNKS = 8


def _block_diag_ones(width):
    idx = np.arange(width) // HEAD_DIM
    return jnp.asarray((idx[:, None] == idx[None, :]).astype(np.float32), dtype=BF16)


def _head_sum(t, bd):
    tb = _bf(t)
    width = t.shape[1]
    parts = [_dot(tb[:, lo:min(lo + MXU_WIDTH, width)], bd[lo:min(lo + MXU_WIDTH, width), lo:min(lo + MXU_WIDTH, width)])
             for lo in range(0, width, MXU_WIDTH)]
    return parts[0] if len(parts) == 1 else jnp.concatenate(parts, axis=1)


def _head_rms(t, bd, gain):
    ms = _head_sum(t * t, bd) * (1.0 / HEAD_DIM)
    return t * lax.rsqrt(ms + EPS) * gain


def _rms_rows(x, g):
    ms = jnp.mean(x * x, axis=-1, keepdims=True)
    return x * lax.rsqrt(ms + EPS) * g


class _Row:
    def __init__(self, table_ref, row, width):
        self.table_ref, self.row, self.width = table_ref, row, width

    def __getitem__(self, _):
        return self.table_ref[self.row:self.row + 1, 0:self.width]


def _proj_kernel(*refs, tm, sub, tiles_per_batch, has_vres):
    it = iter(refs)
    x_ref, vec_ref, w_ref, lora_ref = next(it), next(it), next(it), next(it)
    bd384_ref, bd128_ref, bd256_ref = next(it), next(it), next(it)
    if has_vres:
        vfirst_ref = next(it)
    qa_ref, ka_out_ref, va_ref, qm_ref = next(it), next(it), next(it), next(it)
    r_ref, lw_ref, k_ref, v_ref, a_ref, b_ref, gate_ref = (next(it) for _ in range(7))
    pbs_ref = next(it)
    sw = pbs_ref.shape[1]
    g_ref, qg_ref, kg_ref, mg_ref = (_Row(vec_ref, VEC_ROWS[n], w) for n, w in (
        ("mix_norm_g", D_MODEL), ("q_gain", ATTN_W), ("k_gain", KV_W), ("mq_gain", MEM_W)))
    mu_ref = _Row(vec_ref, VEC_ROWS["mu"], sw)
    w0_ref, a0_ref, kk_ref, ka_ref, v0_ref = (_Row(vec_ref, VEC_ROWS[n], RWKV_W) for n in ("w0", "a0", "k_k", "k_a", "v0"))
    w2_ref, a2_ref, g2_ref, v2_ref = (lora_ref.at[j] for j in range(4))

    @pl.when(pl.program_id(0) % tiles_per_batch == 0)
    def _():
        pbs_ref[0:8, :] = jnp.zeros((8, sw), F32)

    def project(i):
        rows = slice(i * sub, (i + 1) * sub)
        return _dot(_bf(_rms_rows(x_ref[rows, :], g_ref[...])), w_ref[...])

    def attention_outputs(i, proj):
        rows = slice(i * sub, (i + 1) * sub)
        qa_ref[rows, :] = _bf(_head_rms(proj[:, :ATTN_W], bd384_ref[...], qg_ref[...]))
        ka_out_ref[rows, :] = _bf(_head_rms(proj[:, ATTN_W:ATTN_W + KV_W], bd128_ref[...], kg_ref[...]))
        va_ref[rows, :] = _bf(proj[:, ATTN_W + KV_W:PB_OFF])
        qm_ref[rows, :] = _bf(_head_rms(proj[:, QM_OFF:QM_OFF + MEM_W], bd256_ref[...], mg_ref[...]))

    def rwkv_outputs(i, proj):
        rows = slice(i * sub, (i + 1) * sub)
        lo = 8 + i * sub
        pbs_ref[lo:lo + sub, 0:SHIFT_W] = proj[:, PB_OFF:PB_OFF + SHIFT_W]
        if has_vres:
            pbs_ref[lo:lo + sub, SHIFT_W:sw] = proj[:, IN_BASE:IN_BASE + LORA_W]
        cur = pbs_ref[lo:lo + sub, :]
        prev = pbs_ref[lo - 1:lo - 1 + sub, :]
        sh = cur + mu_ref[...] * (prev - cur)
        r = sh[:, 0:RWKV_W]
        k = sh[:, RWKV_W:2 * RWKV_W]
        v = sh[:, 2 * RWKV_W:3 * RWKV_W]
        z = sh[:, 3 * RWKV_W:SHIFT_W]
        t = w0_ref[...] + _dot(_bf(jnp.tanh(z)), w2_ref[...])
        lw_ref[rows, :] = -math.exp(-0.5) * jax.nn.sigmoid(t)
        a = jax.nn.sigmoid(a0_ref[...] + _dot(_bf(z), a2_ref[...]))
        gate_ref[rows, :] = _bf(_dot(_bf(jax.nn.sigmoid(z)), g2_ref[...]))
        if has_vres:
            vd = sh[:, SHIFT_W:sw]
            v = v + (vfirst_ref[rows, :].astype(F32) - v) * jax.nn.sigmoid(v0_ref[...] + _dot(_bf(vd), v2_ref[...]))
        kk = k * kk_ref[...]
        kk = kk / jnp.maximum(jnp.sqrt(_head_sum(kk * kk, bd384_ref[...])), L2_EPS)
        r_ref[rows, :] = _bf(r)
        k_ref[rows, :] = _bf(k * (1.0 + (a - 1.0) * ka_ref[...]))
        v_ref[rows, :] = _bf(v)
        a_ref[rows, :] = _bf(-kk)
        b_ref[rows, :] = _bf(kk * a)

    n_sub = tm // sub
    proj = project(0)
    for i in range(n_sub):
        attention_outputs(i, proj)
        nxt = project(i + 1) if i + 1 < n_sub else None
        rwkv_outputs(i, proj)
        proj = nxt
    pbs_ref[0:8, :] = pbs_ref[tm:tm + 8, :]


def _layer_spec(stacked, l):
    return pl.BlockSpec((None,) + stacked.shape[1:], lambda i: (l,) + (0,) * (stacked.ndim - 1),
                        pipeline_mode=pl.Buffered(1))


VEC_W = SHIFT_W + LORA_W
VEC_ROWS = {n: i for i, n in enumerate((
    "mix_norm_g", "q_gain", "k_gain", "mq_gain", "mu", "w0", "a0", "k_k", "k_a", "v0", "ln_w", "ln_b", "r_k",
    "ffn_norm_g"))}
VEC_TABLE_ROWS = 16


def _pack_params(p):
    layers = p["w_in"].shape[0]
    scale = HEAD_DIM ** -0.5
    zeros = jnp.zeros((RWKV_W,), F32)
    vec_pieces, lora_pieces = [], []
    for l in range(layers):
        vecs = dict(
            mix_norm_g=p["mix_norm_g"][l], q_gain=jnp.tile(p["attn_q_norm"][l], ATTN_HEADS) * scale,
            k_gain=jnp.tile(p["attn_k_norm"][l], ATTN_KV_HEADS), mq_gain=jnp.tile(p["mem_q_norm"][l], MEM_HEADS) * scale,
            mu=p["rwkv_mu"][l], w0=p["rwkv_w0"][l], a0=p["rwkv_a0"][l], k_k=p["rwkv_k_k"][l], k_a=p["rwkv_k_a"][l],
            v0=p["rwkv_v0"][l - 1] if l > 0 else zeros, ln_w=p["rwkv_ln_w"][l], ln_b=p["rwkv_ln_b"][l],
            r_k=p["rwkv_r_k"][l].reshape(-1), ffn_norm_g=p["ffn_norm_g"][l])
        for n in VEC_ROWS:
            row = [vecs[n].astype(F32)]
            if n == "mu" and l > 0:
                row.append(p["rwkv_mu_vres"][l - 1].astype(F32))
            row.append(jnp.zeros((VEC_W - sum(t.shape[0] for t in row),), F32))
            vec_pieces += row
        vec_pieces.append(jnp.zeros(((VEC_TABLE_ROWS - len(VEC_ROWS)) * VEC_W,), F32))
        v2 = p["rwkv_v2"][l - 1] if l > 0 else jnp.zeros((16, RWKV_W), F32)
        for w, at in ((p["rwkv_w2"][l], 0), (p["rwkv_a2"][l], 32), (p["rwkv_g2"][l], 64), (v2, 0)):
            lora_pieces += [jnp.zeros((at, RWKV_W), F32), w.astype(F32),
                            jnp.zeros((LORA_W - at - w.shape[0], RWKV_W), F32)]
    tables = jnp.concatenate(vec_pieces).reshape(layers, VEC_TABLE_ROWS, VEC_W)
    loras = jnp.concatenate([t for t in lora_pieces if t.shape[0]], axis=0).reshape(layers, 4, LORA_W, RWKV_W)
    return tables, _bf(loras)


def _proj_call(x, p, l, v_first, *, nb):
    T, D = x.shape
    tm = PROJ_TM
    assert T % tm == 0 and (T // nb) % tm == 0
    has_vres = l > 0
    w = p["w_in"]
    if has_vres:
        w = jnp.concatenate([w[l], _bf(jnp.pad(p["w_in_vres"][l - 1], ((0, 0), (0, LORA_W - 16))))], axis=1)
    ins = [x, p["vectors"], w, p["lora"], _block_diag_ones(ATTN_W), _block_diag_ones(KV_W), _block_diag_ones(MEM_W)]
    if has_vres:
        ins.append(v_first)

    def spec(a):
        if a.ndim >= 3:
            return _layer_spec(a, l)
        if a.shape[0] == T:
            return pl.BlockSpec((tm, a.shape[1]), lambda i: (i, 0))
        return pl.BlockSpec(a.shape, lambda i: (0, 0), pipeline_mode=pl.Buffered(1))

    out_shapes = [jax.ShapeDtypeStruct((T, ATTN_W), BF16), jax.ShapeDtypeStruct((T, KV_W), BF16),
                  jax.ShapeDtypeStruct((T, KV_W), BF16), jax.ShapeDtypeStruct((T, MEM_W), BF16)]
    out_shapes += [jax.ShapeDtypeStruct((T, RWKV_W), F32 if j == 1 else BF16) for j in range(7)]
    sw = SHIFT_W + (LORA_W if has_vres else 0)
    return pl.pallas_call(
        functools.partial(_proj_kernel, tm=tm, sub=PROJ_SUB, tiles_per_batch=(T // nb) // tm, has_vres=has_vres),
        grid=(T // tm,),
        in_specs=[spec(a) for a in ins],
        out_specs=[pl.BlockSpec((tm, s.shape[1]), lambda i: (i, 0)) for s in out_shapes],
        out_shape=out_shapes,
        scratch_shapes=[pltpu.VMEM((tm + 8, sw), F32)],
        compiler_params=pltpu.CompilerParams(dimension_semantics=("arbitrary",), vmem_limit_bytes=VMEM_LIMIT_BYTES),
        name=f"proj_l{l}",
    )(*ins)


def _bucket_table():
    qi = np.arange(BLOCK_Q)[:, None]
    kj = np.arange(2 * BLOCK_Q)[None, :]
    dist = qi + BLOCK_Q - kj
    in_band = (dist >= 0) & (dist < WINDOW)
    d = np.maximum(dist, 1).astype(np.float32)
    large = MAX_EXACT + (np.log(d / np.float32(MAX_EXACT)) / np.float32(math.log(WINDOW / MAX_EXACT))
                         * np.float32(N_BUCKETS - MAX_EXACT)).astype(np.int32)
    large = np.minimum(large, N_BUCKETS - 1)
    bucket = np.where(dist < MAX_EXACT, np.maximum(dist, 0), large)
    return np.where(in_band, bucket, -1).astype(np.int32)


def _bias_kernel(rb_ref, bucket_ref, out_ref):
    bucket = bucket_ref[...]
    for h in range(ATTN_HEADS):
        acc = jnp.full(bucket.shape, -jnp.inf, F32)
        for j in range(N_BUCKETS):
            acc = jnp.where(bucket == j, rb_ref[j, h], acc)
        out_ref[h] = acc


def _bias_call(rel_bias):
    tab = pl.pallas_call(
        _bias_kernel,
        in_specs=[pl.BlockSpec(memory_space=pltpu.SMEM), pl.BlockSpec(memory_space=pltpu.VMEM)],
        out_specs=pl.BlockSpec(memory_space=pltpu.VMEM),
        out_shape=jax.ShapeDtypeStruct((ATTN_HEADS, BLOCK_Q, 2 * BLOCK_Q), F32),
        name="rel_bias_table",
    )(rel_bias.astype(F32), jnp.asarray(_bucket_table()))
    return tab.reshape(ATTN_KV_HEADS, ATTN_GROUP * BLOCK_Q, 2 * BLOCK_Q)


def _memkv_kernel(mem_ref, g_ref, w_ref, kg_ref, bd_ref, mk_ref, mv_ref):
    hn = _bf(_rms_rows(mem_ref[...], g_ref[...]))
    kv = _dot(hn, w_ref[...])
    mk_ref[...] = _bf(_head_rms(kv[:, :MEM_W], bd_ref[...], kg_ref[...]))
    mv_ref[...] = _bf(kv[:, MEM_W:])


def _memkv_call(mem2d, p, l):
    rows = mem2d.shape[0]
    vm = pl.BlockSpec(memory_space=pltpu.VMEM)
    return pl.pallas_call(
        _memkv_kernel,
        in_specs=[vm] * 5,
        out_specs=[vm, vm],
        out_shape=[jax.ShapeDtypeStruct((rows, MEM_W), BF16)] * 2,
        compiler_params=pltpu.CompilerParams(vmem_limit_bytes=VMEM_LIMIT_BYTES),
        name=f"mem_kv_l{l}",
    )(mem2d, p["mem_norm_g"][l].reshape(1, -1), _bf(p["w_mem_kv"][l]),
      jnp.tile(p["mem_k_norm"][l], MEM_HEADS).reshape(1, -1), _block_diag_ones(MEM_W))


def _mix_kernel(x_ref, qa_ref, kc_ref, kp_ref, vc_ref, vp_ref, bias_ref, sink_ref, qm_ref, mk_ref, mv_ref,
                y_ref, r_ref, k_ref, v_ref, gate_ref, vec_ref, bd_ref, wout_ref, out_ref, att_ref,
                *, tq, tiles_per_batch):
    lnw_ref, lnb_ref, rk_ref = (_Row(vec_ref, VEC_ROWS[n], RWKV_W) for n in ("ln_w", "ln_b", "r_k"))
    N, BQ = HEAD_DIM, BLOCK_Q
    seq_start = pl.program_id(0) % tiles_per_batch == 0
    nqb = tq // BQ

    @pl.when(pl.program_id(0) == 0)
    def _():
        att_ref[...] = jnp.zeros_like(att_ref)

    qa = qa_ref[...]
    kall = jnp.concatenate([kp_ref[...], kc_ref[...]], axis=0)
    vall = jnp.concatenate([vp_ref[...], vc_ref[...]], axis=0)
    k_group = [kall[:, g * N:(g + 1) * N] for g in range(ATTN_KV_HEADS)]
    v_group = [vall[:, g * N:(g + 1) * N] for g in range(ATTN_KV_HEADS)]
    before_seq = lax.broadcasted_iota(jnp.int32, (ATTN_GROUP * BQ, 2 * BQ), 1) < BQ

    qm = qm_ref[...]
    mk = mk_ref[0]
    mv = mv_ref[0]

    swa = [(j, g) for j in range(nqb) for g in range(ATTN_KV_HEADS)]
    logits, values, sinks = [], [], []
    for j, g in swa:
        qg = jnp.concatenate(
            [qa[j * BQ:(j + 1) * BQ, (ATTN_GROUP * g + i) * N:(ATTN_GROUP * g + i + 1) * N] for i in range(ATTN_GROUP)],
            axis=0)
        lg = _dot_nt(qg, k_group[g][j * BQ:(j + 2) * BQ]) + bias_ref[g]
        if j == 0:
            lg = jnp.where(jnp.logical_and(seq_start, before_seq), -jnp.inf, lg)
        logits.append(lg)
        values.append(v_group[g][j * BQ:(j + 2) * BQ])
        sinks.append(sink_ref[g])
    for h in range(MEM_HEADS):
        hs = slice(h * N, (h + 1) * N)
        logits.append(_dot_nt(qm[:, hs], mk[:, hs]))
        values.append(mv[:, hs])
        sinks.append(None)
    bd = bd_ref[...]
    y = y_ref[...]
    d = y - _head_sum(y, bd) * (1.0 / N)
    var = _head_sum(d * d, bd) * (1.0 / N)
    yn = d * lax.rsqrt(var + GN_EPS) * lnw_ref[...] + lnb_ref[...]
    f32 = lambda ref: ref[...].astype(F32)
    bonus = _head_sum(f32(r_ref) * f32(k_ref) * rk_ref[...], bd) * f32(v_ref)
    out_b = (yn + bonus) * f32(gate_ref)
    mixed = jnp.concatenate([att_ref[:, :ATTN_W], _bf(out_b), att_ref[:, ATTN_W:]], axis=-1)
    out_ref[...] = x_ref[...] + _dot(mixed, wout_ref[...])

    row_max = [jnp.max(lg, axis=-1, keepdims=True) for lg in logits]
    m = [rm if s is None else jnp.maximum(rm, s) for rm, s in zip(row_max, sinks)]
    e = [jnp.exp(lg - mm) for lg, mm in zip(logits, m)]
    denom = [jnp.sum(ee, axis=-1, keepdims=True) for ee in e]
    denom = [d if s is None else d + jnp.exp(s - mm) for d, s, mm in zip(denom, sinks, m)]
    outs = [_bf(_dot(_bf(ee), vv) / d) for ee, vv, d in zip(e, values, denom)]

    head_rows = [[None] * nqb for _ in range(ATTN_HEADS)]
    for (j, g), o in zip(swa, outs):
        for i in range(ATTN_GROUP):
            head_rows[ATTN_GROUP * g + i][j] = o[i * BQ:(i + 1) * BQ]
    att_ref[:, :ATTN_W] = jnp.concatenate([jnp.concatenate(rows, axis=0) for rows in head_rows], axis=-1)
    att_ref[:, ATTN_W:] = jnp.concatenate(outs[len(swa):], axis=-1)


def _mix_call(x, qa, ka, va, qm, mk, mv, y, r, k, v, gate, bias, p, l, *, nb):
    T, D = x.shape
    tq = MIX_TQ
    S = T // nb
    assert S % tq == 0 and tq % BLOCK_Q == 0
    tpb = S // tq
    qpb = tq // BLOCK_Q
    mem_tokens = mk.shape[0] // nb
    sink = jnp.repeat(p["attn_sinks"][l].astype(F32), BLOCK_Q).reshape(ATTN_KV_HEADS, ATTN_GROUP * BLOCK_Q, 1)
    n_tiles = T // tq
    att = lambda i: jnp.minimum(i, n_tiles - 1)
    fin = lambda i: jnp.maximum(i - 1, 0)
    att_tile = lambda w: pl.BlockSpec((tq, w), lambda i: (att(i), 0))
    tile = lambda w: pl.BlockSpec((tq, w), lambda i: (fin(i), 0))
    prev = pl.BlockSpec((BLOCK_Q, KV_W), lambda i: (jnp.maximum(att(i) * qpb - 1, 0), 0))
    full = lambda a: pl.BlockSpec(a.shape, lambda i: (0,) * a.ndim)
    memspec = pl.BlockSpec((1, mem_tokens, MEM_W), lambda i: (att(i) // tpb, 0, 0))
    bd = _block_diag_ones(RWKV_W)
    wout = p["w_out"]
    mk3 = mk.reshape(nb, mem_tokens, MEM_W)
    mv3 = mv.reshape(nb, mem_tokens, MEM_W)
    return pl.pallas_call(
        functools.partial(_mix_kernel, tq=tq, tiles_per_batch=tpb),
        grid=(n_tiles + 1,),
        in_specs=[tile(D), att_tile(ATTN_W), att_tile(KV_W), prev, att_tile(KV_W), prev, full(bias), full(sink),
                  att_tile(MEM_W), memspec, memspec] + [tile(RWKV_W)] * 5
                 + [_layer_spec(p["vectors"], l), full(bd), _layer_spec(wout, l)],
        out_specs=tile(D),
        out_shape=jax.ShapeDtypeStruct((T, D), F32),
        scratch_shapes=[pltpu.VMEM((tq, ATTN_W + MEM_W), BF16)],
        compiler_params=pltpu.CompilerParams(dimension_semantics=("arbitrary",), vmem_limit_bytes=VMEM_LIMIT_BYTES),
        name=f"mix_l{l}",
    )(x, qa, ka, ka, va, va, bias, sink, qm, mk3, mv3, y, r, k, v, gate, p["vectors"], bd, wout)


SUBLANES = 8


def _tile_copies(hbm, buf, sem, tile, slot, to_vmem):
    copies = []
    for s in range(SUBLANES):
        src, dst = hbm.at[tile, s], buf.at[slot, :, s, :]
        if not to_vmem:
            src, dst = dst, src
        copies.append(pltpu.make_async_copy(src, dst, sem.at[slot, s]))
    return copies


def _shift_rows(u, before, steps):
    first = lax.broadcasted_iota(jnp.int32, (SUBLANES, u.shape[1]), 0) == 0
    tm = u.shape[0]
    heads = []
    for i in range(steps):
        lo = tm - (steps - i) * SUBLANES
        wrapped = pltpu.roll(u[lo:lo + SUBLANES], 1, axis=0)
        prior = jnp.broadcast_to(before[SUBLANES - steps + i:SUBLANES - steps + i + 1], wrapped.shape)
        heads.append(jnp.where(first, prior, wrapped))
    return jnp.concatenate(heads + [u[:tm - steps * SUBLANES]], axis=0)


def _stage_weights(w_hbm, layer, dst_ref, stage_ref, sem):
    rows = stage_ref.shape[1]
    n = dst_ref.shape[0] // rows
    copy = lambda c: pltpu.make_async_copy(w_hbm.at[layer, pl.ds(c * rows, rows), :], stage_ref.at[c % 2], sem.at[c % 2])
    copy(0).start()
    for c in range(n):
        if c + 1 < n:
            copy(c + 1).start()
        copy(c).wait()
        dst_ref[c * rows:(c + 1) * rows, :] = _bf(stage_ref[c % 2])


def _ffn_kernel(x_hbm, vec_ref, wup_hbm, cw_ref, cb_ref, wdn_hbm, o_hbm, xbuf, obuf, carry_ref, act_ref,
                wup_ref, wdn_ref, up_stage, dn_stage, in_sem, out_sem, w_sem, *, tm, tiles_per_batch, n_tiles, layer):
    D = xbuf.shape[-1]
    g_ref = _Row(vec_ref, VEC_ROWS["ffn_norm_g"], D)
    q = tm // SUBLANES
    i = pl.program_id(0)
    slot = i % 2

    @pl.when(i == 0)
    def _():
        carry_ref[...] = jnp.zeros_like(carry_ref)
        for cp in _tile_copies(x_hbm, xbuf, in_sem, 0, 0, True):
            cp.start()
        _stage_weights(wup_hbm, layer, wup_ref, up_stage, w_sem)
        _stage_weights(wdn_hbm, layer, wdn_ref, dn_stage, w_sem)

    @pl.when(i + 1 < n_tiles)
    def _():
        for cp in _tile_copies(x_hbm, xbuf, in_sem, i + 1, 1 - slot, True):
            cp.start()

    for cp in _tile_copies(x_hbm, xbuf, in_sem, i, slot, True):
        cp.wait()

    @pl.when(i >= 2)
    def _():
        for cp in _tile_copies(o_hbm, obuf, out_sem, i - 2, slot, False):
            cp.wait()

    seq_start = i % tiles_per_batch == 0
    g = g_ref[...]
    x = xbuf[slot].reshape(tm, D)
    before = jnp.where(seq_start, 0.0, _rms_rows(carry_ref[...], g))
    for r, grp in ((SUBLANES - 2, q - 2), (SUBLANES - 1, q - 1)):
        row = grp * SUBLANES + SUBLANES - 1
        carry_ref[r:r + 1, :] = x[row:row + 1]
    h_ext = _bf(jnp.concatenate([before, _rms_rows(x, g)], axis=0))

    n_chunks = D_FF // FFN_FC
    gate_cols = lambda c: slice(c * FFN_FC, (c + 1) * FFN_FC)
    val_cols = lambda c: slice(D_FF + c * FFN_FC, D_FF + (c + 1) * FFN_FC)

    def conv(u_ext, cols):
        w = cw_ref[:, cols]
        u = u_ext[SUBLANES:]
        return (cb_ref[:, cols] + w[0:1] * _shift_rows(u, u_ext[:SUBLANES], 2)
                + w[1:2] * _shift_rows(u, u_ext[:SUBLANES], 1) + w[2:3] * u)

    up = lambda c: (_dot(h_ext, wup_ref[:, gate_cols(c)]), _dot(h_ext, wup_ref[:, val_cols(c)]))
    nxt = up(0)
    for c in range(n_chunks):
        ug, uv = nxt
        if c + 1 < n_chunks:
            nxt = up(c + 1)
        gt = conv(ug, gate_cols(c))
        act_ref[:, gate_cols(c)] = _bf(gt * jax.nn.sigmoid(gt) * conv(uv, val_cols(c)))
    obuf[slot] = (x + _dot(act_ref[...], wdn_ref[...])).reshape(q, SUBLANES, D)

    for cp in _tile_copies(o_hbm, obuf, out_sem, i, slot, False):
        cp.start()

    @pl.when(i == n_tiles - 1)
    def _():
        if n_tiles >= 2:
            for cp in _tile_copies(o_hbm, obuf, out_sem, i - 1, 1 - slot, False):
                cp.wait()
        for cp in _tile_copies(o_hbm, obuf, out_sem, i, slot, False):
            cp.wait()


def _ffn_call(x, p, l, *, nb):
    T, D = x.shape
    tm = FFN_TM
    S = T // nb
    q = tm // SUBLANES
    assert S % tm == 0 and q % SUBLANES == 0 and D_FF % FFN_FC == 0
    n_tiles = T // tm
    wup, wdn = p["w_up"], p["w_down"]
    cw, cb = p["conv_w"], p["conv_b"][:, None, :]
    hbm = pl.BlockSpec(memory_space=pl.ANY)
    assert D % FFN_STAGE_CHUNKS == 0 and D_FF % FFN_STAGE_CHUNKS == 0
    out = pl.pallas_call(
        functools.partial(_ffn_kernel, tm=tm, tiles_per_batch=S // tm, n_tiles=n_tiles, layer=l),
        grid=(n_tiles,),
        in_specs=[hbm, _layer_spec(p["vectors"], l), hbm, _layer_spec(cw, l), _layer_spec(cb, l), hbm],
        out_specs=hbm,
        out_shape=jax.ShapeDtypeStruct((n_tiles, SUBLANES, q, D), F32),
        scratch_shapes=[pltpu.VMEM((2, q, SUBLANES, D), F32), pltpu.VMEM((2, q, SUBLANES, D), F32),
                        pltpu.VMEM((SUBLANES, D), F32), pltpu.VMEM((tm, D_FF), BF16),
                        pltpu.VMEM(wup.shape[1:], BF16), pltpu.VMEM(wdn.shape[1:], BF16),
                        pltpu.VMEM((2, D // FFN_STAGE_CHUNKS, 2 * D_FF), F32),
                        pltpu.VMEM((2, D_FF // FFN_STAGE_CHUNKS, D), F32),
                        pltpu.SemaphoreType.DMA((2, SUBLANES)), pltpu.SemaphoreType.DMA((2, SUBLANES)),
                        pltpu.SemaphoreType.DMA((2,))],
        compiler_params=pltpu.CompilerParams(dimension_semantics=("arbitrary",), vmem_limit_bytes=VMEM_LIMIT_BYTES),
        name=f"ffn_l{l}",
    )(x.reshape(n_tiles, SUBLANES, q, D), p["vectors"], wup, cw, cb, wdn)
    return out.reshape(T, D)


_PARAM_NAMES = (
    "rel_bias", "mix_norm_g", "w_in", "w_in_vres", "attn_q_norm", "attn_k_norm", "attn_sinks", "rwkv_mu",
    "rwkv_mu_vres", "rwkv_w0", "rwkv_w2", "rwkv_a0", "rwkv_a2", "rwkv_v0", "rwkv_v2", "rwkv_g2", "rwkv_k_k",
    "rwkv_k_a", "rwkv_r_k", "rwkv_ln_w", "rwkv_ln_b", "mem_norm_g", "w_mem_kv", "mem_q_norm", "mem_k_norm",
    "w_out", "ffn_norm_g", "w_up", "conv_w", "conv_b", "w_down")


def kernel(x, mem, rel_bias, mix_norm_g, w_in, w_in_vres, attn_q_norm, attn_k_norm, attn_sinks, rwkv_mu,
           rwkv_mu_vres, rwkv_w0, rwkv_w2, rwkv_a0, rwkv_a2, rwkv_v0, rwkv_v2, rwkv_g2, rwkv_k_k, rwkv_k_a,
           rwkv_r_k, rwkv_ln_w, rwkv_ln_b, mem_norm_g, w_mem_kv, mem_q_norm, mem_k_norm, w_out, ffn_norm_g,
           w_up, conv_w, conv_b, w_down):
    p = dict(zip(_PARAM_NAMES, (
        rel_bias, mix_norm_g, w_in, w_in_vres, attn_q_norm, attn_k_norm, attn_sinks, rwkv_mu, rwkv_mu_vres,
        rwkv_w0, rwkv_w2, rwkv_a0, rwkv_a2, rwkv_v0, rwkv_v2, rwkv_g2, rwkv_k_k, rwkv_k_a, rwkv_r_k, rwkv_ln_w,
        rwkv_ln_b, mem_norm_g, w_mem_kv, mem_q_norm, mem_k_norm, w_out, ffn_norm_g, w_up, conv_w, conv_b, w_down)))
    p["vectors"], p["lora"] = _pack_params(p)
    for name in ("w_in", "w_mem_kv", "w_out"):
        p[name] = _bf(p[name])
    nb, S, D = x.shape
    xt = x.reshape(nb * S, D)
    mem2d = mem.reshape(nb * mem.shape[1], D)
    bias = _bias_call(rel_bias)
    v_first = None
    for l in range(w_in.shape[0]):
        qa, ka, va, qm, r, lw, k, v, a, b, gate = _proj_call(xt, p, l, v_first, nb=nb)
        if l == 0:
            v_first = v
        y = _rwkv_scan(r, lw, k, v, a, b, nb=nb)
        mk, mv = _memkv_call(mem2d, p, l)
        xt = _mix_call(xt, qa, ka, va, qm, mk, mv, y, r, k, v, gate, bias, p, l, nb=nb)
        xt = _ffn_call(xt, p, l, nb=nb)
    return xt.reshape(nb, S, D)
```

```python
import functools
import math

import jax
import jax.numpy as jnp
import numpy as np
from jax import lax
from jax.experimental import pallas as pl
from jax.experimental.pallas import tpu as pltpu

F32 = jnp.float32
BF16 = jnp.bfloat16

HEAD_DIM = 64
SCAN_CHUNK = 64
VMEM_LIMIT_BYTES = 56 * 1024 * 1024


def _dot(a, b):
    return jnp.dot(a, b, preferred_element_type=F32)


def _dot_nt(a, b):
    return lax.dot_general(a, b, (((1,), (1,)), ((), ())), preferred_element_type=F32)


def _dot_tn(a, b):
    return lax.dot_general(a, b, (((0,), (0,)), ((), ())), preferred_element_type=F32)


def _bf(x):
    return x.astype(BF16)


SCAN_GROUP_HEADS = 4
SCAN_GROUP_W = SCAN_GROUP_HEADS * HEAD_DIM
SCAN_CHUNKS_PER_STEP = 4


_SCAN_STASH = (("w2", 1, F32), ("y_v", 1, F32), ("m_rb", 1, F32), ("v", 1, F32), ("w1r", 2, F32),
               ("bk", 2, F32), ("g_row", 1, F32))


def _scan_kernel(r_ref, lw_ref, k_ref, v_ref, a_ref, b_ref, bdm_ref, y_ref, h_ref, *stash_refs, nb, nch):
    C, N, GW, HPG = SCAN_CHUNK, HEAD_DIM, SCAN_GROUP_W, SCAN_GROUP_HEADS
    ngroups = h_ref.shape[0]
    stash = {name: ref for (name, _, _), ref in zip(_SCAN_STASH, stash_refs)}

    @pl.when(pl.program_id(0) == 0)
    def _():
        h_ref[...] = jnp.zeros_like(h_ref)
        for ref in stash_refs:
            ref[...] = jnp.zeros_like(ref)

    trow = lax.broadcasted_iota(jnp.int32, (C, C), 0)
    tcol = lax.broadcasted_iota(jnp.int32, (C, C), 1)
    tri = (trow >= tcol).astype(BF16)
    grow = lax.broadcasted_iota(jnp.int32, (C, GW), 0)
    gcol = lax.broadcasted_iota(jnp.int32, (C, GW), 1) % N
    incl = grow >= gcol
    strict = grow > gcol
    eye = (grow == gcol).astype(F32)
    bdm = bdm_ref[...]

    def bd(x):
        return jnp.concatenate([_bf(x)] * HPG, axis=0) * bdm

    def split2(x):
        hi = _bf(x)
        return hi, _bf(x - hi.astype(F32))

    def chunk_units(c):
        rows = slice(c * C, (c + 1) * C)
        per_batch = []
        for b in range(nb):
            lw = lw_ref[b, rows, :]
            l1 = _bf(lw)
            e1 = lw - l1.astype(F32)
            l2 = _bf(e1)
            l3 = _bf(e1 - l2.astype(F32))
            cum = _dot(tri, l1) + _dot(tri, l2) + _dot(tri, l3)
            cum_last = cum[C - 1:C, :]
            g_inv = jnp.exp(-cum)
            g_out = jnp.exp(cum_last - cum)
            f32 = lambda ref: ref[b, rows, :].astype(F32)
            kk, bb = f32(k_ref), f32(b_ref)
            per_batch.append(dict(
                a_t=f32(a_ref) * jnp.exp(cum - lw), r_t=f32(r_ref) * jnp.exp(cum),
                b_t=bb * g_inv, k_t=kk * g_inv, b_h=bb * g_out, k_h=kk * g_out,
                g_c=jnp.broadcast_to(jnp.exp(cum_last), cum.shape), v=f32(v_ref)))
        cat = {n: jnp.concatenate([pb[n] for pb in per_batch], axis=1) for n in per_batch[0]}
        return [{n: t[:, g * GW:(g + 1) * GW] for n, t in cat.items()} for g in range(ngroups)]

    def prepare(chunks):
        units = [u for c in chunks for u in chunk_units(c)]

        nu = len(units)
        ar = [jnp.concatenate([_bf(u["a_t"]), _bf(u["r_t"])], axis=0) for u in units]
        pb = [_dot_nt(ar[i], bd(units[i]["b_t"])) for i in range(nu)]
        pk = [_dot_nt(ar[i], bd(units[i]["k_t"])) for i in range(nu)]
        l_ab = [jnp.where(strict, p[:C], 0.0) for p in pb]
        m_rb = [_bf(jnp.where(incl, p[C:], 0.0)) for p in pb]
        tril2 = jnp.concatenate([strict, incl], axis=0)
        lm_k = [_bf(jnp.where(tril2, p, 0.0)) for p in pk]
        yield
        t_inv = [eye + l for l in l_ab]
        pw = [_dot(_bf(l), bd(l)) for l in l_ab]
        yield
        n_rounds = int(math.log2(C)) - 1
        for rnd in range(n_rounds):
            if rnd + 1 < n_rounds:
                z = [_dot(jnp.concatenate([_bf(p), _bf(t)], axis=0), bd(p)) for p, t in zip(pw, t_inv)]
                pw = [zz[:C] for zz in z]
                t_inv = [t + zz[C:] for t, zz in zip(t_inv, z)]
            else:
                t_inv = [t + _dot(_bf(t), bd(p)) for t, p in zip(t_inv, pw)]
            yield
        t_b = [_bf(t) for t in t_inv]
        w1 = [_dot(t, bd(u["a_t"])) for t, u in zip(t_b, units)]
        lmv = [_dot(m, bd(u["v"])) for m, u in zip(lm_k, units)]
        yield
        w2 = [_dot(t, bd(x[:C])) for t, x in zip(t_b, lmv)]
        parts = [p for u in units for p in split2(u["g_c"] * eye)]
        sums = _dot(jnp.concatenate(parts, axis=0), bdm)
        yield
        for i, u in enumerate(units):
            put = lambda name, val: stash[name].__setitem__(i, val.astype(stash[name].dtype))
            put("w2", w2[i])
            put("y_v", lmv[i][C:])
            put("m_rb", m_rb[i])
            put("v", u["v"])
            put("w1r", jnp.concatenate([w1[i], u["r_t"]], axis=0))
            put("bk", jnp.concatenate([u["b_h"], u["k_h"]], axis=0))
            put("g_row", sums[2 * i * C:(2 * i + 1) * C] + sums[(2 * i + 1) * C:(2 * i + 2) * C])
        yield

    state = [h_ref[g] for g in range(ngroups)]

    def advance(c):
        p = {n: [ref[c * ngroups + g] for g in range(ngroups)] for n, ref in stash.items()}
        for n in ("m_rb", "v", "w1r", "bk"):
            p[n] = [_bf(x) for x in p[n]]
        z = [_dot(w, bd(s)) for w, s in zip(p["w1r"], state)]
        yield
        u = [zz[:C] + w for zz, w in zip(z, p["w2"])]
        y = [zz[C:] + _dot(m, bd(x)) + yv for zz, m, x, yv in zip(z, p["m_rb"], u, p["y_v"])]
        uv = [jnp.concatenate([_bf(x), v], axis=0) for x, v in zip(u, p["v"])]
        full = [_dot_tn(b, x) * bdm.astype(F32) for b, x in zip(p["bk"], uv)]
        yield
        upd = [sum(f[h * N:(h + 1) * N] for h in range(HPG)) for f in full]
        state[:] = [g * s + d for g, s, d in zip(p["g_row"], state, upd)]
        ycat = jnp.concatenate(y, axis=1)
        w = y_ref.shape[2]
        for b in range(nb):
            y_ref[b, c * C:(c + 1) * C, :] = ycat[:, b * w:(b + 1) * w]
        yield

    def run(gen, stages):
        for _ in range(stages):
            next(gen, None)

    prep = prepare(list(range(nch)))
    n_prep = 5 + int(math.log2(C)) - 1
    for c in range(nch):
        seq = advance(c)
        run(prep, 1)
        run(seq, 1)
        run(prep, 1)
        run(seq, 2)
    run(prep, n_prep)
    for g in range(ngroups):
        h_ref[g] = state[g]


def _rwkv_scan(r, lw, k, v, a, b, *, nb):
    T, W = r.shape
    S = T // nb
    C, nch, GW = SCAN_CHUNK, SCAN_CHUNKS_PER_STEP, SCAN_GROUP_W
    assert S % (C * nch) == 0 and (nb * W) % GW == 0 and W % 128 == 0
    ngroups = nb * W // GW
    head = np.arange(GW) // HEAD_DIM
    bdm = jnp.asarray((head[:, None] == head[None, :]).astype(np.float32), dtype=BF16)
    n_blocks = S // (C * nch)
    in_spec = pl.BlockSpec((nb, C * nch, W), lambda s: (0, jnp.minimum(s, n_blocks - 1), 0))
    out_spec = pl.BlockSpec((nb, C * nch, W), lambda s: (0, jnp.maximum(s - 1, 0), 0))
    args = [t.reshape(nb, S, W) for t in (r, lw, k, v, a, b)]
    y = pl.pallas_call(
        functools.partial(_scan_kernel, nb=nb, nch=nch),
        grid=(n_blocks + 1,),
        in_specs=[in_spec] * 6 + [pl.BlockSpec((GW, GW), lambda s: (0, 0))],
        out_specs=out_spec,
        out_shape=jax.ShapeDtypeStruct((nb, S, W), F32),
        scratch_shapes=[pltpu.VMEM((ngroups, HEAD_DIM, GW), F32)]
        + [pltpu.VMEM((nch * ngroups, rows * C, GW), dt) for _, rows, dt in _SCAN_STASH],
        compiler_params=pltpu.CompilerParams(dimension_semantics=("arbitrary",), vmem_limit_bytes=VMEM_LIMIT_BYTES),
        name="rwkv_scan",
    )(*args, bdm)
    return y.reshape(T, W)


D_MODEL = 1024
ATTN_HEADS = 6
ATTN_KV_HEADS = 2
ATTN_GROUP = ATTN_HEADS // ATTN_KV_HEADS
ATTN_W = ATTN_HEADS * HEAD_DIM
KV_W = ATTN_KV_HEADS * HEAD_DIM
BLOCK_Q = 128
WINDOW = 128
N_BUCKETS = 32
MAX_EXACT = N_BUCKETS // 2
RWKV_W = 6 * HEAD_DIM
LORA_W = 128
SHIFT_W = 3 * RWKV_W + LORA_W
MEM_HEADS = 4
MEM_W = MEM_HEADS * HEAD_DIM
IN_BASE = ATTN_W + 2 * KV_W + SHIFT_W + MEM_W
PB_OFF = ATTN_W + 2 * KV_W
QM_OFF = PB_OFF + SHIFT_W
D_FF = 2816
EPS = 1e-6
GN_EPS = 64e-5
L2_EPS = 1e-12
MXU_WIDTH = 256

PROJ_TM = 1024
PROJ_SUB = 256
MIX_TQ = 512
FFN_TM = 1024
FFN_FC = 256
FFN_STAGE_CHUNKS = 8
WEIGHT_STAGE_CHUNKS = 4


def _block_diag_ones(width):
    idx = np.arange(width) // HEAD_DIM
    return jnp.asarray((idx[:, None] == idx[None, :]).astype(np.float32), dtype=BF16)


def _head_sum(t, bd):
    tb = _bf(t)
    width = t.shape[1]
    parts = [_dot(tb[:, lo:min(lo + MXU_WIDTH, width)], bd[lo:min(lo + MXU_WIDTH, width), lo:min(lo + MXU_WIDTH, width)])
             for lo in range(0, width, MXU_WIDTH)]
    return parts[0] if len(parts) == 1 else jnp.concatenate(parts, axis=1)


def _head_rms(t, bd, gain):
    ms = _head_sum(t * t, bd) * (1.0 / HEAD_DIM)
    return t * lax.rsqrt(ms + EPS) * gain


def _rms_rows(x, g):
    ms = jnp.mean(x * x, axis=-1, keepdims=True)
    return x * lax.rsqrt(ms + EPS) * g


class _Row:
    def __init__(self, table_ref, row, width):
        self.table_ref, self.row, self.width = table_ref, row, width

    def __getitem__(self, _):
        return self.table_ref[self.row:self.row + 1, 0:self.width]


def _proj_kernel(*refs, tm, sub, tiles_per_batch, has_vres, layer):
    it = iter(refs)
    x_ref, vec_ref, w_hbm, lora_ref = next(it), next(it), next(it), next(it)
    bd384_ref, bd128_ref, bd256_ref = next(it), next(it), next(it)
    if has_vres:
        wv_ref, vfirst_ref = next(it), next(it)
    qa_ref, ka_out_ref, va_ref, qm_ref = next(it), next(it), next(it), next(it)
    r_ref, lw_ref, k_ref, v_ref, a_ref, b_ref, gate_ref = (next(it) for _ in range(7))
    pbs_ref, w_ref, w_stage, w_sem = next(it), next(it), next(it), next(it)
    sw = pbs_ref.shape[1]

    @pl.when(pl.program_id(0) == 0)
    def _():
        _stage_weights(w_hbm, layer, w_ref, w_stage, w_sem)
        if has_vres:
            w_ref[:, IN_BASE:IN_BASE + LORA_W] = _bf(wv_ref[...])

    g_ref, qg_ref, kg_ref, mg_ref = (_Row(vec_ref, VEC_ROWS[n], w) for n, w in (
        ("mix_norm_g", D_MODEL), ("q_gain", ATTN_W), ("k_gain", KV_W), ("mq_gain", MEM_W)))
    mu_ref = _Row(vec_ref, VEC_ROWS["mu"], sw)
    w0_ref, a0_ref, kk_ref, ka_ref, v0_ref = (_Row(vec_ref, VEC_ROWS[n], RWKV_W) for n in ("w0", "a0", "k_k", "k_a", "v0"))
    w2_ref, a2_ref, g2_ref, v2_ref = (lora_ref.at[j] for j in range(4))

    @pl.when(pl.program_id(0) % tiles_per_batch == 0)
    def _():
        pbs_ref[0:8, :] = jnp.zeros((8, sw), F32)

    def project(i):
        rows = slice(i * sub, (i + 1) * sub)
        return _dot(_bf(_rms_rows(x_ref[rows, :], g_ref[...])), w_ref[...])

    def attention_outputs(i, proj):
        rows = slice(i * sub, (i + 1) * sub)
        qa_ref[rows, :] = _bf(_head_rms(proj[:, :ATTN_W], bd384_ref[...], qg_ref[...]))
        ka_out_ref[rows, :] = _bf(_head_rms(proj[:, ATTN_W:ATTN_W + KV_W], bd128_ref[...], kg_ref[...]))
        va_ref[rows, :] = _bf(proj[:, ATTN_W + KV_W:PB_OFF])
        qm_ref[rows, :] = _bf(_head_rms(proj[:, QM_OFF:QM_OFF + MEM_W], bd256_ref[...], mg_ref[...]))

    def rwkv_outputs(i, proj):
        rows = slice(i * sub, (i + 1) * sub)
        lo = 8 + i * sub
        pbs_ref[lo:lo + sub, 0:SHIFT_W] = proj[:, PB_OFF:PB_OFF + SHIFT_W]
        if has_vres:
            pbs_ref[lo:lo + sub, SHIFT_W:sw] = proj[:, IN_BASE:IN_BASE + LORA_W]
        cur = pbs_ref[lo:lo + sub, :]
        prev = pbs_ref[lo - 1:lo - 1 + sub, :]
        sh = cur + mu_ref[...] * (prev - cur)
        r = sh[:, 0:RWKV_W]
        k = sh[:, RWKV_W:2 * RWKV_W]
        v = sh[:, 2 * RWKV_W:3 * RWKV_W]
        z = sh[:, 3 * RWKV_W:SHIFT_W]
        t = w0_ref[...] + _dot(_bf(jnp.tanh(z)), w2_ref[...])
        lw_ref[rows, :] = -math.exp(-0.5) * jax.nn.sigmoid(t)
        a = jax.nn.sigmoid(a0_ref[...] + _dot(_bf(z), a2_ref[...]))
        gate_ref[rows, :] = _bf(_dot(_bf(jax.nn.sigmoid(z)), g2_ref[...]))
        if has_vres:
            vd = sh[:, SHIFT_W:sw]
            v = v + (vfirst_ref[rows, :].astype(F32) - v) * jax.nn.sigmoid(v0_ref[...] + _dot(_bf(vd), v2_ref[...]))
        kk = k * kk_ref[...]
        kk = kk / jnp.maximum(jnp.sqrt(_head_sum(kk * kk, bd384_ref[...])), L2_EPS)
        r_ref[rows, :] = _bf(r)
        k_ref[rows, :] = _bf(k * (1.0 + (a - 1.0) * ka_ref[...]))
        v_ref[rows, :] = _bf(v)
        a_ref[rows, :] = _bf(-kk)
        b_ref[rows, :] = _bf(kk * a)

    n_sub = tm // sub
    proj = project(0)
    for i in range(n_sub):
        attention_outputs(i, proj)
        nxt = project(i + 1) if i + 1 < n_sub else None
        rwkv_outputs(i, proj)
        proj = nxt
    pbs_ref[0:8, :] = pbs_ref[tm:tm + 8, :]


def _layer_spec(stacked, l):
    return pl.BlockSpec((None,) + stacked.shape[1:], lambda i: (l,) + (0,) * (stacked.ndim - 1),
                        pipeline_mode=pl.Buffered(1))


VEC_W = SHIFT_W + LORA_W
VEC_ROWS = {n: i for i, n in enumerate((
    "mix_norm_g", "q_gain", "k_gain", "mq_gain", "mu", "w0", "a0", "k_k", "k_a", "v0", "ln_w", "ln_b", "r_k",
    "ffn_norm_g"))}
VEC_TABLE_ROWS = 16


def _pack_params(p):
    layers = p["w_in"].shape[0]
    scale = HEAD_DIM ** -0.5
    zeros = jnp.zeros((RWKV_W,), F32)
    vec_pieces, lora_pieces = [], []
    for l in range(layers):
        vecs = dict(
            mix_norm_g=p["mix_norm_g"][l], q_gain=jnp.tile(p["attn_q_norm"][l], ATTN_HEADS) * scale,
            k_gain=jnp.tile(p["attn_k_norm"][l], ATTN_KV_HEADS), mq_gain=jnp.tile(p["mem_q_norm"][l], MEM_HEADS) * scale,
            mu=p["rwkv_mu"][l], w0=p["rwkv_w0"][l], a0=p["rwkv_a0"][l], k_k=p["rwkv_k_k"][l], k_a=p["rwkv_k_a"][l],
            v0=p["rwkv_v0"][l - 1] if l > 0 else zeros, ln_w=p["rwkv_ln_w"][l], ln_b=p["rwkv_ln_b"][l],
            r_k=p["rwkv_r_k"][l].reshape(-1), ffn_norm_g=p["ffn_norm_g"][l])
        for n in VEC_ROWS:
            row = [vecs[n].astype(F32)]
            if n == "mu" and l > 0:
                row.append(p["rwkv_mu_vres"][l - 1].astype(F32))
            row.append(jnp.zeros((VEC_W - sum(t.shape[0] for t in row),), F32))
            vec_pieces += row
        vec_pieces.append(jnp.zeros(((VEC_TABLE_ROWS - len(VEC_ROWS)) * VEC_W,), F32))
        v2 = p["rwkv_v2"][l - 1] if l > 0 else jnp.zeros((16, RWKV_W), F32)
        for w, at in ((p["rwkv_w2"][l], 0), (p["rwkv_a2"][l], 32), (p["rwkv_g2"][l], 64), (v2, 0)):
            lora_pieces += [jnp.zeros((at, RWKV_W), F32), w.astype(F32),
                            jnp.zeros((LORA_W - at - w.shape[0], RWKV_W), F32)]
    tables = jnp.concatenate(vec_pieces).reshape(layers, VEC_TABLE_ROWS, VEC_W)
    loras = jnp.concatenate([t for t in lora_pieces if t.shape[0]], axis=0).reshape(layers, 4, LORA_W, RWKV_W)
    return tables, _bf(loras)


def _proj_call(x, p, l, v_first, *, nb):
    T, D = x.shape
    tm = PROJ_TM
    assert T % tm == 0 and (T // nb) % tm == 0
    has_vres = l > 0
    w = p["w_in"]
    ins = [x, p["vectors"], w, p["lora"], _block_diag_ones(ATTN_W), _block_diag_ones(KV_W), _block_diag_ones(MEM_W)]
    if has_vres:
        ins += [jnp.pad(p["w_in_vres"][l - 1], ((0, 0), (0, LORA_W - 16))), v_first]
    nw = IN_BASE + (LORA_W if has_vres else 0)

    def spec(a):
        if a is w:
            return pl.BlockSpec(memory_space=pl.ANY)
        if a.ndim >= 3:
            return _layer_spec(a, l)
        if a.shape[0] == T:
            return pl.BlockSpec((tm, a.shape[1]), lambda i: (i, 0))
        return pl.BlockSpec(a.shape, lambda i: (0, 0), pipeline_mode=pl.Buffered(1))

    out_shapes = [jax.ShapeDtypeStruct((T, ATTN_W), BF16), jax.ShapeDtypeStruct((T, KV_W), BF16),
                  jax.ShapeDtypeStruct((T, KV_W), BF16), jax.ShapeDtypeStruct((T, MEM_W), BF16)]
    out_shapes += [jax.ShapeDtypeStruct((T, RWKV_W), F32 if j == 1 else BF16) for j in range(7)]
    sw = SHIFT_W + (LORA_W if has_vres else 0)
    return pl.pallas_call(
        functools.partial(_proj_kernel, tm=tm, sub=PROJ_SUB, tiles_per_batch=(T // nb) // tm, has_vres=has_vres,
                          layer=l),
        grid=(T // tm,),
        in_specs=[spec(a) for a in ins],
        out_specs=[pl.BlockSpec((tm, s.shape[1]), lambda i: (i, 0)) for s in out_shapes],
        out_shape=out_shapes,
        scratch_shapes=[pltpu.VMEM((tm + 8, sw), F32), pltpu.VMEM((D, nw), BF16),
                        pltpu.VMEM((2, D // WEIGHT_STAGE_CHUNKS, IN_BASE), F32), pltpu.SemaphoreType.DMA((2,))],
        compiler_params=pltpu.CompilerParams(dimension_semantics=("arbitrary",), vmem_limit_bytes=VMEM_LIMIT_BYTES),
        name=f"proj_l{l}",
    )(*ins)


def _bucket_table():
    qi = np.arange(BLOCK_Q)[:, None]
    kj = np.arange(2 * BLOCK_Q)[None, :]
    dist = qi + BLOCK_Q - kj
    in_band = (dist >= 0) & (dist < WINDOW)
    d = np.maximum(dist, 1).astype(np.float32)
    large = MAX_EXACT + (np.log(d / np.float32(MAX_EXACT)) / np.float32(math.log(WINDOW / MAX_EXACT))
                         * np.float32(N_BUCKETS - MAX_EXACT)).astype(np.int32)
    large = np.minimum(large, N_BUCKETS - 1)
    bucket = np.where(dist < MAX_EXACT, np.maximum(dist, 0), large)
    return np.where(in_band, bucket, -1).astype(np.int32)


def _bias_kernel(rb_ref, bucket_ref, out_ref):
    bucket = bucket_ref[...]
    for h in range(ATTN_HEADS):
        acc = jnp.full(bucket.shape, -jnp.inf, F32)
        for j in range(N_BUCKETS):
            acc = jnp.where(bucket == j, rb_ref[j, h], acc)
        out_ref[h] = acc


def _bias_call(rel_bias):
    tab = pl.pallas_call(
        _bias_kernel,
        in_specs=[pl.BlockSpec(memory_space=pltpu.SMEM), pl.BlockSpec(memory_space=pltpu.VMEM)],
        out_specs=pl.BlockSpec(memory_space=pltpu.VMEM),
        out_shape=jax.ShapeDtypeStruct((ATTN_HEADS, BLOCK_Q, 2 * BLOCK_Q), F32),
        name="rel_bias_table",
    )(rel_bias.astype(F32), jnp.asarray(_bucket_table()))
    return tab.reshape(ATTN_KV_HEADS, ATTN_GROUP * BLOCK_Q, 2 * BLOCK_Q)


def _memkv_kernel(mem_ref, g_ref, w_ref, kg_ref, bd_ref, mk_ref, mv_ref):
    hn = _bf(_rms_rows(mem_ref[...], g_ref[...]))
    kv = _dot(hn, w_ref[...])
    mk_ref[...] = _bf(_head_rms(kv[:, :MEM_W], bd_ref[...], kg_ref[...]))
    mv_ref[...] = _bf(kv[:, MEM_W:])


def _memkv_call(mem2d, p, l):
    rows = mem2d.shape[0]
    vm = pl.BlockSpec(memory_space=pltpu.VMEM)
    return pl.pallas_call(
        _memkv_kernel,
        in_specs=[vm] * 5,
        out_specs=[vm, vm],
        out_shape=[jax.ShapeDtypeStruct((rows, MEM_W), BF16)] * 2,
        compiler_params=pltpu.CompilerParams(vmem_limit_bytes=VMEM_LIMIT_BYTES),
        name=f"mem_kv_l{l}",
    )(mem2d, p["mem_norm_g"][l].reshape(1, -1), _bf(p["w_mem_kv"][l]),
      jnp.tile(p["mem_k_norm"][l], MEM_HEADS).reshape(1, -1), _block_diag_ones(MEM_W))


def _mix_kernel(x_ref, qa_ref, kc_ref, kp_ref, vc_ref, vp_ref, bias_ref, sink_ref, qm_ref, mk_ref, mv_ref,
                y_ref, r_ref, k_ref, v_ref, gate_ref, vec_ref, bd_ref, wout_hbm, out_ref, att_ref,
                wout_ref, w_stage, w_sem, *, tq, tiles_per_batch, layer):
    lnw_ref, lnb_ref, rk_ref = (_Row(vec_ref, VEC_ROWS[n], RWKV_W) for n in ("ln_w", "ln_b", "r_k"))
    N, BQ = HEAD_DIM, BLOCK_Q
    seq_start = pl.program_id(0) % tiles_per_batch == 0
    nqb = tq // BQ

    @pl.when(pl.program_id(0) == 0)
    def _():
        att_ref[...] = jnp.zeros_like(att_ref)
        _stage_weights(wout_hbm, layer, wout_ref, w_stage, w_sem)

    qa = qa_ref[...]
    kall = jnp.concatenate([kp_ref[...], kc_ref[...]], axis=0)
    vall = jnp.concatenate([vp_ref[...], vc_ref[...]], axis=0)
    k_group = [kall[:, g * N:(g + 1) * N] for g in range(ATTN_KV_HEADS)]
    v_group = [vall[:, g * N:(g + 1) * N] for g in range(ATTN_KV_HEADS)]
    before_seq = lax.broadcasted_iota(jnp.int32, (ATTN_GROUP * BQ, 2 * BQ), 1) < BQ

    qm = qm_ref[...]
    mk = mk_ref[0]
    mv = mv_ref[0]

    swa = [(j, g) for j in range(nqb) for g in range(ATTN_KV_HEADS)]
    logits, values, sinks = [], [], []
    for j, g in swa:
        qg = jnp.concatenate(
            [qa[j * BQ:(j + 1) * BQ, (ATTN_GROUP * g + i) * N:(ATTN_GROUP * g + i + 1) * N] for i in range(ATTN_GROUP)],
            axis=0)
        lg = _dot_nt(qg, k_group[g][j * BQ:(j + 2) * BQ]) + bias_ref[g]
        if j == 0:
            lg = jnp.where(jnp.logical_and(seq_start, before_seq), -jnp.inf, lg)
        logits.append(lg)
        values.append(v_group[g][j * BQ:(j + 2) * BQ])
        sinks.append(sink_ref[g])
    for h in range(MEM_HEADS):
        hs = slice(h * N, (h + 1) * N)
        logits.append(_dot_nt(qm[:, hs], mk[:, hs]))
        values.append(mv[:, hs])
        sinks.append(None)
    bd = bd_ref[...]
    y = y_ref[...]
    d = y - _head_sum(y, bd) * (1.0 / N)
    var = _head_sum(d * d, bd) * (1.0 / N)
    yn = d * lax.rsqrt(var + GN_EPS) * lnw_ref[...] + lnb_ref[...]
    f32 = lambda ref: ref[...].astype(F32)
    bonus = _head_sum(f32(r_ref) * f32(k_ref) * rk_ref[...], bd) * f32(v_ref)
    out_b = (yn + bonus) * f32(gate_ref)
    mixed = jnp.concatenate([att_ref[:, :ATTN_W], _bf(out_b), att_ref[:, ATTN_W:]], axis=-1)
    out_ref[...] = x_ref[...] + _dot(mixed, wout_ref[...])

    row_max = [jnp.max(lg, axis=-1, keepdims=True) for lg in logits]
    m = [rm if s is None else jnp.maximum(rm, s) for rm, s in zip(row_max, sinks)]
    e = [jnp.exp(lg - mm) for lg, mm in zip(logits, m)]
    denom = [jnp.sum(ee, axis=-1, keepdims=True) for ee in e]
    denom = [d if s is None else d + jnp.exp(s - mm) for d, s, mm in zip(denom, sinks, m)]
    outs = [_bf(_dot(_bf(ee), vv) / d) for ee, vv, d in zip(e, values, denom)]

    head_rows = [[None] * nqb for _ in range(ATTN_HEADS)]
    for (j, g), o in zip(swa, outs):
        for i in range(ATTN_GROUP):
            head_rows[ATTN_GROUP * g + i][j] = o[i * BQ:(i + 1) * BQ]
    att_ref[:, :ATTN_W] = jnp.concatenate([jnp.concatenate(rows, axis=0) for rows in head_rows], axis=-1)
    att_ref[:, ATTN_W:] = jnp.concatenate(outs[len(swa):], axis=-1)


def _mix_call(x, qa, ka, va, qm, mk, mv, y, r, k, v, gate, bias, p, l, *, nb):
    T, D = x.shape
    tq = MIX_TQ
    S = T // nb
    assert S % tq == 0 and tq % BLOCK_Q == 0
    tpb = S // tq
    qpb = tq // BLOCK_Q
    mem_tokens = mk.shape[0] // nb
    sink = jnp.repeat(p["attn_sinks"][l].astype(F32), BLOCK_Q).reshape(ATTN_KV_HEADS, ATTN_GROUP * BLOCK_Q, 1)
    n_tiles = T // tq
    att = lambda i: jnp.minimum(i, n_tiles - 1)
    fin = lambda i: jnp.maximum(i - 1, 0)
    att_tile = lambda w: pl.BlockSpec((tq, w), lambda i: (att(i), 0))
    tile = lambda w: pl.BlockSpec((tq, w), lambda i: (fin(i), 0))
    prev = pl.BlockSpec((BLOCK_Q, KV_W), lambda i: (jnp.maximum(att(i) * qpb - 1, 0), 0))
    full = lambda a: pl.BlockSpec(a.shape, lambda i: (0,) * a.ndim)
    memspec = pl.BlockSpec((1, mem_tokens, MEM_W), lambda i: (att(i) // tpb, 0, 0))
    bd = _block_diag_ones(RWKV_W)
    wout = p["w_out"]
    mk3 = mk.reshape(nb, mem_tokens, MEM_W)
    mv3 = mv.reshape(nb, mem_tokens, MEM_W)
    return pl.pallas_call(
        functools.partial(_mix_kernel, tq=tq, tiles_per_batch=tpb, layer=l),
        grid=(n_tiles + 1,),
        in_specs=[tile(D), att_tile(ATTN_W), att_tile(KV_W), prev, att_tile(KV_W), prev, full(bias), full(sink),
                  att_tile(MEM_W), memspec, memspec] + [tile(RWKV_W)] * 5
                 + [_layer_spec(p["vectors"], l), full(bd), pl.BlockSpec(memory_space=pl.ANY)],
        out_specs=tile(D),
        out_shape=jax.ShapeDtypeStruct((T, D), F32),
        scratch_shapes=[pltpu.VMEM((tq, ATTN_W + MEM_W), BF16), pltpu.VMEM((D, D), BF16),
                        pltpu.VMEM((2, D // WEIGHT_STAGE_CHUNKS, D), F32), pltpu.SemaphoreType.DMA((2,))],
        compiler_params=pltpu.CompilerParams(dimension_semantics=("arbitrary",), vmem_limit_bytes=VMEM_LIMIT_BYTES),
        name=f"mix_l{l}",
    )(x, qa, ka, ka, va, va, bias, sink, qm, mk3, mv3, y, r, k, v, gate, p["vectors"], bd, wout)


SUBLANES = 8


def _tile_copies(hbm, buf, sem, tile, slot, to_vmem):
    copies = []
    for s in range(SUBLANES):
        src, dst = hbm.at[tile, s], buf.at[slot, :, s, :]
        if not to_vmem:
            src, dst = dst, src
        copies.append(pltpu.make_async_copy(src, dst, sem.at[slot, s]))
    return copies


def _shift_rows(u, before, steps):
    first = lax.broadcasted_iota(jnp.int32, (SUBLANES, u.shape[1]), 0) == 0
    tm = u.shape[0]
    heads = []
    for i in range(steps):
        lo = tm - (steps - i) * SUBLANES
        wrapped = pltpu.roll(u[lo:lo + SUBLANES], 1, axis=0)
        prior = jnp.broadcast_to(before[SUBLANES - steps + i:SUBLANES - steps + i + 1], wrapped.shape)
        heads.append(jnp.where(first, prior, wrapped))
    return jnp.concatenate(heads + [u[:tm - steps * SUBLANES]], axis=0)


def _stage_weights(w_hbm, layer, dst_ref, stage_ref, sem):
    rows = stage_ref.shape[1]
    n = dst_ref.shape[0] // rows
    copy = lambda c: pltpu.make_async_copy(w_hbm.at[layer, pl.ds(c * rows, rows), :], stage_ref.at[c % 2], sem.at[c % 2])
    copy(0).start()
    for c in range(n):
        if c + 1 < n:
            copy(c + 1).start()
        copy(c).wait()
        dst_ref[c * rows:(c + 1) * rows, 0:stage_ref.shape[2]] = _bf(stage_ref[c % 2])


def _ffn_kernel(x_hbm, vec_ref, wup_hbm, cw_ref, cb_ref, wdn_hbm, o_hbm, xbuf, obuf, carry_ref, act_ref,
                wup_ref, wdn_ref, up_stage, dn_stage, in_sem, out_sem, w_sem, *, tm, tiles_per_batch, n_tiles, layer):
    D = xbuf.shape[-1]
    g_ref = _Row(vec_ref, VEC_ROWS["ffn_norm_g"], D)
    q = tm // SUBLANES
    i = pl.program_id(0)
    slot = i % 2

    @pl.when(i == 0)
    def _():
        carry_ref[...] = jnp.zeros_like(carry_ref)
        for cp in _tile_copies(x_hbm, xbuf, in_sem, 0, 0, True):
            cp.start()
        _stage_weights(wup_hbm, layer, wup_ref, up_stage, w_sem)
        _stage_weights(wdn_hbm, layer, wdn_ref, dn_stage, w_sem)

    @pl.when(i + 1 < n_tiles)
    def _():
        for cp in _tile_copies(x_hbm, xbuf, in_sem, i + 1, 1 - slot, True):
            cp.start()

    for cp in _tile_copies(x_hbm, xbuf, in_sem, i, slot, True):
        cp.wait()

    @pl.when(i >= 2)
    def _():
        for cp in _tile_copies(o_hbm, obuf, out_sem, i - 2, slot, False):
            cp.wait()

    seq_start = i % tiles_per_batch == 0
    g = g_ref[...]
    x = xbuf[slot].reshape(tm, D)
    before = jnp.where(seq_start, 0.0, _rms_rows(carry_ref[...], g))
    for r, grp in ((SUBLANES - 2, q - 2), (SUBLANES - 1, q - 1)):
        row = grp * SUBLANES + SUBLANES - 1
        carry_ref[r:r + 1, :] = x[row:row + 1]
    h_ext = _bf(jnp.concatenate([before, _rms_rows(x, g)], axis=0))

    n_chunks = D_FF // FFN_FC
    gate_cols = lambda c: slice(c * FFN_FC, (c + 1) * FFN_FC)
    val_cols = lambda c: slice(D_FF + c * FFN_FC, D_FF + (c + 1) * FFN_FC)

    def conv(u_ext, cols):
        w = cw_ref[:, cols]
        u = u_ext[SUBLANES:]
        return (cb_ref[:, cols] + w[0:1] * _shift_rows(u, u_ext[:SUBLANES], 2)
                + w[1:2] * _shift_rows(u, u_ext[:SUBLANES], 1) + w[2:3] * u)

    up = lambda c: (_dot(h_ext, wup_ref[:, gate_cols(c)]), _dot(h_ext, wup_ref[:, val_cols(c)]))
    nxt = up(0)
    for c in range(n_chunks):
        ug, uv = nxt
        if c + 1 < n_chunks:
            nxt = up(c + 1)
        gt = conv(ug, gate_cols(c))
        act_ref[:, gate_cols(c)] = _bf(gt * jax.nn.sigmoid(gt) * conv(uv, val_cols(c)))
    obuf[slot] = (x + _dot(act_ref[...], wdn_ref[...])).reshape(q, SUBLANES, D)

    for cp in _tile_copies(o_hbm, obuf, out_sem, i, slot, False):
        cp.start()

    @pl.when(i == n_tiles - 1)
    def _():
        if n_tiles >= 2:
            for cp in _tile_copies(o_hbm, obuf, out_sem, i - 1, 1 - slot, False):
                cp.wait()
        for cp in _tile_copies(o_hbm, obuf, out_sem, i, slot, False):
            cp.wait()


def _ffn_call(x, p, l, *, nb):
    T, D = x.shape
    tm = FFN_TM
    S = T // nb
    q = tm // SUBLANES
    assert S % tm == 0 and q % SUBLANES == 0 and D_FF % FFN_FC == 0
    n_tiles = T // tm
    wup, wdn = p["w_up"], p["w_down"]
    cw, cb = p["conv_w"], p["conv_b"][:, None, :]
    hbm = pl.BlockSpec(memory_space=pl.ANY)
    assert D % FFN_STAGE_CHUNKS == 0 and D_FF % FFN_STAGE_CHUNKS == 0
    out = pl.pallas_call(
        functools.partial(_ffn_kernel, tm=tm, tiles_per_batch=S // tm, n_tiles=n_tiles, layer=l),
        grid=(n_tiles,),
        in_specs=[hbm, _layer_spec(p["vectors"], l), hbm, _layer_spec(cw, l), _layer_spec(cb, l), hbm],
        out_specs=hbm,
        out_shape=jax.ShapeDtypeStruct((n_tiles, SUBLANES, q, D), F32),
        scratch_shapes=[pltpu.VMEM((2, q, SUBLANES, D), F32), pltpu.VMEM((2, q, SUBLANES, D), F32),
                        pltpu.VMEM((SUBLANES, D), F32), pltpu.VMEM((tm, D_FF), BF16),
                        pltpu.VMEM(wup.shape[1:], BF16), pltpu.VMEM(wdn.shape[1:], BF16),
                        pltpu.VMEM((2, D // FFN_STAGE_CHUNKS, 2 * D_FF), F32),
                        pltpu.VMEM((2, D_FF // FFN_STAGE_CHUNKS, D), F32),
                        pltpu.SemaphoreType.DMA((2, SUBLANES)), pltpu.SemaphoreType.DMA((2, SUBLANES)),
                        pltpu.SemaphoreType.DMA((2,))],
        compiler_params=pltpu.CompilerParams(dimension_semantics=("arbitrary",), vmem_limit_bytes=VMEM_LIMIT_BYTES),
        name=f"ffn_l{l}",
    )(x.reshape(n_tiles, SUBLANES, q, D), p["vectors"], wup, cw, cb, wdn)
    return out.reshape(T, D)


_PARAM_NAMES = (
    "rel_bias", "mix_norm_g", "w_in", "w_in_vres", "attn_q_norm", "attn_k_norm", "attn_sinks", "rwkv_mu",
    "rwkv_mu_vres", "rwkv_w0", "rwkv_w2", "rwkv_a0", "rwkv_a2", "rwkv_v0", "rwkv_v2", "rwkv_g2", "rwkv_k_k",
    "rwkv_k_a", "rwkv_r_k", "rwkv_ln_w", "rwkv_ln_b", "mem_norm_g", "w_mem_kv", "mem_q_norm", "mem_k_norm",
    "w_out", "ffn_norm_g", "w_up", "conv_w", "conv_b", "w_down")


def kernel(x, mem, rel_bias, mix_norm_g, w_in, w_in_vres, attn_q_norm, attn_k_norm, attn_sinks, rwkv_mu,
           rwkv_mu_vres, rwkv_w0, rwkv_w2, rwkv_a0, rwkv_a2, rwkv_v0, rwkv_v2, rwkv_g2, rwkv_k_k, rwkv_k_a,
           rwkv_r_k, rwkv_ln_w, rwkv_ln_b, mem_norm_g, w_mem_kv, mem_q_norm, mem_k_norm, w_out, ffn_norm_g,
           w_up, conv_w, conv_b, w_down):
    p = dict(zip(_PARAM_NAMES, (
        rel_bias, mix_norm_g, w_in, w_in_vres, attn_q_norm, attn_k_norm, attn_sinks, rwkv_mu, rwkv_mu_vres,
        rwkv_w0, rwkv_w2, rwkv_a0, rwkv_a2, rwkv_v0, rwkv_v2, rwkv_g2, rwkv_k_k, rwkv_k_a, rwkv_r_k, rwkv_ln_w,
        rwkv_ln_b, mem_norm_g, w_mem_kv, mem_q_norm, mem_k_norm, w_out, ffn_norm_g, w_up, conv_w, conv_b, w_down)))
    p["vectors"], p["lora"] = _pack_params(p)
    p["w_mem_kv"] = _bf(p["w_mem_kv"])
    nb, S, D = x.shape
    xt = x.reshape(nb * S, D)
    mem2d = mem.reshape(nb * mem.shape[1], D)
    bias = _bias_call(rel_bias)
    v_first = None
    for l in range(w_in.shape[0]):
        qa, ka, va, qm, r, lw, k, v, a, b, gate = _proj_call(xt, p, l, v_first, nb=nb)
        if l == 0:
            v_first = v
        y = _rwkv_scan(r, lw, k, v, a, b, nb=nb)
        mk, mv = _memkv_call(mem2d, p, l)
        xt = _mix_call(xt, qa, ka, va, qm, mk, mv, y, r, k, v, gate, bias, p, l, nb=nb)
        xt = _ffn_call(xt, p, l, nb=nb)
    return xt.reshape(nb, S, D)
```

```python
import functools
import math

import jax
import jax.numpy as jnp
import numpy as np
from jax import lax
from jax.experimental import pallas as pl
from jax.experimental.pallas import tpu as pltpu

F32 = jnp.float32
BF16 = jnp.bfloat16

HEAD_DIM = 64
SCAN_CHUNK = 64
VMEM_LIMIT_BYTES = 56 * 1024 * 1024


def _dot(a, b):
    return jnp.dot(a, b, preferred_element_type=F32)


def _dot_nt(a, b):
    return lax.dot_general(a, b, (((1,), (1,)), ((), ())), preferred_element_type=F32)


def _dot_tn(a, b):
    return lax.dot_general(a, b, (((0,), (0,)), ((), ())), preferred_element_type=F32)


def _bf(x):
    return x.astype(BF16)


SCAN_GROUP_HEADS = 4
SCAN_GROUP_W = SCAN_GROUP_HEADS * HEAD_DIM
SCAN_CHUNKS_PER_STEP = 4


_SCAN_STASH = (("w2", 1, F32), ("y_v", 1, F32), ("m_rb", 1, F32), ("v", 1, F32), ("w1r", 2, F32),
               ("bk", 2, F32), ("g_row", 1, F32))


def _scan_kernel(r_ref, lw_ref, k_ref, v_ref, a_ref, b_ref, bdm_ref, y_ref, h_ref, *stash_refs, nb, nch):
    C, N, GW, HPG = SCAN_CHUNK, HEAD_DIM, SCAN_GROUP_W, SCAN_GROUP_HEADS
    ngroups = h_ref.shape[0]
    stash = {name: ref for (name, _, _), ref in zip(_SCAN_STASH, stash_refs)}

    @pl.when(pl.program_id(0) == 0)
    def _():
        h_ref[...] = jnp.zeros_like(h_ref)
        for ref in stash_refs:
            ref[...] = jnp.zeros_like(ref)

    trow = lax.broadcasted_iota(jnp.int32, (C, C), 0)
    tcol = lax.broadcasted_iota(jnp.int32, (C, C), 1)
    tri = (trow >= tcol).astype(BF16)
    grow = lax.broadcasted_iota(jnp.int32, (C, GW), 0)
    gcol = lax.broadcasted_iota(jnp.int32, (C, GW), 1) % N
    incl = grow >= gcol
    strict = grow > gcol
    eye = (grow == gcol).astype(F32)
    bdm = bdm_ref[...]

    def bd(x):
        return jnp.concatenate([_bf(x)] * HPG, axis=0) * bdm

    def split2(x):
        hi = _bf(x)
        return hi, _bf(x - hi.astype(F32))

    def chunk_units(c):
        rows = slice(c * C, (c + 1) * C)
        per_batch = []
        for b in range(nb):
            lw = lw_ref[b, rows, :]
            l1 = _bf(lw)
            e1 = lw - l1.astype(F32)
            l2 = _bf(e1)
            l3 = _bf(e1 - l2.astype(F32))
            cum = _dot(tri, l1) + _dot(tri, l2) + _dot(tri, l3)
            cum_last = cum[C - 1:C, :]
            g_inv = jnp.exp(-cum)
            g_out = jnp.exp(cum_last - cum)
            f32 = lambda ref: ref[b, rows, :].astype(F32)
            kk, bb = f32(k_ref), f32(b_ref)
            per_batch.append(dict(
                a_t=f32(a_ref) * jnp.exp(cum - lw), r_t=f32(r_ref) * jnp.exp(cum),
                b_t=bb * g_inv, k_t=kk * g_inv, b_h=bb * g_out, k_h=kk * g_out,
                g_c=jnp.broadcast_to(jnp.exp(cum_last), cum.shape), v=f32(v_ref)))
        cat = {n: jnp.concatenate([pb[n] for pb in per_batch], axis=1) for n in per_batch[0]}
        return [{n: t[:, g * GW:(g + 1) * GW] for n, t in cat.items()} for g in range(ngroups)]

    def prepare(chunks):
        units = [u for c in chunks for u in chunk_units(c)]

        nu = len(units)
        ar = [jnp.concatenate([_bf(u["a_t"]), _bf(u["r_t"])], axis=0) for u in units]
        pb = [_dot_nt(ar[i], bd(units[i]["b_t"])) for i in range(nu)]
        pk = [_dot_nt(ar[i], bd(units[i]["k_t"])) for i in range(nu)]
        l_ab = [jnp.where(strict, p[:C], 0.0) for p in pb]
        m_rb = [_bf(jnp.where(incl, p[C:], 0.0)) for p in pb]
        tril2 = jnp.concatenate([strict, incl], axis=0)
        lm_k = [_bf(jnp.where(tril2, p, 0.0)) for p in pk]
        yield
        t_inv = [eye + l for l in l_ab]
        pw = [_dot(_bf(l), bd(l)) for l in l_ab]
        yield
        n_rounds = int(math.log2(C)) - 1
        for rnd in range(n_rounds):
            if rnd + 1 < n_rounds:
                z = [_dot(jnp.concatenate([_bf(p), _bf(t)], axis=0), bd(p)) for p, t in zip(pw, t_inv)]
                pw = [zz[:C] for zz in z]
                t_inv = [t + zz[C:] for t, zz in zip(t_inv, z)]
            else:
                t_inv = [t + _dot(_bf(t), bd(p)) for t, p in zip(t_inv, pw)]
            yield
        t_b = [_bf(t) for t in t_inv]
        w1 = [_dot(t, bd(u["a_t"])) for t, u in zip(t_b, units)]
        lmv = [_dot(m, bd(u["v"])) for m, u in zip(lm_k, units)]
        yield
        w2 = [_dot(t, bd(x[:C])) for t, x in zip(t_b, lmv)]
        parts = [p for u in units for p in split2(u["g_c"] * eye)]
        sums = _dot(jnp.concatenate(parts, axis=0), bdm)
        yield
        for i, u in enumerate(units):
            put = lambda name, val: stash[name].__setitem__(i, val.astype(stash[name].dtype))
            put("w2", w2[i])
            put("y_v", lmv[i][C:])
            put("m_rb", m_rb[i])
            put("v", u["v"])
            put("w1r", jnp.concatenate([w1[i], u["r_t"]], axis=0))
            put("bk", jnp.concatenate([u["b_h"], u["k_h"]], axis=0))
            put("g_row", sums[2 * i * C:(2 * i + 1) * C] + sums[(2 * i + 1) * C:(2 * i + 2) * C])
        yield

    state = [h_ref[g] for g in range(ngroups)]

    def advance(c):
        p = {n: [ref[c * ngroups + g] for g in range(ngroups)] for n, ref in stash.items()}
        for n in ("m_rb", "v", "w1r", "bk"):
            p[n] = [_bf(x) for x in p[n]]
        z = [_dot(w, bd(s)) for w, s in zip(p["w1r"], state)]
        yield
        u = [zz[:C] + w for zz, w in zip(z, p["w2"])]
        y = [zz[C:] + _dot(m, bd(x)) + yv for zz, m, x, yv in zip(z, p["m_rb"], u, p["y_v"])]
        uv = [jnp.concatenate([_bf(x), v], axis=0) for x, v in zip(u, p["v"])]
        full = [_dot_tn(b, x) * bdm.astype(F32) for b, x in zip(p["bk"], uv)]
        yield
        upd = [sum(f[h * N:(h + 1) * N] for h in range(HPG)) for f in full]
        state[:] = [g * s + d for g, s, d in zip(p["g_row"], state, upd)]
        ycat = jnp.concatenate(y, axis=1)
        w = y_ref.shape[2]
        for b in range(nb):
            y_ref[b, c * C:(c + 1) * C, :] = ycat[:, b * w:(b + 1) * w]
        yield

    def run(gen, stages):
        for _ in range(stages):
            next(gen, None)

    prep = prepare(list(range(nch)))
    n_prep = 5 + int(math.log2(C)) - 1
    for c in range(nch):
        seq = advance(c)
        run(prep, 1)
        run(seq, 1)
        run(prep, 1)
        run(seq, 2)
    run(prep, n_prep)
    for g in range(ngroups):
        h_ref[g] = state[g]


def _rwkv_scan(r, lw, k, v, a, b, *, nb):
    T, W = r.shape
    S = T // nb
    C, nch, GW = SCAN_CHUNK, SCAN_CHUNKS_PER_STEP, SCAN_GROUP_W
    assert S % (C * nch) == 0 and (nb * W) % GW == 0 and W % 128 == 0
    ngroups = nb * W // GW
    head = np.arange(GW) // HEAD_DIM
    bdm = jnp.asarray((head[:, None] == head[None, :]).astype(np.float32), dtype=BF16)
    n_blocks = S // (C * nch)
    in_spec = pl.BlockSpec((nb, C * nch, W), lambda s: (0, jnp.minimum(s, n_blocks - 1), 0))
    out_spec = pl.BlockSpec((nb, C * nch, W), lambda s: (0, jnp.maximum(s - 1, 0), 0))
    args = [t.reshape(nb, S, W) for t in (r, lw, k, v, a, b)]
    y = pl.pallas_call(
        functools.partial(_scan_kernel, nb=nb, nch=nch),
        grid=(n_blocks + 1,),
        in_specs=[in_spec] * 6 + [pl.BlockSpec((GW, GW), lambda s: (0, 0))],
        out_specs=out_spec,
        out_shape=jax.ShapeDtypeStruct((nb, S, W), F32),
        scratch_shapes=[pltpu.VMEM((ngroups, HEAD_DIM, GW), F32)]
        + [pltpu.VMEM((nch * ngroups, rows * C, GW), dt) for _, rows, dt in _SCAN_STASH],
        compiler_params=pltpu.CompilerParams(dimension_semantics=("arbitrary",), vmem_limit_bytes=VMEM_LIMIT_BYTES),
        name="rwkv_scan",
    )(*args, bdm)
    return y.reshape(T, W)


D_MODEL = 1024
ATTN_HEADS = 6
ATTN_KV_HEADS = 2
ATTN_GROUP = ATTN_HEADS // ATTN_KV_HEADS
ATTN_W = ATTN_HEADS * HEAD_DIM
KV_W = ATTN_KV_HEADS * HEAD_DIM
BLOCK_Q = 128
WINDOW = 128
N_BUCKETS = 32
MAX_EXACT = N_BUCKETS // 2
RWKV_W = 6 * HEAD_DIM
LORA_W = 128
SHIFT_W = 3 * RWKV_W + LORA_W
MEM_HEADS = 4
MEM_W = MEM_HEADS * HEAD_DIM
IN_BASE = ATTN_W + 2 * KV_W + SHIFT_W + MEM_W
PB_OFF = ATTN_W + 2 * KV_W
QM_OFF = PB_OFF + SHIFT_W
D_FF = 2816
EPS = 1e-6
GN_EPS = 64e-5
L2_EPS = 1e-12
MXU_WIDTH = 256

PROJ_TM = 1024
PROJ_SUB = 256
MIX_TQ = 512
FFN_TM = 1024
FFN_FC = 256
FFN_STAGE_CHUNKS = 8
WEIGHT_STAGE_CHUNKS = 4


def _block_diag_ones(width):
    idx = np.arange(width) // HEAD_DIM
    return jnp.asarray((idx[:, None] == idx[None, :]).astype(np.float32), dtype=BF16)


def _head_sum(t, bd):
    tb = _bf(t)
    width = t.shape[1]
    parts = [_dot(tb[:, lo:min(lo + MXU_WIDTH, width)], bd[lo:min(lo + MXU_WIDTH, width), lo:min(lo + MXU_WIDTH, width)])
             for lo in range(0, width, MXU_WIDTH)]
    return parts[0] if len(parts) == 1 else jnp.concatenate(parts, axis=1)


def _head_rms(t, bd, gain):
    ms = _head_sum(t * t, bd) * (1.0 / HEAD_DIM)
    return t * lax.rsqrt(ms + EPS) * gain


def _rms_rows(x, g):
    ms = jnp.mean(x * x, axis=-1, keepdims=True)
    return x * lax.rsqrt(ms + EPS) * g


class _Row:
    def __init__(self, table_ref, row, width):
        self.table_ref, self.row, self.width = table_ref, row, width

    def __getitem__(self, _):
        return self.table_ref[self.row:self.row + 1, 0:self.width]


def _proj_kernel(*refs, tm, sub, tiles_per_batch, has_vres, layer):
    it = iter(refs)
    x_ref, vec_ref, w_hbm, lora_ref = next(it), next(it), next(it), next(it)
    bd384_ref, bd128_ref, bd256_ref = next(it), next(it), next(it)
    if has_vres:
        wv_ref, vfirst_ref = next(it), next(it)
    qa_ref, ka_out_ref, va_ref, qm_ref = next(it), next(it), next(it), next(it)
    r_ref, lw_ref, k_ref, v_ref, a_ref, b_ref, gate_ref = (next(it) for _ in range(7))
    pbs_ref, w_ref, w_stage, w_sem = next(it), next(it), next(it), next(it)
    sw = pbs_ref.shape[1]

    @pl.when(pl.program_id(0) == 0)
    def _():
        _stage_weights(w_hbm, layer, w_ref, w_stage, w_sem)
        if has_vres:
            w_ref[:, IN_BASE:IN_BASE + LORA_W] = _bf(wv_ref[...])

    g_ref, qg_ref, kg_ref, mg_ref = (_Row(vec_ref, VEC_ROWS[n], w) for n, w in (
        ("mix_norm_g", D_MODEL), ("q_gain", ATTN_W), ("k_gain", KV_W), ("mq_gain", MEM_W)))
    mu_ref = _Row(vec_ref, VEC_ROWS["mu"], sw)
    w0_ref, a0_ref, kk_ref, ka_ref, v0_ref = (_Row(vec_ref, VEC_ROWS[n], RWKV_W) for n in ("w0", "a0", "k_k", "k_a", "v0"))
    w2_ref, a2_ref, g2_ref, v2_ref = (lora_ref.at[j] for j in range(4))

    @pl.when(pl.program_id(0) % tiles_per_batch == 0)
    def _():
        pbs_ref[0:8, :] = jnp.zeros((8, sw), F32)

    def project(i):
        rows = slice(i * sub, (i + 1) * sub)
        return _dot(_bf(_rms_rows(x_ref[rows, :], g_ref[...])), w_ref[...])

    def attention_outputs(i, proj):
        rows = slice(i * sub, (i + 1) * sub)
        qa_ref[rows, :] = _bf(_head_rms(proj[:, :ATTN_W], bd384_ref[...], qg_ref[...]))
        ka_out_ref[rows, :] = _bf(_head_rms(proj[:, ATTN_W:ATTN_W + KV_W], bd128_ref[...], kg_ref[...]))
        va_ref[rows, :] = _bf(proj[:, ATTN_W + KV_W:PB_OFF])
        qm_ref[rows, :] = _bf(_head_rms(proj[:, QM_OFF:QM_OFF + MEM_W], bd256_ref[...], mg_ref[...]))

    def rwkv_outputs(i, proj):
        rows = slice(i * sub, (i + 1) * sub)
        lo = 8 + i * sub
        pbs_ref[lo:lo + sub, 0:SHIFT_W] = proj[:, PB_OFF:PB_OFF + SHIFT_W]
        if has_vres:
            pbs_ref[lo:lo + sub, SHIFT_W:sw] = proj[:, IN_BASE:IN_BASE + LORA_W]
        cur = pbs_ref[lo:lo + sub, :]
        prev = pbs_ref[lo - 1:lo - 1 + sub, :]
        sh = cur + mu_ref[...] * (prev - cur)
        r = sh[:, 0:RWKV_W]
        k = sh[:, RWKV_W:2 * RWKV_W]
        v = sh[:, 2 * RWKV_W:3 * RWKV_W]
        z = sh[:, 3 * RWKV_W:SHIFT_W]
        t = w0_ref[...] + _dot(_bf(jnp.tanh(z)), w2_ref[...])
        lw_ref[rows, :] = -math.exp(-0.5) * jax.nn.sigmoid(t)
        a = jax.nn.sigmoid(a0_ref[...] + _dot(_bf(z), a2_ref[...]))
        gate_ref[rows, :] = _bf(_dot(_bf(jax.nn.sigmoid(z)), g2_ref[...]))
        if has_vres:
            vd = sh[:, SHIFT_W:sw]
            v = v + (vfirst_ref[rows, :].astype(F32) - v) * jax.nn.sigmoid(v0_ref[...] + _dot(_bf(vd), v2_ref[...]))
        kk = k * kk_ref[...]
        kk = kk / jnp.maximum(jnp.sqrt(_head_sum(kk * kk, bd384_ref[...])), L2_EPS)
        r_ref[rows, :] = _bf(r)
        k_ref[rows, :] = _bf(k * (1.0 + (a - 1.0) * ka_ref[...]))
        v_ref[rows, :] = _bf(v)
        a_ref[rows, :] = _bf(-kk)
        b_ref[rows, :] = _bf(kk * a)

    n_sub = tm // sub
    proj = project(0)
    for i in range(n_sub):
        attention_outputs(i, proj)
        nxt = project(i + 1) if i + 1 < n_sub else None
        rwkv_outputs(i, proj)
        proj = nxt
    pbs_ref[0:8, :] = pbs_ref[tm:tm + 8, :]


def _layer_spec(stacked, l):
    return pl.BlockSpec((None,) + stacked.shape[1:], lambda i: (l,) + (0,) * (stacked.ndim - 1),
                        pipeline_mode=pl.Buffered(1))


VEC_W = SHIFT_W + LORA_W
VEC_ROWS = {n: i for i, n in enumerate((
    "mix_norm_g", "q_gain", "k_gain", "mq_gain", "mu", "w0", "a0", "k_k", "k_a", "v0", "ln_w", "ln_b", "r_k",
    "ffn_norm_g"))}
VEC_TABLE_ROWS = 16


_PACKED = ("mix_norm_g", "attn_q_norm", "attn_k_norm", "mem_q_norm", "rwkv_mu", "rwkv_mu_vres", "rwkv_w0", "rwkv_a0",
           "rwkv_k_k", "rwkv_k_a", "rwkv_v0", "rwkv_ln_w", "rwkv_ln_b", "rwkv_r_k", "ffn_norm_g",
           "rwkv_w2", "rwkv_a2", "rwkv_g2", "rwkv_v2")


def _pack_kernel(*refs, layers):
    src = dict(zip(_PACKED, refs))
    sinks_ref, vec_ref, lora_ref, sink_ref = refs[len(_PACKED):]
    scale = HEAD_DIM ** -0.5
    vec_ref[...] = jnp.zeros_like(vec_ref)
    lora_ref[...] = jnp.zeros_like(lora_ref)
    tiled = lambda v, n: jnp.concatenate([v] * n, axis=1)

    def put(l, name, val):
        vec_ref[l, VEC_ROWS[name]:VEC_ROWS[name] + 1, 0:val.shape[1]] = val

    for l in range(layers):
        row = lambda n, at=l: src[n][at:at + 1, :]
        put(l, "mix_norm_g", row("mix_norm_g"))
        put(l, "q_gain", tiled(row("attn_q_norm"), ATTN_HEADS) * scale)
        put(l, "k_gain", tiled(row("attn_k_norm"), ATTN_KV_HEADS))
        put(l, "mq_gain", tiled(row("mem_q_norm"), MEM_HEADS) * scale)
        put(l, "mu", row("rwkv_mu"))
        for name, key in (("w0", "rwkv_w0"), ("a0", "rwkv_a0"), ("k_k", "rwkv_k_k"), ("k_a", "rwkv_k_a"),
                          ("ln_w", "rwkv_ln_w"), ("ln_b", "rwkv_ln_b"), ("r_k", "rwkv_r_k"), ("ffn_norm_g", "ffn_norm_g")):
            put(l, name, row(key))
        for j, (key, at) in enumerate((("rwkv_w2", 0), ("rwkv_a2", 32), ("rwkv_g2", 64))):
            w = src[key][l]
            lora_ref[l, j, at:at + w.shape[0], :] = _bf(w)
        if l > 0:
            mu_v = row("rwkv_mu_vres", l - 1)
            vec_ref[l, VEC_ROWS["mu"]:VEC_ROWS["mu"] + 1, SHIFT_W:SHIFT_W + mu_v.shape[1]] = mu_v
            put(l, "v0", row("rwkv_v0", l - 1))
            v2 = src["rwkv_v2"][l - 1]
            lora_ref[l, 3, 0:v2.shape[0], :] = _bf(v2)
        for h in range(ATTN_HEADS):
            g, i = divmod(h, ATTN_GROUP)
            sink_ref[l, g, i * BLOCK_Q:(i + 1) * BLOCK_Q, :] = jnp.full((BLOCK_Q, 1), sinks_ref[l, h], F32)


def _pack_params(p):
    layers = p["w_in"].shape[0]
    srcs = [p[n].astype(F32) for n in _PACKED]
    srcs[_PACKED.index("rwkv_r_k")] = p["rwkv_r_k"].reshape(layers, RWKV_W).astype(F32)
    vm = pl.BlockSpec(memory_space=pltpu.VMEM)
    return pl.pallas_call(
        functools.partial(_pack_kernel, layers=layers),
        in_specs=[vm] * len(srcs) + [pl.BlockSpec(memory_space=pltpu.SMEM)],
        out_specs=[vm, vm, vm],
        out_shape=[jax.ShapeDtypeStruct((layers, VEC_TABLE_ROWS, VEC_W), F32),
                   jax.ShapeDtypeStruct((layers, 4, LORA_W, RWKV_W), BF16),
                   jax.ShapeDtypeStruct((layers, ATTN_KV_HEADS, ATTN_GROUP * BLOCK_Q, 1), F32)],
        name="pack_params",
    )(*srcs, p["attn_sinks"].astype(F32))


def _proj_call(x, p, l, v_first, *, nb):
    T, D = x.shape
    tm = PROJ_TM
    assert T % tm == 0 and (T // nb) % tm == 0
    has_vres = l > 0
    w = p["w_in"]
    ins = [x, p["vectors"], w, p["lora"], _block_diag_ones(ATTN_W), _block_diag_ones(KV_W), _block_diag_ones(MEM_W)]
    if has_vres:
        ins += [jnp.pad(p["w_in_vres"][l - 1], ((0, 0), (0, LORA_W - 16))), v_first]
    nw = IN_BASE + (LORA_W if has_vres else 0)

    def spec(a):
        if a is w:
            return pl.BlockSpec(memory_space=pl.ANY)
        if a.ndim >= 3:
            return _layer_spec(a, l)
        if a.shape[0] == T:
            return pl.BlockSpec((tm, a.shape[1]), lambda i: (i, 0))
        return pl.BlockSpec(a.shape, lambda i: (0, 0), pipeline_mode=pl.Buffered(1))

    out_shapes = [jax.ShapeDtypeStruct((T, ATTN_W), BF16), jax.ShapeDtypeStruct((T, KV_W), BF16),
                  jax.ShapeDtypeStruct((T, KV_W), BF16), jax.ShapeDtypeStruct((T, MEM_W), BF16)]
    out_shapes += [jax.ShapeDtypeStruct((T, RWKV_W), F32 if j == 1 else BF16) for j in range(7)]
    sw = SHIFT_W + (LORA_W if has_vres else 0)
    return pl.pallas_call(
        functools.partial(_proj_kernel, tm=tm, sub=PROJ_SUB, tiles_per_batch=(T // nb) // tm, has_vres=has_vres,
                          layer=l),
        grid=(T // tm,),
        in_specs=[spec(a) for a in ins],
        out_specs=[pl.BlockSpec((tm, s.shape[1]), lambda i: (i, 0)) for s in out_shapes],
        out_shape=out_shapes,
        scratch_shapes=[pltpu.VMEM((tm + 8, sw), F32), pltpu.VMEM((D, nw), BF16),
                        pltpu.VMEM((2, D // WEIGHT_STAGE_CHUNKS, IN_BASE), F32), pltpu.SemaphoreType.DMA((2,))],
        compiler_params=pltpu.CompilerParams(dimension_semantics=("arbitrary",), vmem_limit_bytes=VMEM_LIMIT_BYTES),
        name=f"proj_l{l}",
    )(*ins)


def _bucket_table():
    qi = np.arange(BLOCK_Q)[:, None]
    kj = np.arange(2 * BLOCK_Q)[None, :]
    dist = qi + BLOCK_Q - kj
    in_band = (dist >= 0) & (dist < WINDOW)
    d = np.maximum(dist, 1).astype(np.float32)
    large = MAX_EXACT + (np.log(d / np.float32(MAX_EXACT)) / np.float32(math.log(WINDOW / MAX_EXACT))
                         * np.float32(N_BUCKETS - MAX_EXACT)).astype(np.int32)
    large = np.minimum(large, N_BUCKETS - 1)
    bucket = np.where(dist < MAX_EXACT, np.maximum(dist, 0), large)
    return np.where(in_band, bucket, -1).astype(np.int32)


def _bias_kernel(rb_ref, bucket_ref, out_ref):
    bucket = bucket_ref[...]
    for h in range(ATTN_HEADS):
        acc = jnp.full(bucket.shape, -jnp.inf, F32)
        for j in range(N_BUCKETS):
            acc = jnp.where(bucket == j, rb_ref[j, h], acc)
        out_ref[h] = acc


def _bias_call(rel_bias):
    tab = pl.pallas_call(
        _bias_kernel,
        in_specs=[pl.BlockSpec(memory_space=pltpu.SMEM), pl.BlockSpec(memory_space=pltpu.VMEM)],
        out_specs=pl.BlockSpec(memory_space=pltpu.VMEM),
        out_shape=jax.ShapeDtypeStruct((ATTN_HEADS, BLOCK_Q, 2 * BLOCK_Q), F32),
        name="rel_bias_table",
    )(rel_bias.astype(F32), jnp.asarray(_bucket_table()))
    return tab.reshape(ATTN_KV_HEADS, ATTN_GROUP * BLOCK_Q, 2 * BLOCK_Q)


def _memkv_kernel(mem_ref, g_ref, w_ref, kg_ref, bd_ref, mk_ref, mv_ref):
    hn = _bf(_rms_rows(mem_ref[...], g_ref[...]))
    kv = _dot(hn, w_ref[...])
    mk_ref[...] = _bf(_head_rms(kv[:, :MEM_W], bd_ref[...], kg_ref[...]))
    mv_ref[...] = _bf(kv[:, MEM_W:])


def _memkv_call(mem2d, p, l):
    rows = mem2d.shape[0]
    vm = pl.BlockSpec(memory_space=pltpu.VMEM)
    return pl.pallas_call(
        _memkv_kernel,
        in_specs=[vm] * 5,
        out_specs=[vm, vm],
        out_shape=[jax.ShapeDtypeStruct((rows, MEM_W), BF16)] * 2,
        compiler_params=pltpu.CompilerParams(vmem_limit_bytes=VMEM_LIMIT_BYTES),
        name=f"mem_kv_l{l}",
    )(mem2d, p["mem_norm_g"][l].reshape(1, -1), _bf(p["w_mem_kv"][l]),
      jnp.tile(p["mem_k_norm"][l], MEM_HEADS).reshape(1, -1), _block_diag_ones(MEM_W))


def _mix_kernel(x_ref, qa_ref, kc_ref, kp_ref, vc_ref, vp_ref, bias_ref, sink_ref, qm_ref, mk_ref, mv_ref,
                y_ref, r_ref, k_ref, v_ref, gate_ref, vec_ref, bd_ref, wout_hbm, out_ref, att_ref,
                wout_ref, w_stage, w_sem, *, tq, tiles_per_batch, layer):
    lnw_ref, lnb_ref, rk_ref = (_Row(vec_ref, VEC_ROWS[n], RWKV_W) for n in ("ln_w", "ln_b", "r_k"))
    N, BQ = HEAD_DIM, BLOCK_Q
    seq_start = pl.program_id(0) % tiles_per_batch == 0
    nqb = tq // BQ

    @pl.when(pl.program_id(0) == 0)
    def _():
        att_ref[...] = jnp.zeros_like(att_ref)
        _stage_weights(wout_hbm, layer, wout_ref, w_stage, w_sem)

    qa = qa_ref[...]
    kall = jnp.concatenate([kp_ref[...], kc_ref[...]], axis=0)
    vall = jnp.concatenate([vp_ref[...], vc_ref[...]], axis=0)
    k_group = [kall[:, g * N:(g + 1) * N] for g in range(ATTN_KV_HEADS)]
    v_group = [vall[:, g * N:(g + 1) * N] for g in range(ATTN_KV_HEADS)]
    before_seq = lax.broadcasted_iota(jnp.int32, (ATTN_GROUP * BQ, 2 * BQ), 1) < BQ

    qm = qm_ref[...]
    mk = mk_ref[0]
    mv = mv_ref[0]

    swa = [(j, g) for j in range(nqb) for g in range(ATTN_KV_HEADS)]
    logits, values, sinks = [], [], []
    for j, g in swa:
        qg = jnp.concatenate(
            [qa[j * BQ:(j + 1) * BQ, (ATTN_GROUP * g + i) * N:(ATTN_GROUP * g + i + 1) * N] for i in range(ATTN_GROUP)],
            axis=0)
        lg = _dot_nt(qg, k_group[g][j * BQ:(j + 2) * BQ]) + bias_ref[g]
        if j == 0:
            lg = jnp.where(jnp.logical_and(seq_start, before_seq), -jnp.inf, lg)
        logits.append(lg)
        values.append(v_group[g][j * BQ:(j + 2) * BQ])
        sinks.append(sink_ref[g])
    for h in range(MEM_HEADS):
        hs = slice(h * N, (h + 1) * N)
        logits.append(_dot_nt(qm[:, hs], mk[:, hs]))
        values.append(mv[:, hs])
        sinks.append(None)
    bd = bd_ref[...]
    y = y_ref[...]
    d = y - _head_sum(y, bd) * (1.0 / N)
    var = _head_sum(d * d, bd) * (1.0 / N)
    yn = d * lax.rsqrt(var + GN_EPS) * lnw_ref[...] + lnb_ref[...]
    f32 = lambda ref: ref[...].astype(F32)
    bonus = _head_sum(f32(r_ref) * f32(k_ref) * rk_ref[...], bd) * f32(v_ref)
    out_b = (yn + bonus) * f32(gate_ref)
    mixed = jnp.concatenate([att_ref[:, :ATTN_W], _bf(out_b), att_ref[:, ATTN_W:]], axis=-1)
    out_ref[...] = x_ref[...] + _dot(mixed, wout_ref[...])

    row_max = [jnp.max(lg, axis=-1, keepdims=True) for lg in logits]
    m = [rm if s is None else jnp.maximum(rm, s) for rm, s in zip(row_max, sinks)]
    e = [jnp.exp(lg - mm) for lg, mm in zip(logits, m)]
    denom = [jnp.sum(ee, axis=-1, keepdims=True) for ee in e]
    denom = [d if s is None else d + jnp.exp(s - mm) for d, s, mm in zip(denom, sinks, m)]
    outs = [_bf(_dot(_bf(ee), vv) / d) for ee, vv, d in zip(e, values, denom)]

    head_rows = [[None] * nqb for _ in range(ATTN_HEADS)]
    for (j, g), o in zip(swa, outs):
        for i in range(ATTN_GROUP):
            head_rows[ATTN_GROUP * g + i][j] = o[i * BQ:(i + 1) * BQ]
    att_ref[:, :ATTN_W] = jnp.concatenate([jnp.concatenate(rows, axis=0) for rows in head_rows], axis=-1)
    att_ref[:, ATTN_W:] = jnp.concatenate(outs[len(swa):], axis=-1)


def _mix_call(x, qa, ka, va, qm, mk, mv, y, r, k, v, gate, bias, p, l, *, nb):
    T, D = x.shape
    tq = MIX_TQ
    S = T // nb
    assert S % tq == 0 and tq % BLOCK_Q == 0
    tpb = S // tq
    qpb = tq // BLOCK_Q
    mem_tokens = mk.shape[0] // nb
    sink = p["sink_columns"]
    n_tiles = T // tq
    att = lambda i: jnp.minimum(i, n_tiles - 1)
    fin = lambda i: jnp.maximum(i - 1, 0)
    att_tile = lambda w: pl.BlockSpec((tq, w), lambda i: (att(i), 0))
    tile = lambda w: pl.BlockSpec((tq, w), lambda i: (fin(i), 0))
    prev = pl.BlockSpec((BLOCK_Q, KV_W), lambda i: (jnp.maximum(att(i) * qpb - 1, 0), 0))
    full = lambda a: pl.BlockSpec(a.shape, lambda i: (0,) * a.ndim)
    memspec = pl.BlockSpec((1, mem_tokens, MEM_W), lambda i: (att(i) // tpb, 0, 0))
    bd = _block_diag_ones(RWKV_W)
    wout = p["w_out"]
    mk3 = mk.reshape(nb, mem_tokens, MEM_W)
    mv3 = mv.reshape(nb, mem_tokens, MEM_W)
    return pl.pallas_call(
        functools.partial(_mix_kernel, tq=tq, tiles_per_batch=tpb, layer=l),
        grid=(n_tiles + 1,),
        in_specs=[tile(D), att_tile(ATTN_W), att_tile(KV_W), prev, att_tile(KV_W), prev, full(bias), _layer_spec(sink, l),
                  att_tile(MEM_W), memspec, memspec] + [tile(RWKV_W)] * 5
                 + [_layer_spec(p["vectors"], l), full(bd), pl.BlockSpec(memory_space=pl.ANY)],
        out_specs=tile(D),
        out_shape=jax.ShapeDtypeStruct((T, D), F32),
        scratch_shapes=[pltpu.VMEM((tq, ATTN_W + MEM_W), BF16), pltpu.VMEM((D, D), BF16),
                        pltpu.VMEM((2, D // WEIGHT_STAGE_CHUNKS, D), F32), pltpu.SemaphoreType.DMA((2,))],
        compiler_params=pltpu.CompilerParams(dimension_semantics=("arbitrary",), vmem_limit_bytes=VMEM_LIMIT_BYTES),
        name=f"mix_l{l}",
    )(x, qa, ka, ka, va, va, bias, sink, qm, mk3, mv3, y, r, k, v, gate, p["vectors"], bd, wout)


SUBLANES = 8


def _tile_copies(hbm, buf, sem, tile, slot, to_vmem):
    copies = []
    for s in range(SUBLANES):
        src, dst = hbm.at[tile, s], buf.at[slot, :, s, :]
        if not to_vmem:
            src, dst = dst, src
        copies.append(pltpu.make_async_copy(src, dst, sem.at[slot, s]))
    return copies


def _shift_rows(u, before, steps):
    first = lax.broadcasted_iota(jnp.int32, (SUBLANES, u.shape[1]), 0) == 0
    tm = u.shape[0]
    heads = []
    for i in range(steps):
        lo = tm - (steps - i) * SUBLANES
        wrapped = pltpu.roll(u[lo:lo + SUBLANES], 1, axis=0)
        prior = jnp.broadcast_to(before[SUBLANES - steps + i:SUBLANES - steps + i + 1], wrapped.shape)
        heads.append(jnp.where(first, prior, wrapped))
    return jnp.concatenate(heads + [u[:tm - steps * SUBLANES]], axis=0)


def _stage_weights(w_hbm, layer, dst_ref, stage_ref, sem):
    rows = stage_ref.shape[1]
    n = dst_ref.shape[0] // rows
    copy = lambda c: pltpu.make_async_copy(w_hbm.at[layer, pl.ds(c * rows, rows), :], stage_ref.at[c % 2], sem.at[c % 2])
    copy(0).start()
    for c in range(n):
        if c + 1 < n:
            copy(c + 1).start()
        copy(c).wait()
        dst_ref[c * rows:(c + 1) * rows, 0:stage_ref.shape[2]] = _bf(stage_ref[c % 2])


def _ffn_kernel(x_hbm, vec_ref, wup_hbm, cw_ref, cb_ref, wdn_hbm, o_hbm, xbuf, obuf, carry_ref, act_ref,
                wup_ref, wdn_ref, up_stage, dn_stage, in_sem, out_sem, w_sem, *, tm, tiles_per_batch, n_tiles, layer):
    D = xbuf.shape[-1]
    g_ref = _Row(vec_ref, VEC_ROWS["ffn_norm_g"], D)
    q = tm // SUBLANES
    i = pl.program_id(0)
    slot = i % 2

    @pl.when(i == 0)
    def _():
        carry_ref[...] = jnp.zeros_like(carry_ref)
        for cp in _tile_copies(x_hbm, xbuf, in_sem, 0, 0, True):
            cp.start()
        _stage_weights(wup_hbm, layer, wup_ref, up_stage, w_sem)
        _stage_weights(wdn_hbm, layer, wdn_ref, dn_stage, w_sem)

    @pl.when(i + 1 < n_tiles)
    def _():
        for cp in _tile_copies(x_hbm, xbuf, in_sem, i + 1, 1 - slot, True):
            cp.start()

    for cp in _tile_copies(x_hbm, xbuf, in_sem, i, slot, True):
        cp.wait()

    @pl.when(i >= 2)
    def _():
        for cp in _tile_copies(o_hbm, obuf, out_sem, i - 2, slot, False):
            cp.wait()

    seq_start = i % tiles_per_batch == 0
    g = g_ref[...]
    x = xbuf[slot].reshape(tm, D)
    before = jnp.where(seq_start, 0.0, _rms_rows(carry_ref[...], g))
    for r, grp in ((SUBLANES - 2, q - 2), (SUBLANES - 1, q - 1)):
        row = grp * SUBLANES + SUBLANES - 1
        carry_ref[r:r + 1, :] = x[row:row + 1]
    h_ext = _bf(jnp.concatenate([before, _rms_rows(x, g)], axis=0))

    n_chunks = D_FF // FFN_FC
    gate_cols = lambda c: slice(c * FFN_FC, (c + 1) * FFN_FC)
    val_cols = lambda c: slice(D_FF + c * FFN_FC, D_FF + (c + 1) * FFN_FC)

    def conv(u_ext, cols):
        w = cw_ref[:, cols]
        u = u_ext[SUBLANES:]
        return (cb_ref[:, cols] + w[0:1] * _shift_rows(u, u_ext[:SUBLANES], 2)
                + w[1:2] * _shift_rows(u, u_ext[:SUBLANES], 1) + w[2:3] * u)

    up = lambda c: (_dot(h_ext, wup_ref[:, gate_cols(c)]), _dot(h_ext, wup_ref[:, val_cols(c)]))
    nxt = up(0)
    for c in range(n_chunks):
        ug, uv = nxt
        if c + 1 < n_chunks:
            nxt = up(c + 1)
        gt = conv(ug, gate_cols(c))
        act_ref[:, gate_cols(c)] = _bf(gt * jax.nn.sigmoid(gt) * conv(uv, val_cols(c)))
    obuf[slot] = (x + _dot(act_ref[...], wdn_ref[...])).reshape(q, SUBLANES, D)

    for cp in _tile_copies(o_hbm, obuf, out_sem, i, slot, False):
        cp.start()

    @pl.when(i == n_tiles - 1)
    def _():
        if n_tiles >= 2:
            for cp in _tile_copies(o_hbm, obuf, out_sem, i - 1, 1 - slot, False):
                cp.wait()
        for cp in _tile_copies(o_hbm, obuf, out_sem, i, slot, False):
            cp.wait()


def _ffn_call(x, p, l, *, nb):
    T, D = x.shape
    tm = FFN_TM
    S = T // nb
    q = tm // SUBLANES
    assert S % tm == 0 and q % SUBLANES == 0 and D_FF % FFN_FC == 0
    n_tiles = T // tm
    wup, wdn = p["w_up"], p["w_down"]
    cw, cb = p["conv_w"], p["conv_b"][:, None, :]
    hbm = pl.BlockSpec(memory_space=pl.ANY)
    assert D % FFN_STAGE_CHUNKS == 0 and D_FF % FFN_STAGE_CHUNKS == 0
    out = pl.pallas_call(
        functools.partial(_ffn_kernel, tm=tm, tiles_per_batch=S // tm, n_tiles=n_tiles, layer=l),
        grid=(n_tiles,),
        in_specs=[hbm, _layer_spec(p["vectors"], l), hbm, _layer_spec(cw, l), _layer_spec(cb, l), hbm],
        out_specs=hbm,
        out_shape=jax.ShapeDtypeStruct((n_tiles, SUBLANES, q, D), F32),
        scratch_shapes=[pltpu.VMEM((2, q, SUBLANES, D), F32), pltpu.VMEM((2, q, SUBLANES, D), F32),
                        pltpu.VMEM((SUBLANES, D), F32), pltpu.VMEM((tm, D_FF), BF16),
                        pltpu.VMEM(wup.shape[1:], BF16), pltpu.VMEM(wdn.shape[1:], BF16),
                        pltpu.VMEM((2, D // FFN_STAGE_CHUNKS, 2 * D_FF), F32),
                        pltpu.VMEM((2, D_FF // FFN_STAGE_CHUNKS, D), F32),
                        pltpu.SemaphoreType.DMA((2, SUBLANES)), pltpu.SemaphoreType.DMA((2, SUBLANES)),
                        pltpu.SemaphoreType.DMA((2,))],
        compiler_params=pltpu.CompilerParams(dimension_semantics=("arbitrary",), vmem_limit_bytes=VMEM_LIMIT_BYTES),
        name=f"ffn_l{l}",
    )(x.reshape(n_tiles, SUBLANES, q, D), p["vectors"], wup, cw, cb, wdn)
    return out.reshape(T, D)


_PARAM_NAMES = (
    "rel_bias", "mix_norm_g", "w_in", "w_in_vres", "attn_q_norm", "attn_k_norm", "attn_sinks", "rwkv_mu",
    "rwkv_mu_vres", "rwkv_w0", "rwkv_w2", "rwkv_a0", "rwkv_a2", "rwkv_v0", "rwkv_v2", "rwkv_g2", "rwkv_k_k",
    "rwkv_k_a", "rwkv_r_k", "rwkv_ln_w", "rwkv_ln_b", "mem_norm_g", "w_mem_kv", "mem_q_norm", "mem_k_norm",
    "w_out", "ffn_norm_g", "w_up", "conv_w", "conv_b", "w_down")


def kernel(x, mem, rel_bias, mix_norm_g, w_in, w_in_vres, attn_q_norm, attn_k_norm, attn_sinks, rwkv_mu,
           rwkv_mu_vres, rwkv_w0, rwkv_w2, rwkv_a0, rwkv_a2, rwkv_v0, rwkv_v2, rwkv_g2, rwkv_k_k, rwkv_k_a,
           rwkv_r_k, rwkv_ln_w, rwkv_ln_b, mem_norm_g, w_mem_kv, mem_q_norm, mem_k_norm, w_out, ffn_norm_g,
           w_up, conv_w, conv_b, w_down):
    p = dict(zip(_PARAM_NAMES, (
        rel_bias, mix_norm_g, w_in, w_in_vres, attn_q_norm, attn_k_norm, attn_sinks, rwkv_mu, rwkv_mu_vres,
        rwkv_w0, rwkv_w2, rwkv_a0, rwkv_a2, rwkv_v0, rwkv_v2, rwkv_g2, rwkv_k_k, rwkv_k_a, rwkv_r_k, rwkv_ln_w,
        rwkv_ln_b, mem_norm_g, w_mem_kv, mem_q_norm, mem_k_norm, w_out, ffn_norm_g, w_up, conv_w, conv_b, w_down)))
    p["vectors"], p["lora"], p["sink_columns"] = _pack_params(p)
    p["w_mem_kv"] = _bf(p["w_mem_kv"])
    nb, S, D = x.shape
    xt = x.reshape(nb * S, D)
    mem2d = mem.reshape(nb * mem.shape[1], D)
    bias = _bias_call(rel_bias)
    v_first = None
    for l in range(w_in.shape[0]):
        qa, ka, va, qm, r, lw, k, v, a, b, gate = _proj_call(xt, p, l, v_first, nb=nb)
        if l == 0:
            v_first = v
        y = _rwkv_scan(r, lw, k, v, a, b, nb=nb)
        mk, mv = _memkv_call(mem2d, p, l)
        xt = _mix_call(xt, qa, ka, va, qm, mk, mv, y, r, k, v, gate, bias, p, l, nb=nb)
        xt = _ffn_call(xt, p, l, nb=nb)
    return xt.reshape(nb, S, D)
```

```python
import functools
import math

import jax
import jax.numpy as jnp
import numpy as np
from jax import lax
from jax.experimental import pallas as pl
from jax.experimental.pallas import tpu as pltpu

F32 = jnp.float32
BF16 = jnp.bfloat16

HEAD_DIM = 64
SCAN_CHUNK = 64
VMEM_LIMIT_BYTES = 56 * 1024 * 1024


def _dot(a, b):
    return jnp.dot(a, b, preferred_element_type=F32)


def _dot_nt(a, b):
    return lax.dot_general(a, b, (((1,), (1,)), ((), ())), preferred_element_type=F32)


def _dot_tn(a, b):
    return lax.dot_general(a, b, (((0,), (0,)), ((), ())), preferred_element_type=F32)


def _bf(x):
    return x.astype(BF16)


SCAN_GROUP_HEADS = 4
SCAN_GROUP_W = SCAN_GROUP_HEADS * HEAD_DIM
SCAN_CHUNKS_PER_STEP = 4


_SCAN_STASH = (("w2", 1, F32), ("y_v", 1, F32), ("m_rb", 1, F32), ("v", 1, F32), ("w1r", 2, F32),
               ("bk", 2, F32), ("g_row", 1, F32))


def _scan_kernel(r_ref, lw_ref, k_ref, v_ref, a_ref, b_ref, bdm_ref, y_ref, h_ref, *stash_refs, nb, nch):
    C, N, GW, HPG = SCAN_CHUNK, HEAD_DIM, SCAN_GROUP_W, SCAN_GROUP_HEADS
    ngroups = h_ref.shape[0]
    stash = {name: ref for (name, _, _), ref in zip(_SCAN_STASH, stash_refs)}

    @pl.when(pl.program_id(0) == 0)
    def _():
        h_ref[...] = jnp.zeros_like(h_ref)
        for ref in stash_refs:
            ref[...] = jnp.zeros_like(ref)

    trow = lax.broadcasted_iota(jnp.int32, (C, C), 0)
    tcol = lax.broadcasted_iota(jnp.int32, (C, C), 1)
    tri = (trow >= tcol).astype(BF16)
    grow = lax.broadcasted_iota(jnp.int32, (C, GW), 0)
    gcol = lax.broadcasted_iota(jnp.int32, (C, GW), 1) % N
    incl = grow >= gcol
    strict = grow > gcol
    eye = (grow == gcol).astype(F32)
    bdm = bdm_ref[...]

    def bd(x):
        return jnp.concatenate([_bf(x)] * HPG, axis=0) * bdm

    def split2(x):
        hi = _bf(x)
        return hi, _bf(x - hi.astype(F32))

    def chunk_units(c):
        rows = slice(c * C, (c + 1) * C)
        per_batch = []
        for b in range(nb):
            lw = lw_ref[b, rows, :]
            l1 = _bf(lw)
            e1 = lw - l1.astype(F32)
            l2 = _bf(e1)
            l3 = _bf(e1 - l2.astype(F32))
            cum = _dot(tri, l1) + _dot(tri, l2) + _dot(tri, l3)
            cum_last = cum[C - 1:C, :]
            g_inv = jnp.exp(-cum)
            g_out = jnp.exp(cum_last - cum)
            f32 = lambda ref: ref[b, rows, :].astype(F32)
            kk, bb = f32(k_ref), f32(b_ref)
            per_batch.append(dict(
                a_t=f32(a_ref) * jnp.exp(cum - lw), r_t=f32(r_ref) * jnp.exp(cum),
                b_t=bb * g_inv, k_t=kk * g_inv, b_h=bb * g_out, k_h=kk * g_out,
                g_c=jnp.broadcast_to(jnp.exp(cum_last), cum.shape), v=f32(v_ref)))
        cat = {n: jnp.concatenate([pb[n] for pb in per_batch], axis=1) for n in per_batch[0]}
        return [{n: t[:, g * GW:(g + 1) * GW] for n, t in cat.items()} for g in range(ngroups)]

    def prepare(chunks):
        units = [u for c in chunks for u in chunk_units(c)]

        nu = len(units)
        ar = [jnp.concatenate([_bf(u["a_t"]), _bf(u["r_t"])], axis=0) for u in units]
        pb = [_dot_nt(ar[i], bd(units[i]["b_t"])) for i in range(nu)]
        pk = [_dot_nt(ar[i], bd(units[i]["k_t"])) for i in range(nu)]
        l_ab = [jnp.where(strict, p[:C], 0.0) for p in pb]
        m_rb = [_bf(jnp.where(incl, p[C:], 0.0)) for p in pb]
        tril2 = jnp.concatenate([strict, incl], axis=0)
        lm_k = [_bf(jnp.where(tril2, p, 0.0)) for p in pk]
        yield
        t_inv = [eye + l for l in l_ab]
        pw = [_dot(_bf(l), bd(l)) for l in l_ab]
        yield
        n_rounds = int(math.log2(C)) - 1
        for rnd in range(n_rounds):
            if rnd + 1 < n_rounds:
                z = [_dot(jnp.concatenate([_bf(p), _bf(t)], axis=0), bd(p)) for p, t in zip(pw, t_inv)]
                pw = [zz[:C] for zz in z]
                t_inv = [t + zz[C:] for t, zz in zip(t_inv, z)]
            else:
                t_inv = [t + _dot(_bf(t), bd(p)) for t, p in zip(t_inv, pw)]
            yield
        t_b = [_bf(t) for t in t_inv]
        w1 = [_dot(t, bd(u["a_t"])) for t, u in zip(t_b, units)]
        lmv = [_dot(m, bd(u["v"])) for m, u in zip(lm_k, units)]
        yield
        w2 = [_dot(t, bd(x[:C])) for t, x in zip(t_b, lmv)]
        parts = [p for u in units for p in split2(u["g_c"] * eye)]
        sums = _dot(jnp.concatenate(parts, axis=0), bdm)
        yield
        for i, u in enumerate(units):
            put = lambda name, val: stash[name].__setitem__(i, val.astype(stash[name].dtype))
            put("w2", w2[i])
            put("y_v", lmv[i][C:])
            put("m_rb", m_rb[i])
            put("v", u["v"])
            put("w1r", jnp.concatenate([w1[i], u["r_t"]], axis=0))
            put("bk", jnp.concatenate([u["b_h"], u["k_h"]], axis=0))
            put("g_row", sums[2 * i * C:(2 * i + 1) * C] + sums[(2 * i + 1) * C:(2 * i + 2) * C])
        yield

    state = [h_ref[g] for g in range(ngroups)]

    def advance(c):
        p = {n: [ref[c * ngroups + g] for g in range(ngroups)] for n, ref in stash.items()}
        for n in ("m_rb", "v", "w1r", "bk"):
            p[n] = [_bf(x) for x in p[n]]
        z = [_dot(w, bd(s)) for w, s in zip(p["w1r"], state)]
        yield
        u = [zz[:C] + w for zz, w in zip(z, p["w2"])]
        y = [zz[C:] + _dot(m, bd(x)) + yv for zz, m, x, yv in zip(z, p["m_rb"], u, p["y_v"])]
        uv = [jnp.concatenate([_bf(x), v], axis=0) for x, v in zip(u, p["v"])]
        full = [_dot_tn(b, x) * bdm.astype(F32) for b, x in zip(p["bk"], uv)]
        yield
        upd = [sum(f[h * N:(h + 1) * N] for h in range(HPG)) for f in full]
        state[:] = [g * s + d for g, s, d in zip(p["g_row"], state, upd)]
        ycat = jnp.concatenate(y, axis=1)
        w = y_ref.shape[2]
        for b in range(nb):
            y_ref[b, c * C:(c + 1) * C, :] = ycat[:, b * w:(b + 1) * w]
        yield

    def run(gen, stages):
        for _ in range(stages):
            next(gen, None)

    prep = prepare(list(range(nch)))
    n_prep = 5 + int(math.log2(C)) - 1
    for c in range(nch):
        seq = advance(c)
        run(prep, 1)
        run(seq, 1)
        run(prep, 1)
        run(seq, 2)
    run(prep, n_prep)
    for g in range(ngroups):
        h_ref[g] = state[g]


def _rwkv_scan(r, lw, k, v, a, b, *, nb):
    T, W = r.shape
    S = T // nb
    C, nch, GW = SCAN_CHUNK, SCAN_CHUNKS_PER_STEP, SCAN_GROUP_W
    assert S % (C * nch) == 0 and (nb * W) % GW == 0 and W % 128 == 0
    ngroups = nb * W // GW
    head = np.arange(GW) // HEAD_DIM
    bdm = jnp.asarray((head[:, None] == head[None, :]).astype(np.float32), dtype=BF16)
    n_blocks = S // (C * nch)
    in_spec = pl.BlockSpec((nb, C * nch, W), lambda s: (0, jnp.minimum(s, n_blocks - 1), 0))
    out_spec = pl.BlockSpec((nb, C * nch, W), lambda s: (0, jnp.maximum(s - 1, 0), 0))
    args = [t.reshape(nb, S, W) for t in (r, lw, k, v, a, b)]
    y = pl.pallas_call(
        functools.partial(_scan_kernel, nb=nb, nch=nch),
        grid=(n_blocks + 1,),
        in_specs=[in_spec] * 6 + [pl.BlockSpec((GW, GW), lambda s: (0, 0))],
        out_specs=out_spec,
        out_shape=jax.ShapeDtypeStruct((nb, S, W), F32),
        scratch_shapes=[pltpu.VMEM((ngroups, HEAD_DIM, GW), F32)]
        + [pltpu.VMEM((nch * ngroups, rows * C, GW), dt) for _, rows, dt in _SCAN_STASH],
        compiler_params=pltpu.CompilerParams(dimension_semantics=("arbitrary",), vmem_limit_bytes=VMEM_LIMIT_BYTES),
        name="rwkv_scan",
    )(*args, bdm)
    return y.reshape(T, W)


D_MODEL = 1024
ATTN_HEADS = 6
ATTN_KV_HEADS = 2
ATTN_GROUP = ATTN_HEADS // ATTN_KV_HEADS
ATTN_W = ATTN_HEADS * HEAD_DIM
KV_W = ATTN_KV_HEADS * HEAD_DIM
BLOCK_Q = 128
WINDOW = 128
N_BUCKETS = 32
MAX_EXACT = N_BUCKETS // 2
RWKV_W = 6 * HEAD_DIM
LORA_W = 128
SHIFT_W = 3 * RWKV_W + LORA_W
MEM_HEADS = 4
MEM_W = MEM_HEADS * HEAD_DIM
IN_BASE = ATTN_W + 2 * KV_W + SHIFT_W + MEM_W
PB_OFF = ATTN_W + 2 * KV_W
QM_OFF = PB_OFF + SHIFT_W
D_FF = 2816
EPS = 1e-6
GN_EPS = 64e-5
L2_EPS = 1e-12
MXU_WIDTH = 256

PROJ_TM = 1024
PROJ_SUB = 256
MIX_TQ = 512
FFN_TM = 1024
FFN_FC = 256
FFN_STAGE_CHUNKS = 8
WEIGHT_STAGE_CHUNKS = 4


def _block_diag_ones(width):
    idx = np.arange(width) // HEAD_DIM
    return jnp.asarray((idx[:, None] == idx[None, :]).astype(np.float32), dtype=BF16)


def _head_sum(t, bd):
    tb = _bf(t)
    width = t.shape[1]
    parts = [_dot(tb[:, lo:min(lo + MXU_WIDTH, width)], bd[lo:min(lo + MXU_WIDTH, width), lo:min(lo + MXU_WIDTH, width)])
             for lo in range(0, width, MXU_WIDTH)]
    return parts[0] if len(parts) == 1 else jnp.concatenate(parts, axis=1)


def _head_rms(t, bd, gain):
    ms = _head_sum(t * t, bd) * (1.0 / HEAD_DIM)
    return t * lax.rsqrt(ms + EPS) * gain


def _rms_rows(x, g):
    ms = jnp.mean(x * x, axis=-1, keepdims=True)
    return x * lax.rsqrt(ms + EPS) * g


class _Row:
    def __init__(self, table_ref, row, width):
        self.table_ref, self.row, self.width = table_ref, row, width

    def __getitem__(self, _):
        return self.table_ref[self.row:self.row + 1, 0:self.width]


def _proj_kernel(*refs, tm, sub, tiles_per_batch, has_vres, layer):
    it = iter(refs)
    x_ref, vec_ref, w_hbm, lora_ref = next(it), next(it), next(it), next(it)
    bd384_ref, bd128_ref, bd256_ref = next(it), next(it), next(it)
    if has_vres:
        wv_ref, vfirst_ref = next(it), next(it)
    qa_ref, ka_out_ref, va_ref, qm_ref = next(it), next(it), next(it), next(it)
    r_ref, lw_ref, k_ref, v_ref, a_ref, b_ref, gate_ref = (next(it) for _ in range(7))
    pbs_ref, w_ref, w_stage, w_sem = next(it), next(it), next(it), next(it)
    sw = pbs_ref.shape[1]

    @pl.when(pl.program_id(0) == 0)
    def _():
        _stage_weights(w_hbm, layer, w_ref, w_stage, w_sem)
        if has_vres:
            w_ref[:, IN_BASE:IN_BASE + LORA_W] = _bf(wv_ref[...])

    g_ref, qg_ref, kg_ref, mg_ref = (_Row(vec_ref, VEC_ROWS[n], w) for n, w in (
        ("mix_norm_g", D_MODEL), ("q_gain", ATTN_W), ("k_gain", KV_W), ("mq_gain", MEM_W)))
    mu_ref = _Row(vec_ref, VEC_ROWS["mu"], sw)
    w0_ref, a0_ref, kk_ref, ka_ref, v0_ref = (_Row(vec_ref, VEC_ROWS[n], RWKV_W) for n in ("w0", "a0", "k_k", "k_a", "v0"))
    w2_ref, a2_ref, g2_ref, v2_ref = (lora_ref.at[j] for j in range(4))

    @pl.when(pl.program_id(0) % tiles_per_batch == 0)
    def _():
        pbs_ref[0:8, :] = jnp.zeros((8, sw), F32)

    def project(i):
        rows = slice(i * sub, (i + 1) * sub)
        return _dot(_bf(_rms_rows(x_ref[rows, :], g_ref[...])), w_ref[...])

    def attention_outputs(i, proj):
        rows = slice(i * sub, (i + 1) * sub)
        qa_ref[rows, :] = _bf(_head_rms(proj[:, :ATTN_W], bd384_ref[...], qg_ref[...]))
        ka_out_ref[rows, :] = _bf(_head_rms(proj[:, ATTN_W:ATTN_W + KV_W], bd128_ref[...], kg_ref[...]))
        va_ref[rows, :] = _bf(proj[:, ATTN_W + KV_W:PB_OFF])
        qm_ref[rows, :] = _bf(_head_rms(proj[:, QM_OFF:QM_OFF + MEM_W], bd256_ref[...], mg_ref[...]))

    def rwkv_outputs(i, proj):
        rows = slice(i * sub, (i + 1) * sub)
        lo = 8 + i * sub
        pbs_ref[lo:lo + sub, 0:SHIFT_W] = proj[:, PB_OFF:PB_OFF + SHIFT_W]
        if has_vres:
            pbs_ref[lo:lo + sub, SHIFT_W:sw] = proj[:, IN_BASE:IN_BASE + LORA_W]
        cur = pbs_ref[lo:lo + sub, :]
        prev = pbs_ref[lo - 1:lo - 1 + sub, :]
        sh = cur + mu_ref[...] * (prev - cur)
        r = sh[:, 0:RWKV_W]
        k = sh[:, RWKV_W:2 * RWKV_W]
        v = sh[:, 2 * RWKV_W:3 * RWKV_W]
        z = sh[:, 3 * RWKV_W:SHIFT_W]
        t = w0_ref[...] + _dot(_bf(jnp.tanh(z)), w2_ref[...])
        lw_ref[rows, :] = -math.exp(-0.5) * jax.nn.sigmoid(t)
        a = jax.nn.sigmoid(a0_ref[...] + _dot(_bf(z), a2_ref[...]))
        gate_ref[rows, :] = _bf(_dot(_bf(jax.nn.sigmoid(z)), g2_ref[...]))
        if has_vres:
            vd = sh[:, SHIFT_W:sw]
            v = v + (vfirst_ref[rows, :].astype(F32) - v) * jax.nn.sigmoid(v0_ref[...] + _dot(_bf(vd), v2_ref[...]))
        kk = k * kk_ref[...]
        kk = kk / jnp.maximum(jnp.sqrt(_head_sum(kk * kk, bd384_ref[...])), L2_EPS)
        r_ref[rows, :] = _bf(r)
        k_ref[rows, :] = _bf(k * (1.0 + (a - 1.0) * ka_ref[...]))
        v_ref[rows, :] = _bf(v)
        a_ref[rows, :] = _bf(-kk)
        b_ref[rows, :] = _bf(kk * a)

    n_sub = tm // sub
    proj = project(0)
    for i in range(n_sub):
        attention_outputs(i, proj)
        nxt = project(i + 1) if i + 1 < n_sub else None
        rwkv_outputs(i, proj)
        proj = nxt
    pbs_ref[0:8, :] = pbs_ref[tm:tm + 8, :]


def _layer_spec(stacked, l):
    return pl.BlockSpec((None,) + stacked.shape[1:], lambda i: (l,) + (0,) * (stacked.ndim - 1),
                        pipeline_mode=pl.Buffered(1))


VEC_W = SHIFT_W + LORA_W
VEC_ROWS = {n: i for i, n in enumerate((
    "mix_norm_g", "q_gain", "k_gain", "mq_gain", "mu", "w0", "a0", "k_k", "k_a", "v0", "ln_w", "ln_b", "r_k",
    "ffn_norm_g"))}
VEC_TABLE_ROWS = 16


_PACKED = ("mix_norm_g", "attn_q_norm", "attn_k_norm", "mem_q_norm", "rwkv_mu", "rwkv_mu_vres", "rwkv_w0", "rwkv_a0",
           "rwkv_k_k", "rwkv_k_a", "rwkv_v0", "rwkv_ln_w", "rwkv_ln_b", "rwkv_r_k", "ffn_norm_g",
           "rwkv_w2", "rwkv_a2", "rwkv_g2", "rwkv_v2")


def _pack_kernel(*refs, layers):
    src = dict(zip(_PACKED, refs))
    sinks_ref, vec_ref, lora_ref, sink_ref = refs[len(_PACKED):]
    scale = HEAD_DIM ** -0.5
    vec_ref[...] = jnp.zeros_like(vec_ref)
    lora_ref[...] = jnp.zeros_like(lora_ref)
    tiled = lambda v, n: jnp.concatenate([v] * n, axis=1)

    def put(l, name, val):
        vec_ref[l, VEC_ROWS[name]:VEC_ROWS[name] + 1, 0:val.shape[1]] = val

    for l in range(layers):
        row = lambda n, at=l: src[n][at:at + 1, :]
        put(l, "mix_norm_g", row("mix_norm_g"))
        put(l, "q_gain", tiled(row("attn_q_norm"), ATTN_HEADS) * scale)
        put(l, "k_gain", tiled(row("attn_k_norm"), ATTN_KV_HEADS))
        put(l, "mq_gain", tiled(row("mem_q_norm"), MEM_HEADS) * scale)
        put(l, "mu", row("rwkv_mu"))
        for name, key in (("w0", "rwkv_w0"), ("a0", "rwkv_a0"), ("k_k", "rwkv_k_k"), ("k_a", "rwkv_k_a"),
                          ("ln_w", "rwkv_ln_w"), ("ln_b", "rwkv_ln_b"), ("ffn_norm_g", "ffn_norm_g")):
            put(l, name, row(key))
        r_k = src["rwkv_r_k"][l]
        put(l, "r_k", jnp.concatenate([r_k[h:h + 1, :] for h in range(r_k.shape[0])], axis=1))
        for j, (key, at) in enumerate((("rwkv_w2", 0), ("rwkv_a2", 32), ("rwkv_g2", 64))):
            w = src[key][l]
            lora_ref[l, j, at:at + w.shape[0], :] = _bf(w)
        if l > 0:
            mu_v = row("rwkv_mu_vres", l - 1)
            vec_ref[l, VEC_ROWS["mu"]:VEC_ROWS["mu"] + 1, SHIFT_W:SHIFT_W + mu_v.shape[1]] = mu_v
            put(l, "v0", row("rwkv_v0", l - 1))
            v2 = src["rwkv_v2"][l - 1]
            lora_ref[l, 3, 0:v2.shape[0], :] = _bf(v2)
        for h in range(ATTN_HEADS):
            g, i = divmod(h, ATTN_GROUP)
            sink_ref[l, g, i * BLOCK_Q:(i + 1) * BLOCK_Q, :] = jnp.full((BLOCK_Q, 1), sinks_ref[l, h], F32)


def _pack_params(p):
    layers = p["w_in"].shape[0]
    srcs = [p[n].astype(F32) for n in _PACKED]
    vm = pl.BlockSpec(memory_space=pltpu.VMEM)
    return pl.pallas_call(
        functools.partial(_pack_kernel, layers=layers),
        in_specs=[vm] * len(srcs) + [pl.BlockSpec(memory_space=pltpu.SMEM)],
        out_specs=[vm, vm, vm],
        out_shape=[jax.ShapeDtypeStruct((layers, VEC_TABLE_ROWS, VEC_W), F32),
                   jax.ShapeDtypeStruct((layers, 4, LORA_W, RWKV_W), BF16),
                   jax.ShapeDtypeStruct((layers, ATTN_KV_HEADS, ATTN_GROUP * BLOCK_Q, 1), F32)],
        name="pack_params",
    )(*srcs, p["attn_sinks"].astype(F32))


def _proj_call(x, p, l, v_first, *, nb):
    T, D = x.shape
    tm = PROJ_TM
    assert T % tm == 0 and (T // nb) % tm == 0
    has_vres = l > 0
    w = p["w_in"]
    ins = [x, p["vectors"], w, p["lora"], _block_diag_ones(ATTN_W), _block_diag_ones(KV_W), _block_diag_ones(MEM_W)]
    if has_vres:
        ins += [jnp.pad(p["w_in_vres"][l - 1], ((0, 0), (0, LORA_W - 16))), v_first]
    nw = IN_BASE + (LORA_W if has_vres else 0)

    def spec(a):
        if a is w:
            return pl.BlockSpec(memory_space=pl.ANY)
        if a.ndim >= 3:
            return _layer_spec(a, l)
        if a.shape[0] == T:
            return pl.BlockSpec((tm, a.shape[1]), lambda i: (i, 0))
        return pl.BlockSpec(a.shape, lambda i: (0, 0), pipeline_mode=pl.Buffered(1))

    out_shapes = [jax.ShapeDtypeStruct((T, ATTN_W), BF16), jax.ShapeDtypeStruct((T, KV_W), BF16),
                  jax.ShapeDtypeStruct((T, KV_W), BF16), jax.ShapeDtypeStruct((T, MEM_W), BF16)]
    out_shapes += [jax.ShapeDtypeStruct((T, RWKV_W), F32 if j == 1 else BF16) for j in range(7)]
    sw = SHIFT_W + (LORA_W if has_vres else 0)
    return pl.pallas_call(
        functools.partial(_proj_kernel, tm=tm, sub=PROJ_SUB, tiles_per_batch=(T // nb) // tm, has_vres=has_vres,
                          layer=l),
        grid=(T // tm,),
        in_specs=[spec(a) for a in ins],
        out_specs=[pl.BlockSpec((tm, s.shape[1]), lambda i: (i, 0)) for s in out_shapes],
        out_shape=out_shapes,
        scratch_shapes=[pltpu.VMEM((tm + 8, sw), F32), pltpu.VMEM((D, nw), BF16),
                        pltpu.VMEM((2, D // WEIGHT_STAGE_CHUNKS, IN_BASE), F32), pltpu.SemaphoreType.DMA((2,))],
        compiler_params=pltpu.CompilerParams(dimension_semantics=("arbitrary",), vmem_limit_bytes=VMEM_LIMIT_BYTES),
        name=f"proj_l{l}",
    )(*ins)


def _bucket_table():
    qi = np.arange(BLOCK_Q)[:, None]
    kj = np.arange(2 * BLOCK_Q)[None, :]
    dist = qi + BLOCK_Q - kj
    in_band = (dist >= 0) & (dist < WINDOW)
    d = np.maximum(dist, 1).astype(np.float32)
    large = MAX_EXACT + (np.log(d / np.float32(MAX_EXACT)) / np.float32(math.log(WINDOW / MAX_EXACT))
                         * np.float32(N_BUCKETS - MAX_EXACT)).astype(np.int32)
    large = np.minimum(large, N_BUCKETS - 1)
    bucket = np.where(dist < MAX_EXACT, np.maximum(dist, 0), large)
    return np.where(in_band, bucket, -1).astype(np.int32)


def _bias_kernel(rb_ref, bucket_ref, out_ref):
    bucket = bucket_ref[...]
    for h in range(ATTN_HEADS):
        acc = jnp.full(bucket.shape, -jnp.inf, F32)
        for j in range(N_BUCKETS):
            acc = jnp.where(bucket == j, rb_ref[j, h], acc)
        out_ref[h] = acc


def _bias_call(rel_bias):
    tab = pl.pallas_call(
        _bias_kernel,
        in_specs=[pl.BlockSpec(memory_space=pltpu.SMEM), pl.BlockSpec(memory_space=pltpu.VMEM)],
        out_specs=pl.BlockSpec(memory_space=pltpu.VMEM),
        out_shape=jax.ShapeDtypeStruct((ATTN_HEADS, BLOCK_Q, 2 * BLOCK_Q), F32),
        name="rel_bias_table",
    )(rel_bias.astype(F32), jnp.asarray(_bucket_table()))
    return tab.reshape(ATTN_KV_HEADS, ATTN_GROUP * BLOCK_Q, 2 * BLOCK_Q)


def _memkv_kernel(mem_ref, g_ref, w_ref, kn_ref, bd_ref, mk_ref, mv_ref):
    mem = mem_ref[...]
    for l in range(w_ref.shape[0]):
        hn = _bf(_rms_rows(mem, g_ref[l:l + 1, :]))
        kv = _dot(hn, _bf(w_ref[l]))
        gain = jnp.concatenate([kn_ref[l:l + 1, :]] * MEM_HEADS, axis=1)
        mk_ref[l] = _bf(_head_rms(kv[:, :MEM_W], bd_ref[...], gain))
        mv_ref[l] = _bf(kv[:, MEM_W:])


def _memkv_call(mem2d, p):
    rows = mem2d.shape[0]
    layers = p["w_mem_kv"].shape[0]
    vm = pl.BlockSpec(memory_space=pltpu.VMEM)
    return pl.pallas_call(
        _memkv_kernel,
        in_specs=[vm] * 5,
        out_specs=[vm, vm],
        out_shape=[jax.ShapeDtypeStruct((layers, rows, MEM_W), BF16)] * 2,
        compiler_params=pltpu.CompilerParams(vmem_limit_bytes=VMEM_LIMIT_BYTES),
        name="mem_kv",
    )(mem2d, p["mem_norm_g"], p["w_mem_kv"], p["mem_k_norm"], _block_diag_ones(MEM_W))


def _mix_kernel(x_ref, qa_ref, kc_ref, kp_ref, vc_ref, vp_ref, bias_ref, sink_ref, qm_ref, mk_ref, mv_ref,
                y_ref, r_ref, k_ref, v_ref, gate_ref, vec_ref, bd_ref, wout_hbm, out_ref, att_ref,
                wout_ref, w_stage, w_sem, *, tq, tiles_per_batch, layer):
    lnw_ref, lnb_ref, rk_ref = (_Row(vec_ref, VEC_ROWS[n], RWKV_W) for n in ("ln_w", "ln_b", "r_k"))
    N, BQ = HEAD_DIM, BLOCK_Q
    seq_start = pl.program_id(0) % tiles_per_batch == 0
    nqb = tq // BQ

    @pl.when(pl.program_id(0) == 0)
    def _():
        att_ref[...] = jnp.zeros_like(att_ref)
        _stage_weights(wout_hbm, layer, wout_ref, w_stage, w_sem)

    qa = qa_ref[...]
    kall = jnp.concatenate([kp_ref[...], kc_ref[...]], axis=0)
    vall = jnp.concatenate([vp_ref[...], vc_ref[...]], axis=0)
    k_group = [kall[:, g * N:(g + 1) * N] for g in range(ATTN_KV_HEADS)]
    v_group = [vall[:, g * N:(g + 1) * N] for g in range(ATTN_KV_HEADS)]
    before_seq = lax.broadcasted_iota(jnp.int32, (ATTN_GROUP * BQ, 2 * BQ), 1) < BQ

    qm = qm_ref[...]
    mk = mk_ref[0]
    mv = mv_ref[0]

    swa = [(j, g) for j in range(nqb) for g in range(ATTN_KV_HEADS)]
    logits, values, sinks = [], [], []
    for j, g in swa:
        qg = jnp.concatenate(
            [qa[j * BQ:(j + 1) * BQ, (ATTN_GROUP * g + i) * N:(ATTN_GROUP * g + i + 1) * N] for i in range(ATTN_GROUP)],
            axis=0)
        lg = _dot_nt(qg, k_group[g][j * BQ:(j + 2) * BQ]) + bias_ref[g]
        if j == 0:
            lg = jnp.where(jnp.logical_and(seq_start, before_seq), -jnp.inf, lg)
        logits.append(lg)
        values.append(v_group[g][j * BQ:(j + 2) * BQ])
        sinks.append(sink_ref[g])
    for h in range(MEM_HEADS):
        hs = slice(h * N, (h + 1) * N)
        logits.append(_dot_nt(qm[:, hs], mk[:, hs]))
        values.append(mv[:, hs])
        sinks.append(None)
    bd = bd_ref[...]
    y = y_ref[...]
    d = y - _head_sum(y, bd) * (1.0 / N)
    var = _head_sum(d * d, bd) * (1.0 / N)
    yn = d * lax.rsqrt(var + GN_EPS) * lnw_ref[...] + lnb_ref[...]
    f32 = lambda ref: ref[...].astype(F32)
    bonus = _head_sum(f32(r_ref) * f32(k_ref) * rk_ref[...], bd) * f32(v_ref)
    out_b = (yn + bonus) * f32(gate_ref)
    mixed = jnp.concatenate([att_ref[:, :ATTN_W], _bf(out_b), att_ref[:, ATTN_W:]], axis=-1)
    out_ref[...] = x_ref[...] + _dot(mixed, wout_ref[...])

    row_max = [jnp.max(lg, axis=-1, keepdims=True) for lg in logits]
    m = [rm if s is None else jnp.maximum(rm, s) for rm, s in zip(row_max, sinks)]
    e = [jnp.exp(lg - mm) for lg, mm in zip(logits, m)]
    denom = [jnp.sum(ee, axis=-1, keepdims=True) for ee in e]
    denom = [d if s is None else d + jnp.exp(s - mm) for d, s, mm in zip(denom, sinks, m)]
    outs = [_bf(_dot(_bf(ee), vv) / d) for ee, vv, d in zip(e, values, denom)]

    head_rows = [[None] * nqb for _ in range(ATTN_HEADS)]
    for (j, g), o in zip(swa, outs):
        for i in range(ATTN_GROUP):
            head_rows[ATTN_GROUP * g + i][j] = o[i * BQ:(i + 1) * BQ]
    att_ref[:, :ATTN_W] = jnp.concatenate([jnp.concatenate(rows, axis=0) for rows in head_rows], axis=-1)
    att_ref[:, ATTN_W:] = jnp.concatenate(outs[len(swa):], axis=-1)


def _mix_call(x, qa, ka, va, qm, mk, mv, y, r, k, v, gate, bias, p, l, *, nb):
    T, D = x.shape
    tq = MIX_TQ
    S = T // nb
    assert S % tq == 0 and tq % BLOCK_Q == 0
    tpb = S // tq
    qpb = tq // BLOCK_Q
    mem_tokens = mk.shape[1] // nb
    sink = p["sink_columns"]
    n_tiles = T // tq
    att = lambda i: jnp.minimum(i, n_tiles - 1)
    fin = lambda i: jnp.maximum(i - 1, 0)
    att_tile = lambda w: pl.BlockSpec((tq, w), lambda i: (att(i), 0))
    tile = lambda w: pl.BlockSpec((tq, w), lambda i: (fin(i), 0))
    prev = pl.BlockSpec((BLOCK_Q, KV_W), lambda i: (jnp.maximum(att(i) * qpb - 1, 0), 0))
    full = lambda a: pl.BlockSpec(a.shape, lambda i: (0,) * a.ndim)
    memspec = pl.BlockSpec((None, 1, mem_tokens, MEM_W), lambda i: (l, att(i) // tpb, 0, 0))
    bd = _block_diag_ones(RWKV_W)
    wout = p["w_out"]
    mk3 = mk.reshape(mk.shape[0], nb, mem_tokens, MEM_W)
    mv3 = mv.reshape(mv.shape[0], nb, mem_tokens, MEM_W)
    return pl.pallas_call(
        functools.partial(_mix_kernel, tq=tq, tiles_per_batch=tpb, layer=l),
        grid=(n_tiles + 1,),
        in_specs=[tile(D), att_tile(ATTN_W), att_tile(KV_W), prev, att_tile(KV_W), prev, full(bias), _layer_spec(sink, l),
                  att_tile(MEM_W), memspec, memspec] + [tile(RWKV_W)] * 5
                 + [_layer_spec(p["vectors"], l), full(bd), pl.BlockSpec(memory_space=pl.ANY)],
        out_specs=tile(D),
        out_shape=jax.ShapeDtypeStruct((T, D), F32),
        scratch_shapes=[pltpu.VMEM((tq, ATTN_W + MEM_W), BF16), pltpu.VMEM((D, D), BF16),
                        pltpu.VMEM((2, D // WEIGHT_STAGE_CHUNKS, D), F32), pltpu.SemaphoreType.DMA((2,))],
        compiler_params=pltpu.CompilerParams(dimension_semantics=("arbitrary",), vmem_limit_bytes=VMEM_LIMIT_BYTES),
        name=f"mix_l{l}",
    )(x, qa, ka, ka, va, va, bias, sink, qm, mk3, mv3, y, r, k, v, gate, p["vectors"], bd, wout)


SUBLANES = 8


def _tile_copies(hbm, buf, sem, tile, slot, to_vmem):
    copies = []
    for s in range(SUBLANES):
        src, dst = hbm.at[tile, s], buf.at[slot, :, s, :]
        if not to_vmem:
            src, dst = dst, src
        copies.append(pltpu.make_async_copy(src, dst, sem.at[slot, s]))
    return copies


def _shift_rows(u, before, steps):
    first = lax.broadcasted_iota(jnp.int32, (SUBLANES, u.shape[1]), 0) == 0
    tm = u.shape[0]
    heads = []
    for i in range(steps):
        lo = tm - (steps - i) * SUBLANES
        wrapped = pltpu.roll(u[lo:lo + SUBLANES], 1, axis=0)
        prior = jnp.broadcast_to(before[SUBLANES - steps + i:SUBLANES - steps + i + 1], wrapped.shape)
        heads.append(jnp.where(first, prior, wrapped))
    return jnp.concatenate(heads + [u[:tm - steps * SUBLANES]], axis=0)


def _stage_weights(w_hbm, layer, dst_ref, stage_ref, sem):
    rows = stage_ref.shape[1]
    n = dst_ref.shape[0] // rows
    copy = lambda c: pltpu.make_async_copy(w_hbm.at[layer, pl.ds(c * rows, rows), :], stage_ref.at[c % 2], sem.at[c % 2])
    copy(0).start()
    for c in range(n):
        if c + 1 < n:
            copy(c + 1).start()
        copy(c).wait()
        dst_ref[c * rows:(c + 1) * rows, 0:stage_ref.shape[2]] = _bf(stage_ref[c % 2])


def _ffn_kernel(x_hbm, vec_ref, wup_hbm, cw_ref, cb_ref, wdn_hbm, o_hbm, xbuf, obuf, carry_ref, act_ref,
                wup_ref, wdn_ref, up_stage, dn_stage, in_sem, out_sem, w_sem, *, tm, tiles_per_batch, n_tiles, layer):
    D = xbuf.shape[-1]
    g_ref = _Row(vec_ref, VEC_ROWS["ffn_norm_g"], D)
    q = tm // SUBLANES
    i = pl.program_id(0)
    slot = i % 2

    @pl.when(i == 0)
    def _():
        carry_ref[...] = jnp.zeros_like(carry_ref)
        for cp in _tile_copies(x_hbm, xbuf, in_sem, 0, 0, True):
            cp.start()
        _stage_weights(wup_hbm, layer, wup_ref, up_stage, w_sem)
        _stage_weights(wdn_hbm, layer, wdn_ref, dn_stage, w_sem)

    @pl.when(i + 1 < n_tiles)
    def _():
        for cp in _tile_copies(x_hbm, xbuf, in_sem, i + 1, 1 - slot, True):
            cp.start()

    for cp in _tile_copies(x_hbm, xbuf, in_sem, i, slot, True):
        cp.wait()

    @pl.when(i >= 2)
    def _():
        for cp in _tile_copies(o_hbm, obuf, out_sem, i - 2, slot, False):
            cp.wait()

    seq_start = i % tiles_per_batch == 0
    g = g_ref[...]
    x = xbuf[slot].reshape(tm, D)
    before = jnp.where(seq_start, 0.0, _rms_rows(carry_ref[...], g))
    for r, grp in ((SUBLANES - 2, q - 2), (SUBLANES - 1, q - 1)):
        row = grp * SUBLANES + SUBLANES - 1
        carry_ref[r:r + 1, :] = x[row:row + 1]
    h_ext = _bf(jnp.concatenate([before, _rms_rows(x, g)], axis=0))

    n_chunks = D_FF // FFN_FC
    gate_cols = lambda c: slice(c * FFN_FC, (c + 1) * FFN_FC)
    val_cols = lambda c: slice(D_FF + c * FFN_FC, D_FF + (c + 1) * FFN_FC)

    def conv(u_ext, cols):
        w = cw_ref[:, cols]
        u = u_ext[SUBLANES:]
        return (cb_ref[layer:layer + 1, cols] + w[0:1] * _shift_rows(u, u_ext[:SUBLANES], 2)
                + w[1:2] * _shift_rows(u, u_ext[:SUBLANES], 1) + w[2:3] * u)

    up = lambda c: (_dot(h_ext, wup_ref[:, gate_cols(c)]), _dot(h_ext, wup_ref[:, val_cols(c)]))
    nxt = up(0)
    for c in range(n_chunks):
        ug, uv = nxt
        if c + 1 < n_chunks:
            nxt = up(c + 1)
        gt = conv(ug, gate_cols(c))
        act_ref[:, gate_cols(c)] = _bf(gt * jax.nn.sigmoid(gt) * conv(uv, val_cols(c)))
    obuf[slot] = (x + _dot(act_ref[...], wdn_ref[...])).reshape(q, SUBLANES, D)

    for cp in _tile_copies(o_hbm, obuf, out_sem, i, slot, False):
        cp.start()

    @pl.when(i == n_tiles - 1)
    def _():
        if n_tiles >= 2:
            for cp in _tile_copies(o_hbm, obuf, out_sem, i - 1, 1 - slot, False):
                cp.wait()
        for cp in _tile_copies(o_hbm, obuf, out_sem, i, slot, False):
            cp.wait()


def _ffn_call(x, p, l, *, nb):
    T, D = x.shape
    tm = FFN_TM
    S = T // nb
    q = tm // SUBLANES
    assert S % tm == 0 and q % SUBLANES == 0 and D_FF % FFN_FC == 0
    n_tiles = T // tm
    wup, wdn = p["w_up"], p["w_down"]
    cw, cb = p["conv_w"], p["conv_b"]
    hbm = pl.BlockSpec(memory_space=pl.ANY)
    cb_spec = pl.BlockSpec(cb.shape, lambda i: (0, 0), pipeline_mode=pl.Buffered(1))
    assert D % FFN_STAGE_CHUNKS == 0 and D_FF % FFN_STAGE_CHUNKS == 0
    out = pl.pallas_call(
        functools.partial(_ffn_kernel, tm=tm, tiles_per_batch=S // tm, n_tiles=n_tiles, layer=l),
        grid=(n_tiles,),
        in_specs=[hbm, _layer_spec(p["vectors"], l), hbm, _layer_spec(cw, l), cb_spec, hbm],
        out_specs=hbm,
        out_shape=jax.ShapeDtypeStruct((n_tiles, SUBLANES, q, D), F32),
        scratch_shapes=[pltpu.VMEM((2, q, SUBLANES, D), F32), pltpu.VMEM((2, q, SUBLANES, D), F32),
                        pltpu.VMEM((SUBLANES, D), F32), pltpu.VMEM((tm, D_FF), BF16),
                        pltpu.VMEM(wup.shape[1:], BF16), pltpu.VMEM(wdn.shape[1:], BF16),
                        pltpu.VMEM((2, D // FFN_STAGE_CHUNKS, 2 * D_FF), F32),
                        pltpu.VMEM((2, D_FF // FFN_STAGE_CHUNKS, D), F32),
                        pltpu.SemaphoreType.DMA((2, SUBLANES)), pltpu.SemaphoreType.DMA((2, SUBLANES)),
                        pltpu.SemaphoreType.DMA((2,))],
        compiler_params=pltpu.CompilerParams(dimension_semantics=("arbitrary",), vmem_limit_bytes=VMEM_LIMIT_BYTES),
        name=f"ffn_l{l}",
    )(x.reshape(n_tiles, SUBLANES, q, D), p["vectors"], wup, cw, cb, wdn)
    return out.reshape(T, D)


_PARAM_NAMES = (
    "rel_bias", "mix_norm_g", "w_in", "w_in_vres", "attn_q_norm", "attn_k_norm", "attn_sinks", "rwkv_mu",
    "rwkv_mu_vres", "rwkv_w0", "rwkv_w2", "rwkv_a0", "rwkv_a2", "rwkv_v0", "rwkv_v2", "rwkv_g2", "rwkv_k_k",
    "rwkv_k_a", "rwkv_r_k", "rwkv_ln_w", "rwkv_ln_b", "mem_norm_g", "w_mem_kv", "mem_q_norm", "mem_k_norm",
    "w_out", "ffn_norm_g", "w_up", "conv_w", "conv_b", "w_down")


def kernel(x, mem, rel_bias, mix_norm_g, w_in, w_in_vres, attn_q_norm, attn_k_norm, attn_sinks, rwkv_mu,
           rwkv_mu_vres, rwkv_w0, rwkv_w2, rwkv_a0, rwkv_a2, rwkv_v0, rwkv_v2, rwkv_g2, rwkv_k_k, rwkv_k_a,
           rwkv_r_k, rwkv_ln_w, rwkv_ln_b, mem_norm_g, w_mem_kv, mem_q_norm, mem_k_norm, w_out, ffn_norm_g,
           w_up, conv_w, conv_b, w_down):
    p = dict(zip(_PARAM_NAMES, (
        rel_bias, mix_norm_g, w_in, w_in_vres, attn_q_norm, attn_k_norm, attn_sinks, rwkv_mu, rwkv_mu_vres,
        rwkv_w0, rwkv_w2, rwkv_a0, rwkv_a2, rwkv_v0, rwkv_v2, rwkv_g2, rwkv_k_k, rwkv_k_a, rwkv_r_k, rwkv_ln_w,
        rwkv_ln_b, mem_norm_g, w_mem_kv, mem_q_norm, mem_k_norm, w_out, ffn_norm_g, w_up, conv_w, conv_b, w_down)))
    p["vectors"], p["lora"], p["sink_columns"] = _pack_params(p)
    nb, S, D = x.shape
    xt = x.reshape(nb * S, D)
    mem2d = mem.reshape(nb * mem.shape[1], D)
    bias = _bias_call(rel_bias)
    mk, mv = _memkv_call(mem2d, p)
    v_first = None
    for l in range(w_in.shape[0]):
        qa, ka, va, qm, r, lw, k, v, a, b, gate = _proj_call(xt, p, l, v_first, nb=nb)
        if l == 0:
            v_first = v
        y = _rwkv_scan(r, lw, k, v, a, b, nb=nb)
        xt = _mix_call(xt, qa, ka, va, qm, mk, mv, y, r, k, v, gate, bias, p, l, nb=nb)
        xt = _ffn_call(xt, p, l, nb=nb)
    return xt.reshape(nb, S, D)
```

```python
import functools
import math

import jax
import jax.numpy as jnp
import numpy as np
from jax import lax
from jax.experimental import pallas as pl
from jax.experimental.pallas import tpu as pltpu

F32 = jnp.float32
BF16 = jnp.bfloat16

HEAD_DIM = 64
SCAN_CHUNK = 64
VMEM_LIMIT_BYTES = 56 * 1024 * 1024


def _dot(a, b):
    return jnp.dot(a, b, preferred_element_type=F32)


def _dot_nt(a, b):
    return lax.dot_general(a, b, (((1,), (1,)), ((), ())), preferred_element_type=F32)


def _dot_tn(a, b):
    return lax.dot_general(a, b, (((0,), (0,)), ((), ())), preferred_element_type=F32)


def _bf(x):
    return x.astype(BF16)


SCAN_GROUP_HEADS = 4
SCAN_GROUP_W = SCAN_GROUP_HEADS * HEAD_DIM
SCAN_CHUNKS_PER_STEP = 4


_SCAN_STASH = (("w2", 1, F32), ("y_v", 1, F32), ("m_rb", 1, F32), ("v", 1, F32), ("w1r", 2, F32),
               ("bk", 2, F32), ("g_row", 1, F32))


def _scan_kernel(r_ref, lw_ref, k_ref, v_ref, a_ref, b_ref, bdm_ref, y_ref, h_ref, *stash_refs, nb, nch):
    C, N, GW, HPG = SCAN_CHUNK, HEAD_DIM, SCAN_GROUP_W, SCAN_GROUP_HEADS
    ngroups = h_ref.shape[0]
    stash = {name: ref for (name, _, _), ref in zip(_SCAN_STASH, stash_refs)}

    @pl.when(pl.program_id(0) == 0)
    def _():
        h_ref[...] = jnp.zeros_like(h_ref)
        for ref in stash_refs:
            ref[...] = jnp.zeros_like(ref)

    trow = lax.broadcasted_iota(jnp.int32, (C, C), 0)
    tcol = lax.broadcasted_iota(jnp.int32, (C, C), 1)
    tri = (trow >= tcol).astype(BF16)
    grow = lax.broadcasted_iota(jnp.int32, (C, GW), 0)
    gcol = lax.broadcasted_iota(jnp.int32, (C, GW), 1) % N
    incl = grow >= gcol
    strict = grow > gcol
    eye = (grow == gcol).astype(F32)
    bdm = bdm_ref[...]

    def bd(x):
        return jnp.concatenate([_bf(x)] * HPG, axis=0) * bdm

    def split2(x):
        hi = _bf(x)
        return hi, _bf(x - hi.astype(F32))

    def chunk_units(c):
        rows = slice(c * C, (c + 1) * C)
        per_batch = []
        for b in range(nb):
            lw = lw_ref[b, rows, :]
            l1 = _bf(lw)
            e1 = lw - l1.astype(F32)
            l2 = _bf(e1)
            l3 = _bf(e1 - l2.astype(F32))
            cum = _dot(tri, l1) + _dot(tri, l2) + _dot(tri, l3)
            cum_last = cum[C - 1:C, :]
            g_inv = jnp.exp(-cum)
            g_out = jnp.exp(cum_last - cum)
            f32 = lambda ref: ref[b, rows, :].astype(F32)
            kk, bb = f32(k_ref), f32(b_ref)
            per_batch.append(dict(
                a_t=f32(a_ref) * jnp.exp(cum - lw), r_t=f32(r_ref) * jnp.exp(cum),
                b_t=bb * g_inv, k_t=kk * g_inv, b_h=bb * g_out, k_h=kk * g_out,
                g_c=jnp.broadcast_to(jnp.exp(cum_last), cum.shape), v=f32(v_ref)))
        cat = {n: jnp.concatenate([pb[n] for pb in per_batch], axis=1) for n in per_batch[0]}
        return [{n: t[:, g * GW:(g + 1) * GW] for n, t in cat.items()} for g in range(ngroups)]

    def prepare(chunks):
        units = [u for c in chunks for u in chunk_units(c)]

        nu = len(units)
        ar = [jnp.concatenate([_bf(u["a_t"]), _bf(u["r_t"])], axis=0) for u in units]
        pb = [_dot_nt(ar[i], bd(units[i]["b_t"])) for i in range(nu)]
        pk = [_dot_nt(ar[i], bd(units[i]["k_t"])) for i in range(nu)]
        l_ab = [jnp.where(strict, p[:C], 0.0) for p in pb]
        m_rb = [_bf(jnp.where(incl, p[C:], 0.0)) for p in pb]
        tril2 = jnp.concatenate([strict, incl], axis=0)
        lm_k = [_bf(jnp.where(tril2, p, 0.0)) for p in pk]
        yield
        t_inv = [eye + l for l in l_ab]
        pw = [_dot(_bf(l), bd(l)) for l in l_ab]
        yield
        n_rounds = int(math.log2(C)) - 1
        for rnd in range(n_rounds):
            if rnd + 1 < n_rounds:
                z = [_dot(jnp.concatenate([_bf(p), _bf(t)], axis=0), bd(p)) for p, t in zip(pw, t_inv)]
                pw = [zz[:C] for zz in z]
                t_inv = [t + zz[C:] for t, zz in zip(t_inv, z)]
            else:
                t_inv = [t + _dot(_bf(t), bd(p)) for t, p in zip(t_inv, pw)]
            yield
        t_b = [_bf(t) for t in t_inv]
        w1 = [_dot(t, bd(u["a_t"])) for t, u in zip(t_b, units)]
        lmv = [_dot(m, bd(u["v"])) for m, u in zip(lm_k, units)]
        yield
        w2 = [_dot(t, bd(x[:C])) for t, x in zip(t_b, lmv)]
        parts = [p for u in units for p in split2(u["g_c"] * eye)]
        sums = _dot(jnp.concatenate(parts, axis=0), bdm)
        yield
        for i, u in enumerate(units):
            put = lambda name, val: stash[name].__setitem__(i, val.astype(stash[name].dtype))
            put("w2", w2[i])
            put("y_v", lmv[i][C:])
            put("m_rb", m_rb[i])
            put("v", u["v"])
            put("w1r", jnp.concatenate([w1[i], u["r_t"]], axis=0))
            put("bk", jnp.concatenate([u["b_h"], u["k_h"]], axis=0))
            put("g_row", sums[2 * i * C:(2 * i + 1) * C] + sums[(2 * i + 1) * C:(2 * i + 2) * C])
        yield

    state = [h_ref[g] for g in range(ngroups)]

    def advance(c):
        p = {n: [ref[c * ngroups + g] for g in range(ngroups)] for n, ref in stash.items()}
        for n in ("m_rb", "v", "w1r", "bk"):
            p[n] = [_bf(x) for x in p[n]]
        z = [_dot(w, bd(s)) for w, s in zip(p["w1r"], state)]
        yield
        u = [zz[:C] + w for zz, w in zip(z, p["w2"])]
        y = [zz[C:] + _dot(m, bd(x)) + yv for zz, m, x, yv in zip(z, p["m_rb"], u, p["y_v"])]
        uv = [jnp.concatenate([_bf(x), v], axis=0) for x, v in zip(u, p["v"])]
        full = [_dot_tn(b, x) * bdm.astype(F32) for b, x in zip(p["bk"], uv)]
        yield
        upd = [sum(f[h * N:(h + 1) * N] for h in range(HPG)) for f in full]
        state[:] = [g * s + d for g, s, d in zip(p["g_row"], state, upd)]
        ycat = jnp.concatenate(y, axis=1)
        w = y_ref.shape[2]
        for b in range(nb):
            y_ref[b, c * C:(c + 1) * C, :] = ycat[:, b * w:(b + 1) * w]
        yield

    def run(gen, stages):
        for _ in range(stages):
            next(gen, None)

    prep = prepare(list(range(nch)))
    n_prep = 5 + int(math.log2(C)) - 1
    for c in range(nch):
        seq = advance(c)
        run(prep, 1)
        run(seq, 1)
        run(prep, 1)
        run(seq, 2)
    run(prep, n_prep)
    for g in range(ngroups):
        h_ref[g] = state[g]


def _rwkv_scan(r, lw, k, v, a, b, *, nb):
    T, W = r.shape
    S = T // nb
    C, nch, GW = SCAN_CHUNK, SCAN_CHUNKS_PER_STEP, SCAN_GROUP_W
    assert S % (C * nch) == 0 and (nb * W) % GW == 0 and W % 128 == 0
    ngroups = nb * W // GW
    head = np.arange(GW) // HEAD_DIM
    bdm = jnp.asarray((head[:, None] == head[None, :]).astype(np.float32), dtype=BF16)
    n_blocks = S // (C * nch)
    in_spec = pl.BlockSpec((nb, C * nch, W), lambda s: (0, jnp.minimum(s, n_blocks - 1), 0))
    out_spec = pl.BlockSpec((nb, C * nch, W), lambda s: (0, jnp.maximum(s - 1, 0), 0))
    args = [t.reshape(nb, S, W) for t in (r, lw, k, v, a, b)]
    y = pl.pallas_call(
        functools.partial(_scan_kernel, nb=nb, nch=nch),
        grid=(n_blocks + 1,),
        in_specs=[in_spec] * 6 + [pl.BlockSpec((GW, GW), lambda s: (0, 0))],
        out_specs=out_spec,
        out_shape=jax.ShapeDtypeStruct((nb, S, W), F32),
        scratch_shapes=[pltpu.VMEM((ngroups, HEAD_DIM, GW), F32)]
        + [pltpu.VMEM((nch * ngroups, rows * C, GW), dt) for _, rows, dt in _SCAN_STASH],
        compiler_params=pltpu.CompilerParams(dimension_semantics=("arbitrary",), vmem_limit_bytes=VMEM_LIMIT_BYTES),
        name="rwkv_scan",
    )(*args, bdm)
    return y.reshape(T, W)


D_MODEL = 1024
ATTN_HEADS = 6
ATTN_KV_HEADS = 2
ATTN_GROUP = ATTN_HEADS // ATTN_KV_HEADS
ATTN_W = ATTN_HEADS * HEAD_DIM
KV_W = ATTN_KV_HEADS * HEAD_DIM
BLOCK_Q = 128
WINDOW = 128
N_BUCKETS = 32
MAX_EXACT = N_BUCKETS // 2
RWKV_W = 6 * HEAD_DIM
LORA_W = 128
SHIFT_W = 3 * RWKV_W + LORA_W
MEM_HEADS = 4
MEM_W = MEM_HEADS * HEAD_DIM
IN_BASE = ATTN_W + 2 * KV_W + SHIFT_W + MEM_W
PB_OFF = ATTN_W + 2 * KV_W
QM_OFF = PB_OFF + SHIFT_W
D_FF = 2816
EPS = 1e-6
GN_EPS = 64e-5
L2_EPS = 1e-12
MXU_WIDTH = 256

PROJ_TM = 1024
PROJ_SUB = 256
MIX_TQ = 512
FFN_TM = 1024
FFN_FC = 256
FFN_STAGE_CHUNKS = 8
WEIGHT_STAGE_CHUNKS = 4


def _block_diag_ones(width):
    idx = np.arange(width) // HEAD_DIM
    return jnp.asarray((idx[:, None] == idx[None, :]).astype(np.float32), dtype=BF16)


def _head_sum(t, bd):
    tb = _bf(t)
    width = t.shape[1]
    parts = [_dot(tb[:, lo:min(lo + MXU_WIDTH, width)], bd[lo:min(lo + MXU_WIDTH, width), lo:min(lo + MXU_WIDTH, width)])
             for lo in range(0, width, MXU_WIDTH)]
    return parts[0] if len(parts) == 1 else jnp.concatenate(parts, axis=1)


def _head_rms(t, bd, gain):
    ms = _head_sum(t * t, bd) * (1.0 / HEAD_DIM)
    return t * lax.rsqrt(ms + EPS) * gain


def _rms_rows(x, g):
    ms = jnp.mean(x * x, axis=-1, keepdims=True)
    return x * lax.rsqrt(ms + EPS) * g


class _Row:
    def __init__(self, table_ref, row, width):
        self.table_ref, self.row, self.width = table_ref, row, width

    def __getitem__(self, _):
        return self.table_ref[self.row:self.row + 1, 0:self.width]


def _proj_kernel(*refs, tm, sub, tiles_per_batch, has_vres, layer):
    it = iter(refs)
    x_ref, vec_ref, w_hbm, lora_ref = next(it), next(it), next(it), next(it)
    bd384_ref, bd128_ref, bd256_ref = next(it), next(it), next(it)
    if has_vres:
        wv_ref, vfirst_ref = next(it), next(it)
    qa_ref, ka_out_ref, va_ref, qm_ref = next(it), next(it), next(it), next(it)
    r_ref, lw_ref, k_ref, v_ref, a_ref, b_ref, gate_ref = (next(it) for _ in range(7))
    pbs_ref, w_ref, w_stage, w_sem = next(it), next(it), next(it), next(it)
    sw = pbs_ref.shape[1]

    @pl.when(pl.program_id(0) == 0)
    def _():
        _stage_weights(w_hbm, layer, w_ref, w_stage, w_sem)
        if has_vres:
            w_ref[:, IN_BASE:IN_BASE + LORA_W] = _bf(wv_ref[...])

    g_ref, qg_ref, kg_ref, mg_ref = (_Row(vec_ref, VEC_ROWS[n], w) for n, w in (
        ("mix_norm_g", D_MODEL), ("q_gain", ATTN_W), ("k_gain", KV_W), ("mq_gain", MEM_W)))
    mu_ref = _Row(vec_ref, VEC_ROWS["mu"], sw)
    w0_ref, a0_ref, kk_ref, ka_ref, v0_ref = (_Row(vec_ref, VEC_ROWS[n], RWKV_W) for n in ("w0", "a0", "k_k", "k_a", "v0"))
    w2_ref, a2_ref, g2_ref, v2_ref = (lora_ref.at[j] for j in range(4))

    @pl.when(pl.program_id(0) % tiles_per_batch == 0)
    def _():
        pbs_ref[0:8, :] = jnp.zeros((8, sw), F32)

    def project(i):
        rows = slice(i * sub, (i + 1) * sub)
        return _dot(_bf(_rms_rows(x_ref[rows, :], g_ref[...])), w_ref[...])

    def attention_outputs(i, proj):
        rows = slice(i * sub, (i + 1) * sub)
        qa_ref[rows, :] = _bf(_head_rms(proj[:, :ATTN_W], bd384_ref[...], qg_ref[...]))
        ka_out_ref[rows, :] = _bf(_head_rms(proj[:, ATTN_W:ATTN_W + KV_W], bd128_ref[...], kg_ref[...]))
        va_ref[rows, :] = _bf(proj[:, ATTN_W + KV_W:PB_OFF])
        qm_ref[rows, :] = _bf(_head_rms(proj[:, QM_OFF:QM_OFF + MEM_W], bd256_ref[...], mg_ref[...]))

    def rwkv_outputs(i, proj):
        rows = slice(i * sub, (i + 1) * sub)
        lo = 8 + i * sub
        pbs_ref[lo:lo + sub, 0:SHIFT_W] = proj[:, PB_OFF:PB_OFF + SHIFT_W]
        if has_vres:
            pbs_ref[lo:lo + sub, SHIFT_W:sw] = proj[:, IN_BASE:IN_BASE + LORA_W]
        cur = pbs_ref[lo:lo + sub, :]
        prev = pbs_ref[lo - 1:lo - 1 + sub, :]
        sh = cur + mu_ref[...] * (prev - cur)
        r = sh[:, 0:RWKV_W]
        k = sh[:, RWKV_W:2 * RWKV_W]
        v = sh[:, 2 * RWKV_W:3 * RWKV_W]
        z = sh[:, 3 * RWKV_W:SHIFT_W]
        t = w0_ref[...] + _dot(_bf(jnp.tanh(z)), w2_ref[...])
        lw_ref[rows, :] = -math.exp(-0.5) * jax.nn.sigmoid(t)
        a = jax.nn.sigmoid(a0_ref[...] + _dot(_bf(z), a2_ref[...]))
        gate_ref[rows, :] = _bf(_dot(_bf(jax.nn.sigmoid(z)), g2_ref[...]))
        if has_vres:
            vd = sh[:, SHIFT_W:sw]
            v = v + (vfirst_ref[rows, :].astype(F32) - v) * jax.nn.sigmoid(v0_ref[...] + _dot(_bf(vd), v2_ref[...]))
        kk = k * kk_ref[...]
        kk = kk / jnp.maximum(jnp.sqrt(_head_sum(kk * kk, bd384_ref[...])), L2_EPS)
        r_ref[rows, :] = _bf(r)
        k_ref[rows, :] = _bf(k * (1.0 + (a - 1.0) * ka_ref[...]))
        v_ref[rows, :] = _bf(v)
        a_ref[rows, :] = _bf(-kk)
        b_ref[rows, :] = _bf(kk * a)

    n_sub = tm // sub
    proj = project(0)
    for i in range(n_sub):
        attention_outputs(i, proj)
        nxt = project(i + 1) if i + 1 < n_sub else None
        rwkv_outputs(i, proj)
        proj = nxt
    pbs_ref[0:8, :] = pbs_ref[tm:tm + 8, :]


def _layer_spec(stacked, l):
    return pl.BlockSpec((None,) + stacked.shape[1:], lambda i: (l,) + (0,) * (stacked.ndim - 1),
                        pipeline_mode=pl.Buffered(1))


VEC_W = SHIFT_W + LORA_W
VEC_ROWS = {n: i for i, n in enumerate((
    "mix_norm_g", "q_gain", "k_gain", "mq_gain", "mu", "w0", "a0", "k_k", "k_a", "v0", "ln_w", "ln_b", "r_k",
    "ffn_norm_g"))}
VEC_TABLE_ROWS = 16


_PACKED = ("mix_norm_g", "attn_q_norm", "attn_k_norm", "mem_q_norm", "rwkv_mu", "rwkv_mu_vres", "rwkv_w0", "rwkv_a0",
           "rwkv_k_k", "rwkv_k_a", "rwkv_v0", "rwkv_ln_w", "rwkv_ln_b", "rwkv_r_k", "ffn_norm_g",
           "rwkv_w2", "rwkv_a2", "rwkv_g2", "rwkv_v2")


def _pack_kernel(*refs, layers):
    src = dict(zip(_PACKED, refs))
    sinks_ref, rel_bias_ref, bucket_ref, vec_ref, lora_ref, sink_ref, bias_ref = refs[len(_PACKED):]
    _bias_table(rel_bias_ref, bucket_ref, bias_ref)
    scale = HEAD_DIM ** -0.5
    vec_ref[...] = jnp.zeros_like(vec_ref)
    lora_ref[...] = jnp.zeros_like(lora_ref)
    tiled = lambda v, n: jnp.concatenate([v] * n, axis=1)

    def put(l, name, val):
        vec_ref[l, VEC_ROWS[name]:VEC_ROWS[name] + 1, 0:val.shape[1]] = val

    for l in range(layers):
        row = lambda n, at=l: src[n][at:at + 1, :]
        put(l, "mix_norm_g", row("mix_norm_g"))
        put(l, "q_gain", tiled(row("attn_q_norm"), ATTN_HEADS) * scale)
        put(l, "k_gain", tiled(row("attn_k_norm"), ATTN_KV_HEADS))
        put(l, "mq_gain", tiled(row("mem_q_norm"), MEM_HEADS) * scale)
        put(l, "mu", row("rwkv_mu"))
        for name, key in (("w0", "rwkv_w0"), ("a0", "rwkv_a0"), ("k_k", "rwkv_k_k"), ("k_a", "rwkv_k_a"),
                          ("ln_w", "rwkv_ln_w"), ("ln_b", "rwkv_ln_b"), ("ffn_norm_g", "ffn_norm_g")):
            put(l, name, row(key))
        r_k = src["rwkv_r_k"][l]
        put(l, "r_k", jnp.concatenate([r_k[h:h + 1, :] for h in range(r_k.shape[0])], axis=1))
        for j, (key, at) in enumerate((("rwkv_w2", 0), ("rwkv_a2", 32), ("rwkv_g2", 64))):
            w = src[key][l]
            lora_ref[l, j, at:at + w.shape[0], :] = _bf(w)
        if l > 0:
            mu_v = row("rwkv_mu_vres", l - 1)
            vec_ref[l, VEC_ROWS["mu"]:VEC_ROWS["mu"] + 1, SHIFT_W:SHIFT_W + mu_v.shape[1]] = mu_v
            put(l, "v0", row("rwkv_v0", l - 1))
            v2 = src["rwkv_v2"][l - 1]
            lora_ref[l, 3, 0:v2.shape[0], :] = _bf(v2)
        for h in range(ATTN_HEADS):
            g, i = divmod(h, ATTN_GROUP)
            sink_ref[l, g, i * BLOCK_Q:(i + 1) * BLOCK_Q, :] = jnp.full((BLOCK_Q, 1), sinks_ref[l, h], F32)


def _pack_params(p):
    layers = p["w_in"].shape[0]
    srcs = [p[n].astype(F32) for n in _PACKED]
    vm = pl.BlockSpec(memory_space=pltpu.VMEM)
    sm = pl.BlockSpec(memory_space=pltpu.SMEM)
    return pl.pallas_call(
        functools.partial(_pack_kernel, layers=layers),
        in_specs=[vm] * len(srcs) + [sm, sm, vm],
        out_specs=[vm, vm, vm, vm],
        out_shape=[jax.ShapeDtypeStruct((layers, VEC_TABLE_ROWS, VEC_W), F32),
                   jax.ShapeDtypeStruct((layers, 4, LORA_W, RWKV_W), BF16),
                   jax.ShapeDtypeStruct((layers, ATTN_KV_HEADS, ATTN_GROUP * BLOCK_Q, 1), F32),
                   jax.ShapeDtypeStruct((ATTN_HEADS, BLOCK_Q, 2 * BLOCK_Q), F32)],
        name="pack_params",
    )(*srcs, p["attn_sinks"].astype(F32), p["rel_bias"].astype(F32), jnp.asarray(_bucket_table()))


def _proj_call(x, p, l, v_first, *, nb):
    T, D = x.shape
    tm = PROJ_TM
    assert T % tm == 0 and (T // nb) % tm == 0
    has_vres = l > 0
    w = p["w_in"]
    ins = [x, p["vectors"], w, p["lora"], _block_diag_ones(ATTN_W), _block_diag_ones(KV_W), _block_diag_ones(MEM_W)]
    if has_vres:
        ins += [jnp.pad(p["w_in_vres"][l - 1], ((0, 0), (0, LORA_W - 16))), v_first]
    nw = IN_BASE + (LORA_W if has_vres else 0)

    def spec(a):
        if a is w:
            return pl.BlockSpec(memory_space=pl.ANY)
        if a.ndim >= 3:
            return _layer_spec(a, l)
        if a.shape[0] == T:
            return pl.BlockSpec((tm, a.shape[1]), lambda i: (i, 0))
        return pl.BlockSpec(a.shape, lambda i: (0, 0), pipeline_mode=pl.Buffered(1))

    out_shapes = [jax.ShapeDtypeStruct((T, ATTN_W), BF16), jax.ShapeDtypeStruct((T, KV_W), BF16),
                  jax.ShapeDtypeStruct((T, KV_W), BF16), jax.ShapeDtypeStruct((T, MEM_W), BF16)]
    out_shapes += [jax.ShapeDtypeStruct((T, RWKV_W), F32 if j == 1 else BF16) for j in range(7)]
    sw = SHIFT_W + (LORA_W if has_vres else 0)
    return pl.pallas_call(
        functools.partial(_proj_kernel, tm=tm, sub=PROJ_SUB, tiles_per_batch=(T // nb) // tm, has_vres=has_vres,
                          layer=l),
        grid=(T // tm,),
        in_specs=[spec(a) for a in ins],
        out_specs=[pl.BlockSpec((tm, s.shape[1]), lambda i: (i, 0)) for s in out_shapes],
        out_shape=out_shapes,
        scratch_shapes=[pltpu.VMEM((tm + 8, sw), F32), pltpu.VMEM((D, nw), BF16),
                        pltpu.VMEM((2, D // WEIGHT_STAGE_CHUNKS, IN_BASE), F32), pltpu.SemaphoreType.DMA((2,))],
        compiler_params=pltpu.CompilerParams(dimension_semantics=("arbitrary",), vmem_limit_bytes=VMEM_LIMIT_BYTES),
        name=f"proj_l{l}",
    )(*ins)


def _bucket_table():
    qi = np.arange(BLOCK_Q)[:, None]
    kj = np.arange(2 * BLOCK_Q)[None, :]
    dist = qi + BLOCK_Q - kj
    in_band = (dist >= 0) & (dist < WINDOW)
    d = np.maximum(dist, 1).astype(np.float32)
    large = MAX_EXACT + (np.log(d / np.float32(MAX_EXACT)) / np.float32(math.log(WINDOW / MAX_EXACT))
                         * np.float32(N_BUCKETS - MAX_EXACT)).astype(np.int32)
    large = np.minimum(large, N_BUCKETS - 1)
    bucket = np.where(dist < MAX_EXACT, np.maximum(dist, 0), large)
    return np.where(in_band, bucket, -1).astype(np.int32)


def _bias_table(rb_ref, bucket_ref, out_ref):
    bucket = bucket_ref[...]
    for h in range(ATTN_HEADS):
        acc = jnp.full(bucket.shape, -jnp.inf, F32)
        for j in range(N_BUCKETS):
            acc = jnp.where(bucket == j, rb_ref[j, h], acc)
        out_ref[h] = acc


def _memkv_kernel(mem_ref, g_ref, w_ref, kn_ref, bd_ref, mk_ref, mv_ref):
    mem = mem_ref[...]
    for l in range(w_ref.shape[0]):
        hn = _bf(_rms_rows(mem, g_ref[l:l + 1, :]))
        kv = _dot(hn, _bf(w_ref[l]))
        gain = jnp.concatenate([kn_ref[l:l + 1, :]] * MEM_HEADS, axis=1)
        mk_ref[l] = _bf(_head_rms(kv[:, :MEM_W], bd_ref[...], gain))
        mv_ref[l] = _bf(kv[:, MEM_W:])


def _memkv_call(mem2d, p):
    rows = mem2d.shape[0]
    layers = p["w_mem_kv"].shape[0]
    vm = pl.BlockSpec(memory_space=pltpu.VMEM)
    return pl.pallas_call(
        _memkv_kernel,
        in_specs=[vm] * 5,
        out_specs=[vm, vm],
        out_shape=[jax.ShapeDtypeStruct((layers, rows, MEM_W), BF16)] * 2,
        compiler_params=pltpu.CompilerParams(vmem_limit_bytes=VMEM_LIMIT_BYTES),
        name="mem_kv",
    )(mem2d, p["mem_norm_g"], p["w_mem_kv"], p["mem_k_norm"], _block_diag_ones(MEM_W))


def _mix_kernel(x_ref, qa_ref, kc_ref, kp_ref, vc_ref, vp_ref, bias_ref, sink_ref, qm_ref, mk_ref, mv_ref,
                y_ref, r_ref, k_ref, v_ref, gate_ref, vec_ref, bd_ref, wout_hbm, out_ref, att_ref,
                wout_ref, w_stage, w_sem, *, tq, tiles_per_batch, layer):
    lnw_ref, lnb_ref, rk_ref = (_Row(vec_ref, VEC_ROWS[n], RWKV_W) for n in ("ln_w", "ln_b", "r_k"))
    N, BQ = HEAD_DIM, BLOCK_Q
    seq_start = pl.program_id(0) % tiles_per_batch == 0
    nqb = tq // BQ

    @pl.when(pl.program_id(0) == 0)
    def _():
        att_ref[...] = jnp.zeros_like(att_ref)
        _stage_weights(wout_hbm, layer, wout_ref, w_stage, w_sem)

    qa = qa_ref[...]
    kall = jnp.concatenate([kp_ref[...], kc_ref[...]], axis=0)
    vall = jnp.concatenate([vp_ref[...], vc_ref[...]], axis=0)
    k_group = [kall[:, g * N:(g + 1) * N] for g in range(ATTN_KV_HEADS)]
    v_group = [vall[:, g * N:(g + 1) * N] for g in range(ATTN_KV_HEADS)]
    before_seq = lax.broadcasted_iota(jnp.int32, (ATTN_GROUP * BQ, 2 * BQ), 1) < BQ

    qm = qm_ref[...]
    mk = mk_ref[0]
    mv = mv_ref[0]

    swa = [(j, g) for j in range(nqb) for g in range(ATTN_KV_HEADS)]
    logits, values, sinks = [], [], []
    for j, g in swa:
        qg = jnp.concatenate(
            [qa[j * BQ:(j + 1) * BQ, (ATTN_GROUP * g + i) * N:(ATTN_GROUP * g + i + 1) * N] for i in range(ATTN_GROUP)],
            axis=0)
        lg = _dot_nt(qg, k_group[g][j * BQ:(j + 2) * BQ]) + bias_ref[g]
        if j == 0:
            lg = jnp.where(jnp.logical_and(seq_start, before_seq), -jnp.inf, lg)
        logits.append(lg)
        values.append(v_group[g][j * BQ:(j + 2) * BQ])
        sinks.append(sink_ref[g])
    for h in range(MEM_HEADS):
        hs = slice(h * N, (h + 1) * N)
        logits.append(_dot_nt(qm[:, hs], mk[:, hs]))
        values.append(mv[:, hs])
        sinks.append(None)
    bd = bd_ref[...]
    y = y_ref[...]
    d = y - _head_sum(y, bd) * (1.0 / N)
    var = _head_sum(d * d, bd) * (1.0 / N)
    yn = d * lax.rsqrt(var + GN_EPS) * lnw_ref[...] + lnb_ref[...]
    f32 = lambda ref: ref[...].astype(F32)
    bonus = _head_sum(f32(r_ref) * f32(k_ref) * rk_ref[...], bd) * f32(v_ref)
    out_b = (yn + bonus) * f32(gate_ref)
    mixed = jnp.concatenate([att_ref[:, :ATTN_W], _bf(out_b), att_ref[:, ATTN_W:]], axis=-1)
    out_ref[...] = x_ref[...] + _dot(mixed, wout_ref[...])

    row_max = [jnp.max(lg, axis=-1, keepdims=True) for lg in logits]
    m = [rm if s is None else jnp.maximum(rm, s) for rm, s in zip(row_max, sinks)]
    e = [jnp.exp(lg - mm) for lg, mm in zip(logits, m)]
    denom = [jnp.sum(ee, axis=-1, keepdims=True) for ee in e]
    denom = [d if s is None else d + jnp.exp(s - mm) for d, s, mm in zip(denom, sinks, m)]
    outs = [_bf(_dot(_bf(ee), vv) / d) for ee, vv, d in zip(e, values, denom)]

    head_rows = [[None] * nqb for _ in range(ATTN_HEADS)]
    for (j, g), o in zip(swa, outs):
        for i in range(ATTN_GROUP):
            head_rows[ATTN_GROUP * g + i][j] = o[i * BQ:(i + 1) * BQ]
    att_ref[:, :ATTN_W] = jnp.concatenate([jnp.concatenate(rows, axis=0) for rows in head_rows], axis=-1)
    att_ref[:, ATTN_W:] = jnp.concatenate(outs[len(swa):], axis=-1)


def _mix_call(x, qa, ka, va, qm, mk, mv, y, r, k, v, gate, bias, p, l, *, nb):
    T, D = x.shape
    tq = MIX_TQ
    S = T // nb
    assert S % tq == 0 and tq % BLOCK_Q == 0
    tpb = S // tq
    qpb = tq // BLOCK_Q
    mem_tokens = mk.shape[1] // nb
    sink = p["sink_columns"]
    n_tiles = T // tq
    att = lambda i: jnp.minimum(i, n_tiles - 1)
    fin = lambda i: jnp.maximum(i - 1, 0)
    att_tile = lambda w: pl.BlockSpec((tq, w), lambda i: (att(i), 0))
    tile = lambda w: pl.BlockSpec((tq, w), lambda i: (fin(i), 0))
    prev = pl.BlockSpec((BLOCK_Q, KV_W), lambda i: (jnp.maximum(att(i) * qpb - 1, 0), 0))
    full = lambda a: pl.BlockSpec(a.shape, lambda i: (0,) * a.ndim)
    memspec = pl.BlockSpec((None, 1, mem_tokens, MEM_W), lambda i: (l, att(i) // tpb, 0, 0))
    bd = _block_diag_ones(RWKV_W)
    wout = p["w_out"]
    mk3 = mk.reshape(mk.shape[0], nb, mem_tokens, MEM_W)
    mv3 = mv.reshape(mv.shape[0], nb, mem_tokens, MEM_W)
    return pl.pallas_call(
        functools.partial(_mix_kernel, tq=tq, tiles_per_batch=tpb, layer=l),
        grid=(n_tiles + 1,),
        in_specs=[tile(D), att_tile(ATTN_W), att_tile(KV_W), prev, att_tile(KV_W), prev, full(bias), _layer_spec(sink, l),
                  att_tile(MEM_W), memspec, memspec] + [tile(RWKV_W)] * 5
                 + [_layer_spec(p["vectors"], l), full(bd), pl.BlockSpec(memory_space=pl.ANY)],
        out_specs=tile(D),
        out_shape=jax.ShapeDtypeStruct((T, D), F32),
        scratch_shapes=[pltpu.VMEM((tq, ATTN_W + MEM_W), BF16), pltpu.VMEM((D, D), BF16),
                        pltpu.VMEM((2, D // WEIGHT_STAGE_CHUNKS, D), F32), pltpu.SemaphoreType.DMA((2,))],
        compiler_params=pltpu.CompilerParams(dimension_semantics=("arbitrary",), vmem_limit_bytes=VMEM_LIMIT_BYTES),
        name=f"mix_l{l}",
    )(x, qa, ka, ka, va, va, bias, sink, qm, mk3, mv3, y, r, k, v, gate, p["vectors"], bd, wout)


SUBLANES = 8


def _tile_copies(hbm, buf, sem, tile, slot, to_vmem):
    copies = []
    for s in range(SUBLANES):
        src, dst = hbm.at[tile, s], buf.at[slot, :, s, :]
        if not to_vmem:
            src, dst = dst, src
        copies.append(pltpu.make_async_copy(src, dst, sem.at[slot, s]))
    return copies


def _shift_rows(u, before, steps):
    first = lax.broadcasted_iota(jnp.int32, (SUBLANES, u.shape[1]), 0) == 0
    tm = u.shape[0]
    heads = []
    for i in range(steps):
        lo = tm - (steps - i) * SUBLANES
        wrapped = pltpu.roll(u[lo:lo + SUBLANES], 1, axis=0)
        prior = jnp.broadcast_to(before[SUBLANES - steps + i:SUBLANES - steps + i + 1], wrapped.shape)
        heads.append(jnp.where(first, prior, wrapped))
    return jnp.concatenate(heads + [u[:tm - steps * SUBLANES]], axis=0)


def _stage_weights(w_hbm, layer, dst_ref, stage_ref, sem):
    rows = stage_ref.shape[1]
    n = dst_ref.shape[0] // rows
    copy = lambda c: pltpu.make_async_copy(w_hbm.at[layer, pl.ds(c * rows, rows), :], stage_ref.at[c % 2], sem.at[c % 2])
    copy(0).start()
    for c in range(n):
        if c + 1 < n:
            copy(c + 1).start()
        copy(c).wait()
        dst_ref[c * rows:(c + 1) * rows, 0:stage_ref.shape[2]] = _bf(stage_ref[c % 2])


def _ffn_kernel(x_hbm, vec_ref, wup_hbm, cw_ref, cb_ref, wdn_hbm, o_hbm, xbuf, obuf, carry_ref, act_ref,
                wup_ref, wdn_ref, up_stage, dn_stage, in_sem, out_sem, w_sem, *, tm, tiles_per_batch, n_tiles, layer):
    D = xbuf.shape[-1]
    g_ref = _Row(vec_ref, VEC_ROWS["ffn_norm_g"], D)
    q = tm // SUBLANES
    i = pl.program_id(0)
    slot = i % 2

    @pl.when(i == 0)
    def _():
        carry_ref[...] = jnp.zeros_like(carry_ref)
        for cp in _tile_copies(x_hbm, xbuf, in_sem, 0, 0, True):
            cp.start()
        _stage_weights(wup_hbm, layer, wup_ref, up_stage, w_sem)
        _stage_weights(wdn_hbm, layer, wdn_ref, dn_stage, w_sem)

    @pl.when(i + 1 < n_tiles)
    def _():
        for cp in _tile_copies(x_hbm, xbuf, in_sem, i + 1, 1 - slot, True):
            cp.start()

    for cp in _tile_copies(x_hbm, xbuf, in_sem, i, slot, True):
        cp.wait()

    @pl.when(i >= 2)
    def _():
        for cp in _tile_copies(o_hbm, obuf, out_sem, i - 2, slot, False):
            cp.wait()

    seq_start = i % tiles_per_batch == 0
    g = g_ref[...]
    x = xbuf[slot].reshape(tm, D)
    before = jnp.where(seq_start, 0.0, _rms_rows(carry_ref[...], g))
    for r, grp in ((SUBLANES - 2, q - 2), (SUBLANES - 1, q - 1)):
        row = grp * SUBLANES + SUBLANES - 1
        carry_ref[r:r + 1, :] = x[row:row + 1]
    h_ext = _bf(jnp.concatenate([before, _rms_rows(x, g)], axis=0))

    n_chunks = D_FF // FFN_FC
    gate_cols = lambda c: slice(c * FFN_FC, (c + 1) * FFN_FC)
    val_cols = lambda c: slice(D_FF + c * FFN_FC, D_FF + (c + 1) * FFN_FC)

    def conv(u_ext, cols):
        w = cw_ref[:, cols]
        u = u_ext[SUBLANES:]
        return (cb_ref[layer:layer + 1, cols] + w[0:1] * _shift_rows(u, u_ext[:SUBLANES], 2)
                + w[1:2] * _shift_rows(u, u_ext[:SUBLANES], 1) + w[2:3] * u)

    up = lambda c: (_dot(h_ext, wup_ref[:, gate_cols(c)]), _dot(h_ext, wup_ref[:, val_cols(c)]))
    nxt = up(0)
    for c in range(n_chunks):
        ug, uv = nxt
        if c + 1 < n_chunks:
            nxt = up(c + 1)
        gt = conv(ug, gate_cols(c))
        act_ref[:, gate_cols(c)] = _bf(gt * jax.nn.sigmoid(gt) * conv(uv, val_cols(c)))
    obuf[slot] = (x + _dot(act_ref[...], wdn_ref[...])).reshape(q, SUBLANES, D)

    for cp in _tile_copies(o_hbm, obuf, out_sem, i, slot, False):
        cp.start()

    @pl.when(i == n_tiles - 1)
    def _():
        if n_tiles >= 2:
            for cp in _tile_copies(o_hbm, obuf, out_sem, i - 1, 1 - slot, False):
                cp.wait()
        for cp in _tile_copies(o_hbm, obuf, out_sem, i, slot, False):
            cp.wait()


def _ffn_call(x, p, l, *, nb):
    T, D = x.shape
    tm = FFN_TM
    S = T // nb
    q = tm // SUBLANES
    assert S % tm == 0 and q % SUBLANES == 0 and D_FF % FFN_FC == 0
    n_tiles = T // tm
    wup, wdn = p["w_up"], p["w_down"]
    cw, cb = p["conv_w"], p["conv_b"]
    hbm = pl.BlockSpec(memory_space=pl.ANY)
    cb_spec = pl.BlockSpec(cb.shape, lambda i: (0, 0), pipeline_mode=pl.Buffered(1))
    assert D % FFN_STAGE_CHUNKS == 0 and D_FF % FFN_STAGE_CHUNKS == 0
    out = pl.pallas_call(
        functools.partial(_ffn_kernel, tm=tm, tiles_per_batch=S // tm, n_tiles=n_tiles, layer=l),
        grid=(n_tiles,),
        in_specs=[hbm, _layer_spec(p["vectors"], l), hbm, _layer_spec(cw, l), cb_spec, hbm],
        out_specs=hbm,
        out_shape=jax.ShapeDtypeStruct((n_tiles, SUBLANES, q, D), F32),
        scratch_shapes=[pltpu.VMEM((2, q, SUBLANES, D), F32), pltpu.VMEM((2, q, SUBLANES, D), F32),
                        pltpu.VMEM((SUBLANES, D), F32), pltpu.VMEM((tm, D_FF), BF16),
                        pltpu.VMEM(wup.shape[1:], BF16), pltpu.VMEM(wdn.shape[1:], BF16),
                        pltpu.VMEM((2, D // FFN_STAGE_CHUNKS, 2 * D_FF), F32),
                        pltpu.VMEM((2, D_FF // FFN_STAGE_CHUNKS, D), F32),
                        pltpu.SemaphoreType.DMA((2, SUBLANES)), pltpu.SemaphoreType.DMA((2, SUBLANES)),
                        pltpu.SemaphoreType.DMA((2,))],
        compiler_params=pltpu.CompilerParams(dimension_semantics=("arbitrary",), vmem_limit_bytes=VMEM_LIMIT_BYTES),
        name=f"ffn_l{l}",
    )(x.reshape(n_tiles, SUBLANES, q, D), p["vectors"], wup, cw, cb, wdn)
    return out.reshape(T, D)


_PARAM_NAMES = (
    "rel_bias", "mix_norm_g", "w_in", "w_in_vres", "attn_q_norm", "attn_k_norm", "attn_sinks", "rwkv_mu",
    "rwkv_mu_vres", "rwkv_w0", "rwkv_w2", "rwkv_a0", "rwkv_a2", "rwkv_v0", "rwkv_v2", "rwkv_g2", "rwkv_k_k",
    "rwkv_k_a", "rwkv_r_k", "rwkv_ln_w", "rwkv_ln_b", "mem_norm_g", "w_mem_kv", "mem_q_norm", "mem_k_norm",
    "w_out", "ffn_norm_g", "w_up", "conv_w", "conv_b", "w_down")


def kernel(x, mem, rel_bias, mix_norm_g, w_in, w_in_vres, attn_q_norm, attn_k_norm, attn_sinks, rwkv_mu,
           rwkv_mu_vres, rwkv_w0, rwkv_w2, rwkv_a0, rwkv_a2, rwkv_v0, rwkv_v2, rwkv_g2, rwkv_k_k, rwkv_k_a,
           rwkv_r_k, rwkv_ln_w, rwkv_ln_b, mem_norm_g, w_mem_kv, mem_q_norm, mem_k_norm, w_out, ffn_norm_g,
           w_up, conv_w, conv_b, w_down):
    p = dict(zip(_PARAM_NAMES, (
        rel_bias, mix_norm_g, w_in, w_in_vres, attn_q_norm, attn_k_norm, attn_sinks, rwkv_mu, rwkv_mu_vres,
        rwkv_w0, rwkv_w2, rwkv_a0, rwkv_a2, rwkv_v0, rwkv_v2, rwkv_g2, rwkv_k_k, rwkv_k_a, rwkv_r_k, rwkv_ln_w,
        rwkv_ln_b, mem_norm_g, w_mem_kv, mem_q_norm, mem_k_norm, w_out, ffn_norm_g, w_up, conv_w, conv_b, w_down)))
    p["vectors"], p["lora"], p["sink_columns"], bias = _pack_params(p)
    bias = bias.reshape(ATTN_KV_HEADS, ATTN_GROUP * BLOCK_Q, 2 * BLOCK_Q)
    nb, S, D = x.shape
    xt = x.reshape(nb * S, D)
    mem2d = mem.reshape(nb * mem.shape[1], D)
    mk, mv = _memkv_call(mem2d, p)
    v_first = None
    for l in range(w_in.shape[0]):
        qa, ka, va, qm, r, lw, k, v, a, b, gate = _proj_call(xt, p, l, v_first, nb=nb)
        if l == 0:
            v_first = v
        y = _rwkv_scan(r, lw, k, v, a, b, nb=nb)
        xt = _mix_call(xt, qa, ka, va, qm, mk, mv, y, r, k, v, gate, bias, p, l, nb=nb)
        xt = _ffn_call(xt, p, l, nb=nb)
    return xt.reshape(nb, S, D)
```

```python
import functools
import math

import jax
import jax.numpy as jnp
import numpy as np
from jax import lax
from jax.experimental import pallas as pl
from jax.experimental.pallas import tpu as pltpu

F32 = jnp.float32
BF16 = jnp.bfloat16

HEAD_DIM = 64
SCAN_CHUNK = 64
VMEM_LIMIT_BYTES = 56 * 1024 * 1024


def _dot(a, b):
    return jnp.dot(a, b, preferred_element_type=F32)


def _dot_nt(a, b):
    return lax.dot_general(a, b, (((1,), (1,)), ((), ())), preferred_element_type=F32)


def _dot_tn(a, b):
    return lax.dot_general(a, b, (((0,), (0,)), ((), ())), preferred_element_type=F32)


def _bf(x):
    return x.astype(BF16)


SCAN_GROUP_HEADS = 4
SCAN_GROUP_W = SCAN_GROUP_HEADS * HEAD_DIM
SCAN_CHUNKS_PER_STEP = 4
BF16_SUBLANES = 16


_SCAN_STASH = (("w2", 1, F32), ("y_v", 1, F32), ("m_rb", 1, F32), ("v", 1, F32), ("w1r", 2, F32),
               ("bk", 2, F32), ("g_row", 1, F32))


def _scan_kernel(r_ref, lw_ref, k_ref, v_ref, a_ref, b_ref, bdm_ref, wup_ref, wdn_ref,
                 y_ref, wup_bf_ref, wdn_bf_ref, h_ref, *stash_refs, nb, nch):
    C, N, GW, HPG = SCAN_CHUNK, HEAD_DIM, SCAN_GROUP_W, SCAN_GROUP_HEADS
    ngroups = h_ref.shape[0]
    stash = {name: ref for (name, _, _), ref in zip(_SCAN_STASH, stash_refs)}

    wup_bf_ref[...] = _bf(wup_ref[...])
    wdn_bf_ref[...] = _bf(wdn_ref[...])

    @pl.when(pl.program_id(0) == 0)
    def _():
        h_ref[...] = jnp.zeros_like(h_ref)
        for ref in stash_refs:
            ref[...] = jnp.zeros_like(ref)

    trow = lax.broadcasted_iota(jnp.int32, (C, C), 0)
    tcol = lax.broadcasted_iota(jnp.int32, (C, C), 1)
    tri = (trow >= tcol).astype(BF16)
    grow = lax.broadcasted_iota(jnp.int32, (C, GW), 0)
    gcol = lax.broadcasted_iota(jnp.int32, (C, GW), 1) % N
    incl = grow >= gcol
    strict = grow > gcol
    eye = (grow == gcol).astype(F32)
    bdm = bdm_ref[...]

    def bd(x):
        return jnp.concatenate([_bf(x)] * HPG, axis=0) * bdm

    def split2(x):
        hi = _bf(x)
        return hi, _bf(x - hi.astype(F32))

    def chunk_units(c):
        rows = slice(c * C, (c + 1) * C)
        per_batch = []
        for b in range(nb):
            lw = lw_ref[b, rows, :]
            l1 = _bf(lw)
            e1 = lw - l1.astype(F32)
            l2 = _bf(e1)
            l3 = _bf(e1 - l2.astype(F32))
            cum = _dot(tri, l1) + _dot(tri, l2) + _dot(tri, l3)
            cum_last = cum[C - 1:C, :]
            g_inv = jnp.exp(-cum)
            g_out = jnp.exp(cum_last - cum)
            f32 = lambda ref: ref[b, rows, :].astype(F32)
            kk, bb = f32(k_ref), f32(b_ref)
            per_batch.append(dict(
                a_t=f32(a_ref) * jnp.exp(cum - lw), r_t=f32(r_ref) * jnp.exp(cum),
                b_t=bb * g_inv, k_t=kk * g_inv, b_h=bb * g_out, k_h=kk * g_out,
                g_c=jnp.broadcast_to(jnp.exp(cum_last), cum.shape), v=f32(v_ref)))
        cat = {n: jnp.concatenate([pb[n] for pb in per_batch], axis=1) for n in per_batch[0]}
        return [{n: t[:, g * GW:(g + 1) * GW] for n, t in cat.items()} for g in range(ngroups)]

    def prepare(chunks):
        units = [u for c in chunks for u in chunk_units(c)]

        nu = len(units)
        ar = [jnp.concatenate([_bf(u["a_t"]), _bf(u["r_t"])], axis=0) for u in units]
        pb = [_dot_nt(ar[i], bd(units[i]["b_t"])) for i in range(nu)]
        pk = [_dot_nt(ar[i], bd(units[i]["k_t"])) for i in range(nu)]
        l_ab = [jnp.where(strict, p[:C], 0.0) for p in pb]
        m_rb = [_bf(jnp.where(incl, p[C:], 0.0)) for p in pb]
        tril2 = jnp.concatenate([strict, incl], axis=0)
        lm_k = [_bf(jnp.where(tril2, p, 0.0)) for p in pk]
        yield
        t_inv = [eye + l for l in l_ab]
        pw = [_dot(_bf(l), bd(l)) for l in l_ab]
        yield
        n_rounds = int(math.log2(C)) - 1
        for rnd in range(n_rounds):
            if rnd + 1 < n_rounds:
                z = [_dot(jnp.concatenate([_bf(p), _bf(t)], axis=0), bd(p)) for p, t in zip(pw, t_inv)]
                pw = [zz[:C] for zz in z]
                t_inv = [t + zz[C:] for t, zz in zip(t_inv, z)]
            else:
                t_inv = [t + _dot(_bf(t), bd(p)) for t, p in zip(t_inv, pw)]
            yield
        t_b = [_bf(t) for t in t_inv]
        w1 = [_dot(t, bd(u["a_t"])) for t, u in zip(t_b, units)]
        lmv = [_dot(m, bd(u["v"])) for m, u in zip(lm_k, units)]
        yield
        w2 = [_dot(t, bd(x[:C])) for t, x in zip(t_b, lmv)]
        parts = [p for u in units for p in split2(u["g_c"] * eye)]
        sums = _dot(jnp.concatenate(parts, axis=0), bdm)
        yield
        for i, u in enumerate(units):
            put = lambda name, val: stash[name].__setitem__(i, val.astype(stash[name].dtype))
            put("w2", w2[i])
            put("y_v", lmv[i][C:])
            put("m_rb", m_rb[i])
            put("v", u["v"])
            put("w1r", jnp.concatenate([w1[i], u["r_t"]], axis=0))
            put("bk", jnp.concatenate([u["b_h"], u["k_h"]], axis=0))
            put("g_row", sums[2 * i * C:(2 * i + 1) * C] + sums[(2 * i + 1) * C:(2 * i + 2) * C])
        yield

    state = [h_ref[g] for g in range(ngroups)]

    def advance(c):
        p = {n: [ref[c * ngroups + g] for g in range(ngroups)] for n, ref in stash.items()}
        for n in ("m_rb", "v", "w1r", "bk"):
            p[n] = [_bf(x) for x in p[n]]
        z = [_dot(w, bd(s)) for w, s in zip(p["w1r"], state)]
        yield
        u = [zz[:C] + w for zz, w in zip(z, p["w2"])]
        y = [zz[C:] + _dot(m, bd(x)) + yv for zz, m, x, yv in zip(z, p["m_rb"], u, p["y_v"])]
        uv = [jnp.concatenate([_bf(x), v], axis=0) for x, v in zip(u, p["v"])]
        full = [_dot_tn(b, x) * bdm.astype(F32) for b, x in zip(p["bk"], uv)]
        yield
        upd = [sum(f[h * N:(h + 1) * N] for h in range(HPG)) for f in full]
        state[:] = [g * s + d for g, s, d in zip(p["g_row"], state, upd)]
        ycat = jnp.concatenate(y, axis=1)
        w = y_ref.shape[2]
        for b in range(nb):
            y_ref[b, c * C:(c + 1) * C, :] = ycat[:, b * w:(b + 1) * w]
        yield

    def run(gen, stages):
        for _ in range(stages):
            next(gen, None)

    prep = prepare(list(range(nch)))
    n_prep = 5 + int(math.log2(C)) - 1
    for c in range(nch):
        seq = advance(c)
        run(prep, 1)
        run(seq, 1)
        run(prep, 1)
        run(seq, 2)
    run(prep, n_prep)
    for g in range(ngroups):
        h_ref[g] = state[g]


def _cast_rows(rows, n_blocks):
    return next(r for r in range(BF16_SUBLANES, rows + 1, BF16_SUBLANES) if rows % r == 0 and rows // r <= n_blocks)


def _rwkv_scan(r, lw, k, v, a, b, w_up, w_down, layer, *, nb):
    T, W = r.shape
    S = T // nb
    C, nch, GW = SCAN_CHUNK, SCAN_CHUNKS_PER_STEP, SCAN_GROUP_W
    assert S % (C * nch) == 0 and (nb * W) % GW == 0 and W % 128 == 0
    ngroups = nb * W // GW
    head = np.arange(GW) // HEAD_DIM
    bdm = jnp.asarray((head[:, None] == head[None, :]).astype(np.float32), dtype=BF16)
    n_blocks = S // (C * nch)
    in_spec = pl.BlockSpec((nb, C * nch, W), lambda s: (0, jnp.minimum(s, n_blocks - 1), 0))
    out_spec = pl.BlockSpec((nb, C * nch, W), lambda s: (0, jnp.maximum(s - 1, 0), 0))
    args = [t.reshape(nb, S, W) for t in (r, lw, k, v, a, b)]
    w_in_specs, w_out_specs, w_shapes = [], [], []
    for w in (w_up, w_down):
        _, rows, cols = w.shape
        br = _cast_rows(rows, n_blocks)
        last = rows // br - 1
        w_in_specs.append(pl.BlockSpec((None, br, cols), lambda s, last=last: (layer, jnp.minimum(s, last), 0)))
        w_out_specs.append(pl.BlockSpec((br, cols), lambda s, last=last: (jnp.minimum(s, last), 0)))
        w_shapes.append(jax.ShapeDtypeStruct((rows, cols), BF16))
    y, w_up_bf, w_down_bf = pl.pallas_call(
        functools.partial(_scan_kernel, nb=nb, nch=nch),
        grid=(n_blocks + 1,),
        in_specs=[in_spec] * 6 + [pl.BlockSpec((GW, GW), lambda s: (0, 0))] + w_in_specs,
        out_specs=[out_spec] + w_out_specs,
        out_shape=[jax.ShapeDtypeStruct((nb, S, W), F32)] + w_shapes,
        scratch_shapes=[pltpu.VMEM((ngroups, HEAD_DIM, GW), F32)]
        + [pltpu.VMEM((nch * ngroups, rows * C, GW), dt) for _, rows, dt in _SCAN_STASH],
        compiler_params=pltpu.CompilerParams(dimension_semantics=("arbitrary",), vmem_limit_bytes=VMEM_LIMIT_BYTES),
        name="rwkv_scan",
    )(*args, bdm, w_up, w_down)
    return y.reshape(T, W), w_up_bf, w_down_bf


D_MODEL = 1024
ATTN_HEADS = 6
ATTN_KV_HEADS = 2
ATTN_GROUP = ATTN_HEADS // ATTN_KV_HEADS
ATTN_W = ATTN_HEADS * HEAD_DIM
KV_W = ATTN_KV_HEADS * HEAD_DIM
BLOCK_Q = 128
WINDOW = 128
N_BUCKETS = 32
MAX_EXACT = N_BUCKETS // 2
RWKV_W = 6 * HEAD_DIM
LORA_W = 128
SHIFT_W = 3 * RWKV_W + LORA_W
MEM_HEADS = 4
MEM_W = MEM_HEADS * HEAD_DIM
IN_BASE = ATTN_W + 2 * KV_W + SHIFT_W + MEM_W
PB_OFF = ATTN_W + 2 * KV_W
QM_OFF = PB_OFF + SHIFT_W
D_FF = 2816
EPS = 1e-6
GN_EPS = 64e-5
L2_EPS = 1e-12
MXU_WIDTH = 256

PROJ_TM = 1024
PROJ_SUB = 256
MIX_TQ = 512
FFN_TM = 1024
FFN_FC = 256
WEIGHT_STAGE_CHUNKS = 4


def _block_diag_ones(width):
    idx = np.arange(width) // HEAD_DIM
    return jnp.asarray((idx[:, None] == idx[None, :]).astype(np.float32), dtype=BF16)


def _head_sum(t, bd):
    tb = _bf(t)
    width = t.shape[1]
    parts = [_dot(tb[:, lo:min(lo + MXU_WIDTH, width)], bd[lo:min(lo + MXU_WIDTH, width), lo:min(lo + MXU_WIDTH, width)])
             for lo in range(0, width, MXU_WIDTH)]
    return parts[0] if len(parts) == 1 else jnp.concatenate(parts, axis=1)


def _head_rms(t, bd, gain):
    ms = _head_sum(t * t, bd) * (1.0 / HEAD_DIM)
    return t * lax.rsqrt(ms + EPS) * gain


def _rms_rows(x, g):
    ms = jnp.mean(x * x, axis=-1, keepdims=True)
    return x * lax.rsqrt(ms + EPS) * g


class _Row:
    def __init__(self, table_ref, row, width):
        self.table_ref, self.row, self.width = table_ref, row, width

    def __getitem__(self, _):
        return self.table_ref[self.row:self.row + 1, 0:self.width]


def _proj_kernel(*refs, tm, sub, tiles_per_batch, has_vres, layer):
    it = iter(refs)
    x_ref, vec_ref, w_hbm, lora_ref = next(it), next(it), next(it), next(it)
    bd384_ref, bd128_ref, bd256_ref = next(it), next(it), next(it)
    if has_vres:
        wv_ref, vfirst_ref = next(it), next(it)
    qa_ref, ka_out_ref, va_ref, qm_ref = next(it), next(it), next(it), next(it)
    r_ref, lw_ref, k_ref, v_ref, a_ref, b_ref, gate_ref = (next(it) for _ in range(7))
    pbs_ref, w_ref, w_stage, w_sem = next(it), next(it), next(it), next(it)
    sw = pbs_ref.shape[1]

    @pl.when(pl.program_id(0) == 0)
    def _():
        _stage_weights(w_hbm, layer, w_ref, w_stage, w_sem)
        if has_vres:
            w_ref[:, IN_BASE:IN_BASE + LORA_W] = _bf(wv_ref[...])

    g_ref, qg_ref, kg_ref, mg_ref = (_Row(vec_ref, VEC_ROWS[n], w) for n, w in (
        ("mix_norm_g", D_MODEL), ("q_gain", ATTN_W), ("k_gain", KV_W), ("mq_gain", MEM_W)))
    mu_ref = _Row(vec_ref, VEC_ROWS["mu"], sw)
    w0_ref, a0_ref, kk_ref, ka_ref, v0_ref = (_Row(vec_ref, VEC_ROWS[n], RWKV_W) for n in ("w0", "a0", "k_k", "k_a", "v0"))
    w2_ref, a2_ref, g2_ref, v2_ref = (lora_ref.at[j] for j in range(4))

    @pl.when(pl.program_id(0) % tiles_per_batch == 0)
    def _():
        pbs_ref[0:8, :] = jnp.zeros((8, sw), F32)

    def project(i):
        rows = slice(i * sub, (i + 1) * sub)
        return _dot(_bf(_rms_rows(x_ref[rows, :], g_ref[...])), w_ref[...])

    def attention_outputs(i, proj):
        rows = slice(i * sub, (i + 1) * sub)
        qa_ref[rows, :] = _bf(_head_rms(proj[:, :ATTN_W], bd384_ref[...], qg_ref[...]))
        ka_out_ref[rows, :] = _bf(_head_rms(proj[:, ATTN_W:ATTN_W + KV_W], bd128_ref[...], kg_ref[...]))
        va_ref[rows, :] = _bf(proj[:, ATTN_W + KV_W:PB_OFF])
        qm_ref[rows, :] = _bf(_head_rms(proj[:, QM_OFF:QM_OFF + MEM_W], bd256_ref[...], mg_ref[...]))

    def rwkv_outputs(i, proj):
        rows = slice(i * sub, (i + 1) * sub)
        lo = 8 + i * sub
        pbs_ref[lo:lo + sub, 0:SHIFT_W] = proj[:, PB_OFF:PB_OFF + SHIFT_W]
        if has_vres:
            pbs_ref[lo:lo + sub, SHIFT_W:sw] = proj[:, IN_BASE:IN_BASE + LORA_W]
        cur = pbs_ref[lo:lo + sub, :]
        prev = pbs_ref[lo - 1:lo - 1 + sub, :]
        sh = cur + mu_ref[...] * (prev - cur)
        r = sh[:, 0:RWKV_W]
        k = sh[:, RWKV_W:2 * RWKV_W]
        v = sh[:, 2 * RWKV_W:3 * RWKV_W]
        z = sh[:, 3 * RWKV_W:SHIFT_W]
        t = w0_ref[...] + _dot(_bf(jnp.tanh(z)), w2_ref[...])
        lw_ref[rows, :] = -math.exp(-0.5) * jax.nn.sigmoid(t)
        a = jax.nn.sigmoid(a0_ref[...] + _dot(_bf(z), a2_ref[...]))
        gate_ref[rows, :] = _bf(_dot(_bf(jax.nn.sigmoid(z)), g2_ref[...]))
        if has_vres:
            vd = sh[:, SHIFT_W:sw]
            v = v + (vfirst_ref[rows, :].astype(F32) - v) * jax.nn.sigmoid(v0_ref[...] + _dot(_bf(vd), v2_ref[...]))
        kk = k * kk_ref[...]
        kk = kk / jnp.maximum(jnp.sqrt(_head_sum(kk * kk, bd384_ref[...])), L2_EPS)
        r_ref[rows, :] = _bf(r)
        k_ref[rows, :] = _bf(k * (1.0 + (a - 1.0) * ka_ref[...]))
        v_ref[rows, :] = _bf(v)
        a_ref[rows, :] = _bf(-kk)
        b_ref[rows, :] = _bf(kk * a)

    n_sub = tm // sub
    proj = project(0)
    for i in range(n_sub):
        attention_outputs(i, proj)
        nxt = project(i + 1) if i + 1 < n_sub else None
        rwkv_outputs(i, proj)
        proj = nxt
    pbs_ref[0:8, :] = pbs_ref[tm:tm + 8, :]


def _layer_spec(stacked, l):
    return pl.BlockSpec((None,) + stacked.shape[1:], lambda i: (l,) + (0,) * (stacked.ndim - 1),
                        pipeline_mode=pl.Buffered(1))


VEC_W = SHIFT_W + LORA_W
VEC_ROWS = {n: i for i, n in enumerate((
    "mix_norm_g", "q_gain", "k_gain", "mq_gain", "mu", "w0", "a0", "k_k", "k_a", "v0", "ln_w", "ln_b", "r_k",
    "ffn_norm_g"))}
VEC_TABLE_ROWS = 16


_PACKED = ("mix_norm_g", "attn_q_norm", "attn_k_norm", "mem_q_norm", "rwkv_mu", "rwkv_mu_vres", "rwkv_w0", "rwkv_a0",
           "rwkv_k_k", "rwkv_k_a", "rwkv_v0", "rwkv_ln_w", "rwkv_ln_b", "rwkv_r_k", "ffn_norm_g",
           "rwkv_w2", "rwkv_a2", "rwkv_g2", "rwkv_v2")


def _pack_kernel(*refs, layers):
    src = dict(zip(_PACKED, refs))
    sinks_ref, rel_bias_ref, bucket_ref, vec_ref, lora_ref, sink_ref, bias_ref = refs[len(_PACKED):]
    _bias_table(rel_bias_ref, bucket_ref, bias_ref)
    scale = HEAD_DIM ** -0.5
    vec_ref[...] = jnp.zeros_like(vec_ref)
    lora_ref[...] = jnp.zeros_like(lora_ref)
    tiled = lambda v, n: jnp.concatenate([v] * n, axis=1)

    def put(l, name, val):
        vec_ref[l, VEC_ROWS[name]:VEC_ROWS[name] + 1, 0:val.shape[1]] = val

    for l in range(layers):
        row = lambda n, at=l: src[n][at:at + 1, :]
        put(l, "mix_norm_g", row("mix_norm_g"))
        put(l, "q_gain", tiled(row("attn_q_norm"), ATTN_HEADS) * scale)
        put(l, "k_gain", tiled(row("attn_k_norm"), ATTN_KV_HEADS))
        put(l, "mq_gain", tiled(row("mem_q_norm"), MEM_HEADS) * scale)
        put(l, "mu", row("rwkv_mu"))
        for name, key in (("w0", "rwkv_w0"), ("a0", "rwkv_a0"), ("k_k", "rwkv_k_k"), ("k_a", "rwkv_k_a"),
                          ("ln_w", "rwkv_ln_w"), ("ln_b", "rwkv_ln_b"), ("ffn_norm_g", "ffn_norm_g")):
            put(l, name, row(key))
        r_k = src["rwkv_r_k"][l]
        put(l, "r_k", jnp.concatenate([r_k[h:h + 1, :] for h in range(r_k.shape[0])], axis=1))
        for j, (key, at) in enumerate((("rwkv_w2", 0), ("rwkv_a2", 32), ("rwkv_g2", 64))):
            w = src[key][l]
            lora_ref[l, j, at:at + w.shape[0], :] = _bf(w)
        if l > 0:
            mu_v = row("rwkv_mu_vres", l - 1)
            vec_ref[l, VEC_ROWS["mu"]:VEC_ROWS["mu"] + 1, SHIFT_W:SHIFT_W + mu_v.shape[1]] = mu_v
            put(l, "v0", row("rwkv_v0", l - 1))
            v2 = src["rwkv_v2"][l - 1]
            lora_ref[l, 3, 0:v2.shape[0], :] = _bf(v2)
        for h in range(ATTN_HEADS):
            g, i = divmod(h, ATTN_GROUP)
            sink_ref[l, g, i * BLOCK_Q:(i + 1) * BLOCK_Q, :] = jnp.full((BLOCK_Q, 1), sinks_ref[l, h], F32)


def _pack_params(p):
    layers = p["w_in"].shape[0]
    srcs = [p[n].astype(F32) for n in _PACKED]
    vm = pl.BlockSpec(memory_space=pltpu.VMEM)
    sm = pl.BlockSpec(memory_space=pltpu.SMEM)
    return pl.pallas_call(
        functools.partial(_pack_kernel, layers=layers),
        in_specs=[vm] * len(srcs) + [sm, sm, vm],
        out_specs=[vm, vm, vm, vm],
        out_shape=[jax.ShapeDtypeStruct((layers, VEC_TABLE_ROWS, VEC_W), F32),
                   jax.ShapeDtypeStruct((layers, 4, LORA_W, RWKV_W), BF16),
                   jax.ShapeDtypeStruct((layers, ATTN_KV_HEADS, ATTN_GROUP * BLOCK_Q, 1), F32),
                   jax.ShapeDtypeStruct((ATTN_HEADS, BLOCK_Q, 2 * BLOCK_Q), F32)],
        name="pack_params",
    )(*srcs, p["attn_sinks"].astype(F32), p["rel_bias"].astype(F32), jnp.asarray(_bucket_table()))


def _proj_call(x, p, l, v_first, *, nb):
    T, D = x.shape
    tm = PROJ_TM
    assert T % tm == 0 and (T // nb) % tm == 0
    has_vres = l > 0
    w = p["w_in"]
    ins = [x, p["vectors"], w, p["lora"], _block_diag_ones(ATTN_W), _block_diag_ones(KV_W), _block_diag_ones(MEM_W)]
    if has_vres:
        ins += [jnp.pad(p["w_in_vres"][l - 1], ((0, 0), (0, LORA_W - 16))), v_first]
    nw = IN_BASE + (LORA_W if has_vres else 0)

    def spec(a):
        if a is w:
            return pl.BlockSpec(memory_space=pl.ANY)
        if a.ndim >= 3:
            return _layer_spec(a, l)
        if a.shape[0] == T:
            return pl.BlockSpec((tm, a.shape[1]), lambda i: (i, 0))
        return pl.BlockSpec(a.shape, lambda i: (0, 0), pipeline_mode=pl.Buffered(1))

    out_shapes = [jax.ShapeDtypeStruct((T, ATTN_W), BF16), jax.ShapeDtypeStruct((T, KV_W), BF16),
                  jax.ShapeDtypeStruct((T, KV_W), BF16), jax.ShapeDtypeStruct((T, MEM_W), BF16)]
    out_shapes += [jax.ShapeDtypeStruct((T, RWKV_W), F32 if j == 1 else BF16) for j in range(7)]
    sw = SHIFT_W + (LORA_W if has_vres else 0)
    return pl.pallas_call(
        functools.partial(_proj_kernel, tm=tm, sub=PROJ_SUB, tiles_per_batch=(T // nb) // tm, has_vres=has_vres,
                          layer=l),
        grid=(T // tm,),
        in_specs=[spec(a) for a in ins],
        out_specs=[pl.BlockSpec((tm, s.shape[1]), lambda i: (i, 0)) for s in out_shapes],
        out_shape=out_shapes,
        scratch_shapes=[pltpu.VMEM((tm + 8, sw), F32), pltpu.VMEM((D, nw), BF16),
                        pltpu.VMEM((2, D // WEIGHT_STAGE_CHUNKS, IN_BASE), F32), pltpu.SemaphoreType.DMA((2,))],
        compiler_params=pltpu.CompilerParams(dimension_semantics=("arbitrary",), vmem_limit_bytes=VMEM_LIMIT_BYTES),
        name=f"proj_l{l}",
    )(*ins)


def _bucket_table():
    qi = np.arange(BLOCK_Q)[:, None]
    kj = np.arange(2 * BLOCK_Q)[None, :]
    dist = qi + BLOCK_Q - kj
    in_band = (dist >= 0) & (dist < WINDOW)
    d = np.maximum(dist, 1).astype(np.float32)
    large = MAX_EXACT + (np.log(d / np.float32(MAX_EXACT)) / np.float32(math.log(WINDOW / MAX_EXACT))
                         * np.float32(N_BUCKETS - MAX_EXACT)).astype(np.int32)
    large = np.minimum(large, N_BUCKETS - 1)
    bucket = np.where(dist < MAX_EXACT, np.maximum(dist, 0), large)
    return np.where(in_band, bucket, -1).astype(np.int32)


def _bias_table(rb_ref, bucket_ref, out_ref):
    bucket = bucket_ref[...]
    for h in range(ATTN_HEADS):
        acc = jnp.full(bucket.shape, -jnp.inf, F32)
        for j in range(N_BUCKETS):
            acc = jnp.where(bucket == j, rb_ref[j, h], acc)
        out_ref[h] = acc


def _memkv_kernel(mem_ref, g_ref, w_ref, kn_ref, bd_ref, mk_ref, mv_ref):
    mem = mem_ref[...]
    for l in range(w_ref.shape[0]):
        hn = _bf(_rms_rows(mem, g_ref[l:l + 1, :]))
        kv = _dot(hn, _bf(w_ref[l]))
        gain = jnp.concatenate([kn_ref[l:l + 1, :]] * MEM_HEADS, axis=1)
        mk_ref[l] = _bf(_head_rms(kv[:, :MEM_W], bd_ref[...], gain))
        mv_ref[l] = _bf(kv[:, MEM_W:])


def _memkv_call(mem2d, p):
    rows = mem2d.shape[0]
    layers = p["w_mem_kv"].shape[0]
    vm = pl.BlockSpec(memory_space=pltpu.VMEM)
    return pl.pallas_call(
        _memkv_kernel,
        in_specs=[vm] * 5,
        out_specs=[vm, vm],
        out_shape=[jax.ShapeDtypeStruct((layers, rows, MEM_W), BF16)] * 2,
        compiler_params=pltpu.CompilerParams(vmem_limit_bytes=VMEM_LIMIT_BYTES),
        name="mem_kv",
    )(mem2d, p["mem_norm_g"], p["w_mem_kv"], p["mem_k_norm"], _block_diag_ones(MEM_W))


def _mix_kernel(x_ref, qa_ref, kc_ref, kp_ref, vc_ref, vp_ref, bias_ref, sink_ref, qm_ref, mk_ref, mv_ref,
                y_ref, r_ref, k_ref, v_ref, gate_ref, vec_ref, bd_ref, wout_hbm, out_ref, att_ref,
                wout_ref, w_stage, w_sem, *, tq, tiles_per_batch, layer):
    lnw_ref, lnb_ref, rk_ref = (_Row(vec_ref, VEC_ROWS[n], RWKV_W) for n in ("ln_w", "ln_b", "r_k"))
    N, BQ = HEAD_DIM, BLOCK_Q
    seq_start = pl.program_id(0) % tiles_per_batch == 0
    nqb = tq // BQ

    @pl.when(pl.program_id(0) == 0)
    def _():
        att_ref[...] = jnp.zeros_like(att_ref)
        _stage_weights(wout_hbm, layer, wout_ref, w_stage, w_sem)

    qa = qa_ref[...]
    kall = jnp.concatenate([kp_ref[...], kc_ref[...]], axis=0)
    vall = jnp.concatenate([vp_ref[...], vc_ref[...]], axis=0)
    k_group = [kall[:, g * N:(g + 1) * N] for g in range(ATTN_KV_HEADS)]
    v_group = [vall[:, g * N:(g + 1) * N] for g in range(ATTN_KV_HEADS)]
    before_seq = lax.broadcasted_iota(jnp.int32, (ATTN_GROUP * BQ, 2 * BQ), 1) < BQ

    qm = qm_ref[...]
    mk = mk_ref[0]
    mv = mv_ref[0]

    swa = [(j, g) for j in range(nqb) for g in range(ATTN_KV_HEADS)]
    logits, values, sinks = [], [], []
    for j, g in swa:
        qg = jnp.concatenate(
            [qa[j * BQ:(j + 1) * BQ, (ATTN_GROUP * g + i) * N:(ATTN_GROUP * g + i + 1) * N] for i in range(ATTN_GROUP)],
            axis=0)
        lg = _dot_nt(qg, k_group[g][j * BQ:(j + 2) * BQ]) + bias_ref[g]
        if j == 0:
            lg = jnp.where(jnp.logical_and(seq_start, before_seq), -jnp.inf, lg)
        logits.append(lg)
        values.append(v_group[g][j * BQ:(j + 2) * BQ])
        sinks.append(sink_ref[g])
    for h in range(MEM_HEADS):
        hs = slice(h * N, (h + 1) * N)
        logits.append(_dot_nt(qm[:, hs], mk[:, hs]))
        values.append(mv[:, hs])
        sinks.append(None)
    bd = bd_ref[...]
    y = y_ref[...]
    d = y - _head_sum(y, bd) * (1.0 / N)
    var = _head_sum(d * d, bd) * (1.0 / N)
    yn = d * lax.rsqrt(var + GN_EPS) * lnw_ref[...] + lnb_ref[...]
    f32 = lambda ref: ref[...].astype(F32)
    bonus = _head_sum(f32(r_ref) * f32(k_ref) * rk_ref[...], bd) * f32(v_ref)
    out_b = (yn + bonus) * f32(gate_ref)
    mixed = jnp.concatenate([att_ref[:, :ATTN_W], _bf(out_b), att_ref[:, ATTN_W:]], axis=-1)
    out_ref[...] = x_ref[...] + _dot(mixed, wout_ref[...])

    row_max = [jnp.max(lg, axis=-1, keepdims=True) for lg in logits]
    m = [rm if s is None else jnp.maximum(rm, s) for rm, s in zip(row_max, sinks)]
    e = [jnp.exp(lg - mm) for lg, mm in zip(logits, m)]
    denom = [jnp.sum(ee, axis=-1, keepdims=True) for ee in e]
    denom = [d if s is None else d + jnp.exp(s - mm) for d, s, mm in zip(denom, sinks, m)]
    outs = [_bf(_dot(_bf(ee), vv) / d) for ee, vv, d in zip(e, values, denom)]

    head_rows = [[None] * nqb for _ in range(ATTN_HEADS)]
    for (j, g), o in zip(swa, outs):
        for i in range(ATTN_GROUP):
            head_rows[ATTN_GROUP * g + i][j] = o[i * BQ:(i + 1) * BQ]
    att_ref[:, :ATTN_W] = jnp.concatenate([jnp.concatenate(rows, axis=0) for rows in head_rows], axis=-1)
    att_ref[:, ATTN_W:] = jnp.concatenate(outs[len(swa):], axis=-1)


def _mix_call(x, qa, ka, va, qm, mk, mv, y, r, k, v, gate, bias, p, l, *, nb):
    T, D = x.shape
    tq = MIX_TQ
    S = T // nb
    assert S % tq == 0 and tq % BLOCK_Q == 0
    tpb = S // tq
    qpb = tq // BLOCK_Q
    mem_tokens = mk.shape[1] // nb
    sink = p["sink_columns"]
    n_tiles = T // tq
    att = lambda i: jnp.minimum(i, n_tiles - 1)
    fin = lambda i: jnp.maximum(i - 1, 0)
    att_tile = lambda w: pl.BlockSpec((tq, w), lambda i: (att(i), 0))
    tile = lambda w: pl.BlockSpec((tq, w), lambda i: (fin(i), 0))
    prev = pl.BlockSpec((BLOCK_Q, KV_W), lambda i: (jnp.maximum(att(i) * qpb - 1, 0), 0))
    full = lambda a: pl.BlockSpec(a.shape, lambda i: (0,) * a.ndim)
    memspec = pl.BlockSpec((None, 1, mem_tokens, MEM_W), lambda i: (l, att(i) // tpb, 0, 0))
    bd = _block_diag_ones(RWKV_W)
    wout = p["w_out"]
    mk3 = mk.reshape(mk.shape[0], nb, mem_tokens, MEM_W)
    mv3 = mv.reshape(mv.shape[0], nb, mem_tokens, MEM_W)
    return pl.pallas_call(
        functools.partial(_mix_kernel, tq=tq, tiles_per_batch=tpb, layer=l),
        grid=(n_tiles + 1,),
        in_specs=[tile(D), att_tile(ATTN_W), att_tile(KV_W), prev, att_tile(KV_W), prev, full(bias), _layer_spec(sink, l),
                  att_tile(MEM_W), memspec, memspec] + [tile(RWKV_W)] * 5
                 + [_layer_spec(p["vectors"], l), full(bd), pl.BlockSpec(memory_space=pl.ANY)],
        out_specs=tile(D),
        out_shape=jax.ShapeDtypeStruct((T, D), F32),
        scratch_shapes=[pltpu.VMEM((tq, ATTN_W + MEM_W), BF16), pltpu.VMEM((D, D), BF16),
                        pltpu.VMEM((2, D // WEIGHT_STAGE_CHUNKS, D), F32), pltpu.SemaphoreType.DMA((2,))],
        compiler_params=pltpu.CompilerParams(dimension_semantics=("arbitrary",), vmem_limit_bytes=VMEM_LIMIT_BYTES),
        name=f"mix_l{l}",
    )(x, qa, ka, ka, va, va, bias, sink, qm, mk3, mv3, y, r, k, v, gate, p["vectors"], bd, wout)


SUBLANES = 8


def _tile_copies(hbm, buf, sem, tile, slot, to_vmem):
    copies = []
    for s in range(SUBLANES):
        src, dst = hbm.at[tile, s], buf.at[slot, :, s, :]
        if not to_vmem:
            src, dst = dst, src
        copies.append(pltpu.make_async_copy(src, dst, sem.at[slot, s]))
    return copies


def _shift_rows(u, before, steps):
    first = lax.broadcasted_iota(jnp.int32, (SUBLANES, u.shape[1]), 0) == 0
    tm = u.shape[0]
    heads = []
    for i in range(steps):
        lo = tm - (steps - i) * SUBLANES
        wrapped = pltpu.roll(u[lo:lo + SUBLANES], 1, axis=0)
        prior = jnp.broadcast_to(before[SUBLANES - steps + i:SUBLANES - steps + i + 1], wrapped.shape)
        heads.append(jnp.where(first, prior, wrapped))
    return jnp.concatenate(heads + [u[:tm - steps * SUBLANES]], axis=0)


def _stage_weights(w_hbm, layer, dst_ref, stage_ref, sem):
    rows = stage_ref.shape[1]
    n = dst_ref.shape[0] // rows
    copy = lambda c: pltpu.make_async_copy(w_hbm.at[layer, pl.ds(c * rows, rows), :], stage_ref.at[c % 2], sem.at[c % 2])
    copy(0).start()
    for c in range(n):
        if c + 1 < n:
            copy(c + 1).start()
        copy(c).wait()
        dst_ref[c * rows:(c + 1) * rows, 0:stage_ref.shape[2]] = _bf(stage_ref[c % 2])


def _ffn_kernel(x_hbm, vec_ref, wup_ref, cw_ref, cb_ref, wdn_ref, o_hbm, xbuf, obuf, carry_ref, act_ref,
                in_sem, out_sem, *, tm, tiles_per_batch, n_tiles, layer):
    D = xbuf.shape[-1]
    g_ref = _Row(vec_ref, VEC_ROWS["ffn_norm_g"], D)
    q = tm // SUBLANES
    i = pl.program_id(0)
    slot = i % 2

    @pl.when(i == 0)
    def _():
        carry_ref[...] = jnp.zeros_like(carry_ref)
        for cp in _tile_copies(x_hbm, xbuf, in_sem, 0, 0, True):
            cp.start()

    @pl.when(i + 1 < n_tiles)
    def _():
        for cp in _tile_copies(x_hbm, xbuf, in_sem, i + 1, 1 - slot, True):
            cp.start()

    for cp in _tile_copies(x_hbm, xbuf, in_sem, i, slot, True):
        cp.wait()

    @pl.when(i >= 2)
    def _():
        for cp in _tile_copies(o_hbm, obuf, out_sem, i - 2, slot, False):
            cp.wait()

    seq_start = i % tiles_per_batch == 0
    g = g_ref[...]
    x = xbuf[slot].reshape(tm, D)
    before = jnp.where(seq_start, 0.0, _rms_rows(carry_ref[...], g))
    for r, grp in ((SUBLANES - 2, q - 2), (SUBLANES - 1, q - 1)):
        row = grp * SUBLANES + SUBLANES - 1
        carry_ref[r:r + 1, :] = x[row:row + 1]
    h_ext = _bf(jnp.concatenate([before, _rms_rows(x, g)], axis=0))

    n_chunks = D_FF // FFN_FC
    gate_cols = lambda c: slice(c * FFN_FC, (c + 1) * FFN_FC)
    val_cols = lambda c: slice(D_FF + c * FFN_FC, D_FF + (c + 1) * FFN_FC)

    def conv(u_ext, cols):
        w = cw_ref[:, cols]
        u = u_ext[SUBLANES:]
        return (cb_ref[layer:layer + 1, cols] + w[0:1] * _shift_rows(u, u_ext[:SUBLANES], 2)
                + w[1:2] * _shift_rows(u, u_ext[:SUBLANES], 1) + w[2:3] * u)

    up = lambda c: (_dot(h_ext, wup_ref[:, gate_cols(c)]), _dot(h_ext, wup_ref[:, val_cols(c)]))
    nxt = up(0)
    for c in range(n_chunks):
        ug, uv = nxt
        if c + 1 < n_chunks:
            nxt = up(c + 1)
        gt = conv(ug, gate_cols(c))
        act_ref[:, gate_cols(c)] = _bf(gt * jax.nn.sigmoid(gt) * conv(uv, val_cols(c)))
    obuf[slot] = (x + _dot(act_ref[...], wdn_ref[...])).reshape(q, SUBLANES, D)

    for cp in _tile_copies(o_hbm, obuf, out_sem, i, slot, False):
        cp.start()

    @pl.when(i == n_tiles - 1)
    def _():
        if n_tiles >= 2:
            for cp in _tile_copies(o_hbm, obuf, out_sem, i - 1, 1 - slot, False):
                cp.wait()
        for cp in _tile_copies(o_hbm, obuf, out_sem, i, slot, False):
            cp.wait()


def _ffn_call(x, p, l, wup, wdn, *, nb):
    T, D = x.shape
    tm = FFN_TM
    S = T // nb
    q = tm // SUBLANES
    assert S % tm == 0 and q % SUBLANES == 0 and D_FF % FFN_FC == 0
    n_tiles = T // tm
    cw, cb = p["conv_w"], p["conv_b"]
    hbm = pl.BlockSpec(memory_space=pl.ANY)
    whole = lambda a: pl.BlockSpec(a.shape, lambda i: (0, 0), pipeline_mode=pl.Buffered(1))
    out = pl.pallas_call(
        functools.partial(_ffn_kernel, tm=tm, tiles_per_batch=S // tm, n_tiles=n_tiles, layer=l),
        grid=(n_tiles,),
        in_specs=[hbm, _layer_spec(p["vectors"], l), whole(wup), _layer_spec(cw, l), whole(cb), whole(wdn)],
        out_specs=hbm,
        out_shape=jax.ShapeDtypeStruct((n_tiles, SUBLANES, q, D), F32),
        scratch_shapes=[pltpu.VMEM((2, q, SUBLANES, D), F32), pltpu.VMEM((2, q, SUBLANES, D), F32),
                        pltpu.VMEM((SUBLANES, D), F32), pltpu.VMEM((tm, D_FF), BF16),
                        pltpu.SemaphoreType.DMA((2, SUBLANES)), pltpu.SemaphoreType.DMA((2, SUBLANES))],
        compiler_params=pltpu.CompilerParams(dimension_semantics=("arbitrary",), vmem_limit_bytes=VMEM_LIMIT_BYTES),
        name=f"ffn_l{l}",
    )(x.reshape(n_tiles, SUBLANES, q, D), p["vectors"], wup, cw, cb, wdn)
    return out.reshape(T, D)


_PARAM_NAMES = (
    "rel_bias", "mix_norm_g", "w_in", "w_in_vres", "attn_q_norm", "attn_k_norm", "attn_sinks", "rwkv_mu",
    "rwkv_mu_vres", "rwkv_w0", "rwkv_w2", "rwkv_a0", "rwkv_a2", "rwkv_v0", "rwkv_v2", "rwkv_g2", "rwkv_k_k",
    "rwkv_k_a", "rwkv_r_k", "rwkv_ln_w", "rwkv_ln_b", "mem_norm_g", "w_mem_kv", "mem_q_norm", "mem_k_norm",
    "w_out", "ffn_norm_g", "w_up", "conv_w", "conv_b", "w_down")


def kernel(x, mem, rel_bias, mix_norm_g, w_in, w_in_vres, attn_q_norm, attn_k_norm, attn_sinks, rwkv_mu,
           rwkv_mu_vres, rwkv_w0, rwkv_w2, rwkv_a0, rwkv_a2, rwkv_v0, rwkv_v2, rwkv_g2, rwkv_k_k, rwkv_k_a,
           rwkv_r_k, rwkv_ln_w, rwkv_ln_b, mem_norm_g, w_mem_kv, mem_q_norm, mem_k_norm, w_out, ffn_norm_g,
           w_up, conv_w, conv_b, w_down):
    p = dict(zip(_PARAM_NAMES, (
        rel_bias, mix_norm_g, w_in, w_in_vres, attn_q_norm, attn_k_norm, attn_sinks, rwkv_mu, rwkv_mu_vres,
        rwkv_w0, rwkv_w2, rwkv_a0, rwkv_a2, rwkv_v0, rwkv_v2, rwkv_g2, rwkv_k_k, rwkv_k_a, rwkv_r_k, rwkv_ln_w,
        rwkv_ln_b, mem_norm_g, w_mem_kv, mem_q_norm, mem_k_norm, w_out, ffn_norm_g, w_up, conv_w, conv_b, w_down)))
    p["vectors"], p["lora"], p["sink_columns"], bias = _pack_params(p)
    bias = bias.reshape(ATTN_KV_HEADS, ATTN_GROUP * BLOCK_Q, 2 * BLOCK_Q)
    nb, S, D = x.shape
    xt = x.reshape(nb * S, D)
    mem2d = mem.reshape(nb * mem.shape[1], D)
    mk, mv = _memkv_call(mem2d, p)
    v_first = None
    for l in range(w_in.shape[0]):
        qa, ka, va, qm, r, lw, k, v, a, b, gate = _proj_call(xt, p, l, v_first, nb=nb)
        if l == 0:
            v_first = v
        y, w_up_bf, w_down_bf = _rwkv_scan(r, lw, k, v, a, b, w_up, w_down, l, nb=nb)
        xt = _mix_call(xt, qa, ka, va, qm, mk, mv, y, r, k, v, gate, bias, p, l, nb=nb)
        xt = _ffn_call(xt, p, l, w_up_bf, w_down_bf, nb=nb)
    return xt.reshape(nb, S, D)
```

```python
import functools
import math

import jax
import jax.numpy as jnp
import numpy as np
from jax import lax
from jax.experimental import pallas as pl
from jax.experimental.pallas import tpu as pltpu

F32 = jnp.float32
BF16 = jnp.bfloat16

HEAD_DIM = 64
SCAN_CHUNK = 64
VMEM_LIMIT_BYTES = 56 * 1024 * 1024


def _dot(a, b):
    return jnp.dot(a, b, preferred_element_type=F32)


def _dot_nt(a, b):
    return lax.dot_general(a, b, (((1,), (1,)), ((), ())), preferred_element_type=F32)


def _dot_tn(a, b):
    return lax.dot_general(a, b, (((0,), (0,)), ((), ())), preferred_element_type=F32)


def _bf(x):
    return x.astype(BF16)


SCAN_GROUP_HEADS = 4
SCAN_GROUP_W = SCAN_GROUP_HEADS * HEAD_DIM
SCAN_CHUNKS_PER_STEP = 4
BF16_SUBLANES = 16


_SCAN_STASH = (("w2", 1, F32), ("y_v", 1, F32), ("m_rb", 1, F32), ("v", 1, F32), ("w1r", 2, F32),
               ("bk", 2, F32), ("g_row", 1, F32))


def _scan_kernel(r_ref, lw_ref, k_ref, v_ref, a_ref, b_ref, bdm_ref, *refs, nb, nch, n_riders):
    w_refs, y_ref, w_bf_refs = refs[:n_riders], refs[n_riders], refs[n_riders + 1:2 * n_riders + 1]
    h_ref, stash_refs = refs[2 * n_riders + 1], refs[2 * n_riders + 2:]
    C, N, GW, HPG = SCAN_CHUNK, HEAD_DIM, SCAN_GROUP_W, SCAN_GROUP_HEADS
    ngroups = h_ref.shape[0]
    stash = {name: ref for (name, _, _), ref in zip(_SCAN_STASH, stash_refs)}

    for w_ref, w_bf_ref in zip(w_refs, w_bf_refs):
        w_bf_ref[...] = _bf(w_ref[...])

    @pl.when(pl.program_id(0) == 0)
    def _():
        h_ref[...] = jnp.zeros_like(h_ref)
        for ref in stash_refs:
            ref[...] = jnp.zeros_like(ref)

    trow = lax.broadcasted_iota(jnp.int32, (C, C), 0)
    tcol = lax.broadcasted_iota(jnp.int32, (C, C), 1)
    tri = (trow >= tcol).astype(BF16)
    grow = lax.broadcasted_iota(jnp.int32, (C, GW), 0)
    gcol = lax.broadcasted_iota(jnp.int32, (C, GW), 1) % N
    incl = grow >= gcol
    strict = grow > gcol
    eye = (grow == gcol).astype(F32)
    bdm = bdm_ref[...]

    def bd(x):
        return jnp.concatenate([_bf(x)] * HPG, axis=0) * bdm

    def split2(x):
        hi = _bf(x)
        return hi, _bf(x - hi.astype(F32))

    def chunk_units(c):
        rows = slice(c * C, (c + 1) * C)
        per_batch = []
        for b in range(nb):
            lw = lw_ref[b, rows, :]
            l1 = _bf(lw)
            e1 = lw - l1.astype(F32)
            l2 = _bf(e1)
            l3 = _bf(e1 - l2.astype(F32))
            cum = _dot(tri, l1) + _dot(tri, l2) + _dot(tri, l3)
            cum_last = cum[C - 1:C, :]
            g_inv = jnp.exp(-cum)
            g_out = jnp.exp(cum_last - cum)
            f32 = lambda ref: ref[b, rows, :].astype(F32)
            kk, bb = f32(k_ref), f32(b_ref)
            per_batch.append(dict(
                a_t=f32(a_ref) * jnp.exp(cum - lw), r_t=f32(r_ref) * jnp.exp(cum),
                b_t=bb * g_inv, k_t=kk * g_inv, b_h=bb * g_out, k_h=kk * g_out,
                g_c=jnp.broadcast_to(jnp.exp(cum_last), cum.shape), v=f32(v_ref)))
        cat = {n: jnp.concatenate([pb[n] for pb in per_batch], axis=1) for n in per_batch[0]}
        return [{n: t[:, g * GW:(g + 1) * GW] for n, t in cat.items()} for g in range(ngroups)]

    def prepare(chunks):
        units = [u for c in chunks for u in chunk_units(c)]

        nu = len(units)
        ar = [jnp.concatenate([_bf(u["a_t"]), _bf(u["r_t"])], axis=0) for u in units]
        pb = [_dot_nt(ar[i], bd(units[i]["b_t"])) for i in range(nu)]
        pk = [_dot_nt(ar[i], bd(units[i]["k_t"])) for i in range(nu)]
        l_ab = [jnp.where(strict, p[:C], 0.0) for p in pb]
        m_rb = [_bf(jnp.where(incl, p[C:], 0.0)) for p in pb]
        tril2 = jnp.concatenate([strict, incl], axis=0)
        lm_k = [_bf(jnp.where(tril2, p, 0.0)) for p in pk]
        yield
        t_inv = [eye + l for l in l_ab]
        pw = [_dot(_bf(l), bd(l)) for l in l_ab]
        yield
        n_rounds = int(math.log2(C)) - 1
        for rnd in range(n_rounds):
            if rnd + 1 < n_rounds:
                z = [_dot(jnp.concatenate([_bf(p), _bf(t)], axis=0), bd(p)) for p, t in zip(pw, t_inv)]
                pw = [zz[:C] for zz in z]
                t_inv = [t + zz[C:] for t, zz in zip(t_inv, z)]
            else:
                t_inv = [t + _dot(_bf(t), bd(p)) for t, p in zip(t_inv, pw)]
            yield
        t_b = [_bf(t) for t in t_inv]
        w1 = [_dot(t, bd(u["a_t"])) for t, u in zip(t_b, units)]
        lmv = [_dot(m, bd(u["v"])) for m, u in zip(lm_k, units)]
        yield
        w2 = [_dot(t, bd(x[:C])) for t, x in zip(t_b, lmv)]
        parts = [p for u in units for p in split2(u["g_c"] * eye)]
        sums = _dot(jnp.concatenate(parts, axis=0), bdm)
        yield
        for i, u in enumerate(units):
            put = lambda name, val: stash[name].__setitem__(i, val.astype(stash[name].dtype))
            put("w2", w2[i])
            put("y_v", lmv[i][C:])
            put("m_rb", m_rb[i])
            put("v", u["v"])
            put("w1r", jnp.concatenate([w1[i], u["r_t"]], axis=0))
            put("bk", jnp.concatenate([u["b_h"], u["k_h"]], axis=0))
            put("g_row", sums[2 * i * C:(2 * i + 1) * C] + sums[(2 * i + 1) * C:(2 * i + 2) * C])
        yield

    state = [h_ref[g] for g in range(ngroups)]

    def advance(c):
        p = {n: [ref[c * ngroups + g] for g in range(ngroups)] for n, ref in stash.items()}
        for n in ("m_rb", "v", "w1r", "bk"):
            p[n] = [_bf(x) for x in p[n]]
        z = [_dot(w, bd(s)) for w, s in zip(p["w1r"], state)]
        yield
        u = [zz[:C] + w for zz, w in zip(z, p["w2"])]
        y = [zz[C:] + _dot(m, bd(x)) + yv for zz, m, x, yv in zip(z, p["m_rb"], u, p["y_v"])]
        uv = [jnp.concatenate([_bf(x), v], axis=0) for x, v in zip(u, p["v"])]
        full = [_dot_tn(b, x) * bdm.astype(F32) for b, x in zip(p["bk"], uv)]
        yield
        upd = [sum(f[h * N:(h + 1) * N] for h in range(HPG)) for f in full]
        state[:] = [g * s + d for g, s, d in zip(p["g_row"], state, upd)]
        ycat = jnp.concatenate(y, axis=1)
        w = y_ref.shape[2]
        for b in range(nb):
            y_ref[b, c * C:(c + 1) * C, :] = ycat[:, b * w:(b + 1) * w]
        yield

    def run(gen, stages):
        for _ in range(stages):
            next(gen, None)

    prep = prepare(list(range(nch)))
    n_prep = 5 + int(math.log2(C)) - 1
    for c in range(nch):
        seq = advance(c)
        run(prep, 1)
        run(seq, 1)
        run(prep, 1)
        run(seq, 2)
    run(prep, n_prep)
    for g in range(ngroups):
        h_ref[g] = state[g]


def _cast_rows(rows, n_blocks):
    return next(r for r in range(BF16_SUBLANES, rows + 1, BF16_SUBLANES) if rows % r == 0 and rows // r <= n_blocks)


def _rwkv_scan(r, lw, k, v, a, b, weights, layer, *, nb):
    T, W = r.shape
    S = T // nb
    C, nch, GW = SCAN_CHUNK, SCAN_CHUNKS_PER_STEP, SCAN_GROUP_W
    assert S % (C * nch) == 0 and (nb * W) % GW == 0 and W % 128 == 0
    ngroups = nb * W // GW
    head = np.arange(GW) // HEAD_DIM
    bdm = jnp.asarray((head[:, None] == head[None, :]).astype(np.float32), dtype=BF16)
    n_blocks = S // (C * nch)
    in_spec = pl.BlockSpec((nb, C * nch, W), lambda s: (0, jnp.minimum(s, n_blocks - 1), 0))
    out_spec = pl.BlockSpec((nb, C * nch, W), lambda s: (0, jnp.maximum(s - 1, 0), 0))
    args = [t.reshape(nb, S, W) for t in (r, lw, k, v, a, b)]
    w_in_specs, w_out_specs, w_shapes = [], [], []
    for w in weights:
        _, rows, cols = w.shape
        br = _cast_rows(rows, n_blocks)
        last = rows // br - 1
        w_in_specs.append(pl.BlockSpec((None, br, cols), lambda s, last=last: (layer, jnp.minimum(s, last), 0)))
        w_out_specs.append(pl.BlockSpec((br, cols), lambda s, last=last: (jnp.minimum(s, last), 0)))
        w_shapes.append(jax.ShapeDtypeStruct((rows, cols), BF16))
    y, *weights_bf = pl.pallas_call(
        functools.partial(_scan_kernel, nb=nb, nch=nch, n_riders=len(weights)),
        grid=(n_blocks + 1,),
        in_specs=[in_spec] * 6 + [pl.BlockSpec((GW, GW), lambda s: (0, 0))] + w_in_specs,
        out_specs=[out_spec] + w_out_specs,
        out_shape=[jax.ShapeDtypeStruct((nb, S, W), F32)] + w_shapes,
        scratch_shapes=[pltpu.VMEM((ngroups, HEAD_DIM, GW), F32)]
        + [pltpu.VMEM((nch * ngroups, rows * C, GW), dt) for _, rows, dt in _SCAN_STASH],
        compiler_params=pltpu.CompilerParams(dimension_semantics=("arbitrary",), vmem_limit_bytes=VMEM_LIMIT_BYTES),
        name="rwkv_scan",
    )(*args, bdm, *weights)
    return y.reshape(T, W), weights_bf


D_MODEL = 1024
ATTN_HEADS = 6
ATTN_KV_HEADS = 2
ATTN_GROUP = ATTN_HEADS // ATTN_KV_HEADS
ATTN_W = ATTN_HEADS * HEAD_DIM
KV_W = ATTN_KV_HEADS * HEAD_DIM
BLOCK_Q = 128
WINDOW = 128
N_BUCKETS = 32
MAX_EXACT = N_BUCKETS // 2
RWKV_W = 6 * HEAD_DIM
LORA_W = 128
SHIFT_W = 3 * RWKV_W + LORA_W
MEM_HEADS = 4
MEM_W = MEM_HEADS * HEAD_DIM
IN_BASE = ATTN_W + 2 * KV_W + SHIFT_W + MEM_W
PB_OFF = ATTN_W + 2 * KV_W
QM_OFF = PB_OFF + SHIFT_W
D_FF = 2816
EPS = 1e-6
GN_EPS = 64e-5
L2_EPS = 1e-12
MXU_WIDTH = 256

PROJ_TM = 1024
PROJ_SUB = 256
MIX_TQ = 512
FFN_TM = 1024
FFN_FC = 256
WEIGHT_STAGE_CHUNKS = 4


def _block_diag_ones(width):
    idx = np.arange(width) // HEAD_DIM
    return jnp.asarray((idx[:, None] == idx[None, :]).astype(np.float32), dtype=BF16)


def _head_sum(t, bd):
    tb = _bf(t)
    width = t.shape[1]
    parts = [_dot(tb[:, lo:min(lo + MXU_WIDTH, width)], bd[lo:min(lo + MXU_WIDTH, width), lo:min(lo + MXU_WIDTH, width)])
             for lo in range(0, width, MXU_WIDTH)]
    return parts[0] if len(parts) == 1 else jnp.concatenate(parts, axis=1)


def _head_rms(t, bd, gain):
    ms = _head_sum(t * t, bd) * (1.0 / HEAD_DIM)
    return t * lax.rsqrt(ms + EPS) * gain


def _rms_rows(x, g):
    ms = jnp.mean(x * x, axis=-1, keepdims=True)
    return x * lax.rsqrt(ms + EPS) * g


class _Row:
    def __init__(self, table_ref, row, width):
        self.table_ref, self.row, self.width = table_ref, row, width

    def __getitem__(self, _):
        return self.table_ref[self.row:self.row + 1, 0:self.width]


def _proj_kernel(*refs, tm, sub, tiles_per_batch, has_vres, layer):
    it = iter(refs)
    x_ref, vec_ref, w_hbm, lora_ref = next(it), next(it), next(it), next(it)
    bd384_ref, bd128_ref, bd256_ref = next(it), next(it), next(it)
    if has_vres:
        wv_ref, vfirst_ref = next(it), next(it)
    qa_ref, ka_out_ref, va_ref, qm_ref = next(it), next(it), next(it), next(it)
    r_ref, lw_ref, k_ref, v_ref, a_ref, b_ref, gate_ref = (next(it) for _ in range(7))
    pbs_ref, w_ref, w_stage, w_sem = next(it), next(it), next(it), next(it)
    sw = pbs_ref.shape[1]

    @pl.when(pl.program_id(0) == 0)
    def _():
        _stage_weights(w_hbm, layer, w_ref, w_stage, w_sem)
        if has_vres:
            w_ref[:, IN_BASE:IN_BASE + LORA_W] = _bf(wv_ref[...])

    g_ref, qg_ref, kg_ref, mg_ref = (_Row(vec_ref, VEC_ROWS[n], w) for n, w in (
        ("mix_norm_g", D_MODEL), ("q_gain", ATTN_W), ("k_gain", KV_W), ("mq_gain", MEM_W)))
    mu_ref = _Row(vec_ref, VEC_ROWS["mu"], sw)
    w0_ref, a0_ref, kk_ref, ka_ref, v0_ref = (_Row(vec_ref, VEC_ROWS[n], RWKV_W) for n in ("w0", "a0", "k_k", "k_a", "v0"))
    w2_ref, a2_ref, g2_ref, v2_ref = (lora_ref.at[j] for j in range(4))

    @pl.when(pl.program_id(0) % tiles_per_batch == 0)
    def _():
        pbs_ref[0:8, :] = jnp.zeros((8, sw), F32)

    def project(i):
        rows = slice(i * sub, (i + 1) * sub)
        return _dot(_bf(_rms_rows(x_ref[rows, :], g_ref[...])), w_ref[...])

    def attention_outputs(i, proj):
        rows = slice(i * sub, (i + 1) * sub)
        qa_ref[rows, :] = _bf(_head_rms(proj[:, :ATTN_W], bd384_ref[...], qg_ref[...]))
        ka_out_ref[rows, :] = _bf(_head_rms(proj[:, ATTN_W:ATTN_W + KV_W], bd128_ref[...], kg_ref[...]))
        va_ref[rows, :] = _bf(proj[:, ATTN_W + KV_W:PB_OFF])
        qm_ref[rows, :] = _bf(_head_rms(proj[:, QM_OFF:QM_OFF + MEM_W], bd256_ref[...], mg_ref[...]))

    def rwkv_outputs(i, proj):
        rows = slice(i * sub, (i + 1) * sub)
        lo = 8 + i * sub
        pbs_ref[lo:lo + sub, 0:SHIFT_W] = proj[:, PB_OFF:PB_OFF + SHIFT_W]
        if has_vres:
            pbs_ref[lo:lo + sub, SHIFT_W:sw] = proj[:, IN_BASE:IN_BASE + LORA_W]
        cur = pbs_ref[lo:lo + sub, :]
        prev = pbs_ref[lo - 1:lo - 1 + sub, :]
        sh = cur + mu_ref[...] * (prev - cur)
        r = sh[:, 0:RWKV_W]
        k = sh[:, RWKV_W:2 * RWKV_W]
        v = sh[:, 2 * RWKV_W:3 * RWKV_W]
        z = sh[:, 3 * RWKV_W:SHIFT_W]
        t = w0_ref[...] + _dot(_bf(jnp.tanh(z)), w2_ref[...])
        lw_ref[rows, :] = -math.exp(-0.5) * jax.nn.sigmoid(t)
        a = jax.nn.sigmoid(a0_ref[...] + _dot(_bf(z), a2_ref[...]))
        gate_ref[rows, :] = _bf(_dot(_bf(jax.nn.sigmoid(z)), g2_ref[...]))
        if has_vres:
            vd = sh[:, SHIFT_W:sw]
            v = v + (vfirst_ref[rows, :].astype(F32) - v) * jax.nn.sigmoid(v0_ref[...] + _dot(_bf(vd), v2_ref[...]))
        kk = k * kk_ref[...]
        kk = kk / jnp.maximum(jnp.sqrt(_head_sum(kk * kk, bd384_ref[...])), L2_EPS)
        r_ref[rows, :] = _bf(r)
        k_ref[rows, :] = _bf(k * (1.0 + (a - 1.0) * ka_ref[...]))
        v_ref[rows, :] = _bf(v)
        a_ref[rows, :] = _bf(-kk)
        b_ref[rows, :] = _bf(kk * a)

    n_sub = tm // sub
    proj = project(0)
    for i in range(n_sub):
        attention_outputs(i, proj)
        nxt = project(i + 1) if i + 1 < n_sub else None
        rwkv_outputs(i, proj)
        proj = nxt
    pbs_ref[0:8, :] = pbs_ref[tm:tm + 8, :]


def _layer_spec(stacked, l):
    return pl.BlockSpec((None,) + stacked.shape[1:], lambda i: (l,) + (0,) * (stacked.ndim - 1),
                        pipeline_mode=pl.Buffered(1))


VEC_W = SHIFT_W + LORA_W
VEC_ROWS = {n: i for i, n in enumerate((
    "mix_norm_g", "q_gain", "k_gain", "mq_gain", "mu", "w0", "a0", "k_k", "k_a", "v0", "ln_w", "ln_b", "r_k",
    "ffn_norm_g"))}
VEC_TABLE_ROWS = 16


_PACKED = ("mix_norm_g", "attn_q_norm", "attn_k_norm", "mem_q_norm", "rwkv_mu", "rwkv_mu_vres", "rwkv_w0", "rwkv_a0",
           "rwkv_k_k", "rwkv_k_a", "rwkv_v0", "rwkv_ln_w", "rwkv_ln_b", "rwkv_r_k", "ffn_norm_g",
           "rwkv_w2", "rwkv_a2", "rwkv_g2", "rwkv_v2")


def _pack_kernel(*refs, layers):
    src = dict(zip(_PACKED, refs))
    sinks_ref, rel_bias_ref, bucket_ref, vec_ref, lora_ref, sink_ref, bias_ref = refs[len(_PACKED):]
    _bias_table(rel_bias_ref, bucket_ref, bias_ref)
    scale = HEAD_DIM ** -0.5
    vec_ref[...] = jnp.zeros_like(vec_ref)
    lora_ref[...] = jnp.zeros_like(lora_ref)
    tiled = lambda v, n: jnp.concatenate([v] * n, axis=1)

    def put(l, name, val):
        vec_ref[l, VEC_ROWS[name]:VEC_ROWS[name] + 1, 0:val.shape[1]] = val

    for l in range(layers):
        row = lambda n, at=l: src[n][at:at + 1, :]
        put(l, "mix_norm_g", row("mix_norm_g"))
        put(l, "q_gain", tiled(row("attn_q_norm"), ATTN_HEADS) * scale)
        put(l, "k_gain", tiled(row("attn_k_norm"), ATTN_KV_HEADS))
        put(l, "mq_gain", tiled(row("mem_q_norm"), MEM_HEADS) * scale)
        put(l, "mu", row("rwkv_mu"))
        for name, key in (("w0", "rwkv_w0"), ("a0", "rwkv_a0"), ("k_k", "rwkv_k_k"), ("k_a", "rwkv_k_a"),
                          ("ln_w", "rwkv_ln_w"), ("ln_b", "rwkv_ln_b"), ("ffn_norm_g", "ffn_norm_g")):
            put(l, name, row(key))
        r_k = src["rwkv_r_k"][l]
        put(l, "r_k", jnp.concatenate([r_k[h:h + 1, :] for h in range(r_k.shape[0])], axis=1))
        for j, (key, at) in enumerate((("rwkv_w2", 0), ("rwkv_a2", 32), ("rwkv_g2", 64))):
            w = src[key][l]
            lora_ref[l, j, at:at + w.shape[0], :] = _bf(w)
        if l > 0:
            mu_v = row("rwkv_mu_vres", l - 1)
            vec_ref[l, VEC_ROWS["mu"]:VEC_ROWS["mu"] + 1, SHIFT_W:SHIFT_W + mu_v.shape[1]] = mu_v
            put(l, "v0", row("rwkv_v0", l - 1))
            v2 = src["rwkv_v2"][l - 1]
            lora_ref[l, 3, 0:v2.shape[0], :] = _bf(v2)
        for h in range(ATTN_HEADS):
            g, i = divmod(h, ATTN_GROUP)
            sink_ref[l, g, i * BLOCK_Q:(i + 1) * BLOCK_Q, :] = jnp.full((BLOCK_Q, 1), sinks_ref[l, h], F32)


def _pack_params(p):
    layers = p["w_in"].shape[0]
    srcs = [p[n].astype(F32) for n in _PACKED]
    vm = pl.BlockSpec(memory_space=pltpu.VMEM)
    sm = pl.BlockSpec(memory_space=pltpu.SMEM)
    return pl.pallas_call(
        functools.partial(_pack_kernel, layers=layers),
        in_specs=[vm] * len(srcs) + [sm, sm, vm],
        out_specs=[vm, vm, vm, vm],
        out_shape=[jax.ShapeDtypeStruct((layers, VEC_TABLE_ROWS, VEC_W), F32),
                   jax.ShapeDtypeStruct((layers, 4, LORA_W, RWKV_W), BF16),
                   jax.ShapeDtypeStruct((layers, ATTN_KV_HEADS, ATTN_GROUP * BLOCK_Q, 1), F32),
                   jax.ShapeDtypeStruct((ATTN_HEADS, BLOCK_Q, 2 * BLOCK_Q), F32)],
        name="pack_params",
    )(*srcs, p["attn_sinks"].astype(F32), p["rel_bias"].astype(F32), jnp.asarray(_bucket_table()))


def _proj_call(x, p, l, v_first, *, nb):
    T, D = x.shape
    tm = PROJ_TM
    assert T % tm == 0 and (T // nb) % tm == 0
    has_vres = l > 0
    w = p["w_in"]
    ins = [x, p["vectors"], w, p["lora"], _block_diag_ones(ATTN_W), _block_diag_ones(KV_W), _block_diag_ones(MEM_W)]
    if has_vres:
        ins += [jnp.pad(p["w_in_vres"][l - 1], ((0, 0), (0, LORA_W - 16))), v_first]
    nw = IN_BASE + (LORA_W if has_vres else 0)

    def spec(a):
        if a is w:
            return pl.BlockSpec(memory_space=pl.ANY)
        if a.ndim >= 3:
            return _layer_spec(a, l)
        if a.shape[0] == T:
            return pl.BlockSpec((tm, a.shape[1]), lambda i: (i, 0))
        return pl.BlockSpec(a.shape, lambda i: (0, 0), pipeline_mode=pl.Buffered(1))

    out_shapes = [jax.ShapeDtypeStruct((T, ATTN_W), BF16), jax.ShapeDtypeStruct((T, KV_W), BF16),
                  jax.ShapeDtypeStruct((T, KV_W), BF16), jax.ShapeDtypeStruct((T, MEM_W), BF16)]
    out_shapes += [jax.ShapeDtypeStruct((T, RWKV_W), F32 if j == 1 else BF16) for j in range(7)]
    sw = SHIFT_W + (LORA_W if has_vres else 0)
    return pl.pallas_call(
        functools.partial(_proj_kernel, tm=tm, sub=PROJ_SUB, tiles_per_batch=(T // nb) // tm, has_vres=has_vres,
                          layer=l),
        grid=(T // tm,),
        in_specs=[spec(a) for a in ins],
        out_specs=[pl.BlockSpec((tm, s.shape[1]), lambda i: (i, 0)) for s in out_shapes],
        out_shape=out_shapes,
        scratch_shapes=[pltpu.VMEM((tm + 8, sw), F32), pltpu.VMEM((D, nw), BF16),
                        pltpu.VMEM((2, D // WEIGHT_STAGE_CHUNKS, IN_BASE), F32), pltpu.SemaphoreType.DMA((2,))],
        compiler_params=pltpu.CompilerParams(dimension_semantics=("arbitrary",), vmem_limit_bytes=VMEM_LIMIT_BYTES),
        name=f"proj_l{l}",
    )(*ins)


def _bucket_table():
    qi = np.arange(BLOCK_Q)[:, None]
    kj = np.arange(2 * BLOCK_Q)[None, :]
    dist = qi + BLOCK_Q - kj
    in_band = (dist >= 0) & (dist < WINDOW)
    d = np.maximum(dist, 1).astype(np.float32)
    large = MAX_EXACT + (np.log(d / np.float32(MAX_EXACT)) / np.float32(math.log(WINDOW / MAX_EXACT))
                         * np.float32(N_BUCKETS - MAX_EXACT)).astype(np.int32)
    large = np.minimum(large, N_BUCKETS - 1)
    bucket = np.where(dist < MAX_EXACT, np.maximum(dist, 0), large)
    return np.where(in_band, bucket, -1).astype(np.int32)


def _bias_table(rb_ref, bucket_ref, out_ref):
    bucket = bucket_ref[...]
    for h in range(ATTN_HEADS):
        acc = jnp.full(bucket.shape, -jnp.inf, F32)
        for j in range(N_BUCKETS):
            acc = jnp.where(bucket == j, rb_ref[j, h], acc)
        out_ref[h] = acc


def _memkv_kernel(mem_ref, g_ref, w_ref, kn_ref, bd_ref, mk_ref, mv_ref):
    mem = mem_ref[...]
    for l in range(w_ref.shape[0]):
        hn = _bf(_rms_rows(mem, g_ref[l:l + 1, :]))
        kv = _dot(hn, _bf(w_ref[l]))
        gain = jnp.concatenate([kn_ref[l:l + 1, :]] * MEM_HEADS, axis=1)
        mk_ref[l] = _bf(_head_rms(kv[:, :MEM_W], bd_ref[...], gain))
        mv_ref[l] = _bf(kv[:, MEM_W:])


def _memkv_call(mem2d, p):
    rows = mem2d.shape[0]
    layers = p["w_mem_kv"].shape[0]
    vm = pl.BlockSpec(memory_space=pltpu.VMEM)
    return pl.pallas_call(
        _memkv_kernel,
        in_specs=[vm] * 5,
        out_specs=[vm, vm],
        out_shape=[jax.ShapeDtypeStruct((layers, rows, MEM_W), BF16)] * 2,
        compiler_params=pltpu.CompilerParams(vmem_limit_bytes=VMEM_LIMIT_BYTES),
        name="mem_kv",
    )(mem2d, p["mem_norm_g"], p["w_mem_kv"], p["mem_k_norm"], _block_diag_ones(MEM_W))


def _mix_kernel(x_ref, qa_ref, kc_ref, kp_ref, vc_ref, vp_ref, bias_ref, sink_ref, qm_ref, mk_ref, mv_ref,
                y_ref, r_ref, k_ref, v_ref, gate_ref, vec_ref, bd_ref, wout_ref, out_ref, att_ref,
                *, tq, tiles_per_batch):
    lnw_ref, lnb_ref, rk_ref = (_Row(vec_ref, VEC_ROWS[n], RWKV_W) for n in ("ln_w", "ln_b", "r_k"))
    N, BQ = HEAD_DIM, BLOCK_Q
    seq_start = pl.program_id(0) % tiles_per_batch == 0
    nqb = tq // BQ

    @pl.when(pl.program_id(0) == 0)
    def _():
        att_ref[...] = jnp.zeros_like(att_ref)

    qa = qa_ref[...]
    kall = jnp.concatenate([kp_ref[...], kc_ref[...]], axis=0)
    vall = jnp.concatenate([vp_ref[...], vc_ref[...]], axis=0)
    k_group = [kall[:, g * N:(g + 1) * N] for g in range(ATTN_KV_HEADS)]
    v_group = [vall[:, g * N:(g + 1) * N] for g in range(ATTN_KV_HEADS)]
    before_seq = lax.broadcasted_iota(jnp.int32, (ATTN_GROUP * BQ, 2 * BQ), 1) < BQ

    qm = qm_ref[...]
    mk = mk_ref[0]
    mv = mv_ref[0]

    swa = [(j, g) for j in range(nqb) for g in range(ATTN_KV_HEADS)]
    logits, values, sinks = [], [], []
    for j, g in swa:
        qg = jnp.concatenate(
            [qa[j * BQ:(j + 1) * BQ, (ATTN_GROUP * g + i) * N:(ATTN_GROUP * g + i + 1) * N] for i in range(ATTN_GROUP)],
            axis=0)
        lg = _dot_nt(qg, k_group[g][j * BQ:(j + 2) * BQ]) + bias_ref[g]
        if j == 0:
            lg = jnp.where(jnp.logical_and(seq_start, before_seq), -jnp.inf, lg)
        logits.append(lg)
        values.append(v_group[g][j * BQ:(j + 2) * BQ])
        sinks.append(sink_ref[g])
    for h in range(MEM_HEADS):
        hs = slice(h * N, (h + 1) * N)
        logits.append(_dot_nt(qm[:, hs], mk[:, hs]))
        values.append(mv[:, hs])
        sinks.append(None)
    bd = bd_ref[...]
    y = y_ref[...]
    d = y - _head_sum(y, bd) * (1.0 / N)
    var = _head_sum(d * d, bd) * (1.0 / N)
    yn = d * lax.rsqrt(var + GN_EPS) * lnw_ref[...] + lnb_ref[...]
    f32 = lambda ref: ref[...].astype(F32)
    bonus = _head_sum(f32(r_ref) * f32(k_ref) * rk_ref[...], bd) * f32(v_ref)
    out_b = (yn + bonus) * f32(gate_ref)
    mixed = jnp.concatenate([att_ref[:, :ATTN_W], _bf(out_b), att_ref[:, ATTN_W:]], axis=-1)
    out_ref[...] = x_ref[...] + _dot(mixed, wout_ref[...])

    row_max = [jnp.max(lg, axis=-1, keepdims=True) for lg in logits]
    m = [rm if s is None else jnp.maximum(rm, s) for rm, s in zip(row_max, sinks)]
    e = [jnp.exp(lg - mm) for lg, mm in zip(logits, m)]
    denom = [jnp.sum(ee, axis=-1, keepdims=True) for ee in e]
    denom = [d if s is None else d + jnp.exp(s - mm) for d, s, mm in zip(denom, sinks, m)]
    outs = [_bf(_dot(_bf(ee), vv) / d) for ee, vv, d in zip(e, values, denom)]

    head_rows = [[None] * nqb for _ in range(ATTN_HEADS)]
    for (j, g), o in zip(swa, outs):
        for i in range(ATTN_GROUP):
            head_rows[ATTN_GROUP * g + i][j] = o[i * BQ:(i + 1) * BQ]
    att_ref[:, :ATTN_W] = jnp.concatenate([jnp.concatenate(rows, axis=0) for rows in head_rows], axis=-1)
    att_ref[:, ATTN_W:] = jnp.concatenate(outs[len(swa):], axis=-1)


def _mix_call(x, qa, ka, va, qm, mk, mv, y, r, k, v, gate, bias, p, l, wout, *, nb):
    T, D = x.shape
    tq = MIX_TQ
    S = T // nb
    assert S % tq == 0 and tq % BLOCK_Q == 0
    tpb = S // tq
    qpb = tq // BLOCK_Q
    mem_tokens = mk.shape[1] // nb
    sink = p["sink_columns"]
    n_tiles = T // tq
    att = lambda i: jnp.minimum(i, n_tiles - 1)
    fin = lambda i: jnp.maximum(i - 1, 0)
    att_tile = lambda w: pl.BlockSpec((tq, w), lambda i: (att(i), 0))
    tile = lambda w: pl.BlockSpec((tq, w), lambda i: (fin(i), 0))
    prev = pl.BlockSpec((BLOCK_Q, KV_W), lambda i: (jnp.maximum(att(i) * qpb - 1, 0), 0))
    full = lambda a: pl.BlockSpec(a.shape, lambda i: (0,) * a.ndim)
    memspec = pl.BlockSpec((None, 1, mem_tokens, MEM_W), lambda i: (l, att(i) // tpb, 0, 0))
    bd = _block_diag_ones(RWKV_W)
    mk3 = mk.reshape(mk.shape[0], nb, mem_tokens, MEM_W)
    mv3 = mv.reshape(mv.shape[0], nb, mem_tokens, MEM_W)
    return pl.pallas_call(
        functools.partial(_mix_kernel, tq=tq, tiles_per_batch=tpb),
        grid=(n_tiles + 1,),
        in_specs=[tile(D), att_tile(ATTN_W), att_tile(KV_W), prev, att_tile(KV_W), prev, full(bias), _layer_spec(sink, l),
                  att_tile(MEM_W), memspec, memspec] + [tile(RWKV_W)] * 5
                 + [_layer_spec(p["vectors"], l), full(bd),
                    pl.BlockSpec(wout.shape, lambda i: (0, 0), pipeline_mode=pl.Buffered(1))],
        out_specs=tile(D),
        out_shape=jax.ShapeDtypeStruct((T, D), F32),
        scratch_shapes=[pltpu.VMEM((tq, ATTN_W + MEM_W), BF16)],
        compiler_params=pltpu.CompilerParams(dimension_semantics=("arbitrary",), vmem_limit_bytes=VMEM_LIMIT_BYTES),
        name=f"mix_l{l}",
    )(x, qa, ka, ka, va, va, bias, sink, qm, mk3, mv3, y, r, k, v, gate, p["vectors"], bd, wout)


SUBLANES = 8


def _tile_copies(hbm, buf, sem, tile, slot, to_vmem):
    copies = []
    for s in range(SUBLANES):
        src, dst = hbm.at[tile, s], buf.at[slot, :, s, :]
        if not to_vmem:
            src, dst = dst, src
        copies.append(pltpu.make_async_copy(src, dst, sem.at[slot, s]))
    return copies


def _shift_rows(u, before, steps):
    first = lax.broadcasted_iota(jnp.int32, (SUBLANES, u.shape[1]), 0) == 0
    tm = u.shape[0]
    heads = []
    for i in range(steps):
        lo = tm - (steps - i) * SUBLANES
        wrapped = pltpu.roll(u[lo:lo + SUBLANES], 1, axis=0)
        prior = jnp.broadcast_to(before[SUBLANES - steps + i:SUBLANES - steps + i + 1], wrapped.shape)
        heads.append(jnp.where(first, prior, wrapped))
    return jnp.concatenate(heads + [u[:tm - steps * SUBLANES]], axis=0)


def _stage_weights(w_hbm, layer, dst_ref, stage_ref, sem):
    rows = stage_ref.shape[1]
    n = dst_ref.shape[0] // rows
    copy = lambda c: pltpu.make_async_copy(w_hbm.at[layer, pl.ds(c * rows, rows), :], stage_ref.at[c % 2], sem.at[c % 2])
    copy(0).start()
    for c in range(n):
        if c + 1 < n:
            copy(c + 1).start()
        copy(c).wait()
        dst_ref[c * rows:(c + 1) * rows, 0:stage_ref.shape[2]] = _bf(stage_ref[c % 2])


def _ffn_kernel(x_hbm, vec_ref, wup_ref, cw_ref, cb_ref, wdn_ref, o_hbm, xbuf, obuf, carry_ref, act_ref,
                in_sem, out_sem, *, tm, tiles_per_batch, n_tiles, layer):
    D = xbuf.shape[-1]
    g_ref = _Row(vec_ref, VEC_ROWS["ffn_norm_g"], D)
    q = tm // SUBLANES
    i = pl.program_id(0)
    slot = i % 2

    @pl.when(i == 0)
    def _():
        carry_ref[...] = jnp.zeros_like(carry_ref)
        for cp in _tile_copies(x_hbm, xbuf, in_sem, 0, 0, True):
            cp.start()

    @pl.when(i + 1 < n_tiles)
    def _():
        for cp in _tile_copies(x_hbm, xbuf, in_sem, i + 1, 1 - slot, True):
            cp.start()

    for cp in _tile_copies(x_hbm, xbuf, in_sem, i, slot, True):
        cp.wait()

    @pl.when(i >= 2)
    def _():
        for cp in _tile_copies(o_hbm, obuf, out_sem, i - 2, slot, False):
            cp.wait()

    seq_start = i % tiles_per_batch == 0
    g = g_ref[...]
    x = xbuf[slot].reshape(tm, D)
    before = jnp.where(seq_start, 0.0, _rms_rows(carry_ref[...], g))
    for r, grp in ((SUBLANES - 2, q - 2), (SUBLANES - 1, q - 1)):
        row = grp * SUBLANES + SUBLANES - 1
        carry_ref[r:r + 1, :] = x[row:row + 1]
    h_ext = _bf(jnp.concatenate([before, _rms_rows(x, g)], axis=0))

    n_chunks = D_FF // FFN_FC
    gate_cols = lambda c: slice(c * FFN_FC, (c + 1) * FFN_FC)
    val_cols = lambda c: slice(D_FF + c * FFN_FC, D_FF + (c + 1) * FFN_FC)

    def conv(u_ext, cols):
        w = cw_ref[:, cols]
        u = u_ext[SUBLANES:]
        return (cb_ref[layer:layer + 1, cols] + w[0:1] * _shift_rows(u, u_ext[:SUBLANES], 2)
                + w[1:2] * _shift_rows(u, u_ext[:SUBLANES], 1) + w[2:3] * u)

    up = lambda c: (_dot(h_ext, wup_ref[:, gate_cols(c)]), _dot(h_ext, wup_ref[:, val_cols(c)]))
    nxt = up(0)
    for c in range(n_chunks):
        ug, uv = nxt
        if c + 1 < n_chunks:
            nxt = up(c + 1)
        gt = conv(ug, gate_cols(c))
        act_ref[:, gate_cols(c)] = _bf(gt * jax.nn.sigmoid(gt) * conv(uv, val_cols(c)))
    obuf[slot] = (x + _dot(act_ref[...], wdn_ref[...])).reshape(q, SUBLANES, D)

    for cp in _tile_copies(o_hbm, obuf, out_sem, i, slot, False):
        cp.start()

    @pl.when(i == n_tiles - 1)
    def _():
        if n_tiles >= 2:
            for cp in _tile_copies(o_hbm, obuf, out_sem, i - 1, 1 - slot, False):
                cp.wait()
        for cp in _tile_copies(o_hbm, obuf, out_sem, i, slot, False):
            cp.wait()


def _ffn_call(x, p, l, wup, wdn, *, nb):
    T, D = x.shape
    tm = FFN_TM
    S = T // nb
    q = tm // SUBLANES
    assert S % tm == 0 and q % SUBLANES == 0 and D_FF % FFN_FC == 0
    n_tiles = T // tm
    cw, cb = p["conv_w"], p["conv_b"]
    hbm = pl.BlockSpec(memory_space=pl.ANY)
    whole = lambda a: pl.BlockSpec(a.shape, lambda i: (0, 0), pipeline_mode=pl.Buffered(1))
    out = pl.pallas_call(
        functools.partial(_ffn_kernel, tm=tm, tiles_per_batch=S // tm, n_tiles=n_tiles, layer=l),
        grid=(n_tiles,),
        in_specs=[hbm, _layer_spec(p["vectors"], l), whole(wup), _layer_spec(cw, l), whole(cb), whole(wdn)],
        out_specs=hbm,
        out_shape=jax.ShapeDtypeStruct((n_tiles, SUBLANES, q, D), F32),
        scratch_shapes=[pltpu.VMEM((2, q, SUBLANES, D), F32), pltpu.VMEM((2, q, SUBLANES, D), F32),
                        pltpu.VMEM((SUBLANES, D), F32), pltpu.VMEM((tm, D_FF), BF16),
                        pltpu.SemaphoreType.DMA((2, SUBLANES)), pltpu.SemaphoreType.DMA((2, SUBLANES))],
        compiler_params=pltpu.CompilerParams(dimension_semantics=("arbitrary",), vmem_limit_bytes=VMEM_LIMIT_BYTES),
        name=f"ffn_l{l}",
    )(x.reshape(n_tiles, SUBLANES, q, D), p["vectors"], wup, cw, cb, wdn)
    return out.reshape(T, D)


_PARAM_NAMES = (
    "rel_bias", "mix_norm_g", "w_in", "w_in_vres", "attn_q_norm", "attn_k_norm", "attn_sinks", "rwkv_mu",
    "rwkv_mu_vres", "rwkv_w0", "rwkv_w2", "rwkv_a0", "rwkv_a2", "rwkv_v0", "rwkv_v2", "rwkv_g2", "rwkv_k_k",
    "rwkv_k_a", "rwkv_r_k", "rwkv_ln_w", "rwkv_ln_b", "mem_norm_g", "w_mem_kv", "mem_q_norm", "mem_k_norm",
    "w_out", "ffn_norm_g", "w_up", "conv_w", "conv_b", "w_down")


def kernel(x, mem, rel_bias, mix_norm_g, w_in, w_in_vres, attn_q_norm, attn_k_norm, attn_sinks, rwkv_mu,
           rwkv_mu_vres, rwkv_w0, rwkv_w2, rwkv_a0, rwkv_a2, rwkv_v0, rwkv_v2, rwkv_g2, rwkv_k_k, rwkv_k_a,
           rwkv_r_k, rwkv_ln_w, rwkv_ln_b, mem_norm_g, w_mem_kv, mem_q_norm, mem_k_norm, w_out, ffn_norm_g,
           w_up, conv_w, conv_b, w_down):
    p = dict(zip(_PARAM_NAMES, (
        rel_bias, mix_norm_g, w_in, w_in_vres, attn_q_norm, attn_k_norm, attn_sinks, rwkv_mu, rwkv_mu_vres,
        rwkv_w0, rwkv_w2, rwkv_a0, rwkv_a2, rwkv_v0, rwkv_v2, rwkv_g2, rwkv_k_k, rwkv_k_a, rwkv_r_k, rwkv_ln_w,
        rwkv_ln_b, mem_norm_g, w_mem_kv, mem_q_norm, mem_k_norm, w_out, ffn_norm_g, w_up, conv_w, conv_b, w_down)))
    p["vectors"], p["lora"], p["sink_columns"], bias = _pack_params(p)
    bias = bias.reshape(ATTN_KV_HEADS, ATTN_GROUP * BLOCK_Q, 2 * BLOCK_Q)
    nb, S, D = x.shape
    xt = x.reshape(nb * S, D)
    mem2d = mem.reshape(nb * mem.shape[1], D)
    mk, mv = _memkv_call(mem2d, p)
    v_first = None
    for l in range(w_in.shape[0]):
        qa, ka, va, qm, r, lw, k, v, a, b, gate = _proj_call(xt, p, l, v_first, nb=nb)
        if l == 0:
            v_first = v
        y, (w_out_bf, w_up_bf, w_down_bf) = _rwkv_scan(r, lw, k, v, a, b, (w_out, w_up, w_down), l, nb=nb)
        xt = _mix_call(xt, qa, ka, va, qm, mk, mv, y, r, k, v, gate, bias, p, l, w_out_bf, nb=nb)
        xt = _ffn_call(xt, p, l, w_up_bf, w_down_bf, nb=nb)
    return xt.reshape(nb, S, D)
```
